```python
import math
import jax, jax.numpy as jnp
from jax import lax
import numpy as np

D_MODEL = 1024
BATCH = 8
SEQ = 8192
DEPTH = 4

N_MIXERS = 3
N_A = (DEPTH + 2) // 3
N_B = (DEPTH + 1) // 3
N_C = DEPTH // 3
BLOCK = 128
N_META = 16
PAD = BLOCK - N_META
NEG = -1e30

DEEPNORM_ALPHA = (2.0 * DEPTH) ** 0.25
DEEPNORM_BETA = (8.0 * DEPTH) ** -0.25
LN_EPS = 1e-5
RMS_EPS = 1e-6

FOX_HEADS = 16
FOX_HEAD_DIM = 64
FOX_GATE_BIAS = 4.0

SWA_Q_HEADS = 16
SWA_KV_HEADS = 2
SWA_HEAD_DIM = 64
WINDOW = 128
ROPE_THETA = 500000.0
ROPE_DIM = SWA_HEAD_DIM // 4

MLA_HEADS = 16
MLA_Q_LORA = 384
MLA_KV_LORA = 256
MLA_NOPE = 64
MLA_ROPE = 32
MLA_V = 64
MLA_ROPE_THETA = 10000.0

D_FF = 2816
CONV_W = 3

kernel_name = "hybrid_fox_swa_mla_convffn_trunk"


def layer_norm(x, g, b):
    xf = x.astype(jnp.float32)
    mu = jnp.mean(xf, axis=-1, keepdims=True)
    var = jnp.mean(jnp.square(xf - mu), axis=-1, keepdims=True)
    return ((xf - mu) * lax.rsqrt(var + LN_EPS) * g + b).astype(x.dtype)


def rms_norm(x, g):
    xf = x.astype(jnp.float32)
    return (xf * lax.rsqrt(jnp.mean(jnp.square(xf), axis=-1, keepdims=True) + RMS_EPS) * g).astype(x.dtype)


def rope_angles(pos, dim, theta):
    inv = theta ** (-jnp.arange(0, dim, 2, dtype=jnp.float32) / dim)
    ang = pos.astype(jnp.float32)[:, None] * inv[None, :]
    return jnp.cos(ang), jnp.sin(ang)


def apply_rope(x, cos, sin):
    x1, x2 = jnp.split(x.astype(jnp.float32), 2, axis=-1)
    c = cos[None, :, None, :]
    s = sin[None, :, None, :]
    return jnp.concatenate([x1 * c - x2 * s, x1 * s + x2 * c], axis=-1).astype(x.dtype)


def partial_rope(x, cos, sin):
    return jnp.concatenate([apply_rope(x[..., :ROPE_DIM], cos, sin), x[..., ROPE_DIM:]], axis=-1)


def dense_causal_attention(q, k, v, decay=None):
    B, L, H, dk = q.shape
    nblk = L // BLOCK
    scale = dk ** -0.5
    kpos = jnp.arange(L)
    kvalid = kpos >= PAD
    qb = q.reshape(B, nblk, BLOCK, H, dk).transpose(1, 0, 2, 3, 4)
    xs = (jnp.arange(nblk), qb)
    if decay is not None:
        ck = decay.transpose(0, 2, 1)
        cb = decay.reshape(B, nblk, BLOCK, H).transpose(1, 0, 3, 2)
        xs = xs + (cb,)

    def one_block(args):
        i, qi = args[0], args[1]
        s = jnp.einsum('bqhd,bkhd->bhqk', qi, k, preferred_element_type=jnp.float32) * scale
        if decay is not None:
            s = s + args[2][..., None] - ck[:, :, None, :]
        qpos = i * BLOCK + jnp.arange(BLOCK)
        mask = (kpos[None, :] <= qpos[:, None]) & kvalid[None, :]
        s = jnp.where(mask[None, None], s, NEG)
        p = jax.nn.softmax(s, axis=-1)
        return jnp.einsum('bhqk,bkhd->bqhd', p.astype(v.dtype), v)

    out = lax.map(one_block, xs)
    return out.transpose(1, 0, 2, 3, 4).reshape(B, L, H, v.shape[-1])


def fox_mixer(h, w_in, b_f, w_o):
    B, L, _ = h.shape
    hd = FOX_HEADS * FOX_HEAD_DIM
    proj = h @ w_in
    q, k, v, fg = jnp.split(proj, [hd, 2 * hd, 3 * hd], axis=-1)
    q = q.reshape(B, L, FOX_HEADS, FOX_HEAD_DIM)
    k = k.reshape(B, L, FOX_HEADS, FOX_HEAD_DIM)
    v = v.reshape(B, L, FOX_HEADS, FOX_HEAD_DIM)
    log_f = jax.nn.log_sigmoid((fg + b_f).astype(jnp.float32))
    c = jnp.cumsum(log_f, axis=1)
    o = dense_causal_attention(q, k, v, c)
    return o.reshape(B, L, hd) @ w_o


def swa_mixer(h, w_in, sinks, w_o, cos, sin):
    B, L, _ = h.shape
    G = SWA_Q_HEADS // SWA_KV_HEADS
    d = SWA_HEAD_DIM
    nblk = L // BLOCK
    qd, kd = SWA_Q_HEADS * d, SWA_KV_HEADS * d
    q, k, v = jnp.split(h @ w_in, [qd, qd + kd], axis=-1)
    q = partial_rope(q.reshape(B, L, SWA_Q_HEADS, d), cos, sin)
    k = partial_rope(k.reshape(B, L, SWA_KV_HEADS, d), cos, sin)
    v = v.reshape(B, L, SWA_KV_HEADS, d)

    qb = q.reshape(B, nblk, BLOCK, SWA_KV_HEADS, G, d)
    kb = k.reshape(B, nblk, BLOCK, SWA_KV_HEADS, d)
    vb = v.reshape(B, nblk, BLOCK, SWA_KV_HEADS, d)
    pad_blk = ((0, 0), (1, 0), (0, 0), (0, 0), (0, 0))
    kband = jnp.concatenate([jnp.pad(kb, pad_blk)[:, :-1], kb], axis=2)
    vband = jnp.concatenate([jnp.pad(vb, pad_blk)[:, :-1], vb], axis=2)
    kmeta = k[:, PAD:PAD + N_META]
    vmeta = v[:, PAD:PAD + N_META]

    scale = d ** -0.5
    s_band = jnp.einsum('bnqhgd,bnkhd->bnhgqk', qb, kband, preferred_element_type=jnp.float32) * scale
    s_meta = jnp.einsum('bnqhgd,bmhd->bnhgqm', qb, kmeta, preferred_element_type=jnp.float32) * scale

    qpos = jnp.arange(L).reshape(nblk, BLOCK)
    kpos = (jnp.arange(nblk)[:, None] - 1) * BLOCK + jnp.arange(2 * BLOCK)[None, :]
    diff = qpos[:, :, None] - kpos[:, None, :]
    band_mask = (diff >= 0) & (diff < WINDOW) & (kpos[:, None, :] >= PAD + N_META)
    meta_pos = PAD + jnp.arange(N_META)
    meta_mask = meta_pos[None, None, :] <= qpos[:, :, None]
    s_band = jnp.where(band_mask[None, :, None, None], s_band, NEG)
    s_meta = jnp.where(meta_mask[None, :, None, None], s_meta, NEG)

    sink = sinks.astype(jnp.float32).reshape(SWA_KV_HEADS, G)[None, None, :, :, None, None]
    m = jnp.maximum(jnp.maximum(s_band.max(-1, keepdims=True), s_meta.max(-1, keepdims=True)), sink)
    p_band = jnp.exp(s_band - m)
    p_meta = jnp.exp(s_meta - m)
    denom = p_band.sum(-1, keepdims=True) + p_meta.sum(-1, keepdims=True) + jnp.exp(sink - m)
    p_band = (p_band / denom).astype(v.dtype)
    p_meta = (p_meta / denom).astype(v.dtype)
    o = (jnp.einsum('bnhgqk,bnkhd->bnqhgd', p_band, vband)
         + jnp.einsum('bnhgqm,bmhd->bnqhgd', p_meta, vmeta))
    return o.reshape(B, L, qd) @ w_o


def mla_mixer(h, w_a, g_q, g_kv, w_uq, w_ukv, w_o, cos, sin):
    B, L, _ = h.shape
    cq, ckv, kr = jnp.split(h @ w_a, [MLA_Q_LORA, MLA_Q_LORA + MLA_KV_LORA], axis=-1)
    cq = rms_norm(cq, g_q)
    ckv = rms_norm(ckv, g_kv)
    q = (cq @ w_uq).reshape(B, L, MLA_HEADS, MLA_NOPE + MLA_ROPE)
    q_nope, q_rope = jnp.split(q, [MLA_NOPE], axis=-1)
    q_rope = apply_rope(q_rope, cos, sin)
    k_rope = apply_rope(kr[:, :, None, :], cos, sin)
    kv = (ckv @ w_ukv).reshape(B, L, MLA_HEADS, MLA_NOPE + MLA_V)
    k_nope, v = jnp.split(kv, [MLA_NOPE], axis=-1)
    qf = jnp.concatenate([q_nope, q_rope], axis=-1)
    kf = jnp.concatenate([k_nope, jnp.broadcast_to(k_rope, (B, L, MLA_HEADS, MLA_ROPE))], axis=-1)
    o = dense_causal_attention(qf, kf, v)
    return o.reshape(B, L, MLA_HEADS * MLA_V) @ w_o


def conv_glu_ffn(h, w_in, conv_w, conv_b, w_out, valid):
    L = h.shape[1]
    u = (h @ w_in) * valid.astype(h.dtype)[None, :, None]
    up = jnp.pad(u, ((0, 0), (CONV_W - 1, 0), (0, 0)))
    y = conv_b + up[:, 0:L] * conv_w[0]
    for j in range(1, CONV_W):
        y = y + up[:, j:j + L] * conv_w[j]
    g, val = jnp.split(y, 2, axis=-1)
    return (jax.nn.silu(g) * val) @ w_out


def _dense(key, shape, fan_in, gain=1.0):
    return jax.random.normal(key, shape, jnp.float32) * (gain * fan_in ** -0.5)


def _fwd_setup_inputs(seed: int = 0) -> dict:
    key = jax.random.key(seed)
    ks = jax.random.split(key, 24)
    D = D_MODEL
    beta = DEEPNORM_BETA
    nrm = lambda k, s: jax.random.normal(k, s, jnp.float32)

    x = nrm(ks[0], (BATCH, SEQ, D))
    meta_tokens = nrm(ks[1], (N_META, D))
    ln1_g = 1.0 + 0.01 * nrm(ks[2], (DEPTH, D))
    ln1_b = 0.01 * nrm(ks[3], (DEPTH, D))
    ln2_g = 1.0 + 0.01 * nrm(ks[4], (DEPTH, D))
    ln2_b = 0.01 * nrm(ks[5], (DEPTH, D))

    fhd = FOX_HEADS * FOX_HEAD_DIM
    fox_cols = jnp.concatenate([jnp.ones((2 * fhd,), jnp.float32), jnp.full((fhd,), beta, jnp.float32),
                                jnp.ones((FOX_HEADS,), jnp.float32)])
    fox_w_in = _dense(ks[6], (N_A, D, 3 * fhd + FOX_HEADS), D) * fox_cols
    fox_b_f = FOX_GATE_BIAS + 0.5 * nrm(ks[7], (N_A, FOX_HEADS))
    fox_w_o = _dense(ks[8], (N_A, fhd, D), fhd, beta)

    sqd, skd = SWA_Q_HEADS * SWA_HEAD_DIM, SWA_KV_HEADS * SWA_HEAD_DIM
    swa_cols = jnp.concatenate([jnp.ones((sqd + skd,), jnp.float32), jnp.full((skd,), beta, jnp.float32)])
    swa_w_in = _dense(ks[9], (N_B, D, sqd + 2 * skd), D) * swa_cols
    swa_sinks = 0.5 * nrm(ks[10], (N_B, SWA_Q_HEADS))
    swa_w_o = _dense(ks[11], (N_B, sqd, D), sqd, beta)

    mla_w_a = _dense(ks[12], (N_C, D, MLA_Q_LORA + MLA_KV_LORA + MLA_ROPE), D)
    mla_g_q = 1.0 + 0.01 * nrm(ks[13], (N_C, MLA_Q_LORA))
    mla_g_kv = 1.0 + 0.01 * nrm(ks[14], (N_C, MLA_KV_LORA))
    mla_w_uq = _dense(ks[15], (N_C, MLA_Q_LORA, MLA_HEADS * (MLA_NOPE + MLA_ROPE)), MLA_Q_LORA)
    ukv_cols = jnp.tile(jnp.concatenate([jnp.ones((MLA_NOPE,), jnp.float32),
                                         jnp.full((MLA_V,), beta, jnp.float32)]), MLA_HEADS)
    mla_w_ukv = _dense(ks[16], (N_C, MLA_KV_LORA, MLA_HEADS * (MLA_NOPE + MLA_V)), MLA_KV_LORA) * ukv_cols
    mla_w_o = _dense(ks[17], (N_C, MLA_HEADS * MLA_V, D), MLA_HEADS * MLA_V, beta)

    ffn_w_in = _dense(ks[18], (DEPTH, D, 2 * D_FF), D, beta)
    ffn_conv_w = _dense(ks[19], (DEPTH, CONV_W, 2 * D_FF), CONV_W)
    ffn_conv_b = 0.01 * nrm(ks[20], (DEPTH, 2 * D_FF))
    ffn_w_out = _dense(ks[21], (DEPTH, D_FF, D), D_FF, beta)

    return {"x": x, "meta_tokens": meta_tokens, "ln1_g": ln1_g, "ln1_b": ln1_b, "ln2_g": ln2_g, "ln2_b": ln2_b,
            "fox_w_in": fox_w_in, "fox_b_f": fox_b_f, "fox_w_o": fox_w_o,
            "swa_w_in": swa_w_in, "swa_sinks": swa_sinks, "swa_w_o": swa_w_o,
            "mla_w_a": mla_w_a, "mla_g_q": mla_g_q, "mla_g_kv": mla_g_kv, "mla_w_uq": mla_w_uq,
            "mla_w_ukv": mla_w_ukv, "mla_w_o": mla_w_o,
            "ffn_w_in": ffn_w_in, "ffn_conv_w": ffn_conv_w, "ffn_conv_b": ffn_conv_b, "ffn_w_out": ffn_w_out}


def _fwd_reference(x, meta_tokens, ln1_g, ln1_b, ln2_g, ln2_b,
              fox_w_in, fox_b_f, fox_w_o,
              swa_w_in, swa_sinks, swa_w_o,
              mla_w_a, mla_g_q, mla_g_kv, mla_w_uq, mla_w_ukv, mla_w_o,
              ffn_w_in, ffn_conv_w, ffn_conv_b, ffn_w_out):
    B, S, D = x.shape
    h = jnp.concatenate([jnp.zeros((B, PAD, D), x.dtype),
                         jnp.broadcast_to(meta_tokens.astype(x.dtype)[None], (B, N_META, D)), x], axis=1)
    L = h.shape[1]
    idx = jnp.arange(L)
    pos = idx - PAD
    valid = idx >= PAD
    cos_p, sin_p = rope_angles(pos, ROPE_DIM, ROPE_THETA)
    cos_m, sin_m = rope_angles(pos, MLA_ROPE, MLA_ROPE_THETA)

    for i in range(DEPTH):
        kind, j = i % N_MIXERS, i // N_MIXERS
        if kind == 0:
            mix = fox_mixer(h, fox_w_in[j], fox_b_f[j], fox_w_o[j])
        elif kind == 1:
            mix = swa_mixer(h, swa_w_in[j], swa_sinks[j], swa_w_o[j], cos_p, sin_p)
        else:
            mix = mla_mixer(h, mla_w_a[j], mla_g_q[j], mla_g_kv[j], mla_w_uq[j], mla_w_ukv[j], mla_w_o[j],
                            cos_m, sin_m)
        h = layer_norm(DEEPNORM_ALPHA * h + mix, ln1_g[i], ln1_b[i])
        ffn = conv_glu_ffn(h, ffn_w_in[i], ffn_conv_w[i], ffn_conv_b[i], ffn_w_out[i], valid)
        h = layer_norm(DEEPNORM_ALPHA * h + ffn, ln2_g[i], ln2_b[i])
    return h[:, BLOCK:]


import jax as _jax
import jax.numpy as _jnp

TWIN_FORMAT = 'train_step'
FWD_PARAMS = ['x', 'meta_tokens', 'ln1_g', 'ln1_b', 'ln2_g', 'ln2_b', 'fox_w_in', 'fox_b_f', 'fox_w_o', 'swa_w_in', 'swa_sinks', 'swa_w_o', 'mla_w_a', 'mla_g_q', 'mla_g_kv', 'mla_w_uq', 'mla_w_ukv', 'mla_w_o', 'ffn_w_in', 'ffn_conv_w', 'ffn_conv_b', 'ffn_w_out']
TWIN_WEIGHTS = ['meta_tokens', 'ln1_g', 'ln1_b', 'ln2_g', 'ln2_b', 'fox_w_in', 'fox_b_f', 'fox_w_o', 'swa_w_in', 'swa_sinks', 'swa_w_o', 'mla_w_a', 'mla_g_q', 'mla_g_kv', 'mla_w_uq', 'mla_w_ukv', 'mla_w_o', 'ffn_w_in', 'ffn_conv_w', 'ffn_conv_b', 'ffn_w_out']
TWIN_DIFF_INPUT = 'x'
TWIN_INPUTS = ['x', 'meta_tokens', 'ln1_g', 'ln1_b', 'ln2_g', 'ln2_b', 'fox_w_in', 'fox_b_f', 'fox_w_o', 'swa_w_in', 'swa_sinks', 'swa_w_o', 'mla_w_a', 'mla_g_q', 'mla_g_kv', 'mla_w_uq', 'mla_w_ukv', 'mla_w_o', 'ffn_w_in', 'ffn_conv_w', 'ffn_conv_b', 'ffn_w_out', 'loss_target', 'm_meta_tokens', 'm_ln1_g', 'm_ln1_b', 'm_ln2_g', 'm_ln2_b', 'm_fox_w_in', 'm_fox_b_f', 'm_fox_w_o', 'm_swa_w_in', 'm_swa_sinks', 'm_swa_w_o', 'm_mla_w_a', 'm_mla_g_q', 'm_mla_g_kv', 'm_mla_w_uq', 'm_mla_w_ukv', 'm_mla_w_o', 'm_ffn_w_in', 'm_ffn_conv_w', 'm_ffn_conv_b', 'm_ffn_w_out', 'v_meta_tokens', 'v_ln1_g', 'v_ln1_b', 'v_ln2_g', 'v_ln2_b', 'v_fox_w_in', 'v_fox_b_f', 'v_fox_w_o', 'v_swa_w_in', 'v_swa_sinks', 'v_swa_w_o', 'v_mla_w_a', 'v_mla_g_q', 'v_mla_g_kv', 'v_mla_w_uq', 'v_mla_w_ukv', 'v_mla_w_o', 'v_ffn_w_in', 'v_ffn_conv_w', 'v_ffn_conv_b', 'v_ffn_w_out']
TWIN_OUTPUTS = ['loss', 'grad_x', 'grad_meta_tokens', 'grad_ln1_g', 'grad_ln1_b', 'grad_ln2_g', 'grad_ln2_b', 'grad_fox_w_in', 'grad_fox_b_f', 'grad_fox_w_o', 'grad_swa_w_in', 'grad_swa_sinks', 'grad_swa_w_o', 'grad_mla_w_a', 'grad_mla_g_q', 'grad_mla_g_kv', 'grad_mla_w_uq', 'grad_mla_w_ukv', 'grad_mla_w_o', 'grad_ffn_w_in', 'grad_ffn_conv_w', 'grad_ffn_conv_b', 'grad_ffn_w_out', 'delta_meta_tokens', 'delta_ln1_g', 'delta_ln1_b', 'delta_ln2_g', 'delta_ln2_b', 'delta_fox_w_in', 'delta_fox_b_f', 'delta_fox_w_o', 'delta_swa_w_in', 'delta_swa_sinks', 'delta_swa_w_o', 'delta_mla_w_a', 'delta_mla_g_q', 'delta_mla_g_kv', 'delta_mla_w_uq', 'delta_mla_w_ukv', 'delta_mla_w_o', 'delta_ffn_w_in', 'delta_ffn_conv_w', 'delta_ffn_conv_b', 'delta_ffn_w_out', 'new_m_meta_tokens', 'new_m_ln1_g', 'new_m_ln1_b', 'new_m_ln2_g', 'new_m_ln2_b', 'new_m_fox_w_in', 'new_m_fox_b_f', 'new_m_fox_w_o', 'new_m_swa_w_in', 'new_m_swa_sinks', 'new_m_swa_w_o', 'new_m_mla_w_a', 'new_m_mla_g_q', 'new_m_mla_g_kv', 'new_m_mla_w_uq', 'new_m_mla_w_ukv', 'new_m_mla_w_o', 'new_m_ffn_w_in', 'new_m_ffn_conv_w', 'new_m_ffn_conv_b', 'new_m_ffn_w_out', 'new_v_meta_tokens', 'new_v_ln1_g', 'new_v_ln1_b', 'new_v_ln2_g', 'new_v_ln2_b', 'new_v_fox_w_in', 'new_v_fox_b_f', 'new_v_fox_w_o', 'new_v_swa_w_in', 'new_v_swa_sinks', 'new_v_swa_w_o', 'new_v_mla_w_a', 'new_v_mla_g_q', 'new_v_mla_g_kv', 'new_v_mla_w_uq', 'new_v_mla_w_ukv', 'new_v_mla_w_o', 'new_v_ffn_w_in', 'new_v_ffn_conv_w', 'new_v_ffn_conv_b', 'new_v_ffn_w_out']
TWIN_LEAF_KINDS = {'loss': 'loss', 'grad_x': 'grad_x', 'grad_meta_tokens': 'grad_w', 'grad_ln1_g': 'grad_w', 'grad_ln1_b': 'grad_w', 'grad_ln2_g': 'grad_w', 'grad_ln2_b': 'grad_w', 'grad_fox_w_in': 'grad_w', 'grad_fox_b_f': 'grad_w', 'grad_fox_w_o': 'grad_w', 'grad_swa_w_in': 'grad_w', 'grad_swa_sinks': 'grad_w', 'grad_swa_w_o': 'grad_w', 'grad_mla_w_a': 'grad_w', 'grad_mla_g_q': 'grad_w', 'grad_mla_g_kv': 'grad_w', 'grad_mla_w_uq': 'grad_w', 'grad_mla_w_ukv': 'grad_w', 'grad_mla_w_o': 'grad_w', 'grad_ffn_w_in': 'grad_w', 'grad_ffn_conv_w': 'grad_w', 'grad_ffn_conv_b': 'grad_w', 'grad_ffn_w_out': 'grad_w', 'delta_meta_tokens': 'delta_w', 'delta_ln1_g': 'delta_w', 'delta_ln1_b': 'delta_w', 'delta_ln2_g': 'delta_w', 'delta_ln2_b': 'delta_w', 'delta_fox_w_in': 'delta_w', 'delta_fox_b_f': 'delta_w', 'delta_fox_w_o': 'delta_w', 'delta_swa_w_in': 'delta_w', 'delta_swa_sinks': 'delta_w', 'delta_swa_w_o': 'delta_w', 'delta_mla_w_a': 'delta_w', 'delta_mla_g_q': 'delta_w', 'delta_mla_g_kv': 'delta_w', 'delta_mla_w_uq': 'delta_w', 'delta_mla_w_ukv': 'delta_w', 'delta_mla_w_o': 'delta_w', 'delta_ffn_w_in': 'delta_w', 'delta_ffn_conv_w': 'delta_w', 'delta_ffn_conv_b': 'delta_w', 'delta_ffn_w_out': 'delta_w', 'new_m_meta_tokens': 'new_m', 'new_m_ln1_g': 'new_m', 'new_m_ln1_b': 'new_m', 'new_m_ln2_g': 'new_m', 'new_m_ln2_b': 'new_m', 'new_m_fox_w_in': 'new_m', 'new_m_fox_b_f': 'new_m', 'new_m_fox_w_o': 'new_m', 'new_m_swa_w_in': 'new_m', 'new_m_swa_sinks': 'new_m', 'new_m_swa_w_o': 'new_m', 'new_m_mla_w_a': 'new_m', 'new_m_mla_g_q': 'new_m', 'new_m_mla_g_kv': 'new_m', 'new_m_mla_w_uq': 'new_m', 'new_m_mla_w_ukv': 'new_m', 'new_m_mla_w_o': 'new_m', 'new_m_ffn_w_in': 'new_m', 'new_m_ffn_conv_w': 'new_m', 'new_m_ffn_conv_b': 'new_m', 'new_m_ffn_w_out': 'new_m', 'new_v_meta_tokens': 'new_v', 'new_v_ln1_g': 'new_v', 'new_v_ln1_b': 'new_v', 'new_v_ln2_g': 'new_v', 'new_v_ln2_b': 'new_v', 'new_v_fox_w_in': 'new_v', 'new_v_fox_b_f': 'new_v', 'new_v_fox_w_o': 'new_v', 'new_v_swa_w_in': 'new_v', 'new_v_swa_sinks': 'new_v', 'new_v_swa_w_o': 'new_v', 'new_v_mla_w_a': 'new_v', 'new_v_mla_g_q': 'new_v', 'new_v_mla_g_kv': 'new_v', 'new_v_mla_w_uq': 'new_v', 'new_v_mla_w_ukv': 'new_v', 'new_v_mla_w_o': 'new_v', 'new_v_ffn_w_in': 'new_v', 'new_v_ffn_conv_w': 'new_v', 'new_v_ffn_conv_b': 'new_v', 'new_v_ffn_w_out': 'new_v'}


def _forward(args):
    return _fwd_reference(*[args[k] for k in FWD_PARAMS])


def _output_shape():
    def fwd():
        inp = _fwd_setup_inputs(0)
        return _fwd_reference(*[inp[k] for k in FWD_PARAMS])
    out = _jax.eval_shape(fwd)
    return out.shape, out.dtype

N_MICROBATCH = 1
ADAM_LR = 0.001
ADAM_B1 = 0.9
ADAM_B2 = 0.999
ADAM_EPS = 1e-08
ADAM_WD = 0.01
ADAM_STEP = 10
PER_EXAMPLE_BATCH_AXIS = {'x': 0, 'loss_target': 0}
SHARED_INPUTS = []
_WEIGHT_DTYPES = {'meta_tokens': _jnp.float32, 'ln1_g': _jnp.float32, 'ln1_b': _jnp.float32, 'ln2_g': _jnp.float32, 'ln2_b': _jnp.float32, 'fox_w_in': _jnp.float32, 'fox_b_f': _jnp.float32, 'fox_w_o': _jnp.float32, 'swa_w_in': _jnp.float32, 'swa_sinks': _jnp.float32, 'swa_w_o': _jnp.float32, 'mla_w_a': _jnp.float32, 'mla_g_q': _jnp.float32, 'mla_g_kv': _jnp.float32, 'mla_w_uq': _jnp.float32, 'mla_w_ukv': _jnp.float32, 'mla_w_o': _jnp.float32, 'ffn_w_in': _jnp.float32, 'ffn_conv_w': _jnp.float32, 'ffn_conv_b': _jnp.float32, 'ffn_w_out': _jnp.float32}
MOMENT_SCALE = {'meta_tokens': 1.463149e-03, 'ln1_g': 1.023452e+00, 'ln1_b': 5.238266e-01, 'ln2_g': 3.203927e+01, 'ln2_b': 8.349345e-01, 'fox_w_in': 9.159230e-03, 'fox_b_f': 2.693706e-02, 'fox_w_o': 1.402958e-02, 'swa_w_in': 1.110907e-02, 'swa_sinks': 7.405990e-04, 'swa_w_o': 1.123641e-02, 'mla_w_a': 9.152506e-03, 'mla_g_q': 6.612097e-03, 'mla_g_kv': 1.145216e-02, 'mla_w_uq': 3.489441e-03, 'mla_w_ukv': 8.167755e-03, 'mla_w_o': 1.118029e-02, 'ffn_w_in': 8.683948e-03, 'ffn_conv_w': 3.668200e-03, 'ffn_conv_b': 8.893590e-03, 'ffn_w_out': 1.424675e-02}


def _to_microbatches(a, axis):
    t = _jnp.moveaxis(a, axis, 0)
    t = t.reshape((N_MICROBATCH, t.shape[0] // N_MICROBATCH) + t.shape[1:])
    return _jnp.moveaxis(t, 1, axis + 1)


def setup_inputs(seed: int = 0) -> dict:
    inp = _fwd_setup_inputs(seed)
    key = _jax.random.fold_in(_jax.random.key(seed), 7919)
    shape, _ = _output_shape()
    out = dict(inp)
    out["loss_target"] = _jax.random.normal(_jax.random.fold_in(key, 0), shape, _jnp.float32)
    for i, name in enumerate(TWIN_WEIGHTS):
        w = inp[name].astype(_jnp.float32)
        if MOMENT_SCALE is None:
            s = _jnp.sqrt(_jnp.mean(_jnp.square(w)) + 1e-30)
        else:
            s = MOMENT_SCALE[name]
        km, kv = _jax.random.split(_jax.random.fold_in(key, i + 1))
        out[name] = w
        out["m_" + name] = s * _jax.random.normal(km, w.shape, _jnp.float32)
        out["v_" + name] = (s * s) * _jax.random.uniform(kv, w.shape, _jnp.float32, 0.5, 1.5)
    if N_MICROBATCH > 1:
        for name, axis in PER_EXAMPLE_BATCH_AXIS.items():
            out[name] = _to_microbatches(out[name], axis)
    return {'x': out['x'], 'meta_tokens': out['meta_tokens'], 'ln1_g': out['ln1_g'], 'ln1_b': out['ln1_b'], 'ln2_g': out['ln2_g'], 'ln2_b': out['ln2_b'], 'fox_w_in': out['fox_w_in'], 'fox_b_f': out['fox_b_f'], 'fox_w_o': out['fox_w_o'], 'swa_w_in': out['swa_w_in'], 'swa_sinks': out['swa_sinks'], 'swa_w_o': out['swa_w_o'], 'mla_w_a': out['mla_w_a'], 'mla_g_q': out['mla_g_q'], 'mla_g_kv': out['mla_g_kv'], 'mla_w_uq': out['mla_w_uq'], 'mla_w_ukv': out['mla_w_ukv'], 'mla_w_o': out['mla_w_o'], 'ffn_w_in': out['ffn_w_in'], 'ffn_conv_w': out['ffn_conv_w'], 'ffn_conv_b': out['ffn_conv_b'], 'ffn_w_out': out['ffn_w_out'], 'loss_target': out['loss_target'], 'm_meta_tokens': out['m_meta_tokens'], 'm_ln1_g': out['m_ln1_g'], 'm_ln1_b': out['m_ln1_b'], 'm_ln2_g': out['m_ln2_g'], 'm_ln2_b': out['m_ln2_b'], 'm_fox_w_in': out['m_fox_w_in'], 'm_fox_b_f': out['m_fox_b_f'], 'm_fox_w_o': out['m_fox_w_o'], 'm_swa_w_in': out['m_swa_w_in'], 'm_swa_sinks': out['m_swa_sinks'], 'm_swa_w_o': out['m_swa_w_o'], 'm_mla_w_a': out['m_mla_w_a'], 'm_mla_g_q': out['m_mla_g_q'], 'm_mla_g_kv': out['m_mla_g_kv'], 'm_mla_w_uq': out['m_mla_w_uq'], 'm_mla_w_ukv': out['m_mla_w_ukv'], 'm_mla_w_o': out['m_mla_w_o'], 'm_ffn_w_in': out['m_ffn_w_in'], 'm_ffn_conv_w': out['m_ffn_conv_w'], 'm_ffn_conv_b': out['m_ffn_conv_b'], 'm_ffn_w_out': out['m_ffn_w_out'], 'v_meta_tokens': out['v_meta_tokens'], 'v_ln1_g': out['v_ln1_g'], 'v_ln1_b': out['v_ln1_b'], 'v_ln2_g': out['v_ln2_g'], 'v_ln2_b': out['v_ln2_b'], 'v_fox_w_in': out['v_fox_w_in'], 'v_fox_b_f': out['v_fox_b_f'], 'v_fox_w_o': out['v_fox_w_o'], 'v_swa_w_in': out['v_swa_w_in'], 'v_swa_sinks': out['v_swa_sinks'], 'v_swa_w_o': out['v_swa_w_o'], 'v_mla_w_a': out['v_mla_w_a'], 'v_mla_g_q': out['v_mla_g_q'], 'v_mla_g_kv': out['v_mla_g_kv'], 'v_mla_w_uq': out['v_mla_w_uq'], 'v_mla_w_ukv': out['v_mla_w_ukv'], 'v_mla_w_o': out['v_mla_w_o'], 'v_ffn_w_in': out['v_ffn_w_in'], 'v_ffn_conv_w': out['v_ffn_conv_w'], 'v_ffn_conv_b': out['v_ffn_conv_b'], 'v_ffn_w_out': out['v_ffn_w_out']}


def _loss(weights, diff, rest, loss_target):
    with _jax.named_scope("forward"):
        args = {**rest, TWIN_DIFF_INPUT: diff, **{k: w.astype(_WEIGHT_DTYPES[k]) for k, w in weights.items()}}
        y = _forward(args)
    with _jax.named_scope("loss_head"):
        err = _jnp.square(y.astype(_jnp.float32) - loss_target)
        return 0.5 * _jnp.sum(_jnp.mean(err, axis=-1)) if err.ndim else 0.5 * err


def _adamw(w, g, m, v):
    m = ADAM_B1 * m + (1.0 - ADAM_B1) * g
    v = ADAM_B2 * v + (1.0 - ADAM_B2) * _jnp.square(g)
    m_hat = m / (1.0 - ADAM_B1 ** ADAM_STEP)
    v_hat = v / (1.0 - ADAM_B2 ** ADAM_STEP)
    delta = -ADAM_LR * (m_hat / (_jnp.sqrt(v_hat) + ADAM_EPS) + ADAM_WD * w)
    return delta, m, v


def reference(x, meta_tokens, ln1_g, ln1_b, ln2_g, ln2_b, fox_w_in, fox_b_f, fox_w_o, swa_w_in, swa_sinks, swa_w_o, mla_w_a, mla_g_q, mla_g_kv, mla_w_uq, mla_w_ukv, mla_w_o, ffn_w_in, ffn_conv_w, ffn_conv_b, ffn_w_out, loss_target, m_meta_tokens, m_ln1_g, m_ln1_b, m_ln2_g, m_ln2_b, m_fox_w_in, m_fox_b_f, m_fox_w_o, m_swa_w_in, m_swa_sinks, m_swa_w_o, m_mla_w_a, m_mla_g_q, m_mla_g_kv, m_mla_w_uq, m_mla_w_ukv, m_mla_w_o, m_ffn_w_in, m_ffn_conv_w, m_ffn_conv_b, m_ffn_w_out, v_meta_tokens, v_ln1_g, v_ln1_b, v_ln2_g, v_ln2_b, v_fox_w_in, v_fox_b_f, v_fox_w_o, v_swa_w_in, v_swa_sinks, v_swa_w_o, v_mla_w_a, v_mla_g_q, v_mla_g_kv, v_mla_w_uq, v_mla_w_ukv, v_mla_w_o, v_ffn_w_in, v_ffn_conv_w, v_ffn_conv_b, v_ffn_w_out):
    given = dict(x=x, meta_tokens=meta_tokens, ln1_g=ln1_g, ln1_b=ln1_b, ln2_g=ln2_g, ln2_b=ln2_b, fox_w_in=fox_w_in, fox_b_f=fox_b_f, fox_w_o=fox_w_o, swa_w_in=swa_w_in, swa_sinks=swa_sinks, swa_w_o=swa_w_o, mla_w_a=mla_w_a, mla_g_q=mla_g_q, mla_g_kv=mla_g_kv, mla_w_uq=mla_w_uq, mla_w_ukv=mla_w_ukv, mla_w_o=mla_w_o, ffn_w_in=ffn_w_in, ffn_conv_w=ffn_conv_w, ffn_conv_b=ffn_conv_b, ffn_w_out=ffn_w_out, loss_target=loss_target, m_meta_tokens=m_meta_tokens, m_ln1_g=m_ln1_g, m_ln1_b=m_ln1_b, m_ln2_g=m_ln2_g, m_ln2_b=m_ln2_b, m_fox_w_in=m_fox_w_in, m_fox_b_f=m_fox_b_f, m_fox_w_o=m_fox_w_o, m_swa_w_in=m_swa_w_in, m_swa_sinks=m_swa_sinks, m_swa_w_o=m_swa_w_o, m_mla_w_a=m_mla_w_a, m_mla_g_q=m_mla_g_q, m_mla_g_kv=m_mla_g_kv, m_mla_w_uq=m_mla_w_uq, m_mla_w_ukv=m_mla_w_ukv, m_mla_w_o=m_mla_w_o, m_ffn_w_in=m_ffn_w_in, m_ffn_conv_w=m_ffn_conv_w, m_ffn_conv_b=m_ffn_conv_b, m_ffn_w_out=m_ffn_w_out, v_meta_tokens=v_meta_tokens, v_ln1_g=v_ln1_g, v_ln1_b=v_ln1_b, v_ln2_g=v_ln2_g, v_ln2_b=v_ln2_b, v_fox_w_in=v_fox_w_in, v_fox_b_f=v_fox_b_f, v_fox_w_o=v_fox_w_o, v_swa_w_in=v_swa_w_in, v_swa_sinks=v_swa_sinks, v_swa_w_o=v_swa_w_o, v_mla_w_a=v_mla_w_a, v_mla_g_q=v_mla_g_q, v_mla_g_kv=v_mla_g_kv, v_mla_w_uq=v_mla_w_uq, v_mla_w_ukv=v_mla_w_ukv, v_mla_w_o=v_mla_w_o, v_ffn_w_in=v_ffn_w_in, v_ffn_conv_w=v_ffn_conv_w, v_ffn_conv_b=v_ffn_conv_b, v_ffn_w_out=v_ffn_w_out)
    weights = {n: given[n] for n in TWIN_WEIGHTS}
    shared = {n: given[n] for n in SHARED_INPUTS}
    per_example = {n: given[n] for n in ['x']}
    grad_fn = _jax.value_and_grad(_loss, argnums=(0, 1))

    def one_microbatch(ex, loss_target):
        ex = dict(ex)
        diff = ex.pop(TWIN_DIFF_INPUT)
        return grad_fn(weights, diff, {**shared, **ex}, loss_target)

    if N_MICROBATCH == 1:
        loss, (grad_w, grad_x) = one_microbatch(per_example, given["loss_target"])
    else:
        def body(carry, xs):
            loss_sum, grad_sum = carry
            l_k, (gw_k, gx_k) = one_microbatch(xs[0], xs[1])
            with _jax.named_scope("update"):
                return (loss_sum + l_k, _jax.tree.map(_jnp.add, grad_sum, gw_k)), gx_k

        init = (_jnp.zeros((), _jnp.float32), _jax.tree.map(_jnp.zeros_like, weights))
        (loss, grad_w), grad_x = _jax.lax.scan(body, init, (per_example, given["loss_target"]))
    with _jax.named_scope("update"):
        delta_w, new_m, new_v = {}, {}, {}
        for n in TWIN_WEIGHTS:
            delta_w[n], new_m[n], new_v[n] = _adamw(weights[n], grad_w[n], given["m_" + n], given["v_" + n])
    return (loss, grad_x, *[grad_w[n] for n in TWIN_WEIGHTS], *[delta_w[n] for n in TWIN_WEIGHTS],
            *[new_m[n] for n in TWIN_WEIGHTS], *[new_v[n] for n in TWIN_WEIGHTS])
```

```python
import functools
import math

import jax
import jax.numpy as jnp
from jax import lax
from jax.experimental import pallas as pl
from jax.experimental.pallas import tpu as pltpu

F32 = jnp.float32
BF16 = jnp.bfloat16

D_MODEL = 1024
DEPTH = 4
BLOCK = 128
N_META = 16
PAD = BLOCK - N_META
NEG = -1e30
ALPHA = (2.0 * DEPTH) ** 0.25
LN_EPS = 1e-5
RMS_EPS = 1e-6
HEADS = 16
HEAD_DIM = 64
LANES = 128
SWA_KV = 2
SWA_G = HEADS // SWA_KV
WINDOW = 128
ROPE_THETA = 500000.0
ROPE_DIM = 16
MLA_Q_LORA = 384
MLA_KV_LORA = 256
MLA_NOPE = 64
MLA_ROPE = 32
MLA_ROPE_THETA = 10000.0
D_FF = 2816
ADAM_LR = 0.001
ADAM_B1 = 0.9
ADAM_B2 = 0.999
ADAM_EPS = 1e-08
ADAM_WD = 0.01
ADAM_STEP = 10
N_CHIPS = 4
N_DEV = 8
ROW = 1024
VMEM_LIMIT = 48 * 1024 * 1024
MESH = pl.DeviceIdType.MESH

NN = (((1,), (0,)), ((), ()))
NT = (((1,), (1,)), ((), ()))
TN = (((0,), (0,)), ((), ()))


def _pick(n, cands):
    for c in cands:
        if n % c == 0:
            return c
    return n


def _params(*sem):
    return pltpu.CompilerParams(dimension_semantics=sem, vmem_limit_bytes=VMEM_LIMIT)


def _mm(a, b, mode, name):
    if mode == "nn":
        (M, K), (_, N) = a.shape, b.shape
    elif mode == "nt":
        (M, K), (N, _) = a.shape, b.shape
    else:
        (K, M), (_, N) = a.shape, b.shape
    tm = _pick(M, (640, 512, 384, 256, 128))
    tn = _pick(N, (640, 512, 384, 256, 128))
    tk = _pick(K, (640, 512, 384, 256, 128))
    nk = K // tk
    dn = {"nn": NN, "nt": NT, "tn": TN}[mode]

    def body(a_ref, b_ref, o_ref, acc_ref):
        k = pl.program_id(2)

        @pl.when(k == 0)
        def _():
            acc_ref[...] = jnp.zeros_like(acc_ref)

        acc_ref[...] += lax.dot_general(a_ref[...].astype(BF16), b_ref[...].astype(BF16), dn,
                                        preferred_element_type=F32)

        @pl.when(k == nk - 1)
        def _():
            o_ref[...] = acc_ref[...]

    if mode == "tn":
        a_spec = pl.BlockSpec((tk, tm), lambda i, j, k: (k, i))
    else:
        a_spec = pl.BlockSpec((tm, tk), lambda i, j, k: (i, k))
    if mode == "nt":
        b_spec = pl.BlockSpec((tn, tk), lambda i, j, k: (j, k))
    else:
        b_spec = pl.BlockSpec((tk, tn), lambda i, j, k: (k, j))
    return pl.pallas_call(
        body, name=name, grid=(M // tm, N // tn, nk),
        in_specs=[a_spec, b_spec],
        out_specs=pl.BlockSpec((tm, tn), lambda i, j, k: (i, j)),
        out_shape=jax.ShapeDtypeStruct((M, N), F32),
        scratch_shapes=[pltpu.VMEM((tm, tn), F32)],
        compiler_params=_params("parallel", "parallel", "arbitrary"),
    )(a, b)


def _ln_fwd(h, mix, g, b, name):
    L = h.shape[0]
    tr = 128

    def body(h_ref, m_ref, g_ref, b_ref, o_ref, xh_ref, rs_ref):
        z = ALPHA * h_ref[...] + m_ref[...]
        mu = jnp.mean(z, axis=1, keepdims=True)
        zc = z - mu
        var = jnp.mean(zc * zc, axis=1, keepdims=True)
        rstd = lax.rsqrt(var + LN_EPS)
        xh = zc * rstd
        xh_ref[...] = xh
        rs_ref[...] = rstd
        o_ref[...] = xh * g_ref[...] + b_ref[...]

    row = pl.BlockSpec((tr, D_MODEL), lambda i: (i, 0))
    vec = pl.BlockSpec((1, D_MODEL), lambda i: (0, 0))
    return pl.pallas_call(
        body, name=name, grid=(L // tr,),
        in_specs=[row, row, vec, vec],
        out_specs=[row, row, pl.BlockSpec((tr, 1), lambda i: (i, 0))],
        out_shape=[jax.ShapeDtypeStruct((L, D_MODEL), F32), jax.ShapeDtypeStruct((L, D_MODEL), F32),
                   jax.ShapeDtypeStruct((L, 1), F32)],
        compiler_params=_params("parallel"),
    )(h, mix, g.reshape(1, D_MODEL), b.reshape(1, D_MODEL))


def _ln_bwd(ga, gb, xhat, rstd, g, name):
    L = xhat.shape[0]
    tr = 128
    two = ga is not None

    def body(*refs):
        if two:
            ga_ref, gb_ref, xh_ref, rs_ref, g_ref, dz_ref, dg_ref, db_ref = refs
            dy = ALPHA * ga_ref[...] + gb_ref[...]
        else:
            gb_ref, xh_ref, rs_ref, g_ref, dz_ref, dg_ref, db_ref = refs
            dy = gb_ref[...]
        xh = xh_ref[...]
        dxh = dy * g_ref[...]
        c1 = jnp.mean(dxh, axis=1, keepdims=True)
        c2 = jnp.mean(dxh * xh, axis=1, keepdims=True)
        dz_ref[...] = rs_ref[...] * (dxh - c1 - xh * c2)

        @pl.when(pl.program_id(0) == 0)
        def _():
            dg_ref[...] = jnp.zeros_like(dg_ref)
            db_ref[...] = jnp.zeros_like(db_ref)

        dg_ref[...] += jnp.sum(dy * xh, axis=0, keepdims=True)
        db_ref[...] += jnp.sum(dy, axis=0, keepdims=True)

    row = pl.BlockSpec((tr, D_MODEL), lambda i: (i, 0))
    vec = pl.BlockSpec((1, D_MODEL), lambda i: (0, 0))
    ins = ([ga] if two else []) + [gb, xhat, rstd, g.reshape(1, D_MODEL)]
    specs = ([row] if two else []) + [row, row, pl.BlockSpec((tr, 1), lambda i: (i, 0)), vec]
    dz, dg, db = pl.pallas_call(
        body, name=name, grid=(L // tr,),
        in_specs=specs, out_specs=[row, vec, vec],
        out_shape=[jax.ShapeDtypeStruct((L, D_MODEL), F32), jax.ShapeDtypeStruct((1, D_MODEL), F32),
                   jax.ShapeDtypeStruct((1, D_MODEL), F32)],
        compiler_params=_params("arbitrary"),
    )(*ins)
    return dz, dg[0], db[0]


def _axpy(a, b, name):
    L, N = a.shape
    tr = 128

    def body(a_ref, b_ref, o_ref):
        o_ref[...] = ALPHA * a_ref[...] + b_ref[...]

    row = pl.BlockSpec((tr, N), lambda i: (i, 0))
    return pl.pallas_call(body, name=name, grid=(L // tr,), in_specs=[row, row], out_specs=row,
                          out_shape=jax.ShapeDtypeStruct((L, N), F32), compiler_params=_params("parallel"))(a, b)


def _shift_down(cur, prev8, n):
    tr = cur.shape[0]
    rows = lax.broadcasted_iota(jnp.int32, cur.shape, 0)
    out = pltpu.roll(cur, n, 0)
    for r in range(n):
        out = jnp.where(rows == r, prev8[8 - n + r:8 - n + r + 1, :], out)
    return out


def _shift_up(cur, next8, n):
    tr = cur.shape[0]
    rows = lax.broadcasted_iota(jnp.int32, cur.shape, 0)
    out = pltpu.roll(cur, tr - n, 0)
    for r in range(n):
        out = jnp.where(rows == tr - n + r, next8[r:r + 1, :], out)
    return out


def _silu(x):
    return x / (1.0 + jnp.exp(-x))


def _conv_glu_fwd(u, cw, cb, name):
    L, F2 = u.shape
    F = F2 // 2
    tr = 128

    def body(u_ref, up_ref, cw_ref, cb_ref, y_ref, a_ref):
        i = pl.program_id(0)
        rows = i * tr + lax.broadcasted_iota(jnp.int32, (tr, F2), 0)
        cur = jnp.where(rows >= PAD, u_ref[...], 0.0)
        prow = i * tr - 8 + lax.broadcasted_iota(jnp.int32, (8, F2), 0)
        prev = jnp.where(prow >= PAD, up_ref[...], 0.0)
        y = cb_ref[...] + _shift_down(cur, prev, 2) * cw_ref[0:1, :]
        y = y + _shift_down(cur, prev, 1) * cw_ref[1:2, :]
        y = y + cur * cw_ref[2:3, :]
        y_ref[...] = y
        a_ref[...] = _silu(y[:, :F]) * y[:, F:]

    return pl.pallas_call(
        body, name=name, grid=(L // tr,),
        in_specs=[pl.BlockSpec((tr, F2), lambda i: (i, 0)),
                  pl.BlockSpec((8, F2), lambda i: (jnp.maximum(i * (tr // 8) - 1, 0), 0)),
                  pl.BlockSpec((3, F2), lambda i: (0, 0)),
                  pl.BlockSpec((1, F2), lambda i: (0, 0))],
        out_specs=[pl.BlockSpec((tr, F2), lambda i: (i, 0)), pl.BlockSpec((tr, F), lambda i: (i, 0))],
        out_shape=[jax.ShapeDtypeStruct((L, F2), F32), jax.ShapeDtypeStruct((L, F), F32)],
        compiler_params=_params("parallel"),
    )(u, u, cw, cb.reshape(1, F2))


def _conv_glu_bwd(y, da, u, cw, name):
    L, F2 = u.shape
    F = F2 // 2
    tr = 128
    nb = L // tr

    def dy_of(yv, dav):
        g, val = yv[:, :F], yv[:, F:]
        sg = 1.0 / (1.0 + jnp.exp(-g))
        dg = dav * val * (sg * (1.0 + g * (1.0 - sg)))
        dv = dav * (g * sg)
        return jnp.concatenate([dg, dv], axis=1)

    def body(y_ref, yn_ref, da_ref, dan_ref, u_ref, up_ref, cw_ref, du_ref, dcw_ref, dcb_ref):
        i = pl.program_id(0)
        rows = i * tr + lax.broadcasted_iota(jnp.int32, (tr, F2), 0)
        dy = dy_of(y_ref[...], da_ref[...])
        dyn = jnp.where(i < nb - 1, dy_of(yn_ref[...], dan_ref[...]), 0.0)
        du = dy * cw_ref[2:3, :] + _shift_up(dy, dyn, 1) * cw_ref[1:2, :] + _shift_up(dy, dyn, 2) * cw_ref[0:1, :]
        du_ref[...] = jnp.where(rows >= PAD, du, 0.0)
        cur = jnp.where(rows >= PAD, u_ref[...], 0.0)
        prow = i * tr - 8 + lax.broadcasted_iota(jnp.int32, (8, F2), 0)
        prev = jnp.where(prow >= PAD, up_ref[...], 0.0)

        @pl.when(i == 0)
        def _():
            dcw_ref[...] = jnp.zeros_like(dcw_ref)
            dcb_ref[...] = jnp.zeros_like(dcb_ref)

        dcw_ref[0:1, :] += jnp.sum(dy * _shift_down(cur, prev, 2), axis=0, keepdims=True)
        dcw_ref[1:2, :] += jnp.sum(dy * _shift_down(cur, prev, 1), axis=0, keepdims=True)
        dcw_ref[2:3, :] += jnp.sum(dy * cur, axis=0, keepdims=True)
        dcb_ref[...] += jnp.sum(dy, axis=0, keepdims=True)

    nxt = lambda i: (jnp.minimum((i + 1) * (tr // 8), L // 8 - 1), 0)
    prv = lambda i: (jnp.maximum(i * (tr // 8) - 1, 0), 0)
    du, dcw, dcb = pl.pallas_call(
        body, name=name, grid=(nb,),
        in_specs=[pl.BlockSpec((tr, F2), lambda i: (i, 0)), pl.BlockSpec((8, F2), nxt),
                  pl.BlockSpec((tr, F), lambda i: (i, 0)), pl.BlockSpec((8, F), nxt),
                  pl.BlockSpec((tr, F2), lambda i: (i, 0)), pl.BlockSpec((8, F2), prv),
                  pl.BlockSpec((3, F2), lambda i: (0, 0))],
        out_specs=[pl.BlockSpec((tr, F2), lambda i: (i, 0)), pl.BlockSpec((3, F2), lambda i: (0, 0)),
                   pl.BlockSpec((1, F2), lambda i: (0, 0))],
        out_shape=[jax.ShapeDtypeStruct((L, F2), F32), jax.ShapeDtypeStruct((3, F2), F32),
                   jax.ShapeDtypeStruct((1, F2), F32)],
        compiler_params=_params("arbitrary"),
    )(y, y, da, da, u, u, cw)
    return du, dcw, dcb[0]


class _DenseCfg:
    def __init__(self, L):
        self.T = 640 if L % 640 == 0 else 128
        self.nb = L // self.T
        self.nsteps = self.nb
        nb = self.nb
        self.passes = [dict(nkb=nb, nsteps=nb, qblock=lambda kb, j: jnp.maximum(j, kb),
                            active=lambda kb, j: j >= kb, mask=self.mask_kv)]

    def kblock(self, qb, j):
        return jnp.minimum(j, qb)

    def active(self, qb, j):
        return j <= qb

    def mask(self, qpos, kpos, j):
        return (kpos <= qpos) & (kpos >= PAD)

    def mask_kv(self, qpos, kpos):
        return (kpos <= qpos) & (kpos >= PAD)


class _SwaCfg:
    def __init__(self, L):
        self.T = BLOCK
        self.nb = L // BLOCK
        self.nsteps = 3
        nb = self.nb
        self.passes = [
            dict(nkb=nb, nsteps=2, qblock=lambda kb, j: jnp.minimum(kb + j, nb - 1),
                 active=lambda kb, j: kb + j <= nb - 1, mask=self.band),
            dict(nkb=1, nsteps=nb, qblock=lambda kb, j: j, active=None, mask=self.meta),
        ]

    def kblock(self, qb, j):
        return jnp.where(j == 0, 0, jnp.where(j == 1, jnp.maximum(qb - 1, 0), qb))

    active = None

    @staticmethod
    def band(qpos, kpos):
        d = qpos - kpos
        return (kpos >= BLOCK) & (d >= 0) & (d < WINDOW)

    @staticmethod
    def meta(qpos, kpos):
        return (kpos >= PAD) & (kpos < BLOCK) & (kpos <= qpos)

    def mask(self, qpos, kpos, j):
        return (self.meta(qpos, kpos) & (j == 0)) | (self.band(qpos, kpos) & (j > 0))


def _positions(T, qb, kb):
    qpos = qb * T + lax.broadcasted_iota(jnp.int32, (T, T), 0)
    kpos = kb * T + lax.broadcasted_iota(jnp.int32, (T, T), 1)
    return qpos, kpos


def _when(cond, fn):
    if cond is None:
        fn()
    else:
        pl.when(cond)(fn)


def _attention_fwd(qa, q_off, ka, k_off, va, v_off, H, G, cfg, scale, name, cb=None, cr=None, sink=None):
    L = qa.shape[0]
    T, nq, ns = cfg.T, cfg.nb, cfg.nsteps
    decay, has_sink = cb is not None, sink is not None

    def body(*refs):
        it = iter(refs)
        q_ref, k_ref, v_ref = next(it), next(it), next(it)
        cb_ref = next(it) if decay else None
        cr_ref = next(it) if decay else None
        sink_ref = next(it) if has_sink else None
        o_ref, lse_ref, m_sc, l_sc, acc_sc = next(it), next(it), next(it), next(it), next(it)
        qb, j = pl.program_id(1), pl.program_id(2)

        @pl.when(j == 0)
        def _():
            if has_sink:
                m_sc[...] = jnp.broadcast_to(sink_ref[:, 0:1], (T, 1))
                l_sc[...] = jnp.ones((T, 1), F32)
            else:
                m_sc[...] = jnp.full((T, 1), NEG, F32)
                l_sc[...] = jnp.zeros((T, 1), F32)
            acc_sc[...] = jnp.zeros_like(acc_sc)

        def step():
            kb = cfg.kblock(qb, j)
            s = lax.dot_general(q_ref[...].astype(BF16), k_ref[...].astype(BF16), NT,
                                preferred_element_type=F32) * scale
            if decay:
                s = s + cb_ref[:, 0:1] - cr_ref[0]
            qpos, kpos = _positions(T, qb, kb)
            s = jnp.where(cfg.mask(qpos, kpos, j), s, NEG)
            m_prev = m_sc[...]
            m_new = jnp.maximum(m_prev, jnp.max(s, axis=1, keepdims=True))
            alpha = jnp.exp(m_prev - m_new)
            p = jnp.exp(s - m_new)
            l_sc[...] = alpha * l_sc[...] + jnp.sum(p, axis=1, keepdims=True)
            acc_sc[...] = alpha * acc_sc[...] + lax.dot_general(
                p.astype(BF16), v_ref[...].astype(BF16), NN, preferred_element_type=F32)
            m_sc[...] = m_new

        _when(None if cfg.active is None else cfg.active(qb, j), step)

        @pl.when(j == ns - 1)
        def _():
            o_ref[...] = acc_sc[...] / l_sc[...]
            lse_ref[...] = jnp.broadcast_to(m_sc[...] + jnp.log(l_sc[...]), (T, LANES))

    qspec = pl.BlockSpec((T, LANES), lambda h, qb, j: (qb, q_off + h))
    kspec = pl.BlockSpec((T, LANES), lambda h, qb, j: (cfg.kblock(qb, j), k_off + h // G))
    vspec = pl.BlockSpec((T, LANES), lambda h, qb, j: (cfg.kblock(qb, j), v_off + h // G))
    ins, specs = [qa, ka, va], [qspec, kspec, vspec]
    if decay:
        ins += [cb, cr]
        specs += [pl.BlockSpec((T, LANES), lambda h, qb, j: (qb, h)),
                  pl.BlockSpec((1, 1, T), lambda h, qb, j: (h, 0, cfg.kblock(qb, j)))]
    if has_sink:
        ins += [sink]
        specs += [pl.BlockSpec((1, LANES), lambda h, qb, j: (0, h))]
    ospec = pl.BlockSpec((T, LANES), lambda h, qb, j: (qb, h))
    return pl.pallas_call(
        body, name=name, grid=(H, nq, ns), in_specs=specs, out_specs=[ospec, ospec],
        out_shape=[jax.ShapeDtypeStruct((L, H * LANES), F32), jax.ShapeDtypeStruct((L, H * LANES), F32)],
        scratch_shapes=[pltpu.VMEM((T, 1), F32), pltpu.VMEM((T, 1), F32), pltpu.VMEM((T, LANES), F32)],
        compiler_params=_params("parallel", "parallel", "arbitrary"),
    )(*ins)


def _attention_dq(qa, q_off, ka, k_off, va, v_off, do, lse, delta, H, G, cfg, scale, name, cb=None, cr=None):
    L = qa.shape[0]
    T, nq, ns = cfg.T, cfg.nb, cfg.nsteps
    decay = cb is not None

    def body(*refs):
        it = iter(refs)
        q_ref, k_ref, v_ref, do_ref, lse_ref, dl_ref = (next(it) for _ in range(6))
        cb_ref = next(it) if decay else None
        cr_ref = next(it) if decay else None
        dq_ref = next(it)
        dc_ref = next(it) if decay else None
        acc_sc = next(it)
        dc_sc = next(it) if decay else None
        qb, j = pl.program_id(1), pl.program_id(2)

        @pl.when(j == 0)
        def _():
            acc_sc[...] = jnp.zeros_like(acc_sc)
            if decay:
                dc_sc[...] = jnp.zeros_like(dc_sc)

        def step():
            kb = cfg.kblock(qb, j)
            k = k_ref[...].astype(BF16)
            s = lax.dot_general(q_ref[...].astype(BF16), k, NT, preferred_element_type=F32) * scale
            if decay:
                s = s + cb_ref[:, 0:1] - cr_ref[0]
            qpos, kpos = _positions(T, qb, kb)
            live = cfg.mask(qpos, kpos, j) & (qpos >= PAD)
            p = jnp.where(live, jnp.exp(jnp.where(live, s, NEG) - lse_ref[:, 0:1]), 0.0)
            dp = lax.dot_general(do_ref[...].astype(BF16), v_ref[...].astype(BF16), NT,
                                 preferred_element_type=F32)
            ds = p * (dp - dl_ref[:, 0:1])
            acc_sc[...] += lax.dot_general(ds.astype(BF16), k, NN, preferred_element_type=F32)
            if decay:
                dc_sc[...] += jnp.sum(ds, axis=1, keepdims=True)

        _when(None if cfg.active is None else cfg.active(qb, j), step)

        @pl.when(j == ns - 1)
        def _():
            dq_ref[...] = acc_sc[...] * scale
            if decay:
                dc_ref[...] = jnp.broadcast_to(dc_sc[...], (T, LANES))

    qrow = lambda off: pl.BlockSpec((T, LANES), lambda h, qb, j: (qb, off + h))
    ins = [qa, ka, va, do, lse, delta]
    specs = [qrow(q_off),
             pl.BlockSpec((T, LANES), lambda h, qb, j: (cfg.kblock(qb, j), k_off + h // G)),
             pl.BlockSpec((T, LANES), lambda h, qb, j: (cfg.kblock(qb, j), v_off + h // G)),
             qrow(0), qrow(0), qrow(0)]
    if decay:
        ins += [cb, cr]
        specs += [qrow(0), pl.BlockSpec((1, 1, T), lambda h, qb, j: (h, 0, cfg.kblock(qb, j)))]
    shape = jax.ShapeDtypeStruct((L, H * LANES), F32)
    return pl.pallas_call(
        body, name=name, grid=(H, nq, ns), in_specs=specs,
        out_specs=[qrow(0)] * (2 if decay else 1), out_shape=[shape] * (2 if decay else 1),
        scratch_shapes=[pltpu.VMEM((T, LANES), F32)] + ([pltpu.VMEM((T, 1), F32)] if decay else []),
        compiler_params=_params("parallel", "parallel", "arbitrary"),
    )(*ins)


def _attention_delta(qa, q_off, ka, k_off, va, v_off, do, lse, H, G, cfg, scale, name, cb=None, cr=None):
    L = qa.shape[0]
    T, nq, ns = cfg.T, cfg.nb, cfg.nsteps
    decay = cb is not None

    def body(*refs):
        it = iter(refs)
        q_ref, k_ref, v_ref, do_ref, lse_ref = (next(it) for _ in range(5))
        cb_ref = next(it) if decay else None
        cr_ref = next(it) if decay else None
        dl_ref, acc_sc = next(it), next(it)
        qb, j = pl.program_id(1), pl.program_id(2)

        @pl.when(j == 0)
        def _():
            acc_sc[...] = jnp.zeros_like(acc_sc)

        def step():
            kb = cfg.kblock(qb, j)
            s = lax.dot_general(q_ref[...].astype(BF16), k_ref[...].astype(BF16), NT,
                                preferred_element_type=F32) * scale
            if decay:
                s = s + cb_ref[:, 0:1] - cr_ref[0]
            qpos, kpos = _positions(T, qb, kb)
            live = cfg.mask(qpos, kpos, j) & (qpos >= PAD)
            p = jnp.where(live, jnp.exp(jnp.where(live, s, NEG) - lse_ref[:, 0:1]), 0.0)
            dp = lax.dot_general(do_ref[...].astype(BF16), v_ref[...].astype(BF16), NT,
                                 preferred_element_type=F32)
            acc_sc[...] += jnp.sum(p * dp, axis=1, keepdims=True)

        _when(None if cfg.active is None else cfg.active(qb, j), step)

        @pl.when(j == ns - 1)
        def _():
            dl_ref[...] = jnp.broadcast_to(acc_sc[...], (T, LANES))

    qrow = lambda off: pl.BlockSpec((T, LANES), lambda h, qb, j: (qb, off + h))
    ins = [qa, ka, va, do, lse]
    specs = [qrow(q_off),
             pl.BlockSpec((T, LANES), lambda h, qb, j: (cfg.kblock(qb, j), k_off + h // G)),
             pl.BlockSpec((T, LANES), lambda h, qb, j: (cfg.kblock(qb, j), v_off + h // G)),
             qrow(0), qrow(0)]
    if decay:
        ins += [cb, cr]
        specs += [qrow(0), pl.BlockSpec((1, 1, T), lambda h, qb, j: (h, 0, cfg.kblock(qb, j)))]
    return pl.pallas_call(
        body, name=name, grid=(H, nq, ns), in_specs=specs, out_specs=qrow(0),
        out_shape=jax.ShapeDtypeStruct((L, H * LANES), F32),
        scratch_shapes=[pltpu.VMEM((T, 1), F32)],
        compiler_params=_params("parallel", "parallel", "arbitrary"),
    )(*ins)


def _attention_dkv(qa, q_off, ka, k_off, va, v_off, do, lse, delta, HKV, G, cfg, pss, scale, name, cb=None, cr=None):
    T = cfg.T
    nkb, ns, qblock, active, mask = pss["nkb"], pss["nsteps"], pss["qblock"], pss["active"], pss["mask"]
    decay = cb is not None

    def body(*refs):
        it = iter(refs)
        q_ref, k_ref, v_ref, do_ref, lse_ref, dl_ref = (next(it) for _ in range(6))
        cb_ref = next(it) if decay else None
        cr_ref = next(it) if decay else None
        dk_ref, dv_ref = next(it), next(it)
        dc_ref = next(it) if decay else None
        dk_sc, dv_sc = next(it), next(it)
        dc_sc = next(it) if decay else None
        kb, g, j = pl.program_id(1), pl.program_id(2), pl.program_id(3)

        @pl.when((g == 0) & (j == 0))
        def _():
            dk_sc[...] = jnp.zeros_like(dk_sc)
            dv_sc[...] = jnp.zeros_like(dv_sc)
            if decay:
                dc_sc[...] = jnp.zeros_like(dc_sc)

        def step():
            qb = qblock(kb, j)
            q = q_ref[...].astype(BF16)
            dob = do_ref[...].astype(BF16)
            s = lax.dot_general(q, k_ref[...].astype(BF16), NT, preferred_element_type=F32) * scale
            if decay:
                s = s + cb_ref[:, 0:1] - cr_ref[0]
            qpos, kpos = _positions(T, qb, kb)
            live = mask(qpos, kpos) & (qpos >= PAD)
            p = jnp.where(live, jnp.exp(jnp.where(live, s, NEG) - lse_ref[:, 0:1]), 0.0)
            dv_sc[...] += lax.dot_general(p.astype(BF16), dob, TN, preferred_element_type=F32)
            dp = lax.dot_general(dob, v_ref[...].astype(BF16), NT, preferred_element_type=F32)
            ds = p * (dp - dl_ref[:, 0:1])
            dk_sc[...] += lax.dot_general(ds.astype(BF16), q, TN, preferred_element_type=F32)
            if decay:
                dc_sc[...] += jnp.sum(ds, axis=0, keepdims=True)

        _when(None if active is None else active(kb, j), step)

        @pl.when((g == G - 1) & (j == ns - 1))
        def _():
            dk_ref[...] = dk_sc[...] * scale
            dv_ref[...] = dv_sc[...]
            if decay:
                dc_ref[0] = -dc_sc[...]

    qrow = lambda off: pl.BlockSpec((T, LANES), lambda hk, kb, g, j: (qblock(kb, j), off + hk * G + g))
    krow = lambda off: pl.BlockSpec((T, LANES), lambda hk, kb, g, j: (kb, off + hk))
    ins = [qa, ka, va, do, lse, delta]
    specs = [qrow(q_off), krow(k_off), krow(v_off), qrow(0), qrow(0), qrow(0)]
    outs = [jax.ShapeDtypeStruct((nkb * T, HKV * LANES), F32)] * 2
    ospecs = [krow(0), krow(0)]
    scratch = [pltpu.VMEM((T, LANES), F32), pltpu.VMEM((T, LANES), F32)]
    if decay:
        ins += [cb, cr]
        specs += [qrow(0), pl.BlockSpec((1, 1, T), lambda hk, kb, g, j: (hk, 0, kb))]
        outs += [jax.ShapeDtypeStruct((HKV, 1, nkb * T), F32)]
        ospecs += [pl.BlockSpec((1, 1, T), lambda hk, kb, g, j: (hk, 0, kb))]
        scratch += [pltpu.VMEM((1, T), F32)]
    return pl.pallas_call(
        body, name=name, grid=(HKV, nkb, G, ns), in_specs=specs, out_specs=ospecs, out_shape=outs,
        scratch_shapes=scratch,
        compiler_params=_params("parallel", "parallel", "arbitrary", "arbitrary"),
    )(*ins)


def _delta(do, o, name):
    L, HW = do.shape
    T = _pick(L, (640, 128))

    def body(do_ref, o_ref, d_ref):
        d_ref[...] = jnp.broadcast_to(jnp.sum(do_ref[...] * o_ref[...], axis=1, keepdims=True), (T, LANES))

    spec = pl.BlockSpec((T, LANES), lambda h, i: (i, h))
    return pl.pallas_call(body, name=name, grid=(HW // LANES, L // T), in_specs=[spec, spec], out_specs=spec,
                          out_shape=jax.ShapeDtypeStruct((L, HW), F32),
                          compiler_params=_params("parallel", "parallel"))(do, o)


def _sink_grad(lse, delta, sink_b, name):
    L, HW = lse.shape
    tr = 128

    def body(lse_ref, dl_ref, s_ref, o_ref):
        @pl.when(pl.program_id(0) == 0)
        def _():
            o_ref[...] = jnp.zeros_like(o_ref)

        o_ref[...] -= jnp.sum(jnp.exp(s_ref[...] - lse_ref[...]) * dl_ref[...], axis=0, keepdims=True)

    row = pl.BlockSpec((tr, HW), lambda i: (i, 0))
    vec = pl.BlockSpec((1, HW), lambda i: (0, 0))
    return pl.pallas_call(body, name=name, grid=(L // tr,), in_specs=[row, row, vec], out_specs=vec,
                          out_shape=jax.ShapeDtypeStruct((1, HW), F32), compiler_params=_params("arbitrary"))(
        lse, delta, sink_b)


def _tri(lower):
    r = lax.broadcasted_iota(jnp.int32, (BLOCK, BLOCK), 0)
    c = lax.broadcasted_iota(jnp.int32, (BLOCK, BLOCK), 1)
    return jnp.where((c <= r) if lower else (c >= r), 1.0, 0.0).astype(F32)


def _gate_cumsum(proj, fg_tile, b_pad, name):
    L = proj.shape[0]

    def body(fg_ref, b_ref, c_ref, carry):
        @pl.when(pl.program_id(0) == 0)
        def _():
            carry[...] = jnp.zeros_like(carry)

        x = fg_ref[...] + b_ref[...]
        lf = jnp.minimum(x, 0.0) - jnp.log(1.0 + jnp.exp(-jnp.abs(x)))
        c = jnp.dot(_tri(True), lf, precision=lax.Precision.HIGHEST, preferred_element_type=F32) + carry[...]
        c_ref[...] = c
        carry[...] = c[BLOCK - 1:BLOCK, :]

    return pl.pallas_call(
        body, name=name, grid=(L // BLOCK,),
        in_specs=[pl.BlockSpec((BLOCK, LANES), lambda i: (i, fg_tile)), pl.BlockSpec((1, LANES), lambda i: (0, 0))],
        out_specs=pl.BlockSpec((BLOCK, LANES), lambda i: (i, 0)),
        out_shape=jax.ShapeDtypeStruct((L, LANES), F32),
        scratch_shapes=[pltpu.VMEM((1, LANES), F32)],
        compiler_params=_params("arbitrary"),
    )(proj, b_pad)


def _gate_cumsum_bwd(dc, proj, fg_tile, b_pad, name):
    L = proj.shape[0]
    nb = L // BLOCK

    def body(dc_ref, fg_ref, b_ref, dfg_ref, db_ref, carry):
        @pl.when(pl.program_id(0) == 0)
        def _():
            carry[...] = jnp.zeros_like(carry)
            db_ref[...] = jnp.zeros_like(db_ref)

        dlf = jnp.dot(_tri(False), dc_ref[...], precision=lax.Precision.HIGHEST,
                      preferred_element_type=F32) + carry[...]
        carry[...] = dlf[0:1, :]
        x = fg_ref[...] + b_ref[...]
        lanes = lax.broadcasted_iota(jnp.int32, (BLOCK, LANES), 1)
        rows = (nb - 1 - pl.program_id(0)) * BLOCK + lax.broadcasted_iota(jnp.int32, (BLOCK, LANES), 0)
        dfg = jnp.where((lanes < HEADS) & (rows >= PAD), dlf / (1.0 + jnp.exp(x)), 0.0)
        dfg_ref[...] = jnp.concatenate([dfg, jnp.zeros_like(dfg)], axis=1)
        db_ref[...] += jnp.sum(dfg, axis=0, keepdims=True)

    dfg, db = pl.pallas_call(
        body, name=name, grid=(nb,),
        in_specs=[pl.BlockSpec((BLOCK, LANES), lambda i: (nb - 1 - i, 0)),
                  pl.BlockSpec((BLOCK, LANES), lambda i: (nb - 1 - i, fg_tile)),
                  pl.BlockSpec((1, LANES), lambda i: (0, 0))],
        out_specs=[pl.BlockSpec((BLOCK, 2 * LANES), lambda i: (nb - 1 - i, 0)),
                   pl.BlockSpec((1, LANES), lambda i: (0, 0))],
        out_shape=[jax.ShapeDtypeStruct((L, 2 * LANES), F32), jax.ShapeDtypeStruct((1, LANES), F32)],
        scratch_shapes=[pltpu.VMEM((1, LANES), F32)],
        compiler_params=_params("arbitrary"),
    )(dc, proj, b_pad)
    return dfg, db[0]


def _rope_tables(L, dim, theta, lane0):
    half = dim // 2
    pos = (jnp.arange(L) - PAD).astype(F32)
    inv = theta ** (-jnp.arange(0, dim, 2, dtype=F32) / dim)
    ang = pos[:, None] * inv[None, :]
    cos, sin = jnp.cos(ang), jnp.sin(ang)
    C = jnp.ones((L, LANES), F32).at[:, lane0:lane0 + half].set(cos).at[:, lane0 + half:lane0 + dim].set(cos)
    S1 = jnp.zeros((L, LANES), F32).at[:, lane0:lane0 + half].set(-sin)
    S2 = jnp.zeros((L, LANES), F32).at[:, lane0 + half:lane0 + dim].set(sin)
    return C, S1, S2


def _rot(x, C, S1, S2, R):
    return x * C + pltpu.roll(x, LANES - R, 1) * S1 + pltpu.roll(x, R, 1) * S2


def _rot_t(dy, C, S1, S2, R):
    return dy * C + pltpu.roll(dy * S1, R, 1) + pltpu.roll(dy * S2, LANES - R, 1)


def _rope(x, nt, tabs, R, name, transpose=False, shared=None, shared_tile=0):
    L = x.shape[0]
    T = _pick(L, (640, 128))
    fn = _rot_t if transpose else _rot

    def body(*refs):
        if shared is None:
            x_ref, c_ref, s1_ref, s2_ref, o_ref = refs
            o_ref[...] = fn(x_ref[...], c_ref[...], s1_ref[...], s2_ref[...], R)
        else:
            x_ref, sh_ref, c_ref, s1_ref, s2_ref, o_ref = refs
            o_ref[...] = x_ref[...] + fn(sh_ref[...], c_ref[...], s1_ref[...], s2_ref[...], R)

    tile = pl.BlockSpec((T, LANES), lambda h, i: (i, h))
    tab = pl.BlockSpec((T, LANES), lambda h, i: (i, 0))
    ins, specs = [x], [tile]
    if shared is not None:
        ins.append(shared)
        specs.append(pl.BlockSpec((T, LANES), lambda h, i: (i, shared_tile)))
    return pl.pallas_call(body, name=name, grid=(nt, L // T), in_specs=specs + [tab, tab, tab], out_specs=tile,
                          out_shape=jax.ShapeDtypeStruct((L, nt * LANES), F32),
                          compiler_params=_params("parallel", "parallel"))(*ins, *tabs)


def _rope_shared_bwd(dk, nt, tabs, R, name):
    L = dk.shape[0]
    tr = 128

    def body(dk_ref, c_ref, s1_ref, s2_ref, o_ref):
        acc = dk_ref[:, 0:LANES]
        for h in range(1, nt):
            acc = acc + dk_ref[:, h * LANES:(h + 1) * LANES]
        o_ref[...] = _rot_t(acc, c_ref[...], s1_ref[...], s2_ref[...], R)

    tab = pl.BlockSpec((tr, LANES), lambda i: (i, 0))
    return pl.pallas_call(body, name=name, grid=(L // tr,),
                          in_specs=[pl.BlockSpec((tr, nt * LANES), lambda i: (i, 0)), tab, tab, tab], out_specs=tab,
                          out_shape=jax.ShapeDtypeStruct((L, LANES), F32), compiler_params=_params("parallel"))(
        dk, *tabs)


def _rms_fwd(pa, gq, gkv, name):
    L = pa.shape[0]
    tr = 128
    Q, KV = MLA_Q_LORA, MLA_KV_LORA

    def body(pa_ref, gq_ref, gkv_ref, q_ref, kv_ref):
        for lo, n, g_ref, o_ref in ((0, Q, gq_ref, q_ref), (Q, KV, gkv_ref, kv_ref)):
            x = pa_ref[:, lo:lo + n]
            r = lax.rsqrt(jnp.mean(x * x, axis=1, keepdims=True) + RMS_EPS)
            o_ref[...] = x * r * g_ref[...]

    return pl.pallas_call(
        body, name=name, grid=(L // tr,),
        in_specs=[pl.BlockSpec((tr, pa.shape[1]), lambda i: (i, 0)), pl.BlockSpec((1, Q), lambda i: (0, 0)),
                  pl.BlockSpec((1, KV), lambda i: (0, 0))],
        out_specs=[pl.BlockSpec((tr, Q), lambda i: (i, 0)), pl.BlockSpec((tr, KV), lambda i: (i, 0))],
        out_shape=[jax.ShapeDtypeStruct((L, Q), F32), jax.ShapeDtypeStruct((L, KV), F32)],
        compiler_params=_params("parallel"),
    )(pa, gq.reshape(1, Q), gkv.reshape(1, KV))


def _rms_bwd(pa, dq, dkv, dkr, gq, gkv, name):
    L, W = pa.shape
    tr = 128
    Q, KV = MLA_Q_LORA, MLA_KV_LORA

    def body(pa_ref, dq_ref, dkv_ref, dkr_ref, gq_ref, gkv_ref, dpa_ref, dgq_ref, dgkv_ref):
        @pl.when(pl.program_id(0) == 0)
        def _():
            dgq_ref[...] = jnp.zeros_like(dgq_ref)
            dgkv_ref[...] = jnp.zeros_like(dgkv_ref)

        for lo, n, g_ref, dy_ref, dg_ref in ((0, Q, gq_ref, dq_ref, dgq_ref), (Q, KV, gkv_ref, dkv_ref, dgkv_ref)):
            x = pa_ref[:, lo:lo + n]
            r = lax.rsqrt(jnp.mean(x * x, axis=1, keepdims=True) + RMS_EPS)
            xh = x * r
            dy = dy_ref[...]
            dxh = dy * g_ref[...]
            dpa_ref[:, lo:lo + n] = r * (dxh - xh * jnp.mean(dxh * xh, axis=1, keepdims=True))
            dg_ref[...] += jnp.sum(dy * xh, axis=0, keepdims=True)
        dpa_ref[:, Q + KV:W] = dkr_ref[...]

    vq = pl.BlockSpec((1, Q), lambda i: (0, 0))
    vkv = pl.BlockSpec((1, KV), lambda i: (0, 0))
    dpa, dgq, dgkv = pl.pallas_call(
        body, name=name, grid=(L // tr,),
        in_specs=[pl.BlockSpec((tr, W), lambda i: (i, 0)), pl.BlockSpec((tr, Q), lambda i: (i, 0)),
                  pl.BlockSpec((tr, KV), lambda i: (i, 0)), pl.BlockSpec((tr, LANES), lambda i: (i, 0)), vq, vkv],
        out_specs=[pl.BlockSpec((tr, W), lambda i: (i, 0)), vq, vkv],
        out_shape=[jax.ShapeDtypeStruct((L, W), F32), jax.ShapeDtypeStruct((1, Q), F32),
                   jax.ShapeDtypeStruct((1, KV), F32)],
        compiler_params=_params("arbitrary"),
    )(pa, dq, dkv, dkr, gq.reshape(1, Q), gkv.reshape(1, KV))
    return dpa, dgq[0], dgkv[0]


def _loss_head(h, target, name):
    L = h.shape[0]
    tr = BLOCK
    inv = 1.0 / D_MODEL

    def body(h_ref, t_ref, loss_ref, dh_ref):
        i = pl.program_id(0)

        @pl.when(i == 0)
        def _():
            loss_ref[...] = jnp.zeros_like(loss_ref)
            dh_ref[...] = jnp.zeros_like(dh_ref)

        @pl.when(i > 0)
        def _():
            e = h_ref[...] - t_ref[...]
            dh_ref[...] = e * inv
            loss_ref[...] += jnp.sum((e * e).reshape(tr // 8, 8, D_MODEL), axis=0) * (0.5 * inv)

    row = pl.BlockSpec((tr, D_MODEL), lambda i: (i, 0))
    loss, dh = pl.pallas_call(
        body, name=name, grid=(L // tr,),
        in_specs=[row, pl.BlockSpec((tr, D_MODEL), lambda i: (jnp.maximum(i - 1, 0), 0))],
        out_specs=[pl.BlockSpec((8, D_MODEL), lambda i: (0, 0)), row],
        out_shape=[jax.ShapeDtypeStruct((8, D_MODEL), F32), jax.ShapeDtypeStruct((L, D_MODEL), F32)],
        compiler_params=_params("arbitrary"),
    )(h, target)
    return loss, dh


def _pad_heads_cols(w, nh, d, dp=LANES):
    K = w.shape[0]
    return jnp.pad(w.reshape(K, nh, d), ((0, 0), (0, 0), (0, dp - d))).reshape(K, nh * dp)


def _unpad_heads_cols(w, nh, d, dp=LANES):
    K = w.shape[0]
    return w.reshape(K, nh, dp)[:, :, :d].reshape(K, nh * d)


def _pad_heads_rows(w, nh, d):
    N = w.shape[1]
    return jnp.pad(w.reshape(nh, d, N), ((0, 0), (0, LANES - d), (0, 0))).reshape(nh * LANES, N)


def _unpad_heads_rows(w, nh, d):
    N = w.shape[1]
    return w.reshape(nh, LANES, N)[:, :d, :].reshape(nh * d, N)


def _fox_fwd(h, w_in, b_f, w_o, tag):
    L = h.shape[0]
    hd = HEADS * HEAD_DIM
    W = jnp.concatenate([_pad_heads_cols(w_in[:, i * hd:(i + 1) * hd], HEADS, HEAD_DIM) for i in range(3)]
                        + [jnp.pad(w_in[:, 3 * hd:], ((0, 0), (0, 2 * LANES - HEADS)))], axis=1)
    Wo = _pad_heads_rows(w_o, HEADS, HEAD_DIM)
    b_pad = jnp.pad(b_f, (0, LANES - HEADS)).reshape(1, LANES)
    proj = _mm(h, W, "nn", tag + "_proj")
    c = _gate_cumsum(proj, 3 * HEADS, b_pad, tag + "_cumsum")
    cb = jnp.repeat(c[:, :HEADS], LANES, axis=1)
    cr = c[:, :HEADS].T.reshape(HEADS, 1, L)
    cfg = _DenseCfg(L)
    scale = HEAD_DIM ** -0.5
    o, lse = _attention_fwd(proj, 0, proj, HEADS, proj, 2 * HEADS, HEADS, 1, cfg, scale, tag + "_attn", cb=cb, cr=cr)
    mix = _mm(o, Wo, "nn", tag + "_out")
    return mix, (h, W, Wo, b_pad, proj, cb, cr, o, lse)


def _fox_bwd(dmix, res, tag):
    h, W, Wo, b_pad, proj, cb, cr, o, lse = res
    L = h.shape[0]
    hd = HEADS * HEAD_DIM
    cfg = _DenseCfg(L)
    scale = HEAD_DIM ** -0.5
    dWo = _mm(o, dmix, "tn", tag + "_dwo")
    do = _mm(dmix, Wo, "nt", tag + "_do")
    delta = _attention_delta(proj, 0, proj, HEADS, proj, 2 * HEADS, do, lse, HEADS, 1, cfg, scale, tag + "_delta",
                             cb=cb, cr=cr)
    args = (proj, 0, proj, HEADS, proj, 2 * HEADS, do, lse, delta)
    dq, dcq = _attention_dq(*args, HEADS, 1, cfg, scale, tag + "_dq", cb=cb, cr=cr)
    dk, dv, dcr = _attention_dkv(*args, HEADS, 1, cfg, cfg.passes[0], scale, tag + "_dkv", cb=cb, cr=cr)
    dc = jnp.pad(dcq[:, ::LANES] + dcr.reshape(HEADS, L).T, ((0, 0), (0, LANES - HEADS)))
    dfg, db = _gate_cumsum_bwd(dc, proj, 3 * HEADS, b_pad, tag + "_cumsum_bwd")
    dproj = jnp.concatenate([dq, dk, dv, dfg], axis=1)
    dW = _mm(h, dproj, "tn", tag + "_dw")
    dh = _mm(dproj, W, "nt", tag + "_dh")
    hp = HEADS * LANES
    dw_in = jnp.concatenate([_unpad_heads_cols(dW[:, i * hp:(i + 1) * hp], HEADS, HEAD_DIM) for i in range(3)]
                            + [dW[:, 3 * hp:3 * hp + HEADS]], axis=1)
    return dh, dict(w_in=dw_in, b_f=db[:HEADS], w_o=_unpad_heads_rows(dWo, HEADS, HEAD_DIM))


def _swa_fwd(h, w_in, sinks, w_o, tag):
    L = h.shape[0]
    qd, kd = HEADS * HEAD_DIM, SWA_KV * HEAD_DIM
    W = jnp.concatenate([_pad_heads_cols(w_in[:, :qd], HEADS, HEAD_DIM),
                         _pad_heads_cols(w_in[:, qd:qd + kd], SWA_KV, HEAD_DIM),
                         _pad_heads_cols(w_in[:, qd + kd:], SWA_KV, HEAD_DIM)], axis=1)
    Wo = _pad_heads_rows(w_o, HEADS, HEAD_DIM)
    sink_b = jnp.repeat(sinks, LANES).reshape(1, HEADS * LANES)
    tabs = _rope_tables(L, ROPE_DIM, ROPE_THETA, 0)
    proj = _mm(h, W, "nn", tag + "_proj")
    nqk = HEADS + SWA_KV
    qk = _rope(proj, nqk, tabs, ROPE_DIM // 2, tag + "_rope")
    cfg = _SwaCfg(L)
    scale = HEAD_DIM ** -0.5
    o, lse = _attention_fwd(qk, 0, qk, HEADS, proj, nqk, HEADS, SWA_G, cfg, scale, tag + "_attn", sink=sink_b)
    mix = _mm(o, Wo, "nn", tag + "_out")
    return mix, (h, W, Wo, sink_b, tabs, proj, qk, o, lse)


def _swa_bwd(dmix, res, tag):
    h, W, Wo, sink_b, tabs, proj, qk, o, lse = res
    L = h.shape[0]
    qd, kd = HEADS * HEAD_DIM, SWA_KV * HEAD_DIM
    nqk = HEADS + SWA_KV
    cfg = _SwaCfg(L)
    scale = HEAD_DIM ** -0.5
    dWo = _mm(o, dmix, "tn", tag + "_dwo")
    do = _mm(dmix, Wo, "nt", tag + "_do")
    delta = _delta(do, o, tag + "_delta")
    dsink = _sink_grad(lse, delta, sink_b, tag + "_dsink")[0, ::LANES]
    args = (qk, 0, qk, HEADS, proj, nqk, do, lse, delta)
    dq, = _attention_dq(*args, HEADS, SWA_G, cfg, scale, tag + "_dq")
    dkb, dvb = _attention_dkv(*args, SWA_KV, SWA_G, cfg, cfg.passes[0], scale, tag + "_dkv_band")
    dkm, dvm = _attention_dkv(*args, SWA_KV, SWA_G, cfg, cfg.passes[1], scale, tag + "_dkv_meta")
    dk = jnp.concatenate([dkm, dkb[BLOCK:]], axis=0)
    dv = jnp.concatenate([dvm, dvb[BLOCK:]], axis=0)
    dqk = _rope(jnp.concatenate([dq, dk], axis=1), nqk, tabs, ROPE_DIM // 2, tag + "_rope_bwd", transpose=True)
    dproj = jnp.concatenate([dqk, dv], axis=1)
    dW = _mm(h, dproj, "tn", tag + "_dw")
    dh = _mm(dproj, W, "nt", tag + "_dh")
    hp = HEADS * LANES
    dw_in = jnp.concatenate([_unpad_heads_cols(dW[:, :hp], HEADS, HEAD_DIM),
                             _unpad_heads_cols(dW[:, hp:hp + SWA_KV * LANES], SWA_KV, HEAD_DIM),
                             _unpad_heads_cols(dW[:, hp + SWA_KV * LANES:], SWA_KV, HEAD_DIM)], axis=1)
    return dh, dict(w_in=dw_in, sinks=dsink, w_o=_unpad_heads_rows(dWo, HEADS, HEAD_DIM))


def _mla_fwd(h, w_a, g_q, g_kv, w_uq, w_ukv, w_o, tag):
    L = h.shape[0]
    Q, KV = MLA_Q_LORA, MLA_KV_LORA
    dqk = MLA_NOPE + MLA_ROPE
    kr_w = jnp.pad(w_a[:, Q + KV:], ((0, 0), (MLA_NOPE, LANES - dqk)))
    Wa = jnp.concatenate([w_a[:, :Q + KV], kr_w], axis=1)
    Wuq = _pad_heads_cols(w_uq, HEADS, dqk)
    ukv = w_ukv.reshape(KV, HEADS, MLA_NOPE + HEAD_DIM)
    Wukv = jnp.concatenate([_pad_heads_cols(ukv[:, :, :MLA_NOPE].reshape(KV, -1), HEADS, MLA_NOPE),
                            _pad_heads_cols(ukv[:, :, MLA_NOPE:].reshape(KV, -1), HEADS, HEAD_DIM)], axis=1)
    Wo = _pad_heads_rows(w_o, HEADS, HEAD_DIM)
    tabs = _rope_tables(L, MLA_ROPE, MLA_ROPE_THETA, MLA_NOPE)
    R = MLA_ROPE // 2
    pa = _mm(h, Wa, "nn", tag + "_proj")
    cqn, ckvn = _rms_fwd(pa, g_q, g_kv, tag + "_rms")
    q0 = _mm(cqn, Wuq, "nn", tag + "_uq")
    qr = _rope(q0, HEADS, tabs, R, tag + "_rope_q")
    kv0 = _mm(ckvn, Wukv, "nn", tag + "_ukv")
    kk = _rope(kv0, HEADS, tabs, R, tag + "_rope_k", shared=pa, shared_tile=(Q + KV) // LANES)
    cfg = _DenseCfg(L)
    scale = dqk ** -0.5
    o, lse = _attention_fwd(qr, 0, kk, 0, kv0, HEADS, HEADS, 1, cfg, scale, tag + "_attn")
    mix = _mm(o, Wo, "nn", tag + "_out")
    return mix, (h, Wa, Wuq, Wukv, Wo, g_q, g_kv, tabs, pa, cqn, ckvn, qr, kk, kv0, o, lse)


def _mla_bwd(dmix, res, tag):
    h, Wa, Wuq, Wukv, Wo, g_q, g_kv, tabs, pa, cqn, ckvn, qr, kk, kv0, o, lse = res
    L = h.shape[0]
    Q, KV = MLA_Q_LORA, MLA_KV_LORA
    dqk = MLA_NOPE + MLA_ROPE
    R = MLA_ROPE // 2
    cfg = _DenseCfg(L)
    scale = dqk ** -0.5
    dWo = _mm(o, dmix, "tn", tag + "_dwo")
    do = _mm(dmix, Wo, "nt", tag + "_do")
    delta = _delta(do, o, tag + "_delta")
    args = (qr, 0, kk, 0, kv0, HEADS, do, lse, delta)
    dqr, = _attention_dq(*args, HEADS, 1, cfg, scale, tag + "_dq")
    dk, dv = _attention_dkv(*args, HEADS, 1, cfg, cfg.passes[0], scale, tag + "_dkv")
    dq0 = _rope(dqr, HEADS, tabs, R, tag + "_rope_q_bwd", transpose=True)
    dWuq = _mm(cqn, dq0, "tn", tag + "_dwuq")
    dcqn = _mm(dq0, Wuq, "nt", tag + "_dcq")
    dkv = jnp.concatenate([dk, dv], axis=1)
    dWukv = _mm(ckvn, dkv, "tn", tag + "_dwukv")
    dckvn = _mm(dkv, Wukv, "nt", tag + "_dckv")
    dkr = _rope_shared_bwd(dk, HEADS, tabs, R, tag + "_rope_k_bwd")
    dpa, dgq, dgkv = _rms_bwd(pa, dcqn, dckvn, dkr, g_q, g_kv, tag + "_rms_bwd")
    dWa = _mm(h, dpa, "tn", tag + "_dw")
    dh = _mm(dpa, Wa, "nt", tag + "_dh")
    hp = HEADS * LANES
    dw_a = jnp.concatenate([dWa[:, :Q + KV], dWa[:, Q + KV + MLA_NOPE:Q + KV + dqk]], axis=1)
    dk_n = dWukv[:, :hp].reshape(KV, HEADS, LANES)[:, :, :MLA_NOPE]
    dv_n = dWukv[:, hp:].reshape(KV, HEADS, LANES)[:, :, :HEAD_DIM]
    dw_ukv = jnp.concatenate([dk_n, dv_n], axis=2).reshape(KV, HEADS * (MLA_NOPE + HEAD_DIM))
    return dh, dict(w_a=dw_a, g_q=dgq, g_kv=dgkv, w_uq=_unpad_heads_cols(dWuq, HEADS, dqk), w_ukv=dw_ukv,
                    w_o=_unpad_heads_rows(dWo, HEADS, HEAD_DIM))


def _local_step(x, target, w):
    S = x.shape[0]
    L = S + BLOCK
    h = jnp.concatenate([jnp.zeros((PAD, D_MODEL), F32), w["meta_tokens"], x], axis=0)
    saved = []
    for i in range(DEPTH):
        kind, j = i % 3, i // 3
        tag = "l%d" % i
        if kind == 0:
            mix, mres = _fox_fwd(h, w["fox_w_in"][j], w["fox_b_f"][j], w["fox_w_o"][j], tag + "_fox")
        elif kind == 1:
            mix, mres = _swa_fwd(h, w["swa_w_in"][j], w["swa_sinks"][j], w["swa_w_o"][j], tag + "_swa")
        else:
            mix, mres = _mla_fwd(h, w["mla_w_a"][j], w["mla_g_q"][j], w["mla_g_kv"][j], w["mla_w_uq"][j],
                                 w["mla_w_ukv"][j], w["mla_w_o"][j], tag + "_mla")
        h1, xh1, rs1 = _ln_fwd(h, mix, w["ln1_g"][i], w["ln1_b"][i], tag + "_ln1")
        u = _mm(h1, w["ffn_w_in"][i], "nn", tag + "_ffn_in")
        y, a = _conv_glu_fwd(u, w["ffn_conv_w"][i], w["ffn_conv_b"][i], tag + "_conv")
        ffn = _mm(a, w["ffn_w_out"][i], "nn", tag + "_ffn_out")
        h2, xh2, rs2 = _ln_fwd(h1, ffn, w["ln2_g"][i], w["ln2_b"][i], tag + "_ln2")
        saved.append((mres, xh1, rs1, h1, u, y, a, xh2, rs2))
        h = h2
    loss, dh = _loss_head(h, target, "loss_head")

    g = {k: [None] * v.shape[0] for k, v in w.items() if k != "meta_tokens"}
    ga = None
    for i in reversed(range(DEPTH)):
        kind, j = i % 3, i // 3
        tag = "l%d" % i
        mres, xh1, rs1, h1, u, y, a, xh2, rs2 = saved[i]
        dz2, g["ln2_g"][i], g["ln2_b"][i] = _ln_bwd(ga, dh, xh2, rs2, w["ln2_g"][i], tag + "_ln2_bwd")
        g["ffn_w_out"][i] = _mm(a, dz2, "tn", tag + "_dw_out")
        da = _mm(dz2, w["ffn_w_out"][i], "nt", tag + "_da")
        du, g["ffn_conv_w"][i], g["ffn_conv_b"][i] = _conv_glu_bwd(y, da, u, w["ffn_conv_w"][i], tag + "_conv_bwd")
        g["ffn_w_in"][i] = _mm(h1, du, "tn", tag + "_dw_in")
        dh1 = _mm(du, w["ffn_w_in"][i], "nt", tag + "_dh1")
        dz1, g["ln1_g"][i], g["ln1_b"][i] = _ln_bwd(dz2, dh1, xh1, rs1, w["ln1_g"][i], tag + "_ln1_bwd")
        if kind == 0:
            dh, mg = _fox_bwd(dz1, mres, tag + "_fox")
            pre = "fox_"
        elif kind == 1:
            dh, mg = _swa_bwd(dz1, mres, tag + "_swa")
            pre = "swa_"
        else:
            dh, mg = _mla_bwd(dz1, mres, tag + "_mla")
            pre = "mla_"
        for k, v in mg.items():
            g[pre + k][j] = v
        ga = dz1
    dh0 = _axpy(ga, dh, "dh0")
    grads = {k: jnp.stack(v) for k, v in g.items()}
    grads["meta_tokens"] = dh0[PAD:BLOCK]
    return loss, dh0, grads


SHARDED = (("meta_tokens", 1), ("fox_w_in", 2), ("fox_w_o", 1), ("swa_w_in", 2), ("swa_w_o", 1), ("mla_w_a", 1),
           ("mla_g_q", 1), ("mla_g_kv", 1), ("mla_w_uq", 2), ("mla_w_ukv", 2), ("mla_w_o", 1), ("ffn_w_in", 2),
           ("ffn_conv_w", 2), ("ffn_w_out", 1))
REPLICATED = ("ln1_g", "ln1_b", "ln2_g", "ln2_b", "fox_b_f", "swa_sinks", "ffn_conv_b")
WEIGHTS = ("meta_tokens", "ln1_g", "ln1_b", "ln2_g", "ln2_b", "fox_w_in", "fox_b_f", "fox_w_o", "swa_w_in",
           "swa_sinks", "swa_w_o", "mla_w_a", "mla_g_q", "mla_g_kv", "mla_w_uq", "mla_w_ukv", "mla_w_o", "ffn_w_in",
           "ffn_conv_w", "ffn_conv_b", "ffn_w_out")


def _rows(shape):
    return -(-math.prod(shape) // ROW)


def _pack(arrs, multiple):
    parts = []
    for a in arrs:
        n = math.prod(a.shape)
        parts.append(jnp.pad(a.reshape(-1), (0, _rows(a.shape) * ROW - n)).reshape(-1, ROW))
    total = sum(p.shape[0] for p in parts)
    pad = -total % multiple
    if pad:
        parts.append(jnp.zeros((pad, ROW), F32))
    return jnp.concatenate(parts, axis=0)


def _unpack(flat, shapes):
    out, r = [], 0
    for s in shapes:
        n = math.prod(s)
        out.append(flat[r:r + _rows(s)].reshape(-1)[:n].reshape(s))
        r += _rows(s)
    return out


HBM_SPEC = pl.BlockSpec(memory_space=pltpu.HBM)


def _place():
    x, y, c = lax.axis_index("x"), lax.axis_index("y"), lax.axis_index("c")
    chips = [(1 - x, y), (x, 1 - y), (1 - x, 1 - y)]
    return x, y, c, chips


def _gather_weights(shard):
    R = shard.shape[0]
    Rh = R // 2

    def body(s_ref, o_ref, send_sems, recv_sems, local_sem):
        x, y, c, chips = _place()
        sib = (x, y, 1 - c)

        def half(k, hc):
            return o_ref.at[k, pl.ds(hc * Rh, Rh), :]

        def copy(j, src, dst, to):
            return pltpu.make_async_remote_copy(src_ref=src, dst_ref=dst, send_sem=send_sems.at[j],
                                                recv_sem=recv_sems.at[j], device_id=to, device_id_type=MESH)

        me = 2 * x + y
        mine = pltpu.make_async_copy(s_ref, o_ref.at[me], local_sem)
        mine.start()
        first = [copy(j, s_ref.at[pl.ds(c * Rh, Rh), :], half(me, c), (tx, ty, c)) for j, (tx, ty) in enumerate(chips)]
        for cp in first:
            cp.start()
        passed = []
        for j, (tx, ty) in enumerate(chips):
            k = 2 * tx + ty
            copy(j, half(k, c), half(k, c), (tx, ty, c)).wait_recv()
            fw = copy(3 + j, half(k, c), half(k, c), sib)
            fw.start()
            passed.append(fw)
        for j, (tx, ty) in enumerate(chips):
            k = 2 * tx + ty
            copy(3 + j, half(k, 1 - c), half(k, 1 - c), sib).wait_recv()
        for cp in first + passed:
            cp.wait_send()
        mine.wait()

    return pl.pallas_call(
        body, name="gather_weights", out_shape=jax.ShapeDtypeStruct((N_CHIPS, R, ROW), shard.dtype),
        in_specs=[HBM_SPEC], out_specs=HBM_SPEC,
        scratch_shapes=[pltpu.SemaphoreType.DMA((6,)), pltpu.SemaphoreType.DMA((6,)), pltpu.SemaphoreType.DMA],
    )(shard)


def _swap_halves(G):
    R = G.shape[1]
    Rh = R // 2

    def body(g_ref, a_ref, send_sem, recv_sem):
        x, y, c, _ = _place()
        cp = pltpu.make_async_remote_copy(src_ref=g_ref.at[:, pl.ds((1 - c) * Rh, Rh), :], dst_ref=a_ref,
                                          send_sem=send_sem, recv_sem=recv_sem, device_id=(x, y, 1 - c),
                                          device_id_type=MESH)
        cp.start()
        cp.wait()

    return pl.pallas_call(
        body, name="reduce_swap_halves", out_shape=jax.ShapeDtypeStruct((N_CHIPS, Rh, ROW), G.dtype),
        in_specs=[HBM_SPEC], out_specs=HBM_SPEC,
        scratch_shapes=[pltpu.SemaphoreType.DMA, pltpu.SemaphoreType.DMA],
    )(G)


def _exchange_chips(P):
    Rh = P.shape[1]

    def body(p_ref, b_ref, send_sems, recv_sems, local_sem):
        x, y, c, chips = _place()
        me = 2 * x + y

        def copy(j, src, dst, to):
            return pltpu.make_async_remote_copy(src_ref=src, dst_ref=dst, send_sem=send_sems.at[j],
                                                recv_sem=recv_sems.at[j], device_id=to, device_id_type=MESH)

        mine = pltpu.make_async_copy(p_ref.at[me], b_ref.at[me], local_sem)
        mine.start()
        sends = [copy(j, p_ref.at[2 * tx + ty], b_ref.at[me], (tx, ty, c)) for j, (tx, ty) in enumerate(chips)]
        for cp in sends:
            cp.start()
        for j, (tx, ty) in enumerate(chips):
            k = 2 * tx + ty
            copy(j, p_ref.at[k], b_ref.at[k], (tx, ty, c)).wait_recv()
        for cp in sends:
            cp.wait_send()
        mine.wait()

    return pl.pallas_call(
        body, name="reduce_exchange_chips", out_shape=jax.ShapeDtypeStruct(P.shape, P.dtype),
        in_specs=[HBM_SPEC], out_specs=HBM_SPEC,
        scratch_shapes=[pltpu.SemaphoreType.DMA((3,)), pltpu.SemaphoreType.DMA((3,)), pltpu.SemaphoreType.DMA],
    )(P)


def _join_halves(Fh):
    Rh = Fh.shape[0]

    def body(f_ref, o_ref, send_sem, recv_sem, local_sem):
        x, y, c, _ = _place()
        mine = pltpu.make_async_copy(f_ref, o_ref.at[pl.ds(c * Rh, Rh), :], local_sem)
        mine.start()
        cp = pltpu.make_async_remote_copy(src_ref=f_ref, dst_ref=o_ref.at[pl.ds(c * Rh, Rh), :], send_sem=send_sem,
                                          recv_sem=recv_sem, device_id=(x, y, 1 - c), device_id_type=MESH)
        cp.start()
        pltpu.make_async_remote_copy(src_ref=f_ref, dst_ref=o_ref.at[pl.ds((1 - c) * Rh, Rh), :], send_sem=send_sem,
                                     recv_sem=recv_sem, device_id=(x, y, 1 - c), device_id_type=MESH).wait_recv()
        cp.wait_send()
        mine.wait()

    return pl.pallas_call(
        body, name="reduce_join_halves", out_shape=jax.ShapeDtypeStruct((2 * Rh, ROW), Fh.dtype),
        in_specs=[HBM_SPEC], out_specs=HBM_SPEC,
        scratch_shapes=[pltpu.SemaphoreType.DMA, pltpu.SemaphoreType.DMA, pltpu.SemaphoreType.DMA],
    )(Fh)


def _gather_small(v):
    m_per = v.shape[0]

    def body(x_ref, out_ref, send_sems, recv_sems, local_sem):
        x, y, c, chips = _place()
        me, sibling = (x, y, c), (x, y, 1 - c)

        def rows(px, py, pc):
            return out_ref.at[pl.ds((4 * px + 2 * py + pc) * m_per, m_per), :]

        def copy(k, block, to, src=None):
            return pltpu.make_async_remote_copy(src_ref=rows(*block) if src is None else src, dst_ref=rows(*block),
                                                send_sem=send_sems.at[k], recv_sem=recv_sems.at[k], device_id=to,
                                                device_id_type=MESH)

        mine = pltpu.make_async_copy(x_ref, rows(*me), local_sem)
        mine.start()
        first = [copy(0, me, sibling, src=x_ref)]
        first += [copy(1 + j, me, (*chip, c), src=x_ref) for j, chip in enumerate(chips)]
        for cp in first:
            cp.start()
        passed = [copy(4 + j, (*chip, c), sibling) for j, chip in enumerate(chips)]
        for j, chip in enumerate(chips):
            copy(1 + j, (*chip, c), me).wait_recv()
            passed[j].start()
        copy(0, sibling, me).wait_recv()
        for j, chip in enumerate(chips):
            copy(4 + j, (*chip, 1 - c), me).wait_recv()
        for cp in first + passed:
            cp.wait_send()
        mine.wait()

    return pl.pallas_call(
        body, name="gather_small", out_shape=jax.ShapeDtypeStruct((N_DEV * m_per, ROW), v.dtype),
        in_specs=[pl.BlockSpec(memory_space=pltpu.VMEM)], out_specs=pl.BlockSpec(memory_space=pltpu.VMEM),
        scratch_shapes=[pltpu.SemaphoreType.DMA((7,)), pltpu.SemaphoreType.DMA((7,)), pltpu.SemaphoreType.DMA],
    )(v)


def _sum_slots(a, n, name, b=None):
    M = a.shape[0] // n
    tr = _pick(M, (512, 256, 128, 64, 40, 8))
    nb = M // tr

    def body(*refs):
        o_ref = refs[-1]
        acc = refs[0][...]
        for r in refs[1:-1]:
            acc = acc + r[...]
        o_ref[...] = acc

    specs = [pl.BlockSpec((tr, ROW), functools.partial(lambda i, k: (k * nb + i, 0), k=k)) for k in range(n)]
    ins = [a] * n
    if b is not None:
        specs = [pl.BlockSpec((tr, ROW), lambda i: (i, 0))] + specs
        ins = [b] + ins
    return pl.pallas_call(body, name=name, grid=(nb,), in_specs=specs, out_specs=pl.BlockSpec((tr, ROW), lambda i: (i, 0)),
                          out_shape=jax.ShapeDtypeStruct((M, ROW), F32), compiler_params=_params("parallel"))(*ins)


def _add(a, b, name):
    M = a.shape[0]
    tr = _pick(M, (512, 256, 128, 64, 40, 8))

    def body(a_ref, b_ref, o_ref):
        o_ref[...] = a_ref[...] + b_ref[...]

    row = pl.BlockSpec((tr, ROW), lambda i: (i, 0))
    return pl.pallas_call(body, name=name, grid=(M // tr,), in_specs=[row, row], out_specs=row,
                          out_shape=jax.ShapeDtypeStruct((M, ROW), F32), compiler_params=_params("parallel"))(a, b)


def _adamw(g, w, m, v, name):
    M = g.shape[0]
    tr = _pick(M, (512, 256, 128, 64, 40, 8))
    c1 = 1.0 - ADAM_B1 ** ADAM_STEP
    c2 = 1.0 - ADAM_B2 ** ADAM_STEP

    def body(g_ref, w_ref, m_ref, v_ref, d_ref, nm_ref, nv_ref):
        gg = g_ref[...]
        nm = ADAM_B1 * m_ref[...] + (1.0 - ADAM_B1) * gg
        nv = ADAM_B2 * v_ref[...] + (1.0 - ADAM_B2) * (gg * gg)
        nm_ref[...] = nm
        nv_ref[...] = nv
        d_ref[...] = -ADAM_LR * ((nm / c1) / (jnp.sqrt(nv / c2) + ADAM_EPS) + ADAM_WD * w_ref[...])

    row = pl.BlockSpec((tr, ROW), lambda i: (i, 0))
    shape = jax.ShapeDtypeStruct((M, ROW), F32)
    return pl.pallas_call(body, name=name, grid=(M // tr,), in_specs=[row] * 4, out_specs=[row] * 3,
                          out_shape=[shape] * 3, compiler_params=_params("parallel"))(g, w, m, v)


def kernel(x, meta_tokens, ln1_g, ln1_b, ln2_g, ln2_b, fox_w_in, fox_b_f, fox_w_o, swa_w_in, swa_sinks, swa_w_o, mla_w_a, mla_g_q, mla_g_kv, mla_w_uq, mla_w_ukv, mla_w_o, ffn_w_in, ffn_conv_w, ffn_conv_b, ffn_w_out, loss_target, m_meta_tokens, m_ln1_g, m_ln1_b, m_ln2_g, m_ln2_b, m_fox_w_in, m_fox_b_f, m_fox_w_o, m_swa_w_in, m_swa_sinks, m_swa_w_o, m_mla_w_a, m_mla_g_q, m_mla_g_kv, m_mla_w_uq, m_mla_w_ukv, m_mla_w_o, m_ffn_w_in, m_ffn_conv_w, m_ffn_conv_b, m_ffn_w_out, v_meta_tokens, v_ln1_g, v_ln1_b, v_ln2_g, v_ln2_b, v_fox_w_in, v_fox_b_f, v_fox_w_o, v_swa_w_in, v_swa_sinks, v_swa_w_o, v_mla_w_a, v_mla_g_q, v_mla_g_kv, v_mla_w_uq, v_mla_w_ukv, v_mla_w_o, v_ffn_w_in, v_ffn_conv_w, v_ffn_conv_b, v_ffn_w_out):
    given = dict(locals())
    w = {n: given[n] for n in WEIGHTS}
    m = {n: given["m_" + n] for n in WEIGHTS}
    v = {n: given["v_" + n] for n in WEIGHTS}
    sh_names = [n for n, _ in SHARDED]
    sh_shapes = [w[n].shape for n in sh_names]

    gathered = _gather_weights(_pack([w[n] for n in sh_names], 2 * ROW))
    full = dict(w)
    per_chip = [_unpack(gathered[k], sh_shapes) for k in range(N_CHIPS)]
    for t, (n, ax) in enumerate(SHARDED):
        full[n] = jnp.concatenate([per_chip[k][t] for k in range(N_CHIPS)], axis=ax)

    loss_part, dh0, grads = _local_step(x[0], loss_target[0], full)
    loss = lax.psum(jnp.sum(loss_part), ("x", "y", "c"))
    grad_x = dh0[BLOCK:][None]

    split = {n: jnp.split(grads[n], N_CHIPS, axis=ax) for n, ax in SHARDED}
    G = jnp.stack([_pack([split[n][k] for n in sh_names], 2 * ROW) for k in range(N_CHIPS)])
    R = G.shape[1]
    Rh = R // 2
    c = lax.axis_index("c")
    mine = lax.dynamic_slice_in_dim(G, c * Rh, Rh, axis=1)
    P = _add(mine.reshape(N_CHIPS * Rh, ROW), _swap_halves(G).reshape(N_CHIPS * Rh, ROW), "reduce_pair_sum")
    B = _exchange_chips(P.reshape(N_CHIPS, Rh, ROW))
    Fh = _sum_slots(B.reshape(N_CHIPS * Rh, ROW), N_CHIPS, "reduce_chip_sum")
    Fg = _join_halves(Fh)
    d_s, m_s, v_s = _adamw(Fg, _pack([w[n] for n in sh_names], 2 * ROW), _pack([m[n] for n in sh_names], 2 * ROW),
                           _pack([v[n] for n in sh_names], 2 * ROW), "adamw_sharded")

    rp_shapes = [w[n].shape for n in REPLICATED]
    small = _gather_small(_pack([grads[n] for n in REPLICATED], 8))
    g_r = _sum_slots(small, N_DEV, "reduce_small_sum")
    d_r, m_r, v_r = _adamw(g_r, _pack([w[n] for n in REPLICATED], 8), _pack([m[n] for n in REPLICATED], 8),
                           _pack([v[n] for n in REPLICATED], 8), "adamw_replicated")

    out = {}
    for kind, fs, fr in (("grad", Fg, g_r), ("delta", d_s, d_r), ("new_m", m_s, m_r), ("new_v", v_s, v_r)):
        for n, a in zip(sh_names, _unpack(fs, sh_shapes)):
            out[kind, n] = a
        for n, a in zip(REPLICATED, _unpack(fr, rp_shapes)):
            out[kind, n] = a
    return (loss, grad_x, *[out[k, n] for k in ("grad", "delta", "new_m", "new_v") for n in WEIGHTS])
```

```python
import functools
import math

import numpy as np
import jax
import jax.numpy as jnp
from jax import lax
from jax.experimental import pallas as pl
from jax.experimental.pallas import tpu as pltpu

F32 = jnp.float32
BF16 = jnp.bfloat16

D_MODEL = 1024
DEPTH = 4
BLOCK = 128
N_META = 16
PAD = BLOCK - N_META
NEG = -1e30
ALPHA = (2.0 * DEPTH) ** 0.25
LN_EPS = 1e-5
RMS_EPS = 1e-6
HEADS = 16
HEAD_DIM = 64
LANES = 128
SWA_KV = 2
SWA_G = HEADS // SWA_KV
WINDOW = 128
ROPE_THETA = 500000.0
ROPE_DIM = 16
MLA_Q_LORA = 384
MLA_KV_LORA = 256
MLA_NOPE = 64
MLA_ROPE = 32
MLA_ROPE_THETA = 10000.0
D_FF = 2816
ADAM_LR = 0.001
ADAM_B1 = 0.9
ADAM_B2 = 0.999
ADAM_EPS = 1e-08
ADAM_WD = 0.01
ADAM_STEP = 10
N_CHIPS = 4
N_DEV = 8
ROW = 1024
VMEM_LIMIT = 48 * 1024 * 1024
MESH = pl.DeviceIdType.MESH
LOG2E = 1.4426950408889634

NN = (((1,), (0,)), ((), ()))
NT = (((1,), (1,)), ((), ()))
TN = (((0,), (0,)), ((), ()))


def _pick(n, cands):
    for c in cands:
        if n % c == 0:
            return c
    return n


def _params(*sem):
    return pltpu.CompilerParams(dimension_semantics=sem, vmem_limit_bytes=VMEM_LIMIT)


def _bf(x):
    return x if x.dtype == BF16 else x.astype(BF16)


def _mm(a, b, mode, name, out_dtype=F32):
    if mode == "nn":
        (M, K), (_, N) = a.shape, b.shape
    elif mode == "nt":
        (M, K), (N, _) = a.shape, b.shape
    else:
        (K, M), (_, N) = a.shape, b.shape
    tm = _pick(M, (1664, 1024, 640, 512, 384, 256, 128))
    tn = _pick(N, (640, 512, 384, 256, 128))
    tk = K if (K <= 1024 and mode != "tn") else _pick(K, (640, 512, 384, 256, 128))
    nk = K // tk
    dn = {"nn": NN, "nt": NT, "tn": TN}[mode]

    def body(a_ref, b_ref, o_ref, *acc):
        part = lax.dot_general(_bf(a_ref[...]), _bf(b_ref[...]), dn, preferred_element_type=F32)
        if nk == 1:
            o_ref[...] = part.astype(out_dtype)
            return
        acc_ref, = acc
        k = pl.program_id(2)

        @pl.when(k == 0)
        def _():
            acc_ref[...] = part

        @pl.when(k > 0)
        def _():
            acc_ref[...] += part

        @pl.when(k == nk - 1)
        def _():
            o_ref[...] = acc_ref[...].astype(out_dtype)

    if mode == "tn":
        a_spec = pl.BlockSpec((tk, tm), lambda i, j, k: (k, i))
    else:
        a_spec = pl.BlockSpec((tm, tk), lambda i, j, k: (i, k))
    if mode == "nt":
        b_spec = pl.BlockSpec((tn, tk), lambda i, j, k: (j, k))
    else:
        b_spec = pl.BlockSpec((tk, tn), lambda i, j, k: (k, j))
    return pl.pallas_call(
        body, name=name, grid=(M // tm, N // tn, nk),
        in_specs=[a_spec, b_spec],
        out_specs=pl.BlockSpec((tm, tn), lambda i, j, k: (i, j)),
        out_shape=jax.ShapeDtypeStruct((M, N), out_dtype),
        scratch_shapes=[pltpu.VMEM((tm, tn), F32)] if nk > 1 else [],
        compiler_params=_params("parallel", "parallel", "arbitrary"),
    )(a, b)


def _ln_fwd(h, mix, g, b, name):
    L = h.shape[0]
    tr = 128

    def body(h_ref, m_ref, g_ref, b_ref, o_ref, ob_ref, xh_ref, rs_ref):
        z = ALPHA * h_ref[...] + m_ref[...]
        mu = jnp.mean(z, axis=1, keepdims=True)
        zc = z - mu
        var = jnp.mean(zc * zc, axis=1, keepdims=True)
        rstd = lax.rsqrt(var + LN_EPS)
        xh = zc * rstd
        xh_ref[...] = xh
        rs_ref[...] = rstd
        out = xh * g_ref[...] + b_ref[...]
        o_ref[...] = out
        ob_ref[...] = out.astype(BF16)

    row = pl.BlockSpec((tr, D_MODEL), lambda i: (i, 0))
    vec = pl.BlockSpec((1, D_MODEL), lambda i: (0, 0))
    return pl.pallas_call(
        body, name=name, grid=(L // tr,),
        in_specs=[row, row, vec, vec],
        out_specs=[row, row, row, pl.BlockSpec((tr, 1), lambda i: (i, 0))],
        out_shape=[jax.ShapeDtypeStruct((L, D_MODEL), F32), jax.ShapeDtypeStruct((L, D_MODEL), BF16),
                   jax.ShapeDtypeStruct((L, D_MODEL), F32), jax.ShapeDtypeStruct((L, 1), F32)],
        compiler_params=_params("parallel"),
    )(h, mix, g.reshape(1, D_MODEL), b.reshape(1, D_MODEL))


def _ln_bwd(ga, gb, xhat, rstd, g, name):
    L = xhat.shape[0]
    tr = 128
    two = ga is not None

    def body(*refs):
        if two:
            ga_ref, gb_ref, xh_ref, rs_ref, g_ref, dz_ref, dzb_ref, dg_ref, db_ref = refs
            dy = ALPHA * ga_ref[...] + gb_ref[...]
        else:
            gb_ref, xh_ref, rs_ref, g_ref, dz_ref, dzb_ref, dg_ref, db_ref = refs
            dy = gb_ref[...]
        xh = xh_ref[...]
        dxh = dy * g_ref[...]
        c1 = jnp.mean(dxh, axis=1, keepdims=True)
        c2 = jnp.mean(dxh * xh, axis=1, keepdims=True)
        dz = rs_ref[...] * (dxh - c1 - xh * c2)
        dz_ref[...] = dz
        dzb_ref[...] = dz.astype(BF16)

        @pl.when(pl.program_id(0) == 0)
        def _():
            dg_ref[...] = jnp.zeros_like(dg_ref)
            db_ref[...] = jnp.zeros_like(db_ref)

        dg_ref[...] += jnp.sum(dy * xh, axis=0, keepdims=True)
        db_ref[...] += jnp.sum(dy, axis=0, keepdims=True)

    row = pl.BlockSpec((tr, D_MODEL), lambda i: (i, 0))
    vec = pl.BlockSpec((1, D_MODEL), lambda i: (0, 0))
    ins = ([ga] if two else []) + [gb, xhat, rstd, g.reshape(1, D_MODEL)]
    specs = ([row] if two else []) + [row, row, pl.BlockSpec((tr, 1), lambda i: (i, 0)), vec]
    dz, dzb, dg, db = pl.pallas_call(
        body, name=name, grid=(L // tr,),
        in_specs=specs, out_specs=[row, row, vec, vec],
        out_shape=[jax.ShapeDtypeStruct((L, D_MODEL), F32), jax.ShapeDtypeStruct((L, D_MODEL), BF16),
                   jax.ShapeDtypeStruct((1, D_MODEL), F32), jax.ShapeDtypeStruct((1, D_MODEL), F32)],
        compiler_params=_params("arbitrary"),
    )(*ins)
    return dz, dzb, dg[0], db[0]


def _axpy(a, b, name):
    L, N = a.shape
    tr = 128

    def body(a_ref, b_ref, o_ref):
        o_ref[...] = ALPHA * a_ref[...] + b_ref[...]

    row = pl.BlockSpec((tr, N), lambda i: (i, 0))
    return pl.pallas_call(body, name=name, grid=(L // tr,), in_specs=[row, row], out_specs=row,
                          out_shape=jax.ShapeDtypeStruct((L, N), F32), compiler_params=_params("parallel"))(a, b)


def _shift_down(cur, prev8, n):
    rows = lax.broadcasted_iota(jnp.int32, cur.shape, 0)
    out = pltpu.roll(cur, n, 0)
    for r in range(n):
        out = jnp.where(rows == r, prev8[8 - n + r:8 - n + r + 1, :], out)
    return out


def _shift_up(cur, next8, n):
    tr = cur.shape[0]
    rows = lax.broadcasted_iota(jnp.int32, cur.shape, 0)
    out = pltpu.roll(cur, tr - n, 0)
    for r in range(n):
        out = jnp.where(rows == tr - n + r, next8[r:r + 1, :], out)
    return out


def _silu(x):
    return x / (1.0 + jnp.exp(-x))


def _conv_glu_fwd(u, cw, cb, name):
    L, F2 = u.shape
    F = F2 // 2
    tr = 128

    def body(u_ref, up_ref, cw_ref, cb_ref, y_ref, a_ref):
        i = pl.program_id(0)
        rows = i * tr + lax.broadcasted_iota(jnp.int32, (tr, F2), 0)
        cur = jnp.where(rows >= PAD, u_ref[...], 0.0)
        prow = i * tr - 8 + lax.broadcasted_iota(jnp.int32, (8, F2), 0)
        prev = jnp.where(prow >= PAD, up_ref[...], 0.0)
        y = cb_ref[...] + _shift_down(cur, prev, 2) * cw_ref[0:1, :]
        y = y + _shift_down(cur, prev, 1) * cw_ref[1:2, :]
        y = y + cur * cw_ref[2:3, :]
        y_ref[...] = y
        a_ref[...] = (_silu(y[:, :F]) * y[:, F:]).astype(BF16)

    return pl.pallas_call(
        body, name=name, grid=(L // tr,),
        in_specs=[pl.BlockSpec((tr, F2), lambda i: (i, 0)),
                  pl.BlockSpec((8, F2), lambda i: (jnp.maximum(i * (tr // 8) - 1, 0), 0)),
                  pl.BlockSpec((3, F2), lambda i: (0, 0)),
                  pl.BlockSpec((1, F2), lambda i: (0, 0))],
        out_specs=[pl.BlockSpec((tr, F2), lambda i: (i, 0)), pl.BlockSpec((tr, F), lambda i: (i, 0))],
        out_shape=[jax.ShapeDtypeStruct((L, F2), F32), jax.ShapeDtypeStruct((L, F), BF16)],
        compiler_params=_params("parallel"),
    )(u, u, cw, cb.reshape(1, F2))


def _conv_glu_bwd(y, da, u, cw, name):
    L, F2 = u.shape
    F = F2 // 2
    tr = 128
    nb = L // tr

    def dy_of(yv, dav):
        g, val = yv[:, :F], yv[:, F:]
        sg = 1.0 / (1.0 + jnp.exp(-g))
        dg = dav * val * (sg * (1.0 + g * (1.0 - sg)))
        dv = dav * (g * sg)
        return jnp.concatenate([dg, dv], axis=1)

    def body(y_ref, yn_ref, da_ref, dan_ref, u_ref, up_ref, cw_ref, du_ref, dcw_ref, dcb_ref):
        i = pl.program_id(0)
        rows = i * tr + lax.broadcasted_iota(jnp.int32, (tr, F2), 0)
        dy = dy_of(y_ref[...], da_ref[...])
        dyn = jnp.where(i < nb - 1, dy_of(yn_ref[...], dan_ref[...]), 0.0)
        du = dy * cw_ref[2:3, :] + _shift_up(dy, dyn, 1) * cw_ref[1:2, :] + _shift_up(dy, dyn, 2) * cw_ref[0:1, :]
        du_ref[...] = jnp.where(rows >= PAD, du, 0.0).astype(BF16)
        cur = jnp.where(rows >= PAD, u_ref[...], 0.0)
        prow = i * tr - 8 + lax.broadcasted_iota(jnp.int32, (8, F2), 0)
        prev = jnp.where(prow >= PAD, up_ref[...], 0.0)

        @pl.when(i == 0)
        def _():
            dcw_ref[...] = jnp.zeros_like(dcw_ref)
            dcb_ref[...] = jnp.zeros_like(dcb_ref)

        dcw_ref[0:1, :] += jnp.sum(dy * _shift_down(cur, prev, 2), axis=0, keepdims=True)
        dcw_ref[1:2, :] += jnp.sum(dy * _shift_down(cur, prev, 1), axis=0, keepdims=True)
        dcw_ref[2:3, :] += jnp.sum(dy * cur, axis=0, keepdims=True)
        dcb_ref[...] += jnp.sum(dy, axis=0, keepdims=True)

    nxt = lambda i: (jnp.minimum((i + 1) * (tr // 8), L // 8 - 1), 0)
    prv = lambda i: (jnp.maximum(i * (tr // 8) - 1, 0), 0)
    du, dcw, dcb = pl.pallas_call(
        body, name=name, grid=(nb,),
        in_specs=[pl.BlockSpec((tr, F2), lambda i: (i, 0)), pl.BlockSpec((8, F2), nxt),
                  pl.BlockSpec((tr, F), lambda i: (i, 0)), pl.BlockSpec((8, F), nxt),
                  pl.BlockSpec((tr, F2), lambda i: (i, 0)), pl.BlockSpec((8, F2), prv),
                  pl.BlockSpec((3, F2), lambda i: (0, 0))],
        out_specs=[pl.BlockSpec((tr, F2), lambda i: (i, 0)), pl.BlockSpec((3, F2), lambda i: (0, 0)),
                   pl.BlockSpec((1, F2), lambda i: (0, 0))],
        out_shape=[jax.ShapeDtypeStruct((L, F2), BF16), jax.ShapeDtypeStruct((3, F2), F32),
                   jax.ShapeDtypeStruct((1, F2), F32)],
        compiler_params=_params("arbitrary"),
    )(y, y, da, da, u, u, cw)
    return du, dcw, dcb[0]


def _dense_mask(qpos, kpos):
    return (kpos <= qpos) & (kpos >= PAD)


def _swa_mask(qpos, kpos):
    d = qpos - kpos
    return ((kpos >= BLOCK) & (d >= 0) & (d < WINDOW)) | ((kpos >= PAD) & (kpos < BLOCK) & (kpos <= qpos))


def _tables(pairs):
    qt, kt, ft = [], [], []
    for grp in pairs:
        for n, (qb, kb, msk) in enumerate(grp):
            qt.append(qb)
            kt.append(kb)
            ft.append((1 if n == 0 else 0) | (2 if n == len(grp) - 1 else 0) | (4 if msk else 0))
    return tuple(jnp.asarray(np.asarray(t, np.int32)) for t in (qt, kt, ft))


class _Dense:
    mask = staticmethod(_dense_mask)

    def __init__(self, L, pad_in_cr2):
        self.T = T = 640 if L % 640 == 0 else 128
        nb = L // T
        m = lambda qb, kb: kb == qb or (kb == 0 and not pad_in_cr2) or (qb * T < PAD)
        self.q_major = _tables([[(qb, kb, m(qb, kb)) for kb in range(qb + 1)] for qb in range(nb)])
        self.k_major = _tables([[(qb, kb, m(qb, kb)) for qb in range(kb, nb)] for kb in range(nb)])


class _Swa:
    mask = staticmethod(_swa_mask)

    def __init__(self, L):
        self.T = BLOCK
        nb = L // BLOCK
        ks = lambda qb: [0] + ([qb - 1] if qb >= 2 else []) + ([qb] if qb >= 1 else [])
        qs = lambda kb: list(range(nb)) if kb == 0 else [kb] + ([kb + 1] if kb + 1 < nb else [])
        self.q_major = _tables([[(qb, kb, True) for kb in ks(qb)] for qb in range(nb)])
        self.k_major = _tables([[(qb, kb, True) for qb in qs(kb)] for kb in range(nb)])


def _positions(T, qb, kb):
    qpos = qb * T + lax.broadcasted_iota(jnp.int32, (T, T), 0)
    kpos = kb * T + lax.broadcasted_iota(jnp.int32, (T, T), 1)
    return qpos, kpos


def _scores(q, k, c, cr, masked, mask, T, qb, kb, backward):
    s = lax.dot_general(q, k, NT, preferred_element_type=F32) * c
    if cr is not None:
        s = s - cr
    live = None
    if masked:
        qpos, kpos = _positions(T, qb, kb)
        live = mask(qpos, kpos)
        if backward:
            live = live & (qpos >= PAD)
        s = jnp.where(live, s, NEG)
    return s, live


def _prob(s, lse, live):
    p = jnp.exp2(s - lse)
    return p if live is None else jnp.where(live, p, 0.0)


def _both(flag, fn):
    pl.when(flag != 0)(lambda: fn(True))
    pl.when(flag == 0)(lambda: fn(False))


def _flash_q_major(mode, cfg, qa, q_off, ka, k_off, va, v_off, H, hs, scale, name,
                   cr2=None, sink2=None, do=None, lse2=None, delta=None, out_dtype=F32):
    L = qa.shape[0]
    T = cfg.T
    qt, kt, ft = cfg.q_major
    npairs = qt.shape[0]
    c = scale * LOG2E
    decay, has_sink, fwd = cr2 is not None, sink2 is not None, mode == "fwd"
    W = hs * LANES

    def body(qt_ref, kt_ref, ft_ref, *refs):
        it = iter(refs)
        q_ref, k_ref, v_ref = next(it), next(it), next(it)
        cr_ref = next(it) if decay else None
        sink_ref = next(it) if has_sink else None
        do_ref, lse_ref = (None, None) if fwd else (next(it), next(it))
        dl_ref = next(it) if mode == "dq" else None
        out_ref = next(it)
        lse_out = next(it) if fwd else None
        m_sc = next(it) if fwd else None
        l_sc = next(it) if fwd else None
        acc_sc = next(it)
        n = pl.program_id(1)
        qb, kb, f = qt_ref[n], kt_ref[n], ft_ref[n]

        @pl.when((f & 1) != 0)
        def _():
            for i in range(hs):
                acc_sc[i] = jnp.zeros(acc_sc.shape[1:], F32)
                if fwd and has_sink:
                    m_sc[i] = jnp.broadcast_to(sink_ref[:, i * LANES:i * LANES + 1], (T, 1))
                    l_sc[i] = jnp.ones((T, 1), F32)
                elif fwd:
                    m_sc[i] = jnp.full((T, 1), NEG, F32)
                    l_sc[i] = jnp.zeros((T, 1), F32)

        def step(masked):
            k, v = _bf(k_ref[...]), _bf(v_ref[...])
            cr = cr_ref[0] if decay else None
            for i in range(hs):
                cols = slice(i * LANES, (i + 1) * LANES)
                s, live = _scores(_bf(q_ref[:, cols]), k, c, cr, masked, cfg.mask, T, qb, kb, not fwd)
                if fwd:
                    m_prev = m_sc[i]
                    m_new = jnp.maximum(m_prev, jnp.max(s, axis=1, keepdims=True))
                    alpha = jnp.exp2(m_prev - m_new)
                    p = jnp.exp2(s - m_new)
                    l_sc[i] = alpha * l_sc[i] + jnp.sum(p, axis=1, keepdims=True)
                    acc_sc[i] = alpha * acc_sc[i] + lax.dot_general(p.astype(BF16), v, NN, preferred_element_type=F32)
                    m_sc[i] = m_new
                    continue
                p = _prob(s, lse_ref[:, i * LANES:i * LANES + 1], live)
                dp = lax.dot_general(_bf(do_ref[:, cols]), v, NT, preferred_element_type=F32)
                if mode == "delta":
                    acc_sc[i] += jnp.sum(p * dp, axis=1, keepdims=True)
                else:
                    ds = p * (dp - dl_ref[:, i * LANES:i * LANES + 1])
                    acc_sc[i] += lax.dot_general(ds.astype(BF16), k, NN, preferred_element_type=F32)

        _both(f & 4, step)

        @pl.when((f & 2) != 0)
        def _():
            for i in range(hs):
                cols = slice(i * LANES, (i + 1) * LANES)
                acc = acc_sc[i]
                if fwd:
                    l = l_sc[i]
                    out_ref[:, cols] = acc / l
                    lse_out[:, cols] = jnp.broadcast_to(m_sc[i] + jnp.log(l) * LOG2E, (T, LANES))
                elif mode == "delta":
                    out_ref[:, cols] = jnp.broadcast_to(acc, (T, LANES))
                else:
                    out_ref[:, cols] = (acc * scale).astype(out_dtype)

    qrow = lambda off: pl.BlockSpec((T, W), lambda h, n, qt, kt, ft: (qt[n], off // hs + h))
    krow = lambda off: pl.BlockSpec((T, LANES), lambda h, n, qt, kt, ft: (kt[n], off + h))
    ins, specs = [qa, ka, va], [qrow(q_off), krow(k_off), krow(v_off)]
    if decay:
        ins.append(cr2)
        specs.append(pl.BlockSpec((1, 1, T), lambda h, n, qt, kt, ft: (h, 0, kt[n])))
    if has_sink:
        ins.append(sink2)
        specs.append(pl.BlockSpec((1, W), lambda h, n, qt, kt, ft: (0, h)))
    if not fwd:
        ins += [do, lse2]
        specs += [qrow(0), qrow(0)]
    if mode == "dq":
        ins.append(delta)
        specs.append(qrow(0))
    full = jax.ShapeDtypeStruct((L, H * LANES), out_dtype if mode == "dq" else F32)
    scratch = [pltpu.VMEM((hs, T, 1), F32), pltpu.VMEM((hs, T, 1), F32), pltpu.VMEM((hs, T, LANES), F32)] if fwd else \
        [pltpu.VMEM((hs, T, 1 if mode == "delta" else LANES), F32)]
    out = pl.pallas_call(
        body, name=name, out_shape=[full, full] if fwd else full,
        grid_spec=pltpu.PrefetchScalarGridSpec(
            num_scalar_prefetch=3, grid=(H // hs, npairs), in_specs=specs,
            out_specs=[qrow(0), qrow(0)] if fwd else qrow(0), scratch_shapes=scratch),
        compiler_params=_params("parallel", "arbitrary"),
    )(qt, kt, ft, *ins)
    return out


def _flash_k_major(cfg, qa, q_off, ka, k_off, va, v_off, do, lse2, delta, HKV, hs, scale, name,
                   cr2=None, fused=False, out_dtype=F32):
    L = qa.shape[0]
    T = cfg.T
    qt, kt, ft = cfg.k_major
    npairs = qt.shape[0]
    c = scale * LOG2E
    decay = cr2 is not None
    W = hs * LANES

    def body(qt_ref, kt_ref, ft_ref, *refs):
        it = iter(refs)
        q_ref, k_ref, v_ref = next(it), next(it), next(it)
        cr_ref = next(it) if decay else None
        do_ref, lse_ref, dl_ref = next(it), next(it), next(it)
        dk_ref, dv_ref = next(it), next(it)
        dc_ref = next(it) if decay else None
        dq_ref = next(it) if fused else None
        dk_sc, dv_sc = next(it), next(it)
        dc_sc = next(it) if decay else None
        n = pl.program_id(1)
        qb, kb, f = qt_ref[n], kt_ref[n], ft_ref[n]

        if fused:
            @pl.when(n == 0)
            def _():
                dq_ref[...] = jnp.zeros_like(dq_ref)

        @pl.when((f & 1) != 0)
        def _():
            dk_sc[...] = jnp.zeros_like(dk_sc)
            dv_sc[...] = jnp.zeros_like(dv_sc)
            if decay:
                dc_sc[...] = jnp.zeros_like(dc_sc)

        def step(masked):
            k, v = _bf(k_ref[...]), _bf(v_ref[...])
            cr = cr_ref[0] if decay else None
            for i in range(hs):
                cols = slice(i * LANES, (i + 1) * LANES)
                q, dob = _bf(q_ref[:, cols]), _bf(do_ref[:, cols])
                s, live = _scores(q, k, c, cr, masked, cfg.mask, T, qb, kb, True)
                p = _prob(s, lse_ref[:, i * LANES:i * LANES + 1], live)
                dv_sc[...] += lax.dot_general(p.astype(BF16), dob, TN, preferred_element_type=F32)
                dp = lax.dot_general(dob, v, NT, preferred_element_type=F32)
                ds = p * (dp - dl_ref[:, i * LANES:i * LANES + 1])
                dsb = ds.astype(BF16)
                dk_sc[...] += lax.dot_general(dsb, q, TN, preferred_element_type=F32)
                if fused:
                    rows = pl.ds(pl.multiple_of(qb * T, T), T)
                    dq_ref[rows, :] += lax.dot_general(dsb, k, NN, preferred_element_type=F32)
                if decay:
                    dc_sc[...] += jnp.sum(ds, axis=0, keepdims=True)

        _both(f & 4, step)

        @pl.when((f & 2) != 0)
        def _():
            dk_ref[...] = (dk_sc[...] * scale).astype(out_dtype)
            dv_ref[...] = dv_sc[...].astype(out_dtype)
            if decay:
                dc_ref[0] = -dc_sc[...]

        if fused:
            @pl.when(n == npairs - 1)
            def _():
                dq_ref[...] = dq_ref[...] * scale

    qrow = lambda off: pl.BlockSpec((T, W), lambda h, n, qt, kt, ft: (qt[n], off // hs + h))
    krow = lambda off: pl.BlockSpec((T, LANES), lambda h, n, qt, kt, ft: (kt[n], off + h))
    ins, specs = [qa, ka, va], [qrow(q_off), krow(k_off), krow(v_off)]
    if decay:
        ins.append(cr2)
        specs.append(pl.BlockSpec((1, 1, T), lambda h, n, qt, kt, ft: (h, 0, kt[n])))
    ins += [do, lse2, delta]
    specs += [qrow(0), qrow(0), qrow(0)]
    outs = [jax.ShapeDtypeStruct((L, HKV * LANES), out_dtype)] * 2
    ospecs = [krow(0), krow(0)]
    scratch = [pltpu.VMEM((T, LANES), F32), pltpu.VMEM((T, LANES), F32)]
    if decay:
        outs.append(jax.ShapeDtypeStruct((HKV, 1, L), F32))
        ospecs.append(pl.BlockSpec((1, 1, T), lambda h, n, qt, kt, ft: (h, 0, kt[n])))
        scratch.append(pltpu.VMEM((1, T), F32))
    if fused:
        outs.append(jax.ShapeDtypeStruct((L, HKV * LANES), F32))
        ospecs.append(pl.BlockSpec((L, LANES), lambda h, n, qt, kt, ft: (0, h)))
    return pl.pallas_call(
        body, name=name, out_shape=outs,
        grid_spec=pltpu.PrefetchScalarGridSpec(num_scalar_prefetch=3, grid=(HKV, npairs), in_specs=specs,
                                               out_specs=ospecs, scratch_shapes=scratch),
        compiler_params=_params("parallel", "arbitrary"),
    )(qt, kt, ft, *ins)


def _delta(do, o, name):
    L, HW = do.shape
    T = _pick(L, (640, 128))

    def body(do_ref, o_ref, d_ref):
        d_ref[...] = jnp.broadcast_to(jnp.sum(do_ref[...].astype(F32) * o_ref[...], axis=1, keepdims=True), (T, LANES))

    spec = pl.BlockSpec((T, LANES), lambda h, i: (i, h))
    return pl.pallas_call(body, name=name, grid=(HW // LANES, L // T), in_specs=[spec, spec], out_specs=spec,
                          out_shape=jax.ShapeDtypeStruct((L, HW), F32),
                          compiler_params=_params("parallel", "parallel"))(do, o)


def _sink_grad(lse2, delta, sink2, name):
    L, HW = lse2.shape
    tr = 128

    def body(lse_ref, dl_ref, s_ref, o_ref):
        @pl.when(pl.program_id(0) == 0)
        def _():
            o_ref[...] = jnp.zeros_like(o_ref)

        o_ref[...] -= jnp.sum(jnp.exp2(s_ref[...] - lse_ref[...]) * dl_ref[...], axis=0, keepdims=True)

    row = pl.BlockSpec((tr, HW), lambda i: (i, 0))
    vec = pl.BlockSpec((1, HW), lambda i: (0, 0))
    return pl.pallas_call(body, name=name, grid=(L // tr,), in_specs=[row, row, vec], out_specs=vec,
                          out_shape=jax.ShapeDtypeStruct((1, HW), F32), compiler_params=_params("arbitrary"))(
        lse2, delta, sink2)


def _tri(lower):
    r = lax.broadcasted_iota(jnp.int32, (BLOCK, BLOCK), 0)
    c = lax.broadcasted_iota(jnp.int32, (BLOCK, BLOCK), 1)
    return jnp.where((c <= r) if lower else (c >= r), 1.0, 0.0).astype(F32)


def _gate_cumsum(proj, fg_tile, b_pad, name):
    L = proj.shape[0]

    def body(fg_ref, b_ref, c_ref, carry):
        @pl.when(pl.program_id(0) == 0)
        def _():
            carry[...] = jnp.zeros_like(carry)

        x = fg_ref[...] + b_ref[...]
        lf = jnp.minimum(x, 0.0) - jnp.log(1.0 + jnp.exp(-jnp.abs(x)))
        c = jnp.dot(_tri(True), lf, precision=lax.Precision.HIGHEST, preferred_element_type=F32) + carry[...]
        c_ref[...] = c
        carry[...] = c[BLOCK - 1:BLOCK, :]

    return pl.pallas_call(
        body, name=name, grid=(L // BLOCK,),
        in_specs=[pl.BlockSpec((BLOCK, LANES), lambda i: (i, fg_tile)), pl.BlockSpec((1, LANES), lambda i: (0, 0))],
        out_specs=pl.BlockSpec((BLOCK, LANES), lambda i: (i, 0)),
        out_shape=jax.ShapeDtypeStruct((L, LANES), F32),
        scratch_shapes=[pltpu.VMEM((1, LANES), F32)],
        compiler_params=_params("arbitrary"),
    )(proj, b_pad)


def _gate_cumsum_bwd(dc, proj, fg_tile, b_pad, name):
    L = proj.shape[0]
    nb = L // BLOCK

    def body(dc_ref, fg_ref, b_ref, dfg_ref, db_ref, carry):
        @pl.when(pl.program_id(0) == 0)
        def _():
            carry[...] = jnp.zeros_like(carry)
            db_ref[...] = jnp.zeros_like(db_ref)

        dlf = jnp.dot(_tri(False), dc_ref[...], precision=lax.Precision.HIGHEST,
                      preferred_element_type=F32) + carry[...]
        carry[...] = dlf[0:1, :]
        x = fg_ref[...] + b_ref[...]
        lanes = lax.broadcasted_iota(jnp.int32, (BLOCK, LANES), 1)
        rows = (nb - 1 - pl.program_id(0)) * BLOCK + lax.broadcasted_iota(jnp.int32, (BLOCK, LANES), 0)
        dfg = jnp.where((lanes < HEADS) & (rows >= PAD), dlf / (1.0 + jnp.exp(x)), 0.0)
        dfg_ref[...] = jnp.concatenate([dfg, jnp.zeros_like(dfg)], axis=1)
        db_ref[...] += jnp.sum(dfg, axis=0, keepdims=True)

    dfg, db = pl.pallas_call(
        body, name=name, grid=(nb,),
        in_specs=[pl.BlockSpec((BLOCK, LANES), lambda i: (nb - 1 - i, 0)),
                  pl.BlockSpec((BLOCK, LANES), lambda i: (nb - 1 - i, fg_tile)),
                  pl.BlockSpec((1, LANES), lambda i: (0, 0))],
        out_specs=[pl.BlockSpec((BLOCK, 2 * LANES), lambda i: (nb - 1 - i, 0)),
                   pl.BlockSpec((1, LANES), lambda i: (0, 0))],
        out_shape=[jax.ShapeDtypeStruct((L, 2 * LANES), F32), jax.ShapeDtypeStruct((1, LANES), F32)],
        scratch_shapes=[pltpu.VMEM((1, LANES), F32)],
        compiler_params=_params("arbitrary"),
    )(dc, proj, b_pad)
    return dfg, db[0]


def _rope_tables(L, dim, theta, lane0):
    half = dim // 2
    pos = (jnp.arange(L) - PAD).astype(F32)
    inv = theta ** (-jnp.arange(0, dim, 2, dtype=F32) / dim)
    ang = pos[:, None] * inv[None, :]
    cos, sin = jnp.cos(ang), jnp.sin(ang)
    C = jnp.ones((L, LANES), F32).at[:, lane0:lane0 + half].set(cos).at[:, lane0 + half:lane0 + dim].set(cos)
    S1 = jnp.zeros((L, LANES), F32).at[:, lane0:lane0 + half].set(-sin)
    S2 = jnp.zeros((L, LANES), F32).at[:, lane0 + half:lane0 + dim].set(sin)
    return C, S1, S2


def _rot(x, C, S1, S2, R):
    return x * C + pltpu.roll(x, LANES - R, 1) * S1 + pltpu.roll(x, R, 1) * S2


def _rot_t(dy, C, S1, S2, R):
    return dy * C + pltpu.roll(dy * S1, R, 1) + pltpu.roll(dy * S2, LANES - R, 1)


def _rope(x, nt, tabs, R, name, transpose=False, shared=None, shared_tile=0, out_dtype=F32):
    L = x.shape[0]
    T = _pick(L, (640, 128))
    fn = _rot_t if transpose else _rot

    def body(*refs):
        if shared is None:
            x_ref, c_ref, s1_ref, s2_ref, o_ref = refs
            o_ref[...] = fn(x_ref[...], c_ref[...], s1_ref[...], s2_ref[...], R).astype(out_dtype)
        else:
            x_ref, sh_ref, c_ref, s1_ref, s2_ref, o_ref = refs
            o_ref[...] = (x_ref[...] + fn(sh_ref[...], c_ref[...], s1_ref[...], s2_ref[...], R)).astype(out_dtype)

    tile = pl.BlockSpec((T, LANES), lambda h, i: (i, h))
    tab = pl.BlockSpec((T, LANES), lambda h, i: (i, 0))
    ins, specs = [x], [tile]
    if shared is not None:
        ins.append(shared)
        specs.append(pl.BlockSpec((T, LANES), lambda h, i: (i, shared_tile)))
    return pl.pallas_call(body, name=name, grid=(nt, L // T), in_specs=specs + [tab, tab, tab], out_specs=tile,
                          out_shape=jax.ShapeDtypeStruct((L, nt * LANES), out_dtype),
                          compiler_params=_params("parallel", "parallel"))(*ins, *tabs)


def _rope_shared_bwd(dk, nt, tabs, R, name):
    L = dk.shape[0]
    tr = 128

    def body(dk_ref, c_ref, s1_ref, s2_ref, o_ref):
        acc = dk_ref[:, 0:LANES]
        for h in range(1, nt):
            acc = acc + dk_ref[:, h * LANES:(h + 1) * LANES]
        o_ref[...] = _rot_t(acc, c_ref[...], s1_ref[...], s2_ref[...], R)

    tab = pl.BlockSpec((tr, LANES), lambda i: (i, 0))
    return pl.pallas_call(body, name=name, grid=(L // tr,),
                          in_specs=[pl.BlockSpec((tr, nt * LANES), lambda i: (i, 0)), tab, tab, tab], out_specs=tab,
                          out_shape=jax.ShapeDtypeStruct((L, LANES), F32), compiler_params=_params("parallel"))(
        dk, *tabs)


def _rms_fwd(pa, gq, gkv, name):
    L = pa.shape[0]
    tr = 128
    Q, KV = MLA_Q_LORA, MLA_KV_LORA

    def body(pa_ref, gq_ref, gkv_ref, q_ref, kv_ref):
        for lo, n, g_ref, o_ref in ((0, Q, gq_ref, q_ref), (Q, KV, gkv_ref, kv_ref)):
            x = pa_ref[:, lo:lo + n]
            r = lax.rsqrt(jnp.mean(x * x, axis=1, keepdims=True) + RMS_EPS)
            o_ref[...] = (x * r * g_ref[...]).astype(BF16)

    return pl.pallas_call(
        body, name=name, grid=(L // tr,),
        in_specs=[pl.BlockSpec((tr, pa.shape[1]), lambda i: (i, 0)), pl.BlockSpec((1, Q), lambda i: (0, 0)),
                  pl.BlockSpec((1, KV), lambda i: (0, 0))],
        out_specs=[pl.BlockSpec((tr, Q), lambda i: (i, 0)), pl.BlockSpec((tr, KV), lambda i: (i, 0))],
        out_shape=[jax.ShapeDtypeStruct((L, Q), BF16), jax.ShapeDtypeStruct((L, KV), BF16)],
        compiler_params=_params("parallel"),
    )(pa, gq.reshape(1, Q), gkv.reshape(1, KV))


def _rms_bwd(pa, dq, dkv, dkr, gq, gkv, name):
    L, W = pa.shape
    tr = 128
    Q, KV = MLA_Q_LORA, MLA_KV_LORA

    def body(pa_ref, dq_ref, dkv_ref, dkr_ref, gq_ref, gkv_ref, dpa_ref, dgq_ref, dgkv_ref):
        @pl.when(pl.program_id(0) == 0)
        def _():
            dgq_ref[...] = jnp.zeros_like(dgq_ref)
            dgkv_ref[...] = jnp.zeros_like(dgkv_ref)

        for lo, n, g_ref, dy_ref, dg_ref in ((0, Q, gq_ref, dq_ref, dgq_ref), (Q, KV, gkv_ref, dkv_ref, dgkv_ref)):
            x = pa_ref[:, lo:lo + n]
            r = lax.rsqrt(jnp.mean(x * x, axis=1, keepdims=True) + RMS_EPS)
            xh = x * r
            dy = dy_ref[...]
            dxh = dy * g_ref[...]
            dpa_ref[:, lo:lo + n] = (r * (dxh - xh * jnp.mean(dxh * xh, axis=1, keepdims=True))).astype(BF16)
            dg_ref[...] += jnp.sum(dy * xh, axis=0, keepdims=True)
        dpa_ref[:, Q + KV:W] = dkr_ref[...].astype(BF16)

    vq = pl.BlockSpec((1, Q), lambda i: (0, 0))
    vkv = pl.BlockSpec((1, KV), lambda i: (0, 0))
    dpa, dgq, dgkv = pl.pallas_call(
        body, name=name, grid=(L // tr,),
        in_specs=[pl.BlockSpec((tr, W), lambda i: (i, 0)), pl.BlockSpec((tr, Q), lambda i: (i, 0)),
                  pl.BlockSpec((tr, KV), lambda i: (i, 0)), pl.BlockSpec((tr, LANES), lambda i: (i, 0)), vq, vkv],
        out_specs=[pl.BlockSpec((tr, W), lambda i: (i, 0)), vq, vkv],
        out_shape=[jax.ShapeDtypeStruct((L, W), BF16), jax.ShapeDtypeStruct((1, Q), F32),
                   jax.ShapeDtypeStruct((1, KV), F32)],
        compiler_params=_params("arbitrary"),
    )(pa, dq, dkv, dkr, gq.reshape(1, Q), gkv.reshape(1, KV))
    return dpa, dgq[0], dgkv[0]


def _loss_head(h, target, name):
    L = h.shape[0]
    tr = BLOCK
    inv = 1.0 / D_MODEL

    def body(h_ref, t_ref, loss_ref, dh_ref):
        i = pl.program_id(0)

        @pl.when(i == 0)
        def _():
            loss_ref[...] = jnp.zeros_like(loss_ref)
            dh_ref[...] = jnp.zeros_like(dh_ref)

        @pl.when(i > 0)
        def _():
            e = h_ref[...] - t_ref[...]
            dh_ref[...] = e * inv
            loss_ref[...] += jnp.sum((e * e).reshape(tr // 8, 8, D_MODEL), axis=0) * (0.5 * inv)

    row = pl.BlockSpec((tr, D_MODEL), lambda i: (i, 0))
    loss, dh = pl.pallas_call(
        body, name=name, grid=(L // tr,),
        in_specs=[row, pl.BlockSpec((tr, D_MODEL), lambda i: (jnp.maximum(i - 1, 0), 0))],
        out_specs=[pl.BlockSpec((8, D_MODEL), lambda i: (0, 0)), row],
        out_shape=[jax.ShapeDtypeStruct((8, D_MODEL), F32), jax.ShapeDtypeStruct((L, D_MODEL), F32)],
        compiler_params=_params("arbitrary"),
    )(h, target)
    return loss, dh


def _pad_heads_cols(w, nh, d, dp=LANES):
    K = w.shape[0]
    return jnp.pad(w.reshape(K, nh, d), ((0, 0), (0, 0), (0, dp - d))).reshape(K, nh * dp)


def _unpad_heads_cols(w, nh, d, dp=LANES):
    K = w.shape[0]
    return w.reshape(K, nh, dp)[:, :, :d].reshape(K, nh * d)


def _pad_heads_rows(w, nh, d):
    N = w.shape[1]
    return jnp.pad(w.reshape(nh, d, N), ((0, 0), (0, LANES - d), (0, 0))).reshape(nh * LANES, N)


def _unpad_heads_rows(w, nh, d):
    N = w.shape[1]
    return w.reshape(nh, LANES, N)[:, :d, :].reshape(nh * d, N)


def _fox_fwd(h, w_in, b_f, w_o, tag):
    L = h.shape[0]
    hd = HEADS * HEAD_DIM
    W = jnp.concatenate([_pad_heads_cols(w_in[:, i * hd:(i + 1) * hd], HEADS, HEAD_DIM) for i in range(3)]
                        + [jnp.pad(w_in[:, 3 * hd:], ((0, 0), (0, 2 * LANES - HEADS)))], axis=1)
    Wo = _pad_heads_rows(w_o, HEADS, HEAD_DIM)
    b_pad = jnp.pad(b_f, (0, LANES - HEADS)).reshape(1, LANES)
    proj = _mm(h, W, "nn", tag + "_proj")
    c = _gate_cumsum(proj, 3 * HEADS, b_pad, tag + "_cumsum")
    dead = (jnp.arange(L) < PAD)[:, None]
    cr2 = jnp.where(dead, -NEG, c[:, :HEADS] * LOG2E).T.reshape(HEADS, 1, L)
    cfg = _Dense(L, True)
    scale = HEAD_DIM ** -0.5
    o, lse2 = _flash_q_major("fwd", cfg, proj, 0, proj, HEADS, proj, 2 * HEADS, HEADS, 1, scale, tag + "_attn", cr2=cr2)
    mix = _mm(o, Wo, "nn", tag + "_out")
    return mix, (h, W, Wo, b_pad, proj, cr2, o, lse2)


def _fox_bwd(dmix, res, tag):
    h, W, Wo, b_pad, proj, cr2, o, lse2 = res
    L = h.shape[0]
    cfg = _Dense(L, True)
    scale = HEAD_DIM ** -0.5
    dWo = _mm(o, dmix, "tn", tag + "_dwo")
    do = _mm(dmix, Wo, "nt", tag + "_do", out_dtype=BF16)
    qkv = (proj, 0, proj, HEADS, proj, 2 * HEADS)
    delta = _flash_q_major("delta", cfg, *qkv, HEADS, 1, scale, tag + "_delta", cr2=cr2, do=do, lse2=lse2)
    dk, dv, dcr, dq = _flash_k_major(cfg, *qkv, do, lse2, delta, HEADS, 1, scale, tag + "_bwd", cr2=cr2, fused=True,
                                     out_dtype=BF16)
    dc = jnp.pad(dcr.reshape(HEADS, L).T, ((0, 0), (0, LANES - HEADS)))
    dfg, db = _gate_cumsum_bwd(dc, proj, 3 * HEADS, b_pad, tag + "_cumsum_bwd")
    dproj = jnp.concatenate([dq.astype(BF16), dk, dv, dfg.astype(BF16)], axis=1)
    dW = _mm(h, dproj, "tn", tag + "_dw")
    dh = _mm(dproj, W, "nt", tag + "_dh")
    hp = HEADS * LANES
    dw_in = jnp.concatenate([_unpad_heads_cols(dW[:, i * hp:(i + 1) * hp], HEADS, HEAD_DIM) for i in range(3)]
                            + [dW[:, 3 * hp:3 * hp + HEADS]], axis=1)
    return dh, dict(w_in=dw_in, b_f=db[:HEADS], w_o=_unpad_heads_rows(dWo, HEADS, HEAD_DIM))


def _swa_fwd(h, w_in, sinks, w_o, tag):
    L = h.shape[0]
    qd, kd = HEADS * HEAD_DIM, SWA_KV * HEAD_DIM
    W = jnp.concatenate([_pad_heads_cols(w_in[:, :qd], HEADS, HEAD_DIM),
                         _pad_heads_cols(w_in[:, qd:qd + kd], SWA_KV, HEAD_DIM),
                         _pad_heads_cols(w_in[:, qd + kd:], SWA_KV, HEAD_DIM)], axis=1)
    Wo = _pad_heads_rows(w_o, HEADS, HEAD_DIM)
    sink2 = jnp.repeat(sinks * LOG2E, LANES).reshape(1, HEADS * LANES)
    tabs = _rope_tables(L, ROPE_DIM, ROPE_THETA, 0)
    proj = _mm(h, W, "nn", tag + "_proj")
    nqk = HEADS + SWA_KV
    qk = _rope(proj, nqk, tabs, ROPE_DIM // 2, tag + "_rope")
    cfg = _Swa(L)
    scale = HEAD_DIM ** -0.5
    o, lse2 = _flash_q_major("fwd", cfg, qk, 0, qk, HEADS, proj, nqk, HEADS, SWA_G, scale, tag + "_attn", sink2=sink2)
    mix = _mm(o, Wo, "nn", tag + "_out")
    return mix, (h, W, Wo, sink2, tabs, proj, qk, o, lse2)


def _swa_bwd(dmix, res, tag):
    h, W, Wo, sink2, tabs, proj, qk, o, lse2 = res
    L = h.shape[0]
    nqk = HEADS + SWA_KV
    cfg = _Swa(L)
    scale = HEAD_DIM ** -0.5
    dWo = _mm(o, dmix, "tn", tag + "_dwo")
    do = _mm(dmix, Wo, "nt", tag + "_do", out_dtype=BF16)
    delta = _delta(do, o, tag + "_delta")
    dsink = _sink_grad(lse2, delta, sink2, tag + "_dsink")[0, ::LANES]
    qkv = (qk, 0, qk, HEADS, proj, nqk)
    dq = _flash_q_major("dq", cfg, *qkv, HEADS, SWA_G, scale, tag + "_dq", do=do, lse2=lse2, delta=delta)
    dk, dv = _flash_k_major(cfg, *qkv, do, lse2, delta, SWA_KV, SWA_G, scale, tag + "_dkv")
    dqk = _rope(jnp.concatenate([dq, dk], axis=1), nqk, tabs, ROPE_DIM // 2, tag + "_rope_bwd", transpose=True,
                out_dtype=BF16)
    dproj = jnp.concatenate([dqk, dv.astype(BF16)], axis=1)
    dW = _mm(h, dproj, "tn", tag + "_dw")
    dh = _mm(dproj, W, "nt", tag + "_dh")
    hp = HEADS * LANES
    dw_in = jnp.concatenate([_unpad_heads_cols(dW[:, :hp], HEADS, HEAD_DIM),
                             _unpad_heads_cols(dW[:, hp:hp + SWA_KV * LANES], SWA_KV, HEAD_DIM),
                             _unpad_heads_cols(dW[:, hp + SWA_KV * LANES:], SWA_KV, HEAD_DIM)], axis=1)
    return dh, dict(w_in=dw_in, sinks=dsink, w_o=_unpad_heads_rows(dWo, HEADS, HEAD_DIM))


def _mla_fwd(h, w_a, g_q, g_kv, w_uq, w_ukv, w_o, tag):
    L = h.shape[0]
    Q, KV = MLA_Q_LORA, MLA_KV_LORA
    dqk = MLA_NOPE + MLA_ROPE
    kr_w = jnp.pad(w_a[:, Q + KV:], ((0, 0), (MLA_NOPE, LANES - dqk)))
    Wa = jnp.concatenate([w_a[:, :Q + KV], kr_w], axis=1)
    Wuq = _pad_heads_cols(w_uq, HEADS, dqk)
    ukv = w_ukv.reshape(KV, HEADS, MLA_NOPE + HEAD_DIM)
    Wukv = jnp.concatenate([_pad_heads_cols(ukv[:, :, :MLA_NOPE].reshape(KV, -1), HEADS, MLA_NOPE),
                            _pad_heads_cols(ukv[:, :, MLA_NOPE:].reshape(KV, -1), HEADS, HEAD_DIM)], axis=1)
    Wo = _pad_heads_rows(w_o, HEADS, HEAD_DIM)
    tabs = _rope_tables(L, MLA_ROPE, MLA_ROPE_THETA, MLA_NOPE)
    R = MLA_ROPE // 2
    pa = _mm(h, Wa, "nn", tag + "_proj")
    cqn, ckvn = _rms_fwd(pa, g_q, g_kv, tag + "_rms")
    q0 = _mm(cqn, Wuq, "nn", tag + "_uq")
    qr = _rope(q0, HEADS, tabs, R, tag + "_rope_q")
    kv0 = _mm(ckvn, Wukv, "nn", tag + "_ukv")
    kk = _rope(kv0, HEADS, tabs, R, tag + "_rope_k", shared=pa, shared_tile=(Q + KV) // LANES)
    cfg = _Dense(L, False)
    scale = dqk ** -0.5
    o, lse2 = _flash_q_major("fwd", cfg, qr, 0, kk, 0, kv0, HEADS, HEADS, 1, scale, tag + "_attn")
    mix = _mm(o, Wo, "nn", tag + "_out")
    return mix, (h, Wa, Wuq, Wukv, Wo, g_q, g_kv, tabs, pa, cqn, ckvn, qr, kk, kv0, o, lse2)


def _mla_bwd(dmix, res, tag):
    h, Wa, Wuq, Wukv, Wo, g_q, g_kv, tabs, pa, cqn, ckvn, qr, kk, kv0, o, lse2 = res
    L = h.shape[0]
    Q, KV = MLA_Q_LORA, MLA_KV_LORA
    dqk = MLA_NOPE + MLA_ROPE
    R = MLA_ROPE // 2
    cfg = _Dense(L, False)
    scale = dqk ** -0.5
    dWo = _mm(o, dmix, "tn", tag + "_dwo")
    do = _mm(dmix, Wo, "nt", tag + "_do", out_dtype=BF16)
    delta = _delta(do, o, tag + "_delta")
    dk, dv, dqr = _flash_k_major(cfg, qr, 0, kk, 0, kv0, HEADS, do, lse2, delta, HEADS, 1, scale, tag + "_bwd",
                                 fused=True)
    dq0 = _rope(dqr, HEADS, tabs, R, tag + "_rope_q_bwd", transpose=True, out_dtype=BF16)
    dWuq = _mm(cqn, dq0, "tn", tag + "_dwuq")
    dcqn = _mm(dq0, Wuq, "nt", tag + "_dcq")
    dkv = jnp.concatenate([dk, dv], axis=1).astype(BF16)
    dWukv = _mm(ckvn, dkv, "tn", tag + "_dwukv")
    dckvn = _mm(dkv, Wukv, "nt", tag + "_dckv")
    dkr = _rope_shared_bwd(dk, HEADS, tabs, R, tag + "_rope_k_bwd")
    dpa, dgq, dgkv = _rms_bwd(pa, dcqn, dckvn, dkr, g_q, g_kv, tag + "_rms_bwd")
    dWa = _mm(h, dpa, "tn", tag + "_dw")
    dh = _mm(dpa, Wa, "nt", tag + "_dh")
    hp = HEADS * LANES
    dw_a = jnp.concatenate([dWa[:, :Q + KV], dWa[:, Q + KV + MLA_NOPE:Q + KV + dqk]], axis=1)
    dk_n = dWukv[:, :hp].reshape(KV, HEADS, LANES)[:, :, :MLA_NOPE]
    dv_n = dWukv[:, hp:].reshape(KV, HEADS, LANES)[:, :, :HEAD_DIM]
    dw_ukv = jnp.concatenate([dk_n, dv_n], axis=2).reshape(KV, HEADS * (MLA_NOPE + HEAD_DIM))
    return dh, dict(w_a=dw_a, g_q=dgq, g_kv=dgkv, w_uq=_unpad_heads_cols(dWuq, HEADS, dqk), w_ukv=dw_ukv,
                    w_o=_unpad_heads_rows(dWo, HEADS, HEAD_DIM))


MATMUL_WEIGHTS = ("fox_w_in", "fox_w_o", "swa_w_in", "swa_w_o", "mla_w_a", "mla_w_uq", "mla_w_ukv", "mla_w_o",
                  "ffn_w_in", "ffn_w_out")


def _local_step(x, target, w):
    w = {k: (_bf(v) if k in MATMUL_WEIGHTS else v) for k, v in w.items()}
    h = jnp.concatenate([jnp.zeros((PAD, D_MODEL), F32), w["meta_tokens"], x], axis=0)
    hb = h.astype(BF16)
    saved = []
    for i in range(DEPTH):
        kind, j = i % 3, i // 3
        tag = "l%d" % i
        if kind == 0:
            mix, mres = _fox_fwd(hb, w["fox_w_in"][j], w["fox_b_f"][j], w["fox_w_o"][j], tag + "_fox")
        elif kind == 1:
            mix, mres = _swa_fwd(hb, w["swa_w_in"][j], w["swa_sinks"][j], w["swa_w_o"][j], tag + "_swa")
        else:
            mix, mres = _mla_fwd(hb, w["mla_w_a"][j], w["mla_g_q"][j], w["mla_g_kv"][j], w["mla_w_uq"][j],
                                 w["mla_w_ukv"][j], w["mla_w_o"][j], tag + "_mla")
        h1, h1b, xh1, rs1 = _ln_fwd(h, mix, w["ln1_g"][i], w["ln1_b"][i], tag + "_ln1")
        u = _mm(h1b, w["ffn_w_in"][i], "nn", tag + "_ffn_in")
        y, a = _conv_glu_fwd(u, w["ffn_conv_w"][i], w["ffn_conv_b"][i], tag + "_conv")
        ffn = _mm(a, w["ffn_w_out"][i], "nn", tag + "_ffn_out")
        h2, h2b, xh2, rs2 = _ln_fwd(h1, ffn, w["ln2_g"][i], w["ln2_b"][i], tag + "_ln2")
        saved.append((mres, xh1, rs1, h1b, u, y, a, xh2, rs2))
        h, hb = h2, h2b
    loss, dh = _loss_head(h, target, "loss_head")

    g = {k: [None] * v.shape[0] for k, v in w.items() if k != "meta_tokens"}
    ga = None
    for i in reversed(range(DEPTH)):
        kind, j = i % 3, i // 3
        tag = "l%d" % i
        mres, xh1, rs1, h1b, u, y, a, xh2, rs2 = saved[i]
        dz2, dz2b, g["ln2_g"][i], g["ln2_b"][i] = _ln_bwd(ga, dh, xh2, rs2, w["ln2_g"][i], tag + "_ln2_bwd")
        g["ffn_w_out"][i] = _mm(a, dz2b, "tn", tag + "_dw_out")
        da = _mm(dz2b, w["ffn_w_out"][i], "nt", tag + "_da")
        du, g["ffn_conv_w"][i], g["ffn_conv_b"][i] = _conv_glu_bwd(y, da, u, w["ffn_conv_w"][i], tag + "_conv_bwd")
        g["ffn_w_in"][i] = _mm(h1b, du, "tn", tag + "_dw_in")
        dh1 = _mm(du, w["ffn_w_in"][i], "nt", tag + "_dh1")
        dz1, dz1b, g["ln1_g"][i], g["ln1_b"][i] = _ln_bwd(dz2, dh1, xh1, rs1, w["ln1_g"][i], tag + "_ln1_bwd")
        if kind == 0:
            dh, mg = _fox_bwd(dz1b, mres, tag + "_fox")
            pre = "fox_"
        elif kind == 1:
            dh, mg = _swa_bwd(dz1b, mres, tag + "_swa")
            pre = "swa_"
        else:
            dh, mg = _mla_bwd(dz1b, mres, tag + "_mla")
            pre = "mla_"
        for k, v in mg.items():
            g[pre + k][j] = v
        ga = dz1
    dh0 = _axpy(ga, dh, "dh0")
    grads = {k: jnp.stack(v) for k, v in g.items()}
    grads["meta_tokens"] = dh0[PAD:BLOCK]
    return loss, dh0, grads


SHARDED = (("meta_tokens", 1), ("fox_w_in", 2), ("fox_w_o", 1), ("swa_w_in", 2), ("swa_w_o", 1), ("mla_w_a", 1),
           ("mla_g_q", 1), ("mla_g_kv", 1), ("mla_w_uq", 2), ("mla_w_ukv", 2), ("mla_w_o", 1), ("ffn_w_in", 2),
           ("ffn_conv_w", 2), ("ffn_w_out", 1))
REPLICATED = ("ln1_g", "ln1_b", "ln2_g", "ln2_b", "fox_b_f", "swa_sinks", "ffn_conv_b")
WEIGHTS = ("meta_tokens", "ln1_g", "ln1_b", "ln2_g", "ln2_b", "fox_w_in", "fox_b_f", "fox_w_o", "swa_w_in",
           "swa_sinks", "swa_w_o", "mla_w_a", "mla_g_q", "mla_g_kv", "mla_w_uq", "mla_w_ukv", "mla_w_o", "ffn_w_in",
           "ffn_conv_w", "ffn_conv_b", "ffn_w_out")


def _rows(n):
    return -(-n // ROW)


def _pack(arrs, multiple):
    parts = []
    for a in arrs:
        n = math.prod(a.shape)
        parts.append(jnp.pad(a.reshape(-1), (0, _rows(n) * ROW - n)).reshape(-1, ROW))
    total = sum(p.shape[0] for p in parts)
    pad = -total % multiple
    if pad:
        parts.append(jnp.zeros((pad, ROW), parts[0].dtype))
    return jnp.concatenate(parts, axis=0)


def _unpack(flat, shapes):
    out, r = [], 0
    for s in shapes:
        n = math.prod(s)
        out.append(flat[r:r + _rows(n)].reshape(-1)[:n].reshape(s))
        r += _rows(n)
    return out


def _pack_bf16(w, names):
    return _pack([_bf(w[n]) if n in MATMUL_WEIGHTS else lax.bitcast_convert_type(w[n], BF16) for n in names], 2 * ROW)


def _unpack_bf16(flat, names, shapes):
    sh = [s if n in MATMUL_WEIGHTS else s + (2,) for n, s in zip(names, shapes)]
    parts = _unpack(flat, sh)
    return [p if n in MATMUL_WEIGHTS else lax.bitcast_convert_type(p, F32) for n, p in zip(names, parts)]


HBM_SPEC = pl.BlockSpec(memory_space=pltpu.HBM)


def _place():
    x, y, c = lax.axis_index("x"), lax.axis_index("y"), lax.axis_index("c")
    chips = [(1 - x, y), (x, 1 - y), (1 - x, 1 - y)]
    return x, y, c, chips


def _gather_weights(shard):
    R = shard.shape[0]
    Rh = R // 2

    def body(s_ref, o_ref, send_sems, recv_sems, local_sem):
        x, y, c, chips = _place()
        sib = (x, y, 1 - c)

        def half(k, hc):
            return o_ref.at[k, pl.ds(hc * Rh, Rh), :]

        def copy(j, src, dst, to):
            return pltpu.make_async_remote_copy(src_ref=src, dst_ref=dst, send_sem=send_sems.at[j],
                                                recv_sem=recv_sems.at[j], device_id=to, device_id_type=MESH)

        me = 2 * x + y
        mine = pltpu.make_async_copy(s_ref, o_ref.at[me], local_sem)
        mine.start()
        first = [copy(j, s_ref.at[pl.ds(c * Rh, Rh), :], half(me, c), (tx, ty, c)) for j, (tx, ty) in enumerate(chips)]
        for cp in first:
            cp.start()
        passed = []
        for j, (tx, ty) in enumerate(chips):
            k = 2 * tx + ty
            copy(j, half(k, c), half(k, c), (tx, ty, c)).wait_recv()
            fw = copy(3 + j, half(k, c), half(k, c), sib)
            fw.start()
            passed.append(fw)
        for j, (tx, ty) in enumerate(chips):
            k = 2 * tx + ty
            copy(3 + j, half(k, 1 - c), half(k, 1 - c), sib).wait_recv()
        for cp in first + passed:
            cp.wait_send()
        mine.wait()

    return pl.pallas_call(
        body, name="gather_weights", out_shape=jax.ShapeDtypeStruct((N_CHIPS, R, ROW), shard.dtype),
        in_specs=[HBM_SPEC], out_specs=HBM_SPEC,
        scratch_shapes=[pltpu.SemaphoreType.DMA((6,)), pltpu.SemaphoreType.DMA((6,)), pltpu.SemaphoreType.DMA],
    )(shard)


def _swap_halves(G):
    R = G.shape[1]
    Rh = R // 2

    def body(g_ref, a_ref, send_sem, recv_sem):
        x, y, c, _ = _place()
        cp = pltpu.make_async_remote_copy(src_ref=g_ref.at[:, pl.ds((1 - c) * Rh, Rh), :], dst_ref=a_ref,
                                          send_sem=send_sem, recv_sem=recv_sem, device_id=(x, y, 1 - c),
                                          device_id_type=MESH)
        cp.start()
        cp.wait()

    return pl.pallas_call(
        body, name="reduce_swap_halves", out_shape=jax.ShapeDtypeStruct((N_CHIPS, Rh, ROW), G.dtype),
        in_specs=[HBM_SPEC], out_specs=HBM_SPEC,
        scratch_shapes=[pltpu.SemaphoreType.DMA, pltpu.SemaphoreType.DMA],
    )(G)


def _exchange_chips(P):
    def body(p_ref, b_ref, send_sems, recv_sems, local_sem):
        x, y, c, chips = _place()
        me = 2 * x + y

        def copy(j, src, dst, to):
            return pltpu.make_async_remote_copy(src_ref=src, dst_ref=dst, send_sem=send_sems.at[j],
                                                recv_sem=recv_sems.at[j], device_id=to, device_id_type=MESH)

        mine = pltpu.make_async_copy(p_ref.at[me], b_ref.at[me], local_sem)
        mine.start()
        sends = [copy(j, p_ref.at[2 * tx + ty], b_ref.at[me], (tx, ty, c)) for j, (tx, ty) in enumerate(chips)]
        for cp in sends:
            cp.start()
        for j, (tx, ty) in enumerate(chips):
            k = 2 * tx + ty
            copy(j, p_ref.at[k], b_ref.at[k], (tx, ty, c)).wait_recv()
        for cp in sends:
            cp.wait_send()
        mine.wait()

    return pl.pallas_call(
        body, name="reduce_exchange_chips", out_shape=jax.ShapeDtypeStruct(P.shape, P.dtype),
        in_specs=[HBM_SPEC], out_specs=HBM_SPEC,
        scratch_shapes=[pltpu.SemaphoreType.DMA((3,)), pltpu.SemaphoreType.DMA((3,)), pltpu.SemaphoreType.DMA],
    )(P)


def _join_halves(Fh):
    Rh = Fh.shape[0]

    def body(f_ref, o_ref, send_sem, recv_sem, local_sem):
        x, y, c, _ = _place()
        mine = pltpu.make_async_copy(f_ref, o_ref.at[pl.ds(c * Rh, Rh), :], local_sem)
        mine.start()
        cp = pltpu.make_async_remote_copy(src_ref=f_ref, dst_ref=o_ref.at[pl.ds(c * Rh, Rh), :], send_sem=send_sem,
                                          recv_sem=recv_sem, device_id=(x, y, 1 - c), device_id_type=MESH)
        cp.start()
        pltpu.make_async_remote_copy(src_ref=f_ref, dst_ref=o_ref.at[pl.ds((1 - c) * Rh, Rh), :], send_sem=send_sem,
                                     recv_sem=recv_sem, device_id=(x, y, 1 - c), device_id_type=MESH).wait_recv()
        cp.wait_send()
        mine.wait()

    return pl.pallas_call(
        body, name="reduce_join_halves", out_shape=jax.ShapeDtypeStruct((2 * Rh, ROW), Fh.dtype),
        in_specs=[HBM_SPEC], out_specs=HBM_SPEC,
        scratch_shapes=[pltpu.SemaphoreType.DMA, pltpu.SemaphoreType.DMA, pltpu.SemaphoreType.DMA],
    )(Fh)


def _gather_small(v):
    m_per = v.shape[0]

    def body(x_ref, out_ref, send_sems, recv_sems, local_sem):
        x, y, c, chips = _place()
        me, sibling = (x, y, c), (x, y, 1 - c)

        def rows(px, py, pc):
            return out_ref.at[pl.ds((4 * px + 2 * py + pc) * m_per, m_per), :]

        def copy(k, block, to, src=None):
            return pltpu.make_async_remote_copy(src_ref=rows(*block) if src is None else src, dst_ref=rows(*block),
                                                send_sem=send_sems.at[k], recv_sem=recv_sems.at[k], device_id=to,
                                                device_id_type=MESH)

        mine = pltpu.make_async_copy(x_ref, rows(*me), local_sem)
        mine.start()
        first = [copy(0, me, sibling, src=x_ref)]
        first += [copy(1 + j, me, (*chip, c), src=x_ref) for j, chip in enumerate(chips)]
        for cp in first:
            cp.start()
        passed = [copy(4 + j, (*chip, c), sibling) for j, chip in enumerate(chips)]
        for j, chip in enumerate(chips):
            copy(1 + j, (*chip, c), me).wait_recv()
            passed[j].start()
        copy(0, sibling, me).wait_recv()
        for j, chip in enumerate(chips):
            copy(4 + j, (*chip, 1 - c), me).wait_recv()
        for cp in first + passed:
            cp.wait_send()
        mine.wait()

    return pl.pallas_call(
        body, name="gather_small", out_shape=jax.ShapeDtypeStruct((N_DEV * m_per, ROW), v.dtype),
        in_specs=[pl.BlockSpec(memory_space=pltpu.VMEM)], out_specs=pl.BlockSpec(memory_space=pltpu.VMEM),
        scratch_shapes=[pltpu.SemaphoreType.DMA((7,)), pltpu.SemaphoreType.DMA((7,)), pltpu.SemaphoreType.DMA],
    )(v)


def _sum_slots(a, n, name):
    M = a.shape[0] // n
    tr = _pick(M, (512, 256, 128, 64, 40, 8))
    nb = M // tr

    def body(*refs):
        acc = refs[0][...]
        for r in refs[1:-1]:
            acc = acc + r[...]
        refs[-1][...] = acc

    specs = [pl.BlockSpec((tr, ROW), functools.partial(lambda i, k: (k * nb + i, 0), k=k)) for k in range(n)]
    return pl.pallas_call(body, name=name, grid=(nb,), in_specs=specs,
                          out_specs=pl.BlockSpec((tr, ROW), lambda i: (i, 0)),
                          out_shape=jax.ShapeDtypeStruct((M, ROW), F32), compiler_params=_params("parallel"))(*([a] * n))


def _add(a, b, name):
    M = a.shape[0]
    tr = _pick(M, (512, 256, 128, 64, 40, 8))

    def body(a_ref, b_ref, o_ref):
        o_ref[...] = a_ref[...] + b_ref[...]

    row = pl.BlockSpec((tr, ROW), lambda i: (i, 0))
    return pl.pallas_call(body, name=name, grid=(M // tr,), in_specs=[row, row], out_specs=row,
                          out_shape=jax.ShapeDtypeStruct((M, ROW), F32), compiler_params=_params("parallel"))(a, b)


def _adamw(g, w, m, v, name):
    M = g.shape[0]
    tr = _pick(M, (512, 256, 128, 64, 40, 8))
    c1 = 1.0 - ADAM_B1 ** ADAM_STEP
    c2 = 1.0 - ADAM_B2 ** ADAM_STEP

    def body(g_ref, w_ref, m_ref, v_ref, d_ref, nm_ref, nv_ref):
        gg = g_ref[...]
        nm = ADAM_B1 * m_ref[...] + (1.0 - ADAM_B1) * gg
        nv = ADAM_B2 * v_ref[...] + (1.0 - ADAM_B2) * (gg * gg)
        nm_ref[...] = nm
        nv_ref[...] = nv
        d_ref[...] = -ADAM_LR * ((nm / c1) / (jnp.sqrt(nv / c2) + ADAM_EPS) + ADAM_WD * w_ref[...])

    row = pl.BlockSpec((tr, ROW), lambda i: (i, 0))
    shape = jax.ShapeDtypeStruct((M, ROW), F32)
    return pl.pallas_call(body, name=name, grid=(M // tr,), in_specs=[row] * 4, out_specs=[row] * 3,
                          out_shape=[shape] * 3, compiler_params=_params("parallel"))(g, w, m, v)


def kernel(x, meta_tokens, ln1_g, ln1_b, ln2_g, ln2_b, fox_w_in, fox_b_f, fox_w_o, swa_w_in, swa_sinks, swa_w_o, mla_w_a, mla_g_q, mla_g_kv, mla_w_uq, mla_w_ukv, mla_w_o, ffn_w_in, ffn_conv_w, ffn_conv_b, ffn_w_out, loss_target, m_meta_tokens, m_ln1_g, m_ln1_b, m_ln2_g, m_ln2_b, m_fox_w_in, m_fox_b_f, m_fox_w_o, m_swa_w_in, m_swa_sinks, m_swa_w_o, m_mla_w_a, m_mla_g_q, m_mla_g_kv, m_mla_w_uq, m_mla_w_ukv, m_mla_w_o, m_ffn_w_in, m_ffn_conv_w, m_ffn_conv_b, m_ffn_w_out, v_meta_tokens, v_ln1_g, v_ln1_b, v_ln2_g, v_ln2_b, v_fox_w_in, v_fox_b_f, v_fox_w_o, v_swa_w_in, v_swa_sinks, v_swa_w_o, v_mla_w_a, v_mla_g_q, v_mla_g_kv, v_mla_w_uq, v_mla_w_ukv, v_mla_w_o, v_ffn_w_in, v_ffn_conv_w, v_ffn_conv_b, v_ffn_w_out):
    given = dict(locals())
    w = {n: given[n] for n in WEIGHTS}
    m = {n: given["m_" + n] for n in WEIGHTS}
    v = {n: given["v_" + n] for n in WEIGHTS}
    sh_names = [n for n, _ in SHARDED]
    sh_shapes = [w[n].shape for n in sh_names]

    gathered = _gather_weights(_pack_bf16(w, sh_names))
    full = dict(w)
    per_chip = [_unpack_bf16(gathered[k], sh_names, sh_shapes) for k in range(N_CHIPS)]
    for t, (n, ax) in enumerate(SHARDED):
        full[n] = jnp.concatenate([per_chip[k][t] for k in range(N_CHIPS)], axis=ax)

    loss_part, dh0, grads = _local_step(x[0], loss_target[0], full)
    loss = lax.psum(jnp.sum(loss_part), ("x", "y", "c"))
    grad_x = dh0[BLOCK:][None]

    split = {n: jnp.split(grads[n], N_CHIPS, axis=ax) for n, ax in SHARDED}
    G = jnp.stack([_pack([split[n][k] for n in sh_names], 2 * ROW) for k in range(N_CHIPS)])
    Rh = G.shape[1] // 2
    c = lax.axis_index("c")
    mine = lax.dynamic_slice_in_dim(G, c * Rh, Rh, axis=1)
    P = _add(mine.reshape(N_CHIPS * Rh, ROW), _swap_halves(G).reshape(N_CHIPS * Rh, ROW), "reduce_pair_sum")
    B = _exchange_chips(P.reshape(N_CHIPS, Rh, ROW))
    Fh = _sum_slots(B.reshape(N_CHIPS * Rh, ROW), N_CHIPS, "reduce_chip_sum")
    Fg = _join_halves(Fh)
    d_s, m_s, v_s = _adamw(Fg, _pack([w[n] for n in sh_names], 2 * ROW), _pack([m[n] for n in sh_names], 2 * ROW),
                           _pack([v[n] for n in sh_names], 2 * ROW), "adamw_sharded")

    rp_shapes = [w[n].shape for n in REPLICATED]
    small = _gather_small(_pack([grads[n] for n in REPLICATED], 8))
    g_r = _sum_slots(small, N_DEV, "reduce_small_sum")
    d_r, m_r, v_r = _adamw(g_r, _pack([w[n] for n in REPLICATED], 8), _pack([m[n] for n in REPLICATED], 8),
                           _pack([v[n] for n in REPLICATED], 8), "adamw_replicated")

    out = {}
    for kind, fs, fr in (("grad", Fg, g_r), ("delta", d_s, d_r), ("new_m", m_s, m_r), ("new_v", v_s, v_r)):
        for n, a in zip(sh_names, _unpack(fs, sh_shapes)):
            out[kind, n] = a
        for n, a in zip(REPLICATED, _unpack(fr, rp_shapes)):
            out[kind, n] = a
    return (loss, grad_x, *[out[k, n] for k in ("grad", "delta", "new_m", "new_v") for n in WEIGHTS])
```

```python
import functools
import math

import numpy as np
import jax
import jax.numpy as jnp
from jax import lax
from jax.experimental import pallas as pl
from jax.experimental.pallas import tpu as pltpu

F32 = jnp.float32
BF16 = jnp.bfloat16

D_MODEL = 1024
DEPTH = 4
BLOCK = 128
N_META = 16
PAD = BLOCK - N_META
NEG = -1e30
ALPHA = (2.0 * DEPTH) ** 0.25
LN_EPS = 1e-5
RMS_EPS = 1e-6
HEADS = 16
HEAD_DIM = 64
LANES = 128
SWA_KV = 2
SWA_G = HEADS // SWA_KV
WINDOW = 128
ROPE_THETA = 500000.0
ROPE_DIM = 16
MLA_Q_LORA = 384
MLA_KV_LORA = 256
MLA_NOPE = 64
MLA_ROPE = 32
MLA_ROPE_THETA = 10000.0
D_FF = 2816
ADAM_LR = 0.001
ADAM_B1 = 0.9
ADAM_B2 = 0.999
ADAM_EPS = 1e-08
ADAM_WD = 0.01
ADAM_STEP = 10
N_CHIPS = 4
N_DEV = 8
ROW = 1024
VMEM_LIMIT = 48 * 1024 * 1024
MESH = pl.DeviceIdType.MESH
LOG2E = 1.4426950408889634

NN = (((1,), (0,)), ((), ()))
NT = (((1,), (1,)), ((), ()))
TN = (((0,), (0,)), ((), ()))


def _pick(n, cands):
    for c in cands:
        if n % c == 0:
            return c
    return n


def _params(*sem):
    return pltpu.CompilerParams(dimension_semantics=sem, vmem_limit_bytes=VMEM_LIMIT)


def _bf(x):
    return x if x.dtype == BF16 else x.astype(BF16)


def _mm(a, b, mode, name, out_dtype=F32):
    if mode == "nn":
        (M, K), (_, N) = a.shape, b.shape
    elif mode == "nt":
        (M, K), (N, _) = a.shape, b.shape
    else:
        (K, M), (_, N) = a.shape, b.shape
    tm = _pick(M, (1664, 1408, 1024, 640, 512, 384, 256, 128))
    tn = _pick(N, (640, 512, 384, 1408, 256, 128))
    tk = K if (K <= 1024 and mode != "tn") else _pick(K, (640, 512, 384, 1408, 256, 128))
    nk = K // tk
    dn = {"nn": NN, "nt": NT, "tn": TN}[mode]

    def body(a_ref, b_ref, o_ref, *acc):
        part = lax.dot_general(_bf(a_ref[...]), _bf(b_ref[...]), dn, preferred_element_type=F32)
        if nk == 1:
            o_ref[...] = part.astype(out_dtype)
            return
        acc_ref, = acc
        k = pl.program_id(2)

        @pl.when(k == 0)
        def _():
            acc_ref[...] = part

        @pl.when(k > 0)
        def _():
            acc_ref[...] += part

        @pl.when(k == nk - 1)
        def _():
            o_ref[...] = acc_ref[...].astype(out_dtype)

    if mode == "tn":
        a_spec = pl.BlockSpec((tk, tm), lambda i, j, k: (k, i))
    else:
        a_spec = pl.BlockSpec((tm, tk), lambda i, j, k: (i, k))
    if mode == "nt":
        b_spec = pl.BlockSpec((tn, tk), lambda i, j, k: (j, k))
    else:
        b_spec = pl.BlockSpec((tk, tn), lambda i, j, k: (k, j))
    return pl.pallas_call(
        body, name=name, grid=(M // tm, N // tn, nk),
        in_specs=[a_spec, b_spec],
        out_specs=pl.BlockSpec((tm, tn), lambda i, j, k: (i, j)),
        out_shape=jax.ShapeDtypeStruct((M, N), out_dtype),
        scratch_shapes=[pltpu.VMEM((tm, tn), F32)] if nk > 1 else [],
        compiler_params=_params("parallel", "parallel", "arbitrary"),
    )(a, b)


def _ln_fwd(h, mix, g, b, name):
    L = h.shape[0]
    tr = 128

    def body(h_ref, m_ref, g_ref, b_ref, o_ref, ob_ref, xh_ref, rs_ref):
        z = ALPHA * h_ref[...] + m_ref[...]
        mu = jnp.mean(z, axis=1, keepdims=True)
        zc = z - mu
        var = jnp.mean(zc * zc, axis=1, keepdims=True)
        rstd = lax.rsqrt(var + LN_EPS)
        xh = zc * rstd
        xh_ref[...] = xh
        rs_ref[...] = rstd
        out = xh * g_ref[...] + b_ref[...]
        o_ref[...] = out
        ob_ref[...] = out.astype(BF16)

    row = pl.BlockSpec((tr, D_MODEL), lambda i: (i, 0))
    vec = pl.BlockSpec((1, D_MODEL), lambda i: (0, 0))
    return pl.pallas_call(
        body, name=name, grid=(L // tr,),
        in_specs=[row, row, vec, vec],
        out_specs=[row, row, row, pl.BlockSpec((tr, 1), lambda i: (i, 0))],
        out_shape=[jax.ShapeDtypeStruct((L, D_MODEL), F32), jax.ShapeDtypeStruct((L, D_MODEL), BF16),
                   jax.ShapeDtypeStruct((L, D_MODEL), F32), jax.ShapeDtypeStruct((L, 1), F32)],
        compiler_params=_params("parallel"),
    )(h, mix, g.reshape(1, D_MODEL), b.reshape(1, D_MODEL))


def _ln_bwd(ga, gb, xhat, rstd, g, name):
    L = xhat.shape[0]
    tr = 128
    two = ga is not None

    def body(*refs):
        if two:
            ga_ref, gb_ref, xh_ref, rs_ref, g_ref, dz_ref, dzb_ref, dg_ref, db_ref = refs
            dy = ALPHA * ga_ref[...] + gb_ref[...]
        else:
            gb_ref, xh_ref, rs_ref, g_ref, dz_ref, dzb_ref, dg_ref, db_ref = refs
            dy = gb_ref[...]
        xh = xh_ref[...]
        dxh = dy * g_ref[...]
        c1 = jnp.mean(dxh, axis=1, keepdims=True)
        c2 = jnp.mean(dxh * xh, axis=1, keepdims=True)
        dz = rs_ref[...] * (dxh - c1 - xh * c2)
        dz_ref[...] = dz
        dzb_ref[...] = dz.astype(BF16)

        @pl.when(pl.program_id(0) == 0)
        def _():
            dg_ref[...] = jnp.zeros_like(dg_ref)
            db_ref[...] = jnp.zeros_like(db_ref)

        dg_ref[...] += jnp.sum(dy * xh, axis=0, keepdims=True)
        db_ref[...] += jnp.sum(dy, axis=0, keepdims=True)

    row = pl.BlockSpec((tr, D_MODEL), lambda i: (i, 0))
    vec = pl.BlockSpec((1, D_MODEL), lambda i: (0, 0))
    ins = ([ga] if two else []) + [gb, xhat, rstd, g.reshape(1, D_MODEL)]
    specs = ([row] if two else []) + [row, row, pl.BlockSpec((tr, 1), lambda i: (i, 0)), vec]
    dz, dzb, dg, db = pl.pallas_call(
        body, name=name, grid=(L // tr,),
        in_specs=specs, out_specs=[row, row, vec, vec],
        out_shape=[jax.ShapeDtypeStruct((L, D_MODEL), F32), jax.ShapeDtypeStruct((L, D_MODEL), BF16),
                   jax.ShapeDtypeStruct((1, D_MODEL), F32), jax.ShapeDtypeStruct((1, D_MODEL), F32)],
        compiler_params=_params("arbitrary"),
    )(*ins)
    return dz, dzb, dg[0], db[0]


def _axpy(a, b, name):
    L, N = a.shape
    tr = 128

    def body(a_ref, b_ref, o_ref):
        o_ref[...] = ALPHA * a_ref[...] + b_ref[...]

    row = pl.BlockSpec((tr, N), lambda i: (i, 0))
    return pl.pallas_call(body, name=name, grid=(L // tr,), in_specs=[row, row], out_specs=row,
                          out_shape=jax.ShapeDtypeStruct((L, N), F32), compiler_params=_params("parallel"))(a, b)


def _shift_down(cur, prev8, n):
    rows = lax.broadcasted_iota(jnp.int32, cur.shape, 0)
    out = pltpu.roll(cur, n, 0)
    for r in range(n):
        out = jnp.where(rows == r, prev8[8 - n + r:8 - n + r + 1, :], out)
    return out


def _shift_up(cur, next8, n):
    tr = cur.shape[0]
    rows = lax.broadcasted_iota(jnp.int32, cur.shape, 0)
    out = pltpu.roll(cur, tr - n, 0)
    for r in range(n):
        out = jnp.where(rows == tr - n + r, next8[r:r + 1, :], out)
    return out


def _silu(x):
    return x / (1.0 + jnp.exp(-x))


def _conv_glu_fwd(u, cw, cb, name):
    L, F2 = u.shape
    F = F2 // 2
    tr = 128

    def body(u_ref, up_ref, cw_ref, cb_ref, y_ref, a_ref):
        i = pl.program_id(0)
        rows = i * tr + lax.broadcasted_iota(jnp.int32, (tr, F2), 0)
        cur = jnp.where(rows >= PAD, u_ref[...], 0.0)
        prow = i * tr - 8 + lax.broadcasted_iota(jnp.int32, (8, F2), 0)
        prev = jnp.where(prow >= PAD, up_ref[...], 0.0)
        y = cb_ref[...] + _shift_down(cur, prev, 2) * cw_ref[0:1, :]
        y = y + _shift_down(cur, prev, 1) * cw_ref[1:2, :]
        y = y + cur * cw_ref[2:3, :]
        y_ref[...] = y
        a_ref[...] = (_silu(y[:, :F]) * y[:, F:]).astype(BF16)

    return pl.pallas_call(
        body, name=name, grid=(L // tr,),
        in_specs=[pl.BlockSpec((tr, F2), lambda i: (i, 0)),
                  pl.BlockSpec((8, F2), lambda i: (jnp.maximum(i * (tr // 8) - 1, 0), 0)),
                  pl.BlockSpec((3, F2), lambda i: (0, 0)),
                  pl.BlockSpec((1, F2), lambda i: (0, 0))],
        out_specs=[pl.BlockSpec((tr, F2), lambda i: (i, 0)), pl.BlockSpec((tr, F), lambda i: (i, 0))],
        out_shape=[jax.ShapeDtypeStruct((L, F2), F32), jax.ShapeDtypeStruct((L, F), BF16)],
        compiler_params=_params("parallel"),
    )(u, u, cw, cb.reshape(1, F2))


def _conv_glu_bwd(y, da, u, cw, name):
    L, F2 = u.shape
    F = F2 // 2
    tr = 128
    nb = L // tr

    def dy_of(yv, dav):
        g, val = yv[:, :F], yv[:, F:]
        sg = 1.0 / (1.0 + jnp.exp(-g))
        dg = dav * val * (sg * (1.0 + g * (1.0 - sg)))
        dv = dav * (g * sg)
        return jnp.concatenate([dg, dv], axis=1)

    def body(y_ref, yn_ref, da_ref, dan_ref, u_ref, up_ref, cw_ref, du_ref, dcw_ref, dcb_ref):
        i = pl.program_id(0)
        rows = i * tr + lax.broadcasted_iota(jnp.int32, (tr, F2), 0)
        dy = dy_of(y_ref[...], da_ref[...])
        dyn = jnp.where(i < nb - 1, dy_of(yn_ref[...], dan_ref[...]), 0.0)
        du = dy * cw_ref[2:3, :] + _shift_up(dy, dyn, 1) * cw_ref[1:2, :] + _shift_up(dy, dyn, 2) * cw_ref[0:1, :]
        du_ref[...] = jnp.where(rows >= PAD, du, 0.0).astype(BF16)
        cur = jnp.where(rows >= PAD, u_ref[...], 0.0)
        prow = i * tr - 8 + lax.broadcasted_iota(jnp.int32, (8, F2), 0)
        prev = jnp.where(prow >= PAD, up_ref[...], 0.0)

        @pl.when(i == 0)
        def _():
            dcw_ref[...] = jnp.zeros_like(dcw_ref)
            dcb_ref[...] = jnp.zeros_like(dcb_ref)

        dcw_ref[0:1, :] += jnp.sum(dy * _shift_down(cur, prev, 2), axis=0, keepdims=True)
        dcw_ref[1:2, :] += jnp.sum(dy * _shift_down(cur, prev, 1), axis=0, keepdims=True)
        dcw_ref[2:3, :] += jnp.sum(dy * cur, axis=0, keepdims=True)
        dcb_ref[...] += jnp.sum(dy, axis=0, keepdims=True)

    nxt = lambda i: (jnp.minimum((i + 1) * (tr // 8), L // 8 - 1), 0)
    prv = lambda i: (jnp.maximum(i * (tr // 8) - 1, 0), 0)
    du, dcw, dcb = pl.pallas_call(
        body, name=name, grid=(nb,),
        in_specs=[pl.BlockSpec((tr, F2), lambda i: (i, 0)), pl.BlockSpec((8, F2), nxt),
                  pl.BlockSpec((tr, F), lambda i: (i, 0)), pl.BlockSpec((8, F), nxt),
                  pl.BlockSpec((tr, F2), lambda i: (i, 0)), pl.BlockSpec((8, F2), prv),
                  pl.BlockSpec((3, F2), lambda i: (0, 0))],
        out_specs=[pl.BlockSpec((tr, F2), lambda i: (i, 0)), pl.BlockSpec((3, F2), lambda i: (0, 0)),
                   pl.BlockSpec((1, F2), lambda i: (0, 0))],
        out_shape=[jax.ShapeDtypeStruct((L, F2), BF16), jax.ShapeDtypeStruct((3, F2), F32),
                   jax.ShapeDtypeStruct((1, F2), F32)],
        compiler_params=_params("arbitrary"),
    )(y, y, da, da, u, u, cw)
    return du, dcw, dcb[0]


def _dense_mask(qpos, kpos):
    return (kpos <= qpos) & (kpos >= PAD)


def _swa_mask(qpos, kpos):
    d = qpos - kpos
    return ((kpos >= BLOCK) & (d >= 0) & (d < WINDOW)) | ((kpos >= PAD) & (kpos < BLOCK) & (kpos <= qpos))


def _tables(pairs):
    qt, kt, ft = [], [], []
    for grp in pairs:
        for n, (qb, kb, msk) in enumerate(grp):
            qt.append(qb)
            kt.append(kb)
            ft.append((1 if n == 0 else 0) | (2 if n == len(grp) - 1 else 0) | (4 if msk else 0))
    return tuple(jnp.asarray(np.asarray(t, np.int32)) for t in (qt, kt, ft))


class _Dense:
    mask = staticmethod(_dense_mask)

    def __init__(self, L, pad_in_cr2):
        self.T = T = 640 if L % 640 == 0 else 128
        nb = L // T
        m = lambda qb, kb: kb == qb or (kb == 0 and not pad_in_cr2) or (qb * T < PAD)
        self.q_major = _tables([[(qb, kb, m(qb, kb)) for kb in range(qb + 1)] for qb in range(nb)])
        self.k_major = _tables([[(qb, kb, m(qb, kb)) for qb in range(kb, nb)] for kb in range(nb)])


class _Swa:
    mask = staticmethod(_swa_mask)

    def __init__(self, L):
        self.T = BLOCK
        nb = L // BLOCK
        ks = lambda qb: [0] + ([qb - 1] if qb >= 2 else []) + ([qb] if qb >= 1 else [])
        qs = lambda kb: list(range(nb)) if kb == 0 else [kb] + ([kb + 1] if kb + 1 < nb else [])
        self.q_major = _tables([[(qb, kb, True) for kb in ks(qb)] for qb in range(nb)])
        self.k_major = _tables([[(qb, kb, True) for qb in qs(kb)] for kb in range(nb)])


def _positions(T, qb, kb):
    qpos = qb * T + lax.broadcasted_iota(jnp.int32, (T, T), 0)
    kpos = kb * T + lax.broadcasted_iota(jnp.int32, (T, T), 1)
    return qpos, kpos


def _scores(q, k, c, cr, masked, mask, T, qb, kb, backward):
    s = lax.dot_general(q, k, NT, preferred_element_type=F32) * c
    if cr is not None:
        s = s - cr
    live = None
    if masked:
        qpos, kpos = _positions(T, qb, kb)
        live = mask(qpos, kpos)
        if backward:
            live = live & (qpos >= PAD)
        s = jnp.where(live, s, NEG)
    return s, live


def _prob(s, lse, live):
    p = jnp.exp2(s - lse)
    return p if live is None else jnp.where(live, p, 0.0)


def _both(flag, fn):
    pl.when(flag != 0)(lambda: fn(True))
    pl.when(flag == 0)(lambda: fn(False))


def _flash_q_major(mode, cfg, qa, q_off, ka, k_off, va, v_off, H, hs, scale, name,
                   cr2=None, sink2=None, do=None, lse2=None, delta=None, out_dtype=F32):
    L = qa.shape[0]
    T = cfg.T
    qt, kt, ft = cfg.q_major
    npairs = qt.shape[0]
    c = scale * LOG2E
    decay, has_sink, fwd = cr2 is not None, sink2 is not None, mode == "fwd"
    W = hs * LANES

    def body(qt_ref, kt_ref, ft_ref, *refs):
        it = iter(refs)
        q_ref, k_ref, v_ref = next(it), next(it), next(it)
        cr_ref = next(it) if decay else None
        sink_ref = next(it) if has_sink else None
        do_ref, lse_ref = (None, None) if fwd else (next(it), next(it))
        dl_ref = next(it) if mode == "dq" else None
        out_ref = next(it)
        lse_out = next(it) if fwd else None
        m_sc = next(it) if fwd else None
        l_sc = next(it) if fwd else None
        acc_sc = next(it)
        n = pl.program_id(1)
        qb, kb, f = qt_ref[n], kt_ref[n], ft_ref[n]

        @pl.when((f & 1) != 0)
        def _():
            for i in range(hs):
                acc_sc[i] = jnp.zeros(acc_sc.shape[1:], F32)
                if fwd and has_sink:
                    m_sc[i] = jnp.broadcast_to(sink_ref[:, i * LANES:i * LANES + 1], (T, 1))
                    l_sc[i] = jnp.ones((T, 1), F32)
                elif fwd:
                    m_sc[i] = jnp.full((T, 1), NEG, F32)
                    l_sc[i] = jnp.zeros((T, 1), F32)

        def step(masked):
            k, v = _bf(k_ref[...]), _bf(v_ref[...])
            cr = cr_ref[0] if decay else None
            for i in range(hs):
                cols = slice(i * LANES, (i + 1) * LANES)
                s, live = _scores(_bf(q_ref[:, cols]), k, c, cr, masked, cfg.mask, T, qb, kb, not fwd)
                if fwd:
                    m_prev = m_sc[i]
                    m_new = jnp.maximum(m_prev, jnp.max(s, axis=1, keepdims=True))
                    alpha = jnp.exp2(m_prev - m_new)
                    p = jnp.exp2(s - m_new)
                    l_sc[i] = alpha * l_sc[i] + jnp.sum(p, axis=1, keepdims=True)
                    acc_sc[i] = alpha * acc_sc[i] + lax.dot_general(p.astype(BF16), v, NN, preferred_element_type=F32)
                    m_sc[i] = m_new
                    continue
                p = _prob(s, lse_ref[:, i * LANES:i * LANES + 1], live)
                dp = lax.dot_general(_bf(do_ref[:, cols]), v, NT, preferred_element_type=F32)
                if mode == "delta":
                    acc_sc[i] += jnp.sum(p * dp, axis=1, keepdims=True)
                else:
                    ds = p * (dp - dl_ref[:, i * LANES:i * LANES + 1])
                    acc_sc[i] += lax.dot_general(ds.astype(BF16), k, NN, preferred_element_type=F32)

        _both(f & 4, step)

        @pl.when((f & 2) != 0)
        def _():
            for i in range(hs):
                cols = slice(i * LANES, (i + 1) * LANES)
                acc = acc_sc[i]
                if fwd:
                    l = l_sc[i]
                    out_ref[:, cols] = acc / l
                    lse_out[:, cols] = jnp.broadcast_to(m_sc[i] + jnp.log(l) * LOG2E, (T, LANES))
                elif mode == "delta":
                    out_ref[:, cols] = jnp.broadcast_to(acc, (T, LANES))
                else:
                    out_ref[:, cols] = (acc * scale).astype(out_dtype)

    qrow = lambda off: pl.BlockSpec((T, W), lambda h, n, qt, kt, ft: (qt[n], off // hs + h))
    krow = lambda off: pl.BlockSpec((T, LANES), lambda h, n, qt, kt, ft: (kt[n], off + h))
    ins, specs = [qa, ka, va], [qrow(q_off), krow(k_off), krow(v_off)]
    if decay:
        ins.append(cr2)
        specs.append(pl.BlockSpec((1, 1, T), lambda h, n, qt, kt, ft: (h, 0, kt[n])))
    if has_sink:
        ins.append(sink2)
        specs.append(pl.BlockSpec((1, W), lambda h, n, qt, kt, ft: (0, h)))
    if not fwd:
        ins += [do, lse2]
        specs += [qrow(0), qrow(0)]
    if mode == "dq":
        ins.append(delta)
        specs.append(qrow(0))
    full = jax.ShapeDtypeStruct((L, H * LANES), out_dtype if mode == "dq" else F32)
    scratch = [pltpu.VMEM((hs, T, 1), F32), pltpu.VMEM((hs, T, 1), F32), pltpu.VMEM((hs, T, LANES), F32)] if fwd else \
        [pltpu.VMEM((hs, T, 1 if mode == "delta" else LANES), F32)]
    out = pl.pallas_call(
        body, name=name, out_shape=[full, full] if fwd else full,
        grid_spec=pltpu.PrefetchScalarGridSpec(
            num_scalar_prefetch=3, grid=(H // hs, npairs), in_specs=specs,
            out_specs=[qrow(0), qrow(0)] if fwd else qrow(0), scratch_shapes=scratch),
        compiler_params=_params("parallel", "arbitrary"),
    )(qt, kt, ft, *ins)
    return out


def _flash_k_major(cfg, qa, q_off, ka, k_off, va, v_off, do, lse2, delta, HKV, hs, scale, name,
                   cr2=None, fused=False, lane_sums=False, out_dtype=F32):
    L = qa.shape[0]
    T = cfg.T
    qt, kt, ft = cfg.k_major
    npairs = qt.shape[0]
    c = scale * LOG2E
    decay = cr2 is not None
    W = hs * LANES

    def body(qt_ref, kt_ref, ft_ref, *refs):
        it = iter(refs)
        q_ref, k_ref, v_ref = next(it), next(it), next(it)
        cr_ref = next(it) if decay else None
        do_ref, lse_ref, dl_ref = next(it), next(it), next(it)
        dk_ref, dv_ref = next(it), next(it)
        dq_ref = next(it) if fused else None
        dk_sc, dv_sc = next(it), next(it)
        n = pl.program_id(1)
        qb, kb, f = qt_ref[n], kt_ref[n], ft_ref[n]

        if fused:
            @pl.when(n == 0)
            def _():
                dq_ref[...] = jnp.zeros_like(dq_ref)

        @pl.when((f & 1) != 0)
        def _():
            dk_sc[...] = jnp.zeros_like(dk_sc)
            dv_sc[...] = jnp.zeros_like(dv_sc)

        def step(masked):
            k, v = _bf(k_ref[...]), _bf(v_ref[...])
            cr = cr_ref[0] if decay else None
            last = lax.broadcasted_iota(jnp.int32, (T, LANES), 1) == LANES - 1
            k1 = jnp.where(last, 1.0, k_ref[...]).astype(BF16) if lane_sums else k
            for i in range(hs):
                cols = slice(i * LANES, (i + 1) * LANES)
                q, dob = _bf(q_ref[:, cols]), _bf(do_ref[:, cols])
                q1 = jnp.where(last, 1.0, q_ref[:, cols]).astype(BF16) if lane_sums else q
                s, live = _scores(q, k, c, cr, masked, cfg.mask, T, qb, kb, True)
                p = _prob(s, lse_ref[:, i * LANES:i * LANES + 1], live)
                dv_sc[...] += lax.dot_general(p.astype(BF16), dob, TN, preferred_element_type=F32)
                dp = lax.dot_general(dob, v, NT, preferred_element_type=F32)
                ds = p * (dp - dl_ref[:, i * LANES:i * LANES + 1])
                dsb = ds.astype(BF16)
                dk_sc[...] += lax.dot_general(dsb, q1, TN, preferred_element_type=F32)
                if fused:
                    rows = pl.ds(pl.multiple_of(qb * T, T), T)
                    dq_ref[rows, :] += lax.dot_general(dsb, k1, NN, preferred_element_type=F32)

        _both(f & 4, step)

        @pl.when((f & 2) != 0)
        def _():
            dk_ref[...] = (dk_sc[...] * scale).astype(out_dtype)
            dv_ref[...] = dv_sc[...].astype(out_dtype)

        if fused:
            @pl.when(n == npairs - 1)
            def _():
                dq_ref[...] = dq_ref[...] * scale

    qrow = lambda off: pl.BlockSpec((T, W), lambda h, n, qt, kt, ft: (qt[n], off // hs + h))
    krow = lambda off: pl.BlockSpec((T, LANES), lambda h, n, qt, kt, ft: (kt[n], off + h))
    ins, specs = [qa, ka, va], [qrow(q_off), krow(k_off), krow(v_off)]
    if decay:
        ins.append(cr2)
        specs.append(pl.BlockSpec((1, 1, T), lambda h, n, qt, kt, ft: (h, 0, kt[n])))
    ins += [do, lse2, delta]
    specs += [qrow(0), qrow(0), qrow(0)]
    outs = [jax.ShapeDtypeStruct((L, HKV * LANES), out_dtype)] * 2
    ospecs = [krow(0), krow(0)]
    scratch = [pltpu.VMEM((T, LANES), F32), pltpu.VMEM((T, LANES), F32)]
    if fused:
        outs.append(jax.ShapeDtypeStruct((L, HKV * LANES), F32))
        ospecs.append(pl.BlockSpec((L, LANES), lambda h, n, qt, kt, ft: (0, h)))
    return pl.pallas_call(
        body, name=name, out_shape=outs,
        grid_spec=pltpu.PrefetchScalarGridSpec(num_scalar_prefetch=3, grid=(HKV, npairs), in_specs=specs,
                                               out_specs=ospecs, scratch_shapes=scratch),
        compiler_params=_params("parallel", "arbitrary"),
    )(qt, kt, ft, *ins)


def _delta(do, o, name):
    L, HW = do.shape
    T = _pick(L, (640, 128))

    def body(do_ref, o_ref, d_ref):
        d_ref[...] = jnp.broadcast_to(jnp.sum(do_ref[...].astype(F32) * o_ref[...], axis=1, keepdims=True), (T, LANES))

    spec = pl.BlockSpec((T, LANES), lambda h, i: (i, h))
    return pl.pallas_call(body, name=name, grid=(HW // LANES, L // T), in_specs=[spec, spec], out_specs=spec,
                          out_shape=jax.ShapeDtypeStruct((L, HW), F32),
                          compiler_params=_params("parallel", "parallel"))(do, o)


def _sink_grad(lse2, delta, sink2, name):
    L, HW = lse2.shape
    tr = 128

    def body(lse_ref, dl_ref, s_ref, o_ref):
        @pl.when(pl.program_id(0) == 0)
        def _():
            o_ref[...] = jnp.zeros_like(o_ref)

        o_ref[...] -= jnp.sum(jnp.exp2(s_ref[...] - lse_ref[...]) * dl_ref[...], axis=0, keepdims=True)

    row = pl.BlockSpec((tr, HW), lambda i: (i, 0))
    vec = pl.BlockSpec((1, HW), lambda i: (0, 0))
    return pl.pallas_call(body, name=name, grid=(L // tr,), in_specs=[row, row, vec], out_specs=vec,
                          out_shape=jax.ShapeDtypeStruct((1, HW), F32), compiler_params=_params("arbitrary"))(
        lse2, delta, sink2)


def _tri(lower):
    r = lax.broadcasted_iota(jnp.int32, (BLOCK, BLOCK), 0)
    c = lax.broadcasted_iota(jnp.int32, (BLOCK, BLOCK), 1)
    return jnp.where((c <= r) if lower else (c >= r), 1.0, 0.0).astype(F32)


def _gate_cumsum(proj, fg_tile, b_pad, name):
    L = proj.shape[0]

    def body(fg_ref, b_ref, c_ref, carry):
        @pl.when(pl.program_id(0) == 0)
        def _():
            carry[...] = jnp.zeros_like(carry)

        x = fg_ref[...] + b_ref[...]
        lf = jnp.minimum(x, 0.0) - jnp.log(1.0 + jnp.exp(-jnp.abs(x)))
        c = jnp.dot(_tri(True), lf, precision=lax.Precision.HIGHEST, preferred_element_type=F32) + carry[...]
        c_ref[...] = c
        carry[...] = c[BLOCK - 1:BLOCK, :]

    return pl.pallas_call(
        body, name=name, grid=(L // BLOCK,),
        in_specs=[pl.BlockSpec((BLOCK, LANES), lambda i: (i, fg_tile)), pl.BlockSpec((1, LANES), lambda i: (0, 0))],
        out_specs=pl.BlockSpec((BLOCK, LANES), lambda i: (i, 0)),
        out_shape=jax.ShapeDtypeStruct((L, LANES), F32),
        scratch_shapes=[pltpu.VMEM((1, LANES), F32)],
        compiler_params=_params("arbitrary"),
    )(proj, b_pad)


def _gate_cumsum_bwd(dc, proj, fg_tile, b_pad, name):
    L = proj.shape[0]
    nb = L // BLOCK

    def body(dc_ref, fg_ref, b_ref, dfg_ref, db_ref, carry):
        @pl.when(pl.program_id(0) == 0)
        def _():
            carry[...] = jnp.zeros_like(carry)
            db_ref[...] = jnp.zeros_like(db_ref)

        dlf = jnp.dot(_tri(False), dc_ref[...], precision=lax.Precision.HIGHEST,
                      preferred_element_type=F32) + carry[...]
        carry[...] = dlf[0:1, :]
        x = fg_ref[...] + b_ref[...]
        lanes = lax.broadcasted_iota(jnp.int32, (BLOCK, LANES), 1)
        rows = (nb - 1 - pl.program_id(0)) * BLOCK + lax.broadcasted_iota(jnp.int32, (BLOCK, LANES), 0)
        dfg = jnp.where((lanes < HEADS) & (rows >= PAD), dlf / (1.0 + jnp.exp(x)), 0.0)
        dfg_ref[...] = jnp.concatenate([dfg, jnp.zeros_like(dfg)], axis=1)
        db_ref[...] += jnp.sum(dfg, axis=0, keepdims=True)

    dfg, db = pl.pallas_call(
        body, name=name, grid=(nb,),
        in_specs=[pl.BlockSpec((BLOCK, LANES), lambda i: (nb - 1 - i, 0)),
                  pl.BlockSpec((BLOCK, LANES), lambda i: (nb - 1 - i, fg_tile)),
                  pl.BlockSpec((1, LANES), lambda i: (0, 0))],
        out_specs=[pl.BlockSpec((BLOCK, 2 * LANES), lambda i: (nb - 1 - i, 0)),
                   pl.BlockSpec((1, LANES), lambda i: (0, 0))],
        out_shape=[jax.ShapeDtypeStruct((L, 2 * LANES), F32), jax.ShapeDtypeStruct((1, LANES), F32)],
        scratch_shapes=[pltpu.VMEM((1, LANES), F32)],
        compiler_params=_params("arbitrary"),
    )(dc, proj, b_pad)
    return dfg, db[0]


def _rope_tables(L, dim, theta, lane0):
    half = dim // 2
    pos = (jnp.arange(L) - PAD).astype(F32)
    inv = theta ** (-jnp.arange(0, dim, 2, dtype=F32) / dim)
    ang = pos[:, None] * inv[None, :]
    cos, sin = jnp.cos(ang), jnp.sin(ang)
    C = jnp.ones((L, LANES), F32).at[:, lane0:lane0 + half].set(cos).at[:, lane0 + half:lane0 + dim].set(cos)
    S1 = jnp.zeros((L, LANES), F32).at[:, lane0:lane0 + half].set(-sin)
    S2 = jnp.zeros((L, LANES), F32).at[:, lane0 + half:lane0 + dim].set(sin)
    return C, S1, S2


def _rot(x, C, S1, S2, R):
    return x * C + pltpu.roll(x, LANES - R, 1) * S1 + pltpu.roll(x, R, 1) * S2


def _rot_t(dy, C, S1, S2, R):
    return dy * C + pltpu.roll(dy * S1, R, 1) + pltpu.roll(dy * S2, LANES - R, 1)


def _rope(x, nt, tabs, R, name, transpose=False, shared=None, shared_tile=0, out_dtype=F32):
    L = x.shape[0]
    T = BLOCK
    fn = _rot_t if transpose else _rot

    def body(*refs):
        if shared is None:
            x_ref, c_ref, s1_ref, s2_ref, o_ref = refs
        else:
            x_ref, sh_ref, c_ref, s1_ref, s2_ref, o_ref = refs
            rs = fn(sh_ref[...], c_ref[...], s1_ref[...], s2_ref[...], R)
        for h in range(nt):
            cols = slice(h * LANES, (h + 1) * LANES)
            if shared is None:
                o_ref[:, cols] = fn(x_ref[:, cols], c_ref[...], s1_ref[...], s2_ref[...], R).astype(out_dtype)
            else:
                o_ref[:, cols] = (x_ref[:, cols] + rs).astype(out_dtype)

    wide = pl.BlockSpec((T, nt * LANES), lambda i: (i, 0))
    tab = pl.BlockSpec((T, LANES), lambda i: (i, 0))
    ins, specs = [x], [wide]
    if shared is not None:
        ins.append(shared)
        specs.append(pl.BlockSpec((T, LANES), lambda i: (i, shared_tile)))
    return pl.pallas_call(body, name=name, grid=(L // T,), in_specs=specs + [tab, tab, tab], out_specs=wide,
                          out_shape=jax.ShapeDtypeStruct((L, nt * LANES), out_dtype),
                          compiler_params=_params("parallel"))(*ins, *tabs)


def _rope_shared_bwd(dk, nt, tabs, R, name):
    L = dk.shape[0]
    tr = 128

    def body(dk_ref, c_ref, s1_ref, s2_ref, o_ref):
        acc = dk_ref[:, 0:LANES]
        for h in range(1, nt):
            acc = acc + dk_ref[:, h * LANES:(h + 1) * LANES]
        o_ref[...] = _rot_t(acc, c_ref[...], s1_ref[...], s2_ref[...], R)

    tab = pl.BlockSpec((tr, LANES), lambda i: (i, 0))
    return pl.pallas_call(body, name=name, grid=(L // tr,),
                          in_specs=[pl.BlockSpec((tr, nt * LANES), lambda i: (i, 0)), tab, tab, tab], out_specs=tab,
                          out_shape=jax.ShapeDtypeStruct((L, LANES), F32), compiler_params=_params("parallel"))(
        dk, *tabs)


def _rms_fwd(pa, gq, gkv, name):
    L = pa.shape[0]
    tr = 128
    Q, KV = MLA_Q_LORA, MLA_KV_LORA

    def body(pa_ref, gq_ref, gkv_ref, q_ref, kv_ref):
        for lo, n, g_ref, o_ref in ((0, Q, gq_ref, q_ref), (Q, KV, gkv_ref, kv_ref)):
            x = pa_ref[:, lo:lo + n]
            r = lax.rsqrt(jnp.mean(x * x, axis=1, keepdims=True) + RMS_EPS)
            o_ref[...] = (x * r * g_ref[...]).astype(BF16)

    return pl.pallas_call(
        body, name=name, grid=(L // tr,),
        in_specs=[pl.BlockSpec((tr, pa.shape[1]), lambda i: (i, 0)), pl.BlockSpec((1, Q), lambda i: (0, 0)),
                  pl.BlockSpec((1, KV), lambda i: (0, 0))],
        out_specs=[pl.BlockSpec((tr, Q), lambda i: (i, 0)), pl.BlockSpec((tr, KV), lambda i: (i, 0))],
        out_shape=[jax.ShapeDtypeStruct((L, Q), BF16), jax.ShapeDtypeStruct((L, KV), BF16)],
        compiler_params=_params("parallel"),
    )(pa, gq.reshape(1, Q), gkv.reshape(1, KV))


def _rms_bwd(pa, dq, dkv, dkr, gq, gkv, name):
    L, W = pa.shape
    tr = 128
    Q, KV = MLA_Q_LORA, MLA_KV_LORA

    def body(pa_ref, dq_ref, dkv_ref, dkr_ref, gq_ref, gkv_ref, dpa_ref, dgq_ref, dgkv_ref):
        @pl.when(pl.program_id(0) == 0)
        def _():
            dgq_ref[...] = jnp.zeros_like(dgq_ref)
            dgkv_ref[...] = jnp.zeros_like(dgkv_ref)

        for lo, n, g_ref, dy_ref, dg_ref in ((0, Q, gq_ref, dq_ref, dgq_ref), (Q, KV, gkv_ref, dkv_ref, dgkv_ref)):
            x = pa_ref[:, lo:lo + n]
            r = lax.rsqrt(jnp.mean(x * x, axis=1, keepdims=True) + RMS_EPS)
            xh = x * r
            dy = dy_ref[...]
            dxh = dy * g_ref[...]
            dpa_ref[:, lo:lo + n] = (r * (dxh - xh * jnp.mean(dxh * xh, axis=1, keepdims=True))).astype(BF16)
            dg_ref[...] += jnp.sum(dy * xh, axis=0, keepdims=True)
        dpa_ref[:, Q + KV:W] = dkr_ref[...].astype(BF16)

    vq = pl.BlockSpec((1, Q), lambda i: (0, 0))
    vkv = pl.BlockSpec((1, KV), lambda i: (0, 0))
    dpa, dgq, dgkv = pl.pallas_call(
        body, name=name, grid=(L // tr,),
        in_specs=[pl.BlockSpec((tr, W), lambda i: (i, 0)), pl.BlockSpec((tr, Q), lambda i: (i, 0)),
                  pl.BlockSpec((tr, KV), lambda i: (i, 0)), pl.BlockSpec((tr, LANES), lambda i: (i, 0)), vq, vkv],
        out_specs=[pl.BlockSpec((tr, W), lambda i: (i, 0)), vq, vkv],
        out_shape=[jax.ShapeDtypeStruct((L, W), BF16), jax.ShapeDtypeStruct((1, Q), F32),
                   jax.ShapeDtypeStruct((1, KV), F32)],
        compiler_params=_params("arbitrary"),
    )(pa, dq, dkv, dkr, gq.reshape(1, Q), gkv.reshape(1, KV))
    return dpa, dgq[0], dgkv[0]


def _loss_head(h, target, name):
    L = h.shape[0]
    tr = BLOCK
    inv = 1.0 / D_MODEL

    def body(h_ref, t_ref, loss_ref, dh_ref):
        i = pl.program_id(0)

        @pl.when(i == 0)
        def _():
            loss_ref[...] = jnp.zeros_like(loss_ref)
            dh_ref[...] = jnp.zeros_like(dh_ref)

        @pl.when(i > 0)
        def _():
            e = h_ref[...] - t_ref[...]
            dh_ref[...] = e * inv
            loss_ref[...] += jnp.sum((e * e).reshape(tr // 8, 8, D_MODEL), axis=0) * (0.5 * inv)

    row = pl.BlockSpec((tr, D_MODEL), lambda i: (i, 0))
    loss, dh = pl.pallas_call(
        body, name=name, grid=(L // tr,),
        in_specs=[row, pl.BlockSpec((tr, D_MODEL), lambda i: (jnp.maximum(i - 1, 0), 0))],
        out_specs=[pl.BlockSpec((8, D_MODEL), lambda i: (0, 0)), row],
        out_shape=[jax.ShapeDtypeStruct((8, D_MODEL), F32), jax.ShapeDtypeStruct((L, D_MODEL), F32)],
        compiler_params=_params("arbitrary"),
    )(h, target)
    return loss, dh


def _pad_heads_cols(w, nh, d, dp=LANES):
    K = w.shape[0]
    return jnp.pad(w.reshape(K, nh, d), ((0, 0), (0, 0), (0, dp - d))).reshape(K, nh * dp)


def _unpad_heads_cols(w, nh, d, dp=LANES):
    K = w.shape[0]
    return w.reshape(K, nh, dp)[:, :, :d].reshape(K, nh * d)


def _pad_heads_rows(w, nh, d):
    N = w.shape[1]
    return jnp.pad(w.reshape(nh, d, N), ((0, 0), (0, LANES - d), (0, 0))).reshape(nh * LANES, N)


def _unpad_heads_rows(w, nh, d):
    N = w.shape[1]
    return w.reshape(nh, LANES, N)[:, :d, :].reshape(nh * d, N)


def _fox_fwd(h, w_in, b_f, w_o, tag):
    L = h.shape[0]
    hd = HEADS * HEAD_DIM
    W = jnp.concatenate([_pad_heads_cols(w_in[:, i * hd:(i + 1) * hd], HEADS, HEAD_DIM) for i in range(3)]
                        + [jnp.pad(w_in[:, 3 * hd:], ((0, 0), (0, 2 * LANES - HEADS)))], axis=1)
    Wo = _pad_heads_rows(w_o, HEADS, HEAD_DIM)
    b_pad = jnp.pad(b_f, (0, LANES - HEADS)).reshape(1, LANES)
    proj = _mm(h, W, "nn", tag + "_proj")
    c = _gate_cumsum(proj, 3 * HEADS, b_pad, tag + "_cumsum")
    dead = (jnp.arange(L) < PAD)[:, None]
    cr2 = jnp.where(dead, -NEG, c[:, :HEADS] * LOG2E).T.reshape(HEADS, 1, L)
    cfg = _Dense(L, True)
    scale = HEAD_DIM ** -0.5
    o, lse2 = _flash_q_major("fwd", cfg, proj, 0, proj, HEADS, proj, 2 * HEADS, HEADS, 1, scale, tag + "_attn", cr2=cr2)
    mix = _mm(o, Wo, "nn", tag + "_out")
    return mix, (h, W, Wo, b_pad, proj, cr2, o, lse2)


def _fox_bwd(dmix, res, tag):
    h, W, Wo, b_pad, proj, cr2, o, lse2 = res
    L = h.shape[0]
    cfg = _Dense(L, True)
    scale = HEAD_DIM ** -0.5
    dWo = _mm(o, dmix, "tn", tag + "_dwo")
    do = _mm(dmix, Wo, "nt", tag + "_do", out_dtype=BF16)
    qkv = (proj, 0, proj, HEADS, proj, 2 * HEADS)
    delta = _delta(do, o, tag + "_delta")
    dk, dv, dq = _flash_k_major(cfg, *qkv, do, lse2, delta, HEADS, 1, scale, tag + "_bwd", cr2=cr2, fused=True,
                                lane_sums=True)
    dc = jnp.pad((dq[:, LANES - 1::LANES] - dk[:, LANES - 1::LANES]) * (1.0 / scale), ((0, 0), (0, LANES - HEADS)))
    dfg, db = _gate_cumsum_bwd(dc, proj, 3 * HEADS, b_pad, tag + "_cumsum_bwd")
    dproj = jnp.concatenate([dq.astype(BF16), dk.astype(BF16), dv.astype(BF16), dfg.astype(BF16)], axis=1)
    dW = _mm(h, dproj, "tn", tag + "_dw")
    dh = _mm(dproj, W, "nt", tag + "_dh")
    hp = HEADS * LANES
    dw_in = jnp.concatenate([_unpad_heads_cols(dW[:, i * hp:(i + 1) * hp], HEADS, HEAD_DIM) for i in range(3)]
                            + [dW[:, 3 * hp:3 * hp + HEADS]], axis=1)
    return dh, dict(w_in=dw_in, b_f=db[:HEADS], w_o=_unpad_heads_rows(dWo, HEADS, HEAD_DIM))


def _swa_fwd(h, w_in, sinks, w_o, tag):
    L = h.shape[0]
    qd, kd = HEADS * HEAD_DIM, SWA_KV * HEAD_DIM
    W = jnp.concatenate([_pad_heads_cols(w_in[:, :qd], HEADS, HEAD_DIM),
                         _pad_heads_cols(w_in[:, qd:qd + kd], SWA_KV, HEAD_DIM),
                         _pad_heads_cols(w_in[:, qd + kd:], SWA_KV, HEAD_DIM)], axis=1)
    Wo = _pad_heads_rows(w_o, HEADS, HEAD_DIM)
    sink2 = jnp.repeat(sinks * LOG2E, LANES).reshape(1, HEADS * LANES)
    tabs = _rope_tables(L, ROPE_DIM, ROPE_THETA, 0)
    proj = _mm(h, W, "nn", tag + "_proj")
    nqk = HEADS + SWA_KV
    qk = _rope(proj, nqk, tabs, ROPE_DIM // 2, tag + "_rope")
    cfg = _Swa(L)
    scale = HEAD_DIM ** -0.5
    o, lse2 = _flash_q_major("fwd", cfg, qk, 0, qk, HEADS, proj, nqk, HEADS, SWA_G, scale, tag + "_attn", sink2=sink2)
    mix = _mm(o, Wo, "nn", tag + "_out")
    return mix, (h, W, Wo, sink2, tabs, proj, qk, o, lse2)


def _swa_bwd(dmix, res, tag):
    h, W, Wo, sink2, tabs, proj, qk, o, lse2 = res
    L = h.shape[0]
    nqk = HEADS + SWA_KV
    cfg = _Swa(L)
    scale = HEAD_DIM ** -0.5
    dWo = _mm(o, dmix, "tn", tag + "_dwo")
    do = _mm(dmix, Wo, "nt", tag + "_do", out_dtype=BF16)
    delta = _delta(do, o, tag + "_delta")
    dsink = _sink_grad(lse2, delta, sink2, tag + "_dsink")[0, ::LANES]
    qkv = (qk, 0, qk, HEADS, proj, nqk)
    dq = _flash_q_major("dq", cfg, *qkv, HEADS, SWA_G, scale, tag + "_dq", do=do, lse2=lse2, delta=delta)
    dk, dv = _flash_k_major(cfg, *qkv, do, lse2, delta, SWA_KV, SWA_G, scale, tag + "_dkv")
    dqk = _rope(jnp.concatenate([dq, dk], axis=1), nqk, tabs, ROPE_DIM // 2, tag + "_rope_bwd", transpose=True,
                out_dtype=BF16)
    dproj = jnp.concatenate([dqk, dv.astype(BF16)], axis=1)
    dW = _mm(h, dproj, "tn", tag + "_dw")
    dh = _mm(dproj, W, "nt", tag + "_dh")
    hp = HEADS * LANES
    dw_in = jnp.concatenate([_unpad_heads_cols(dW[:, :hp], HEADS, HEAD_DIM),
                             _unpad_heads_cols(dW[:, hp:hp + SWA_KV * LANES], SWA_KV, HEAD_DIM),
                             _unpad_heads_cols(dW[:, hp + SWA_KV * LANES:], SWA_KV, HEAD_DIM)], axis=1)
    return dh, dict(w_in=dw_in, sinks=dsink, w_o=_unpad_heads_rows(dWo, HEADS, HEAD_DIM))


def _mla_fwd(h, w_a, g_q, g_kv, w_uq, w_ukv, w_o, tag):
    L = h.shape[0]
    Q, KV = MLA_Q_LORA, MLA_KV_LORA
    dqk = MLA_NOPE + MLA_ROPE
    kr_w = jnp.pad(w_a[:, Q + KV:], ((0, 0), (MLA_NOPE, LANES - dqk)))
    Wa = jnp.concatenate([w_a[:, :Q + KV], kr_w], axis=1)
    Wuq = _pad_heads_cols(w_uq, HEADS, dqk)
    ukv = w_ukv.reshape(KV, HEADS, MLA_NOPE + HEAD_DIM)
    Wukv = jnp.concatenate([_pad_heads_cols(ukv[:, :, :MLA_NOPE].reshape(KV, -1), HEADS, MLA_NOPE),
                            _pad_heads_cols(ukv[:, :, MLA_NOPE:].reshape(KV, -1), HEADS, HEAD_DIM)], axis=1)
    Wo = _pad_heads_rows(w_o, HEADS, HEAD_DIM)
    tabs = _rope_tables(L, MLA_ROPE, MLA_ROPE_THETA, MLA_NOPE)
    R = MLA_ROPE // 2
    pa = _mm(h, Wa, "nn", tag + "_proj")
    cqn, ckvn = _rms_fwd(pa, g_q, g_kv, tag + "_rms")
    q0 = _mm(cqn, Wuq, "nn", tag + "_uq")
    qr = _rope(q0, HEADS, tabs, R, tag + "_rope_q")
    kv0 = _mm(ckvn, Wukv, "nn", tag + "_ukv")
    kk = _rope(kv0, HEADS, tabs, R, tag + "_rope_k", shared=pa, shared_tile=(Q + KV) // LANES)
    cfg = _Dense(L, False)
    scale = dqk ** -0.5
    o, lse2 = _flash_q_major("fwd", cfg, qr, 0, kk, 0, kv0, HEADS, HEADS, 1, scale, tag + "_attn")
    mix = _mm(o, Wo, "nn", tag + "_out")
    return mix, (h, Wa, Wuq, Wukv, Wo, g_q, g_kv, tabs, pa, cqn, ckvn, qr, kk, kv0, o, lse2)


def _mla_bwd(dmix, res, tag):
    h, Wa, Wuq, Wukv, Wo, g_q, g_kv, tabs, pa, cqn, ckvn, qr, kk, kv0, o, lse2 = res
    L = h.shape[0]
    Q, KV = MLA_Q_LORA, MLA_KV_LORA
    dqk = MLA_NOPE + MLA_ROPE
    R = MLA_ROPE // 2
    cfg = _Dense(L, False)
    scale = dqk ** -0.5
    dWo = _mm(o, dmix, "tn", tag + "_dwo")
    do = _mm(dmix, Wo, "nt", tag + "_do", out_dtype=BF16)
    delta = _delta(do, o, tag + "_delta")
    dk, dv, dqr = _flash_k_major(cfg, qr, 0, kk, 0, kv0, HEADS, do, lse2, delta, HEADS, 1, scale, tag + "_bwd",
                                 fused=True)
    dq0 = _rope(dqr, HEADS, tabs, R, tag + "_rope_q_bwd", transpose=True, out_dtype=BF16)
    dWuq = _mm(cqn, dq0, "tn", tag + "_dwuq")
    dcqn = _mm(dq0, Wuq, "nt", tag + "_dcq")
    dkv = jnp.concatenate([dk, dv], axis=1).astype(BF16)
    dWukv = _mm(ckvn, dkv, "tn", tag + "_dwukv")
    dckvn = _mm(dkv, Wukv, "nt", tag + "_dckv")
    dkr = _rope_shared_bwd(dk, HEADS, tabs, R, tag + "_rope_k_bwd")
    dpa, dgq, dgkv = _rms_bwd(pa, dcqn, dckvn, dkr, g_q, g_kv, tag + "_rms_bwd")
    dWa = _mm(h, dpa, "tn", tag + "_dw")
    dh = _mm(dpa, Wa, "nt", tag + "_dh")
    hp = HEADS * LANES
    dw_a = jnp.concatenate([dWa[:, :Q + KV], dWa[:, Q + KV + MLA_NOPE:Q + KV + dqk]], axis=1)
    dk_n = dWukv[:, :hp].reshape(KV, HEADS, LANES)[:, :, :MLA_NOPE]
    dv_n = dWukv[:, hp:].reshape(KV, HEADS, LANES)[:, :, :HEAD_DIM]
    dw_ukv = jnp.concatenate([dk_n, dv_n], axis=2).reshape(KV, HEADS * (MLA_NOPE + HEAD_DIM))
    return dh, dict(w_a=dw_a, g_q=dgq, g_kv=dgkv, w_uq=_unpad_heads_cols(dWuq, HEADS, dqk), w_ukv=dw_ukv,
                    w_o=_unpad_heads_rows(dWo, HEADS, HEAD_DIM))


MATMUL_WEIGHTS = ("fox_w_in", "fox_w_o", "swa_w_in", "swa_w_o", "mla_w_a", "mla_w_uq", "mla_w_ukv", "mla_w_o",
                  "ffn_w_in", "ffn_w_out")


def _local_step(x, target, w):
    w = {k: (_bf(v) if k in MATMUL_WEIGHTS else v) for k, v in w.items()}
    h = jnp.concatenate([jnp.zeros((PAD, D_MODEL), F32), w["meta_tokens"], x], axis=0)
    hb = h.astype(BF16)
    saved = []
    for i in range(DEPTH):
        kind, j = i % 3, i // 3
        tag = "l%d" % i
        if kind == 0:
            mix, mres = _fox_fwd(hb, w["fox_w_in"][j], w["fox_b_f"][j], w["fox_w_o"][j], tag + "_fox")
        elif kind == 1:
            mix, mres = _swa_fwd(hb, w["swa_w_in"][j], w["swa_sinks"][j], w["swa_w_o"][j], tag + "_swa")
        else:
            mix, mres = _mla_fwd(hb, w["mla_w_a"][j], w["mla_g_q"][j], w["mla_g_kv"][j], w["mla_w_uq"][j],
                                 w["mla_w_ukv"][j], w["mla_w_o"][j], tag + "_mla")
        h1, h1b, xh1, rs1 = _ln_fwd(h, mix, w["ln1_g"][i], w["ln1_b"][i], tag + "_ln1")
        u = _mm(h1b, w["ffn_w_in"][i], "nn", tag + "_ffn_in")
        y, a = _conv_glu_fwd(u, w["ffn_conv_w"][i], w["ffn_conv_b"][i], tag + "_conv")
        ffn = _mm(a, w["ffn_w_out"][i], "nn", tag + "_ffn_out")
        h2, h2b, xh2, rs2 = _ln_fwd(h1, ffn, w["ln2_g"][i], w["ln2_b"][i], tag + "_ln2")
        saved.append((mres, xh1, rs1, h1b, u, y, a, xh2, rs2))
        h, hb = h2, h2b
    loss, dh = _loss_head(h, target, "loss_head")

    g = {k: [None] * v.shape[0] for k, v in w.items() if k != "meta_tokens"}
    ga = None
    for i in reversed(range(DEPTH)):
        kind, j = i % 3, i // 3
        tag = "l%d" % i
        mres, xh1, rs1, h1b, u, y, a, xh2, rs2 = saved[i]
        dz2, dz2b, g["ln2_g"][i], g["ln2_b"][i] = _ln_bwd(ga, dh, xh2, rs2, w["ln2_g"][i], tag + "_ln2_bwd")
        g["ffn_w_out"][i] = _mm(a, dz2b, "tn", tag + "_dw_out")
        da = _mm(dz2b, w["ffn_w_out"][i], "nt", tag + "_da")
        du, g["ffn_conv_w"][i], g["ffn_conv_b"][i] = _conv_glu_bwd(y, da, u, w["ffn_conv_w"][i], tag + "_conv_bwd")
        g["ffn_w_in"][i] = _mm(h1b, du, "tn", tag + "_dw_in")
        dh1 = _mm(du, w["ffn_w_in"][i], "nt", tag + "_dh1")
        dz1, dz1b, g["ln1_g"][i], g["ln1_b"][i] = _ln_bwd(dz2, dh1, xh1, rs1, w["ln1_g"][i], tag + "_ln1_bwd")
        if kind == 0:
            dh, mg = _fox_bwd(dz1b, mres, tag + "_fox")
            pre = "fox_"
        elif kind == 1:
            dh, mg = _swa_bwd(dz1b, mres, tag + "_swa")
            pre = "swa_"
        else:
            dh, mg = _mla_bwd(dz1b, mres, tag + "_mla")
            pre = "mla_"
        for k, v in mg.items():
            g[pre + k][j] = v
        ga = dz1
    dh0 = _axpy(ga, dh, "dh0")
    grads = {k: jnp.stack(v) for k, v in g.items()}
    grads["meta_tokens"] = dh0[PAD:BLOCK]
    return loss, dh0, grads


SHARDED = (("meta_tokens", 1), ("fox_w_in", 2), ("fox_w_o", 1), ("swa_w_in", 2), ("swa_w_o", 1), ("mla_w_a", 1),
           ("mla_g_q", 1), ("mla_g_kv", 1), ("mla_w_uq", 2), ("mla_w_ukv", 2), ("mla_w_o", 1), ("ffn_w_in", 2),
           ("ffn_conv_w", 2), ("ffn_w_out", 1))
REPLICATED = ("ln1_g", "ln1_b", "ln2_g", "ln2_b", "fox_b_f", "swa_sinks", "ffn_conv_b")
WEIGHTS = ("meta_tokens", "ln1_g", "ln1_b", "ln2_g", "ln2_b", "fox_w_in", "fox_b_f", "fox_w_o", "swa_w_in",
           "swa_sinks", "swa_w_o", "mla_w_a", "mla_g_q", "mla_g_kv", "mla_w_uq", "mla_w_ukv", "mla_w_o", "ffn_w_in",
           "ffn_conv_w", "ffn_conv_b", "ffn_w_out")


def _rows(n):
    return -(-n // ROW)


def _pack(arrs, multiple):
    parts = []
    for a in arrs:
        n = math.prod(a.shape)
        parts.append(jnp.pad(a.reshape(-1), (0, _rows(n) * ROW - n)).reshape(-1, ROW))
    total = sum(p.shape[0] for p in parts)
    pad = -total % multiple
    if pad:
        parts.append(jnp.zeros((pad, ROW), parts[0].dtype))
    return jnp.concatenate(parts, axis=0)


def _unpack(flat, shapes):
    out, r = [], 0
    for s in shapes:
        n = math.prod(s)
        out.append(flat[r:r + _rows(n)].reshape(-1)[:n].reshape(s))
        r += _rows(n)
    return out


def _pack_bf16(w, names):
    return _pack([_bf(w[n]) if n in MATMUL_WEIGHTS else lax.bitcast_convert_type(w[n], BF16) for n in names], 2 * ROW)


def _unpack_bf16(flat, names, shapes):
    sh = [s if n in MATMUL_WEIGHTS else s + (2,) for n, s in zip(names, shapes)]
    parts = _unpack(flat, sh)
    return [p if n in MATMUL_WEIGHTS else lax.bitcast_convert_type(p, F32) for n, p in zip(names, parts)]


HBM_SPEC = pl.BlockSpec(memory_space=pltpu.HBM)


def _place():
    x, y, c = lax.axis_index("x"), lax.axis_index("y"), lax.axis_index("c")
    chips = [(1 - x, y), (x, 1 - y), (1 - x, 1 - y)]
    return x, y, c, chips


def _gather_weights(shard):
    R = shard.shape[0]
    Rh = R // 2

    def body(s_ref, o_ref, send_sems, recv_sems, local_sem):
        x, y, c, chips = _place()
        sib = (x, y, 1 - c)

        def half(k, hc):
            return o_ref.at[k, pl.ds(hc * Rh, Rh), :]

        def copy(j, src, dst, to):
            return pltpu.make_async_remote_copy(src_ref=src, dst_ref=dst, send_sem=send_sems.at[j],
                                                recv_sem=recv_sems.at[j], device_id=to, device_id_type=MESH)

        me = 2 * x + y
        mine = pltpu.make_async_copy(s_ref, o_ref.at[me], local_sem)
        mine.start()
        first = [copy(j, s_ref.at[pl.ds(c * Rh, Rh), :], half(me, c), (tx, ty, c)) for j, (tx, ty) in enumerate(chips)]
        for cp in first:
            cp.start()
        passed = []
        for j, (tx, ty) in enumerate(chips):
            k = 2 * tx + ty
            copy(j, half(k, c), half(k, c), (tx, ty, c)).wait_recv()
            fw = copy(3 + j, half(k, c), half(k, c), sib)
            fw.start()
            passed.append(fw)
        for j, (tx, ty) in enumerate(chips):
            k = 2 * tx + ty
            copy(3 + j, half(k, 1 - c), half(k, 1 - c), sib).wait_recv()
        for cp in first + passed:
            cp.wait_send()
        mine.wait()

    return pl.pallas_call(
        body, name="gather_weights", out_shape=jax.ShapeDtypeStruct((N_CHIPS, R, ROW), shard.dtype),
        in_specs=[HBM_SPEC], out_specs=HBM_SPEC,
        scratch_shapes=[pltpu.SemaphoreType.DMA((6,)), pltpu.SemaphoreType.DMA((6,)), pltpu.SemaphoreType.DMA],
    )(shard)


def _swap_halves(G):
    R = G.shape[1]
    Rh = R // 2

    def body(g_ref, a_ref, send_sem, recv_sem):
        x, y, c, _ = _place()
        cp = pltpu.make_async_remote_copy(src_ref=g_ref.at[:, pl.ds((1 - c) * Rh, Rh), :], dst_ref=a_ref,
                                          send_sem=send_sem, recv_sem=recv_sem, device_id=(x, y, 1 - c),
                                          device_id_type=MESH)
        cp.start()
        cp.wait()

    return pl.pallas_call(
        body, name="reduce_swap_halves", out_shape=jax.ShapeDtypeStruct((N_CHIPS, Rh, ROW), G.dtype),
        in_specs=[HBM_SPEC], out_specs=HBM_SPEC,
        scratch_shapes=[pltpu.SemaphoreType.DMA, pltpu.SemaphoreType.DMA],
    )(G)


def _exchange_chips(P):
    def body(p_ref, b_ref, send_sems, recv_sems, local_sem):
        x, y, c, chips = _place()
        me = 2 * x + y

        def copy(j, src, dst, to):
            return pltpu.make_async_remote_copy(src_ref=src, dst_ref=dst, send_sem=send_sems.at[j],
                                                recv_sem=recv_sems.at[j], device_id=to, device_id_type=MESH)

        mine = pltpu.make_async_copy(p_ref.at[me], b_ref.at[me], local_sem)
        mine.start()
        sends = [copy(j, p_ref.at[2 * tx + ty], b_ref.at[me], (tx, ty, c)) for j, (tx, ty) in enumerate(chips)]
        for cp in sends:
            cp.start()
        for j, (tx, ty) in enumerate(chips):
            k = 2 * tx + ty
            copy(j, p_ref.at[k], b_ref.at[k], (tx, ty, c)).wait_recv()
        for cp in sends:
            cp.wait_send()
        mine.wait()

    return pl.pallas_call(
        body, name="reduce_exchange_chips", out_shape=jax.ShapeDtypeStruct(P.shape, P.dtype),
        in_specs=[HBM_SPEC], out_specs=HBM_SPEC,
        scratch_shapes=[pltpu.SemaphoreType.DMA((3,)), pltpu.SemaphoreType.DMA((3,)), pltpu.SemaphoreType.DMA],
    )(P)


def _join_halves(Fh):
    Rh = Fh.shape[0]

    def body(f_ref, o_ref, send_sem, recv_sem, local_sem):
        x, y, c, _ = _place()
        mine = pltpu.make_async_copy(f_ref, o_ref.at[pl.ds(c * Rh, Rh), :], local_sem)
        mine.start()
        cp = pltpu.make_async_remote_copy(src_ref=f_ref, dst_ref=o_ref.at[pl.ds(c * Rh, Rh), :], send_sem=send_sem,
                                          recv_sem=recv_sem, device_id=(x, y, 1 - c), device_id_type=MESH)
        cp.start()
        pltpu.make_async_remote_copy(src_ref=f_ref, dst_ref=o_ref.at[pl.ds((1 - c) * Rh, Rh), :], send_sem=send_sem,
                                     recv_sem=recv_sem, device_id=(x, y, 1 - c), device_id_type=MESH).wait_recv()
        cp.wait_send()
        mine.wait()

    return pl.pallas_call(
        body, name="reduce_join_halves", out_shape=jax.ShapeDtypeStruct((2 * Rh, ROW), Fh.dtype),
        in_specs=[HBM_SPEC], out_specs=HBM_SPEC,
        scratch_shapes=[pltpu.SemaphoreType.DMA, pltpu.SemaphoreType.DMA, pltpu.SemaphoreType.DMA],
    )(Fh)


def _gather_small(v):
    m_per = v.shape[0]

    def body(x_ref, out_ref, send_sems, recv_sems, local_sem):
        x, y, c, chips = _place()
        me, sibling = (x, y, c), (x, y, 1 - c)

        def rows(px, py, pc):
            return out_ref.at[pl.ds((4 * px + 2 * py + pc) * m_per, m_per), :]

        def copy(k, block, to, src=None):
            return pltpu.make_async_remote_copy(src_ref=rows(*block) if src is None else src, dst_ref=rows(*block),
                                                send_sem=send_sems.at[k], recv_sem=recv_sems.at[k], device_id=to,
                                                device_id_type=MESH)

        mine = pltpu.make_async_copy(x_ref, rows(*me), local_sem)
        mine.start()
        first = [copy(0, me, sibling, src=x_ref)]
        first += [copy(1 + j, me, (*chip, c), src=x_ref) for j, chip in enumerate(chips)]
        for cp in first:
            cp.start()
        passed = [copy(4 + j, (*chip, c), sibling) for j, chip in enumerate(chips)]
        for j, chip in enumerate(chips):
            copy(1 + j, (*chip, c), me).wait_recv()
            passed[j].start()
        copy(0, sibling, me).wait_recv()
        for j, chip in enumerate(chips):
            copy(4 + j, (*chip, 1 - c), me).wait_recv()
        for cp in first + passed:
            cp.wait_send()
        mine.wait()

    return pl.pallas_call(
        body, name="gather_small", out_shape=jax.ShapeDtypeStruct((N_DEV * m_per, ROW), v.dtype),
        in_specs=[pl.BlockSpec(memory_space=pltpu.VMEM)], out_specs=pl.BlockSpec(memory_space=pltpu.VMEM),
        scratch_shapes=[pltpu.SemaphoreType.DMA((7,)), pltpu.SemaphoreType.DMA((7,)), pltpu.SemaphoreType.DMA],
    )(v)


def _sum_slots(a, n, name):
    M = a.shape[0] // n
    tr = _pick(M, (512, 256, 128, 64, 40, 8))
    nb = M // tr

    def body(*refs):
        acc = refs[0][...]
        for r in refs[1:-1]:
            acc = acc + r[...]
        refs[-1][...] = acc

    specs = [pl.BlockSpec((tr, ROW), functools.partial(lambda i, k: (k * nb + i, 0), k=k)) for k in range(n)]
    return pl.pallas_call(body, name=name, grid=(nb,), in_specs=specs,
                          out_specs=pl.BlockSpec((tr, ROW), lambda i: (i, 0)),
                          out_shape=jax.ShapeDtypeStruct((M, ROW), F32), compiler_params=_params("parallel"))(*([a] * n))


def _add(a, b, name):
    M = a.shape[0]
    tr = _pick(M, (512, 256, 128, 64, 40, 8))

    def body(a_ref, b_ref, o_ref):
        o_ref[...] = a_ref[...] + b_ref[...]

    row = pl.BlockSpec((tr, ROW), lambda i: (i, 0))
    return pl.pallas_call(body, name=name, grid=(M // tr,), in_specs=[row, row], out_specs=row,
                          out_shape=jax.ShapeDtypeStruct((M, ROW), F32), compiler_params=_params("parallel"))(a, b)


def _adamw(g, w, m, v, name):
    M = g.shape[0]
    tr = _pick(M, (512, 256, 128, 64, 40, 8))
    c1 = 1.0 - ADAM_B1 ** ADAM_STEP
    c2 = 1.0 - ADAM_B2 ** ADAM_STEP

    def body(g_ref, w_ref, m_ref, v_ref, d_ref, nm_ref, nv_ref):
        gg = g_ref[...]
        nm = ADAM_B1 * m_ref[...] + (1.0 - ADAM_B1) * gg
        nv = ADAM_B2 * v_ref[...] + (1.0 - ADAM_B2) * (gg * gg)
        nm_ref[...] = nm
        nv_ref[...] = nv
        d_ref[...] = -ADAM_LR * ((nm / c1) / (jnp.sqrt(nv / c2) + ADAM_EPS) + ADAM_WD * w_ref[...])

    row = pl.BlockSpec((tr, ROW), lambda i: (i, 0))
    shape = jax.ShapeDtypeStruct((M, ROW), F32)
    return pl.pallas_call(body, name=name, grid=(M // tr,), in_specs=[row] * 4, out_specs=[row] * 3,
                          out_shape=[shape] * 3, compiler_params=_params("parallel"))(g, w, m, v)


def kernel(x, meta_tokens, ln1_g, ln1_b, ln2_g, ln2_b, fox_w_in, fox_b_f, fox_w_o, swa_w_in, swa_sinks, swa_w_o, mla_w_a, mla_g_q, mla_g_kv, mla_w_uq, mla_w_ukv, mla_w_o, ffn_w_in, ffn_conv_w, ffn_conv_b, ffn_w_out, loss_target, m_meta_tokens, m_ln1_g, m_ln1_b, m_ln2_g, m_ln2_b, m_fox_w_in, m_fox_b_f, m_fox_w_o, m_swa_w_in, m_swa_sinks, m_swa_w_o, m_mla_w_a, m_mla_g_q, m_mla_g_kv, m_mla_w_uq, m_mla_w_ukv, m_mla_w_o, m_ffn_w_in, m_ffn_conv_w, m_ffn_conv_b, m_ffn_w_out, v_meta_tokens, v_ln1_g, v_ln1_b, v_ln2_g, v_ln2_b, v_fox_w_in, v_fox_b_f, v_fox_w_o, v_swa_w_in, v_swa_sinks, v_swa_w_o, v_mla_w_a, v_mla_g_q, v_mla_g_kv, v_mla_w_uq, v_mla_w_ukv, v_mla_w_o, v_ffn_w_in, v_ffn_conv_w, v_ffn_conv_b, v_ffn_w_out):
    given = dict(locals())
    w = {n: given[n] for n in WEIGHTS}
    m = {n: given["m_" + n] for n in WEIGHTS}
    v = {n: given["v_" + n] for n in WEIGHTS}
    sh_names = [n for n, _ in SHARDED]
    sh_shapes = [w[n].shape for n in sh_names]

    gathered = _gather_weights(_pack_bf16(w, sh_names))
    full = dict(w)
    per_chip = [_unpack_bf16(gathered[k], sh_names, sh_shapes) for k in range(N_CHIPS)]
    for t, (n, ax) in enumerate(SHARDED):
        full[n] = jnp.concatenate([per_chip[k][t] for k in range(N_CHIPS)], axis=ax)

    loss_part, dh0, grads = _local_step(x[0], loss_target[0], full)
    loss = lax.psum(jnp.sum(loss_part), ("x", "y", "c"))
    grad_x = dh0[BLOCK:][None]

    split = {n: jnp.split(grads[n], N_CHIPS, axis=ax) for n, ax in SHARDED}
    G = jnp.stack([_pack([split[n][k] for n in sh_names], 2 * ROW) for k in range(N_CHIPS)])
    Rh = G.shape[1] // 2
    c = lax.axis_index("c")
    mine = lax.dynamic_slice_in_dim(G, c * Rh, Rh, axis=1)
    P = _add(mine.reshape(N_CHIPS * Rh, ROW), _swap_halves(G).reshape(N_CHIPS * Rh, ROW), "reduce_pair_sum")
    B = _exchange_chips(P.reshape(N_CHIPS, Rh, ROW))
    Fh = _sum_slots(B.reshape(N_CHIPS * Rh, ROW), N_CHIPS, "reduce_chip_sum")
    Fg = _join_halves(Fh)
    d_s, m_s, v_s = _adamw(Fg, _pack([w[n] for n in sh_names], 2 * ROW), _pack([m[n] for n in sh_names], 2 * ROW),
                           _pack([v[n] for n in sh_names], 2 * ROW), "adamw_sharded")

    rp_shapes = [w[n].shape for n in REPLICATED]
    small = _gather_small(_pack([grads[n] for n in REPLICATED], 8))
    g_r = _sum_slots(small, N_DEV, "reduce_small_sum")
    d_r, m_r, v_r = _adamw(g_r, _pack([w[n] for n in REPLICATED], 8), _pack([m[n] for n in REPLICATED], 8),
                           _pack([v[n] for n in REPLICATED], 8), "adamw_replicated")

    out = {}
    for kind, fs, fr in (("grad", Fg, g_r), ("delta", d_s, d_r), ("new_m", m_s, m_r), ("new_v", v_s, v_r)):
        for n, a in zip(sh_names, _unpack(fs, sh_shapes)):
            out[kind, n] = a
        for n, a in zip(REPLICATED, _unpack(fr, rp_shapes)):
            out[kind, n] = a
    return (loss, grad_x, *[out[k, n] for k in ("grad", "delta", "new_m", "new_v") for n in WEIGHTS])
```

```python
import functools
import math

import numpy as np
import jax
import jax.numpy as jnp
from jax import lax
from jax.experimental import pallas as pl
from jax.experimental.pallas import tpu as pltpu

F32 = jnp.float32
BF16 = jnp.bfloat16

D_MODEL = 1024
DEPTH = 4
BLOCK = 128
N_META = 16
PAD = BLOCK - N_META
NEG = -1e30
ALPHA = (2.0 * DEPTH) ** 0.25
LN_EPS = 1e-5
RMS_EPS = 1e-6
HEADS = 16
HEAD_DIM = 64
LANES = 128
SWA_KV = 2
SWA_G = HEADS // SWA_KV
WINDOW = 128
ROPE_THETA = 500000.0
ROPE_DIM = 16
MLA_Q_LORA = 384
MLA_KV_LORA = 256
MLA_NOPE = 64
MLA_ROPE = 32
MLA_ROPE_THETA = 10000.0
D_FF = 2816
ADAM_LR = 0.001
ADAM_B1 = 0.9
ADAM_B2 = 0.999
ADAM_EPS = 1e-08
ADAM_WD = 0.01
ADAM_STEP = 10
N_CHIPS = 4
N_DEV = 8
ROW = 1024
VMEM_LIMIT = 48 * 1024 * 1024
MESH = pl.DeviceIdType.MESH
LOG2E = 1.4426950408889634
DENSE_HS = 2

NN = (((1,), (0,)), ((), ()))
NT = (((1,), (1,)), ((), ()))
TN = (((0,), (0,)), ((), ()))


def _pick(n, cands):
    for c in cands:
        if n % c == 0:
            return c
    return n


def _params(*sem):
    return pltpu.CompilerParams(dimension_semantics=sem, vmem_limit_bytes=VMEM_LIMIT)


def _bf(x):
    return x if x.dtype == BF16 else x.astype(BF16)


def _mm(a, b, mode, name, out_dtype=F32):
    if mode == "nn":
        (M, K), (_, N) = a.shape, b.shape
    elif mode == "nt":
        (M, K), (N, _) = a.shape, b.shape
    else:
        (K, M), (_, N) = a.shape, b.shape
    tm = _pick(M, (1664, 1408, 1024, 640, 512, 384, 256, 128))
    tn = _pick(N, (640, 512, 384, 1408, 256, 128))
    tk = K if (K <= 1024 and mode != "tn") else _pick(K, (640, 512, 384, 1408, 256, 128))
    nk = K // tk
    dn = {"nn": NN, "nt": NT, "tn": TN}[mode]

    def body(a_ref, b_ref, o_ref, *acc):
        part = lax.dot_general(_bf(a_ref[...]), _bf(b_ref[...]), dn, preferred_element_type=F32)
        if nk == 1:
            o_ref[...] = part.astype(out_dtype)
            return
        acc_ref, = acc
        k = pl.program_id(2)

        @pl.when(k == 0)
        def _():
            acc_ref[...] = part

        @pl.when(k > 0)
        def _():
            acc_ref[...] += part

        @pl.when(k == nk - 1)
        def _():
            o_ref[...] = acc_ref[...].astype(out_dtype)

    if mode == "tn":
        a_spec = pl.BlockSpec((tk, tm), lambda i, j, k: (k, i))
    else:
        a_spec = pl.BlockSpec((tm, tk), lambda i, j, k: (i, k))
    if mode == "nt":
        b_spec = pl.BlockSpec((tn, tk), lambda i, j, k: (j, k))
    else:
        b_spec = pl.BlockSpec((tk, tn), lambda i, j, k: (k, j))
    return pl.pallas_call(
        body, name=name, grid=(M // tm, N // tn, nk),
        in_specs=[a_spec, b_spec],
        out_specs=pl.BlockSpec((tm, tn), lambda i, j, k: (i, j)),
        out_shape=jax.ShapeDtypeStruct((M, N), out_dtype),
        scratch_shapes=[pltpu.VMEM((tm, tn), F32)] if nk > 1 else [],
        compiler_params=_params("parallel", "parallel", "arbitrary"),
    )(a, b)


def _ln_fwd(h, mix, g, b, name):
    L = h.shape[0]
    tr = 128

    def body(h_ref, m_ref, g_ref, b_ref, o_ref, ob_ref, xh_ref, rs_ref):
        z = ALPHA * h_ref[...] + m_ref[...]
        mu = jnp.mean(z, axis=1, keepdims=True)
        zc = z - mu
        var = jnp.mean(zc * zc, axis=1, keepdims=True)
        rstd = lax.rsqrt(var + LN_EPS)
        xh = zc * rstd
        xh_ref[...] = xh
        rs_ref[...] = rstd
        out = xh * g_ref[...] + b_ref[...]
        o_ref[...] = out
        ob_ref[...] = out.astype(BF16)

    row = pl.BlockSpec((tr, D_MODEL), lambda i: (i, 0))
    vec = pl.BlockSpec((1, D_MODEL), lambda i: (0, 0))
    return pl.pallas_call(
        body, name=name, grid=(L // tr,),
        in_specs=[row, row, vec, vec],
        out_specs=[row, row, row, pl.BlockSpec((tr, 1), lambda i: (i, 0))],
        out_shape=[jax.ShapeDtypeStruct((L, D_MODEL), F32), jax.ShapeDtypeStruct((L, D_MODEL), BF16),
                   jax.ShapeDtypeStruct((L, D_MODEL), F32), jax.ShapeDtypeStruct((L, 1), F32)],
        compiler_params=_params("parallel"),
    )(h, mix, g.reshape(1, D_MODEL), b.reshape(1, D_MODEL))


def _ln_bwd(ga, gb, xhat, rstd, g, name):
    L = xhat.shape[0]
    tr = 128
    two = ga is not None

    def body(*refs):
        if two:
            ga_ref, gb_ref, xh_ref, rs_ref, g_ref, dz_ref, dzb_ref, dg_ref, db_ref = refs
            dy = ALPHA * ga_ref[...] + gb_ref[...]
        else:
            gb_ref, xh_ref, rs_ref, g_ref, dz_ref, dzb_ref, dg_ref, db_ref = refs
            dy = gb_ref[...]
        xh = xh_ref[...]
        dxh = dy * g_ref[...]
        c1 = jnp.mean(dxh, axis=1, keepdims=True)
        c2 = jnp.mean(dxh * xh, axis=1, keepdims=True)
        dz = rs_ref[...] * (dxh - c1 - xh * c2)
        dz_ref[...] = dz
        dzb_ref[...] = dz.astype(BF16)

        @pl.when(pl.program_id(0) == 0)
        def _():
            dg_ref[...] = jnp.zeros_like(dg_ref)
            db_ref[...] = jnp.zeros_like(db_ref)

        dg_ref[...] += jnp.sum(dy * xh, axis=0, keepdims=True)
        db_ref[...] += jnp.sum(dy, axis=0, keepdims=True)

    row = pl.BlockSpec((tr, D_MODEL), lambda i: (i, 0))
    vec = pl.BlockSpec((1, D_MODEL), lambda i: (0, 0))
    ins = ([ga] if two else []) + [gb, xhat, rstd, g.reshape(1, D_MODEL)]
    specs = ([row] if two else []) + [row, row, pl.BlockSpec((tr, 1), lambda i: (i, 0)), vec]
    dz, dzb, dg, db = pl.pallas_call(
        body, name=name, grid=(L // tr,),
        in_specs=specs, out_specs=[row, row, vec, vec],
        out_shape=[jax.ShapeDtypeStruct((L, D_MODEL), F32), jax.ShapeDtypeStruct((L, D_MODEL), BF16),
                   jax.ShapeDtypeStruct((1, D_MODEL), F32), jax.ShapeDtypeStruct((1, D_MODEL), F32)],
        compiler_params=_params("arbitrary"),
    )(*ins)
    return dz, dzb, dg[0], db[0]


def _axpy(a, b, name):
    L, N = a.shape
    tr = 128

    def body(a_ref, b_ref, o_ref):
        o_ref[...] = ALPHA * a_ref[...] + b_ref[...]

    row = pl.BlockSpec((tr, N), lambda i: (i, 0))
    return pl.pallas_call(body, name=name, grid=(L // tr,), in_specs=[row, row], out_specs=row,
                          out_shape=jax.ShapeDtypeStruct((L, N), F32), compiler_params=_params("parallel"))(a, b)


def _shift_down(cur, prev8, n):
    rows = lax.broadcasted_iota(jnp.int32, cur.shape, 0)
    out = pltpu.roll(cur, n, 0)
    for r in range(n):
        out = jnp.where(rows == r, prev8[8 - n + r:8 - n + r + 1, :], out)
    return out


def _shift_up(cur, next8, n):
    tr = cur.shape[0]
    rows = lax.broadcasted_iota(jnp.int32, cur.shape, 0)
    out = pltpu.roll(cur, tr - n, 0)
    for r in range(n):
        out = jnp.where(rows == tr - n + r, next8[r:r + 1, :], out)
    return out


def _silu(x):
    return x / (1.0 + jnp.exp(-x))


def _conv_glu_fwd(u, cw, cb, name):
    L, F2 = u.shape
    F = F2 // 2
    tr = 128

    def body(u_ref, up_ref, cw_ref, cb_ref, y_ref, a_ref):
        i = pl.program_id(0)
        rows = i * tr + lax.broadcasted_iota(jnp.int32, (tr, F2), 0)
        cur = jnp.where(rows >= PAD, u_ref[...], 0.0)
        prow = i * tr - 8 + lax.broadcasted_iota(jnp.int32, (8, F2), 0)
        prev = jnp.where(prow >= PAD, up_ref[...], 0.0)
        y = cb_ref[...] + _shift_down(cur, prev, 2) * cw_ref[0:1, :]
        y = y + _shift_down(cur, prev, 1) * cw_ref[1:2, :]
        y = y + cur * cw_ref[2:3, :]
        y_ref[...] = y
        a_ref[...] = (_silu(y[:, :F]) * y[:, F:]).astype(BF16)

    return pl.pallas_call(
        body, name=name, grid=(L // tr,),
        in_specs=[pl.BlockSpec((tr, F2), lambda i: (i, 0)),
                  pl.BlockSpec((8, F2), lambda i: (jnp.maximum(i * (tr // 8) - 1, 0), 0)),
                  pl.BlockSpec((3, F2), lambda i: (0, 0)),
                  pl.BlockSpec((1, F2), lambda i: (0, 0))],
        out_specs=[pl.BlockSpec((tr, F2), lambda i: (i, 0)), pl.BlockSpec((tr, F), lambda i: (i, 0))],
        out_shape=[jax.ShapeDtypeStruct((L, F2), F32), jax.ShapeDtypeStruct((L, F), BF16)],
        compiler_params=_params("parallel"),
    )(u, u, cw, cb.reshape(1, F2))


def _conv_glu_bwd(y, da, u, cw, name):
    L, F2 = u.shape
    F = F2 // 2
    tr = 128
    nb = L // tr

    def dy_of(yv, dav):
        g, val = yv[:, :F], yv[:, F:]
        sg = 1.0 / (1.0 + jnp.exp(-g))
        dg = dav * val * (sg * (1.0 + g * (1.0 - sg)))
        dv = dav * (g * sg)
        return jnp.concatenate([dg, dv], axis=1)

    def body(y_ref, yn_ref, da_ref, dan_ref, u_ref, up_ref, cw_ref, du_ref, dcw_ref, dcb_ref):
        i = pl.program_id(0)
        rows = i * tr + lax.broadcasted_iota(jnp.int32, (tr, F2), 0)
        dy = dy_of(y_ref[...], da_ref[...])
        dyn = jnp.where(i < nb - 1, dy_of(yn_ref[...], dan_ref[...]), 0.0)
        du = dy * cw_ref[2:3, :] + _shift_up(dy, dyn, 1) * cw_ref[1:2, :] + _shift_up(dy, dyn, 2) * cw_ref[0:1, :]
        du_ref[...] = jnp.where(rows >= PAD, du, 0.0).astype(BF16)
        cur = jnp.where(rows >= PAD, u_ref[...], 0.0)
        prow = i * tr - 8 + lax.broadcasted_iota(jnp.int32, (8, F2), 0)
        prev = jnp.where(prow >= PAD, up_ref[...], 0.0)

        @pl.when(i == 0)
        def _():
            dcw_ref[...] = jnp.zeros_like(dcw_ref)
            dcb_ref[...] = jnp.zeros_like(dcb_ref)

        dcw_ref[0:1, :] += jnp.sum(dy * _shift_down(cur, prev, 2), axis=0, keepdims=True)
        dcw_ref[1:2, :] += jnp.sum(dy * _shift_down(cur, prev, 1), axis=0, keepdims=True)
        dcw_ref[2:3, :] += jnp.sum(dy * cur, axis=0, keepdims=True)
        dcb_ref[...] += jnp.sum(dy, axis=0, keepdims=True)

    nxt = lambda i: (jnp.minimum((i + 1) * (tr // 8), L // 8 - 1), 0)
    prv = lambda i: (jnp.maximum(i * (tr // 8) - 1, 0), 0)
    du, dcw, dcb = pl.pallas_call(
        body, name=name, grid=(nb,),
        in_specs=[pl.BlockSpec((tr, F2), lambda i: (i, 0)), pl.BlockSpec((8, F2), nxt),
                  pl.BlockSpec((tr, F), lambda i: (i, 0)), pl.BlockSpec((8, F), nxt),
                  pl.BlockSpec((tr, F2), lambda i: (i, 0)), pl.BlockSpec((8, F2), prv),
                  pl.BlockSpec((3, F2), lambda i: (0, 0))],
        out_specs=[pl.BlockSpec((tr, F2), lambda i: (i, 0)), pl.BlockSpec((3, F2), lambda i: (0, 0)),
                   pl.BlockSpec((1, F2), lambda i: (0, 0))],
        out_shape=[jax.ShapeDtypeStruct((L, F2), BF16), jax.ShapeDtypeStruct((3, F2), F32),
                   jax.ShapeDtypeStruct((1, F2), F32)],
        compiler_params=_params("arbitrary"),
    )(y, y, da, da, u, u, cw)
    return du, dcw, dcb[0]


def _dense_mask(qpos, kpos):
    return (kpos <= qpos) & (kpos >= PAD)


def _tables(pairs):
    qt, kt, ft = [], [], []
    for grp in pairs:
        for n, (qb, kb, msk) in enumerate(grp):
            qt.append(qb)
            kt.append(kb)
            ft.append((1 if n == 0 else 0) | (2 if n == len(grp) - 1 else 0) | (4 if msk else 0))
    return tuple(jnp.asarray(np.asarray(t, np.int32)) for t in (qt, kt, ft))


class _Dense:
    mask = staticmethod(_dense_mask)

    def __init__(self, L, pad_in_cr2):
        self.T = T = 640 if L % 640 == 0 else 128
        nb = L // T
        m = lambda qb, kb: kb == qb or (kb == 0 and not pad_in_cr2) or (qb * T < PAD)
        self.q_major = _tables([[(qb, kb, m(qb, kb)) for kb in range(qb + 1)] for qb in range(nb)])
        self.k_major = _tables([[(qb, kb, m(qb, kb)) for qb in range(kb, nb)] for kb in range(nb)])


def _positions(T, qb, kb):
    qpos = qb * T + lax.broadcasted_iota(jnp.int32, (T, T), 0)
    kpos = kb * T + lax.broadcasted_iota(jnp.int32, (T, T), 1)
    return qpos, kpos


def _scores(q, k, c, cr, masked, mask, T, qb, kb, backward):
    s = lax.dot_general(q, k, NT, preferred_element_type=F32) * c
    if cr is not None:
        s = s - cr
    live = None
    if masked:
        qpos, kpos = _positions(T, qb, kb)
        live = mask(qpos, kpos)
        if backward:
            live = live & (qpos >= PAD)
        s = jnp.where(live, s, NEG)
    return s, live


def _prob(s, lse, live):
    p = jnp.exp2(s - lse)
    return p if live is None else jnp.where(live, p, 0.0)


def _both(flag, fn):
    pl.when(flag != 0)(lambda: fn(True))
    pl.when(flag == 0)(lambda: fn(False))


def _flash_fwd(cfg, qa, q_off, ka, k_off, va, v_off, H, hs, scale, name, cr2=None):
    L = qa.shape[0]
    T = cfg.T
    qt, kt, ft = cfg.q_major
    npairs = qt.shape[0]
    c = scale * LOG2E
    decay = cr2 is not None
    W = hs * LANES

    def body(qt_ref, kt_ref, ft_ref, *refs):
        it = iter(refs)
        q_ref, k_ref, v_ref = next(it), next(it), next(it)
        cr_ref = next(it) if decay else None
        o_ref, lse_ref, m_sc, l_sc, acc_sc = next(it), next(it), next(it), next(it), next(it)
        n = pl.program_id(1)
        qb, kb, f = qt_ref[n], kt_ref[n], ft_ref[n]

        @pl.when((f & 1) != 0)
        def _():
            m_sc[...] = jnp.full(m_sc.shape, NEG, F32)
            l_sc[...] = jnp.zeros_like(l_sc)
            acc_sc[...] = jnp.zeros_like(acc_sc)

        def step(masked):
            for i in range(hs):
                cols = slice(i * LANES, (i + 1) * LANES)
                cr = cr_ref[i] if decay else None
                s, _ = _scores(_bf(q_ref[:, cols]), _bf(k_ref[:, cols]), c, cr, masked, cfg.mask, T, qb, kb, False)
                m_prev = m_sc[i]
                m_new = jnp.maximum(m_prev, jnp.max(s, axis=1, keepdims=True))
                alpha = jnp.exp2(m_prev - m_new)
                p = jnp.exp2(s - m_new)
                l_sc[i] = alpha * l_sc[i] + jnp.sum(p, axis=1, keepdims=True)
                acc_sc[i] = alpha * acc_sc[i] + lax.dot_general(p.astype(BF16), _bf(v_ref[:, cols]), NN,
                                                                preferred_element_type=F32)
                m_sc[i] = m_new

        _both(f & 4, step)

        @pl.when((f & 2) != 0)
        def _():
            for i in range(hs):
                cols = slice(i * LANES, (i + 1) * LANES)
                l = l_sc[i]
                o_ref[:, cols] = acc_sc[i] / l
                lse_ref[:, cols] = jnp.broadcast_to(m_sc[i] + jnp.log(l) * LOG2E, (T, LANES))

    qrow = lambda off: pl.BlockSpec((T, W), lambda h, n, qt, kt, ft: (qt[n], off // hs + h))
    krow = lambda off: pl.BlockSpec((T, W), lambda h, n, qt, kt, ft: (kt[n], off // hs + h))
    ins, specs = [qa, ka, va], [qrow(q_off), krow(k_off), krow(v_off)]
    if decay:
        ins.append(cr2)
        specs.append(pl.BlockSpec((hs, 1, T), lambda h, n, qt, kt, ft: (h, 0, kt[n])))
    full = jax.ShapeDtypeStruct((L, H * LANES), F32)
    return pl.pallas_call(
        body, name=name, out_shape=[full, full],
        grid_spec=pltpu.PrefetchScalarGridSpec(
            num_scalar_prefetch=3, grid=(H // hs, npairs), in_specs=specs, out_specs=[qrow(0), qrow(0)],
            scratch_shapes=[pltpu.VMEM((hs, T, 1), F32), pltpu.VMEM((hs, T, 1), F32), pltpu.VMEM((hs, T, LANES), F32)]),
        compiler_params=_params("parallel", "arbitrary"),
    )(qt, kt, ft, *ins)


def _flash_bwd(cfg, qa, q_off, ka, k_off, va, v_off, do, lse2, delta, H, hs, scale, name, cr2=None, lane_sums=False):
    L = qa.shape[0]
    T = cfg.T
    qt, kt, ft = cfg.k_major
    npairs = qt.shape[0]
    c = scale * LOG2E
    decay = cr2 is not None
    W = hs * LANES

    def body(qt_ref, kt_ref, ft_ref, *refs):
        it = iter(refs)
        q_ref, k_ref, v_ref = next(it), next(it), next(it)
        cr_ref = next(it) if decay else None
        do_ref, lse_ref, dl_ref, dk_ref, dv_ref, dq_ref, dk_sc, dv_sc = (next(it) for _ in range(8))
        n = pl.program_id(1)
        qb, kb, f = qt_ref[n], kt_ref[n], ft_ref[n]

        @pl.when(n == 0)
        def _():
            dq_ref[...] = jnp.zeros_like(dq_ref)

        @pl.when((f & 1) != 0)
        def _():
            dk_sc[...] = jnp.zeros_like(dk_sc)
            dv_sc[...] = jnp.zeros_like(dv_sc)

        def step(masked):
            last = lax.broadcasted_iota(jnp.int32, (T, LANES), 1) == LANES - 1
            rows = pl.ds(pl.multiple_of(qb * T, T), T)
            for i in range(hs):
                cols = slice(i * LANES, (i + 1) * LANES)
                q, k, v, dob = _bf(q_ref[:, cols]), _bf(k_ref[:, cols]), _bf(v_ref[:, cols]), _bf(do_ref[:, cols])
                k1 = jnp.where(last, 1.0, k_ref[:, cols]).astype(BF16) if lane_sums else k
                q1 = jnp.where(last, 1.0, q_ref[:, cols]).astype(BF16) if lane_sums else q
                s, live = _scores(q, k, c, cr_ref[i] if decay else None, masked, cfg.mask, T, qb, kb, True)
                p = _prob(s, lse_ref[:, i * LANES:i * LANES + 1], live)
                dv_sc[i] += lax.dot_general(p.astype(BF16), dob, TN, preferred_element_type=F32)
                dp = lax.dot_general(dob, v, NT, preferred_element_type=F32)
                dsb = (p * (dp - dl_ref[:, i * LANES:i * LANES + 1])).astype(BF16)
                dk_sc[i] += lax.dot_general(dsb, q1, TN, preferred_element_type=F32)
                dq_ref[rows, cols] += lax.dot_general(dsb, k1, NN, preferred_element_type=F32)

        _both(f & 4, step)

        @pl.when((f & 2) != 0)
        def _():
            for i in range(hs):
                cols = slice(i * LANES, (i + 1) * LANES)
                dk_ref[:, cols] = dk_sc[i] * scale
                dv_ref[:, cols] = dv_sc[i]

        @pl.when(n == npairs - 1)
        def _():
            dq_ref[...] = dq_ref[...] * scale

    qrow = lambda off: pl.BlockSpec((T, W), lambda h, n, qt, kt, ft: (qt[n], off // hs + h))
    krow = lambda off: pl.BlockSpec((T, W), lambda h, n, qt, kt, ft: (kt[n], off // hs + h))
    ins, specs = [qa, ka, va], [qrow(q_off), krow(k_off), krow(v_off)]
    if decay:
        ins.append(cr2)
        specs.append(pl.BlockSpec((hs, 1, T), lambda h, n, qt, kt, ft: (h, 0, kt[n])))
    ins += [do, lse2, delta]
    specs += [qrow(0), qrow(0), qrow(0)]
    full = jax.ShapeDtypeStruct((L, H * LANES), F32)
    return pl.pallas_call(
        body, name=name, out_shape=[full, full, full],
        grid_spec=pltpu.PrefetchScalarGridSpec(
            num_scalar_prefetch=3, grid=(H // hs, npairs), in_specs=specs,
            out_specs=[krow(0), krow(0), pl.BlockSpec((L, W), lambda h, n, qt, kt, ft: (0, h))],
            scratch_shapes=[pltpu.VMEM((hs, T, LANES), F32), pltpu.VMEM((hs, T, LANES), F32)]),
        compiler_params=_params("parallel", "arbitrary"),
    )(qt, kt, ft, *ins)


def _swa_parts(qb, q_ref, km_ref, kp_ref, kc_ref, vm_ref, vp_ref, vc_ref, c):
    G, B = SWA_G, BLOCK
    q = jnp.concatenate([_bf(q_ref[:, i * LANES:(i + 1) * LANES]) for i in range(G)], axis=0)
    kc = jnp.concatenate([_bf(km_ref[...]), _bf(kp_ref[...]), _bf(kc_ref[...])], axis=0)
    vc = jnp.concatenate([_bf(vm_ref[...]), _bf(vp_ref[...]), _bf(vc_ref[...])], axis=0)
    s = lax.dot_general(q, kc, NT, preferred_element_type=F32) * c
    row = lax.broadcasted_iota(jnp.int32, (G * B, 3 * B), 0)
    col = lax.broadcasted_iota(jnp.int32, (G * B, 3 * B), 1)
    qpos = qb * B + (row & (B - 1))
    kpos = jnp.where(col < B, col, jnp.where(col < 2 * B, (qb - 1) * B + col - B, qb * B + col - 2 * B))
    d = qpos - kpos
    live = ((col < B) & (kpos >= PAD) & (kpos <= qpos)) | ((col >= B) & (kpos >= B) & (d >= 0) & (d < WINDOW))
    return q, kc, vc, jnp.where(live, s, NEG), live


def _stack_col(ref):
    return jnp.concatenate([ref[:, i * LANES:i * LANES + 1] for i in range(SWA_G)], axis=0)


def _swa_specs(nqk):
    G, B = SWA_G, BLOCK
    qrow = pl.BlockSpec((B, G * LANES), lambda hk, qb: (qb, hk))
    kv = lambda off, blk: pl.BlockSpec((B, LANES), lambda hk, qb: (blk(qb), off + hk))
    zero, prev, cur = (lambda qb: 0), (lambda qb: jnp.maximum(qb - 1, 0)), (lambda qb: qb)
    keys = [kv(HEADS, zero), kv(HEADS, prev), kv(HEADS, cur)]
    vals = [kv(nqk, zero), kv(nqk, prev), kv(nqk, cur)]
    return qrow, keys, vals


def _swa_attn_fwd(qk, proj, sink2, scale, name):
    L = qk.shape[0]
    G, B = SWA_G, BLOCK
    nqk = HEADS + SWA_KV
    c = scale * LOG2E

    def body(q_ref, km_ref, kp_ref, kc_ref, vm_ref, vp_ref, vc_ref, sink_ref, o_ref, lse_ref):
        qb = pl.program_id(1)
        q, kc, vc, s, live = _swa_parts(qb, q_ref, km_ref, kp_ref, kc_ref, vm_ref, vp_ref, vc_ref, c)
        sink = jnp.concatenate([jnp.broadcast_to(sink_ref[:, i * LANES:i * LANES + 1], (B, 1)) for i in range(G)], axis=0)
        m = jnp.maximum(jnp.max(s, axis=1, keepdims=True), sink)
        p = jnp.exp2(s - m)
        l = jnp.sum(p, axis=1, keepdims=True) + jnp.exp2(sink - m)
        o = lax.dot_general(p.astype(BF16), vc, NN, preferred_element_type=F32) / l
        lse = m + jnp.log(l) * LOG2E
        for i in range(G):
            o_ref[:, i * LANES:(i + 1) * LANES] = o[i * B:(i + 1) * B]
            lse_ref[:, i * LANES:(i + 1) * LANES] = jnp.broadcast_to(lse[i * B:(i + 1) * B], (B, LANES))

    qrow, keys, vals = _swa_specs(nqk)
    shape = jax.ShapeDtypeStruct((L, HEADS * LANES), F32)
    return pl.pallas_call(
        body, name=name, grid=(SWA_KV, L // B),
        in_specs=[qrow] + keys + vals + [pl.BlockSpec((1, G * LANES), lambda hk, qb: (0, hk))],
        out_specs=[qrow, qrow], out_shape=[shape, shape],
        compiler_params=_params("parallel", "parallel"),
    )(qk, qk, qk, qk, proj, proj, proj, sink2)


def _swa_attn_bwd(qk, proj, do, lse2, delta, scale, name):
    L = qk.shape[0]
    G, B = SWA_G, BLOCK
    nqk = HEADS + SWA_KV
    c = scale * LOG2E

    def body(q_ref, km_ref, kp_ref, kc_ref, vm_ref, vp_ref, vc_ref, do_ref, lse_ref, dl_ref, dq_ref, dk_ref, dv_ref):
        qb = pl.program_id(1)

        @pl.when(qb == 0)
        def _():
            dk_ref[...] = jnp.zeros_like(dk_ref)
            dv_ref[...] = jnp.zeros_like(dv_ref)

        q, kc, vc, s, live = _swa_parts(qb, q_ref, km_ref, kp_ref, kc_ref, vm_ref, vp_ref, vc_ref, c)
        dob = jnp.concatenate([_bf(do_ref[:, i * LANES:(i + 1) * LANES]) for i in range(G)], axis=0)
        p = jnp.where(live, jnp.exp2(s - _stack_col(lse_ref)), 0.0)
        dp = lax.dot_general(dob, vc, NT, preferred_element_type=F32)
        dsb = (p * (dp - _stack_col(dl_ref))).astype(BF16)
        dq = lax.dot_general(dsb, kc, NN, preferred_element_type=F32) * scale
        for i in range(G):
            dq_ref[:, i * LANES:(i + 1) * LANES] = dq[i * B:(i + 1) * B]
        dkc = lax.dot_general(dsb, q, TN, preferred_element_type=F32) * scale
        dvc = lax.dot_general(p.astype(BF16), dob, TN, preferred_element_type=F32)
        starts = (0, pl.multiple_of(jnp.maximum(qb - 1, 0) * B, B), pl.multiple_of(qb * B, B))
        for n, st in enumerate(starts):
            dk_ref[pl.ds(st, B), :] += dkc[n * B:(n + 1) * B]
            dv_ref[pl.ds(st, B), :] += dvc[n * B:(n + 1) * B]

    qrow, keys, vals = _swa_specs(nqk)
    res = pl.BlockSpec((L, LANES), lambda hk, qb: (0, hk))
    return pl.pallas_call(
        body, name=name, grid=(SWA_KV, L // B),
        in_specs=[qrow] + keys + vals + [qrow, qrow, qrow],
        out_specs=[qrow, res, res],
        out_shape=[jax.ShapeDtypeStruct((L, HEADS * LANES), F32), jax.ShapeDtypeStruct((L, SWA_KV * LANES), F32),
                   jax.ShapeDtypeStruct((L, SWA_KV * LANES), F32)],
        compiler_params=_params("parallel", "arbitrary"),
    )(qk, qk, qk, qk, proj, proj, proj, do, lse2, delta)


def _delta(do, o, name):
    L, HW = do.shape
    T = _pick(L, (640, 128))

    def body(do_ref, o_ref, d_ref):
        d_ref[...] = jnp.broadcast_to(jnp.sum(do_ref[...].astype(F32) * o_ref[...], axis=1, keepdims=True), (T, LANES))

    spec = pl.BlockSpec((T, LANES), lambda h, i: (i, h))
    return pl.pallas_call(body, name=name, grid=(HW // LANES, L // T), in_specs=[spec, spec], out_specs=spec,
                          out_shape=jax.ShapeDtypeStruct((L, HW), F32),
                          compiler_params=_params("parallel", "parallel"))(do, o)


def _sink_grad(lse2, delta, sink2, name):
    L, HW = lse2.shape
    tr = 128

    def body(lse_ref, dl_ref, s_ref, o_ref):
        @pl.when(pl.program_id(0) == 0)
        def _():
            o_ref[...] = jnp.zeros_like(o_ref)

        o_ref[...] -= jnp.sum(jnp.exp2(s_ref[...] - lse_ref[...]) * dl_ref[...], axis=0, keepdims=True)

    row = pl.BlockSpec((tr, HW), lambda i: (i, 0))
    vec = pl.BlockSpec((1, HW), lambda i: (0, 0))
    return pl.pallas_call(body, name=name, grid=(L // tr,), in_specs=[row, row, vec], out_specs=vec,
                          out_shape=jax.ShapeDtypeStruct((1, HW), F32), compiler_params=_params("arbitrary"))(
        lse2, delta, sink2)


def _tri(lower):
    r = lax.broadcasted_iota(jnp.int32, (BLOCK, BLOCK), 0)
    c = lax.broadcasted_iota(jnp.int32, (BLOCK, BLOCK), 1)
    return jnp.where((c <= r) if lower else (c >= r), 1.0, 0.0).astype(F32)


def _gate_cumsum(proj, fg_tile, b_pad, name):
    L = proj.shape[0]

    def body(fg_ref, b_ref, c_ref, carry):
        @pl.when(pl.program_id(0) == 0)
        def _():
            carry[...] = jnp.zeros_like(carry)

        x = fg_ref[...] + b_ref[...]
        lf = jnp.minimum(x, 0.0) - jnp.log(1.0 + jnp.exp(-jnp.abs(x)))
        c = jnp.dot(_tri(True), lf, precision=lax.Precision.HIGHEST, preferred_element_type=F32) + carry[...]
        c_ref[...] = c
        carry[...] = c[BLOCK - 1:BLOCK, :]

    return pl.pallas_call(
        body, name=name, grid=(L // BLOCK,),
        in_specs=[pl.BlockSpec((BLOCK, LANES), lambda i: (i, fg_tile)), pl.BlockSpec((1, LANES), lambda i: (0, 0))],
        out_specs=pl.BlockSpec((BLOCK, LANES), lambda i: (i, 0)),
        out_shape=jax.ShapeDtypeStruct((L, LANES), F32),
        scratch_shapes=[pltpu.VMEM((1, LANES), F32)],
        compiler_params=_params("arbitrary"),
    )(proj, b_pad)


def _gate_cumsum_bwd(dc, proj, fg_tile, b_pad, name):
    L = proj.shape[0]
    nb = L // BLOCK

    def body(dc_ref, fg_ref, b_ref, dfg_ref, db_ref, carry):
        @pl.when(pl.program_id(0) == 0)
        def _():
            carry[...] = jnp.zeros_like(carry)
            db_ref[...] = jnp.zeros_like(db_ref)

        dlf = jnp.dot(_tri(False), dc_ref[...], precision=lax.Precision.HIGHEST,
                      preferred_element_type=F32) + carry[...]
        carry[...] = dlf[0:1, :]
        x = fg_ref[...] + b_ref[...]
        lanes = lax.broadcasted_iota(jnp.int32, (BLOCK, LANES), 1)
        rows = (nb - 1 - pl.program_id(0)) * BLOCK + lax.broadcasted_iota(jnp.int32, (BLOCK, LANES), 0)
        dfg = jnp.where((lanes < HEADS) & (rows >= PAD), dlf / (1.0 + jnp.exp(x)), 0.0)
        dfg_ref[...] = jnp.concatenate([dfg, jnp.zeros_like(dfg)], axis=1)
        db_ref[...] += jnp.sum(dfg, axis=0, keepdims=True)

    dfg, db = pl.pallas_call(
        body, name=name, grid=(nb,),
        in_specs=[pl.BlockSpec((BLOCK, LANES), lambda i: (nb - 1 - i, 0)),
                  pl.BlockSpec((BLOCK, LANES), lambda i: (nb - 1 - i, fg_tile)),
                  pl.BlockSpec((1, LANES), lambda i: (0, 0))],
        out_specs=[pl.BlockSpec((BLOCK, 2 * LANES), lambda i: (nb - 1 - i, 0)),
                   pl.BlockSpec((1, LANES), lambda i: (0, 0))],
        out_shape=[jax.ShapeDtypeStruct((L, 2 * LANES), F32), jax.ShapeDtypeStruct((1, LANES), F32)],
        scratch_shapes=[pltpu.VMEM((1, LANES), F32)],
        compiler_params=_params("arbitrary"),
    )(dc, proj, b_pad)
    return dfg, db[0]


def _rope_tables(L, dim, theta, lane0):
    half = dim // 2
    pos = (jnp.arange(L) - PAD).astype(F32)
    inv = theta ** (-jnp.arange(0, dim, 2, dtype=F32) / dim)
    ang = pos[:, None] * inv[None, :]
    cos, sin = jnp.cos(ang), jnp.sin(ang)
    C = jnp.ones((L, LANES), F32).at[:, lane0:lane0 + half].set(cos).at[:, lane0 + half:lane0 + dim].set(cos)
    S1 = jnp.zeros((L, LANES), F32).at[:, lane0:lane0 + half].set(-sin)
    S2 = jnp.zeros((L, LANES), F32).at[:, lane0 + half:lane0 + dim].set(sin)
    return C, S1, S2


def _rot(x, C, S1, S2, R):
    return x * C + pltpu.roll(x, LANES - R, 1) * S1 + pltpu.roll(x, R, 1) * S2


def _rot_t(dy, C, S1, S2, R):
    return dy * C + pltpu.roll(dy * S1, R, 1) + pltpu.roll(dy * S2, LANES - R, 1)


def _rope(x, nt, tabs, R, name, transpose=False, shared=None, shared_tile=0, out_dtype=F32):
    L = x.shape[0]
    T = BLOCK
    fn = _rot_t if transpose else _rot

    def body(*refs):
        if shared is None:
            x_ref, c_ref, s1_ref, s2_ref, o_ref = refs
        else:
            x_ref, sh_ref, c_ref, s1_ref, s2_ref, o_ref = refs
            rs = fn(sh_ref[...], c_ref[...], s1_ref[...], s2_ref[...], R)
        for h in range(nt):
            cols = slice(h * LANES, (h + 1) * LANES)
            if shared is None:
                o_ref[:, cols] = fn(x_ref[:, cols], c_ref[...], s1_ref[...], s2_ref[...], R).astype(out_dtype)
            else:
                o_ref[:, cols] = (x_ref[:, cols] + rs).astype(out_dtype)

    wide = pl.BlockSpec((T, nt * LANES), lambda i: (i, 0))
    tab = pl.BlockSpec((T, LANES), lambda i: (i, 0))
    ins, specs = [x], [wide]
    if shared is not None:
        ins.append(shared)
        specs.append(pl.BlockSpec((T, LANES), lambda i: (i, shared_tile)))
    return pl.pallas_call(body, name=name, grid=(L // T,), in_specs=specs + [tab, tab, tab], out_specs=wide,
                          out_shape=jax.ShapeDtypeStruct((L, nt * LANES), out_dtype),
                          compiler_params=_params("parallel"))(*ins, *tabs)


def _rope_shared_bwd(dk, nt, tabs, R, name):
    L = dk.shape[0]
    tr = 128

    def body(dk_ref, c_ref, s1_ref, s2_ref, o_ref):
        acc = dk_ref[:, 0:LANES]
        for h in range(1, nt):
            acc = acc + dk_ref[:, h * LANES:(h + 1) * LANES]
        o_ref[...] = _rot_t(acc, c_ref[...], s1_ref[...], s2_ref[...], R)

    tab = pl.BlockSpec((tr, LANES), lambda i: (i, 0))
    return pl.pallas_call(body, name=name, grid=(L // tr,),
                          in_specs=[pl.BlockSpec((tr, nt * LANES), lambda i: (i, 0)), tab, tab, tab], out_specs=tab,
                          out_shape=jax.ShapeDtypeStruct((L, LANES), F32), compiler_params=_params("parallel"))(
        dk, *tabs)


def _rms_fwd(pa, gq, gkv, name):
    L = pa.shape[0]
    tr = 128
    Q, KV = MLA_Q_LORA, MLA_KV_LORA

    def body(pa_ref, gq_ref, gkv_ref, q_ref, kv_ref):
        for lo, n, g_ref, o_ref in ((0, Q, gq_ref, q_ref), (Q, KV, gkv_ref, kv_ref)):
            x = pa_ref[:, lo:lo + n]
            r = lax.rsqrt(jnp.mean(x * x, axis=1, keepdims=True) + RMS_EPS)
            o_ref[...] = (x * r * g_ref[...]).astype(BF16)

    return pl.pallas_call(
        body, name=name, grid=(L // tr,),
        in_specs=[pl.BlockSpec((tr, pa.shape[1]), lambda i: (i, 0)), pl.BlockSpec((1, Q), lambda i: (0, 0)),
                  pl.BlockSpec((1, KV), lambda i: (0, 0))],
        out_specs=[pl.BlockSpec((tr, Q), lambda i: (i, 0)), pl.BlockSpec((tr, KV), lambda i: (i, 0))],
        out_shape=[jax.ShapeDtypeStruct((L, Q), BF16), jax.ShapeDtypeStruct((L, KV), BF16)],
        compiler_params=_params("parallel"),
    )(pa, gq.reshape(1, Q), gkv.reshape(1, KV))


def _rms_bwd(pa, dq, dkv, dkr, gq, gkv, name):
    L, W = pa.shape
    tr = 128
    Q, KV = MLA_Q_LORA, MLA_KV_LORA

    def body(pa_ref, dq_ref, dkv_ref, dkr_ref, gq_ref, gkv_ref, dpa_ref, dgq_ref, dgkv_ref):
        @pl.when(pl.program_id(0) == 0)
        def _():
            dgq_ref[...] = jnp.zeros_like(dgq_ref)
            dgkv_ref[...] = jnp.zeros_like(dgkv_ref)

        for lo, n, g_ref, dy_ref, dg_ref in ((0, Q, gq_ref, dq_ref, dgq_ref), (Q, KV, gkv_ref, dkv_ref, dgkv_ref)):
            x = pa_ref[:, lo:lo + n]
            r = lax.rsqrt(jnp.mean(x * x, axis=1, keepdims=True) + RMS_EPS)
            xh = x * r
            dy = dy_ref[...]
            dxh = dy * g_ref[...]
            dpa_ref[:, lo:lo + n] = (r * (dxh - xh * jnp.mean(dxh * xh, axis=1, keepdims=True))).astype(BF16)
            dg_ref[...] += jnp.sum(dy * xh, axis=0, keepdims=True)
        dpa_ref[:, Q + KV:W] = dkr_ref[...].astype(BF16)

    vq = pl.BlockSpec((1, Q), lambda i: (0, 0))
    vkv = pl.BlockSpec((1, KV), lambda i: (0, 0))
    dpa, dgq, dgkv = pl.pallas_call(
        body, name=name, grid=(L // tr,),
        in_specs=[pl.BlockSpec((tr, W), lambda i: (i, 0)), pl.BlockSpec((tr, Q), lambda i: (i, 0)),
                  pl.BlockSpec((tr, KV), lambda i: (i, 0)), pl.BlockSpec((tr, LANES), lambda i: (i, 0)), vq, vkv],
        out_specs=[pl.BlockSpec((tr, W), lambda i: (i, 0)), vq, vkv],
        out_shape=[jax.ShapeDtypeStruct((L, W), BF16), jax.ShapeDtypeStruct((1, Q), F32),
                   jax.ShapeDtypeStruct((1, KV), F32)],
        compiler_params=_params("arbitrary"),
    )(pa, dq, dkv, dkr, gq.reshape(1, Q), gkv.reshape(1, KV))
    return dpa, dgq[0], dgkv[0]


def _loss_head(h, target, name):
    L = h.shape[0]
    tr = BLOCK
    inv = 1.0 / D_MODEL

    def body(h_ref, t_ref, loss_ref, dh_ref):
        i = pl.program_id(0)

        @pl.when(i == 0)
        def _():
            loss_ref[...] = jnp.zeros_like(loss_ref)
            dh_ref[...] = jnp.zeros_like(dh_ref)

        @pl.when(i > 0)
        def _():
            e = h_ref[...] - t_ref[...]
            dh_ref[...] = e * inv
            loss_ref[...] += jnp.sum((e * e).reshape(tr // 8, 8, D_MODEL), axis=0) * (0.5 * inv)

    row = pl.BlockSpec((tr, D_MODEL), lambda i: (i, 0))
    loss, dh = pl.pallas_call(
        body, name=name, grid=(L // tr,),
        in_specs=[row, pl.BlockSpec((tr, D_MODEL), lambda i: (jnp.maximum(i - 1, 0), 0))],
        out_specs=[pl.BlockSpec((8, D_MODEL), lambda i: (0, 0)), row],
        out_shape=[jax.ShapeDtypeStruct((8, D_MODEL), F32), jax.ShapeDtypeStruct((L, D_MODEL), F32)],
        compiler_params=_params("arbitrary"),
    )(h, target)
    return loss, dh


def _pad_heads_cols(w, nh, d, dp=LANES):
    K = w.shape[0]
    return jnp.pad(w.reshape(K, nh, d), ((0, 0), (0, 0), (0, dp - d))).reshape(K, nh * dp)


def _unpad_heads_cols(w, nh, d, dp=LANES):
    K = w.shape[0]
    return w.reshape(K, nh, dp)[:, :, :d].reshape(K, nh * d)


def _pad_heads_rows(w, nh, d):
    N = w.shape[1]
    return jnp.pad(w.reshape(nh, d, N), ((0, 0), (0, LANES - d), (0, 0))).reshape(nh * LANES, N)


def _unpad_heads_rows(w, nh, d):
    N = w.shape[1]
    return w.reshape(nh, LANES, N)[:, :d, :].reshape(nh * d, N)


def _fox_fwd(h, w_in, b_f, w_o, tag):
    L = h.shape[0]
    hd = HEADS * HEAD_DIM
    W = jnp.concatenate([_pad_heads_cols(w_in[:, i * hd:(i + 1) * hd], HEADS, HEAD_DIM) for i in range(3)]
                        + [jnp.pad(w_in[:, 3 * hd:], ((0, 0), (0, 2 * LANES - HEADS)))], axis=1)
    Wo = _pad_heads_rows(w_o, HEADS, HEAD_DIM)
    b_pad = jnp.pad(b_f, (0, LANES - HEADS)).reshape(1, LANES)
    proj = _mm(h, W, "nn", tag + "_proj")
    c = _gate_cumsum(proj, 3 * HEADS, b_pad, tag + "_cumsum")
    dead = (jnp.arange(L) < PAD)[:, None]
    cr2 = jnp.where(dead, -NEG, c[:, :HEADS] * LOG2E).T.reshape(HEADS, 1, L)
    cfg = _Dense(L, True)
    scale = HEAD_DIM ** -0.5
    o, lse2 = _flash_fwd(cfg, proj, 0, proj, HEADS, proj, 2 * HEADS, HEADS, DENSE_HS, scale, tag + "_attn", cr2=cr2)
    mix = _mm(o, Wo, "nn", tag + "_out")
    return mix, (h, W, Wo, b_pad, proj, cr2, o, lse2)


def _fox_bwd(dmix, res, tag):
    h, W, Wo, b_pad, proj, cr2, o, lse2 = res
    L = h.shape[0]
    cfg = _Dense(L, True)
    scale = HEAD_DIM ** -0.5
    dWo = _mm(o, dmix, "tn", tag + "_dwo")
    do = _mm(dmix, Wo, "nt", tag + "_do", out_dtype=BF16)
    qkv = (proj, 0, proj, HEADS, proj, 2 * HEADS)
    delta = _delta(do, o, tag + "_delta")
    dk, dv, dq = _flash_bwd(cfg, *qkv, do, lse2, delta, HEADS, DENSE_HS, scale, tag + "_bwd", cr2=cr2, lane_sums=True)
    dc = jnp.pad((dq[:, LANES - 1::LANES] - dk[:, LANES - 1::LANES]) * (1.0 / scale), ((0, 0), (0, LANES - HEADS)))
    dfg, db = _gate_cumsum_bwd(dc, proj, 3 * HEADS, b_pad, tag + "_cumsum_bwd")
    dproj = jnp.concatenate([dq.astype(BF16), dk.astype(BF16), dv.astype(BF16), dfg.astype(BF16)], axis=1)
    dW = _mm(h, dproj, "tn", tag + "_dw")
    dh = _mm(dproj, W, "nt", tag + "_dh")
    hp = HEADS * LANES
    dw_in = jnp.concatenate([_unpad_heads_cols(dW[:, i * hp:(i + 1) * hp], HEADS, HEAD_DIM) for i in range(3)]
                            + [dW[:, 3 * hp:3 * hp + HEADS]], axis=1)
    return dh, dict(w_in=dw_in, b_f=db[:HEADS], w_o=_unpad_heads_rows(dWo, HEADS, HEAD_DIM))


def _swa_fwd(h, w_in, sinks, w_o, tag):
    L = h.shape[0]
    qd, kd = HEADS * HEAD_DIM, SWA_KV * HEAD_DIM
    W = jnp.concatenate([_pad_heads_cols(w_in[:, :qd], HEADS, HEAD_DIM),
                         _pad_heads_cols(w_in[:, qd:qd + kd], SWA_KV, HEAD_DIM),
                         _pad_heads_cols(w_in[:, qd + kd:], SWA_KV, HEAD_DIM)], axis=1)
    Wo = _pad_heads_rows(w_o, HEADS, HEAD_DIM)
    sink2 = jnp.repeat(sinks * LOG2E, LANES).reshape(1, HEADS * LANES)
    tabs = _rope_tables(L, ROPE_DIM, ROPE_THETA, 0)
    proj = _mm(h, W, "nn", tag + "_proj")
    nqk = HEADS + SWA_KV
    qk = _rope(proj, nqk, tabs, ROPE_DIM // 2, tag + "_rope")
    scale = HEAD_DIM ** -0.5
    o, lse2 = _swa_attn_fwd(qk, proj, sink2, scale, tag + "_attn")
    mix = _mm(o, Wo, "nn", tag + "_out")
    return mix, (h, W, Wo, sink2, tabs, proj, qk, o, lse2)


def _swa_bwd(dmix, res, tag):
    h, W, Wo, sink2, tabs, proj, qk, o, lse2 = res
    L = h.shape[0]
    nqk = HEADS + SWA_KV
    scale = HEAD_DIM ** -0.5
    dWo = _mm(o, dmix, "tn", tag + "_dwo")
    do = _mm(dmix, Wo, "nt", tag + "_do", out_dtype=BF16)
    delta = _delta(do, o, tag + "_delta")
    dsink = _sink_grad(lse2, delta, sink2, tag + "_dsink")[0, ::LANES]
    dq, dk, dv = _swa_attn_bwd(qk, proj, do, lse2, delta, scale, tag + "_bwd")
    dqk = _rope(jnp.concatenate([dq, dk], axis=1), nqk, tabs, ROPE_DIM // 2, tag + "_rope_bwd", transpose=True,
                out_dtype=BF16)
    dproj = jnp.concatenate([dqk, dv.astype(BF16)], axis=1)
    dW = _mm(h, dproj, "tn", tag + "_dw")
    dh = _mm(dproj, W, "nt", tag + "_dh")
    hp = HEADS * LANES
    dw_in = jnp.concatenate([_unpad_heads_cols(dW[:, :hp], HEADS, HEAD_DIM),
                             _unpad_heads_cols(dW[:, hp:hp + SWA_KV * LANES], SWA_KV, HEAD_DIM),
                             _unpad_heads_cols(dW[:, hp + SWA_KV * LANES:], SWA_KV, HEAD_DIM)], axis=1)
    return dh, dict(w_in=dw_in, sinks=dsink, w_o=_unpad_heads_rows(dWo, HEADS, HEAD_DIM))


def _mla_fwd(h, w_a, g_q, g_kv, w_uq, w_ukv, w_o, tag):
    L = h.shape[0]
    Q, KV = MLA_Q_LORA, MLA_KV_LORA
    dqk = MLA_NOPE + MLA_ROPE
    kr_w = jnp.pad(w_a[:, Q + KV:], ((0, 0), (MLA_NOPE, LANES - dqk)))
    Wa = jnp.concatenate([w_a[:, :Q + KV], kr_w], axis=1)
    Wuq = _pad_heads_cols(w_uq, HEADS, dqk)
    ukv = w_ukv.reshape(KV, HEADS, MLA_NOPE + HEAD_DIM)
    Wukv = jnp.concatenate([_pad_heads_cols(ukv[:, :, :MLA_NOPE].reshape(KV, -1), HEADS, MLA_NOPE),
                            _pad_heads_cols(ukv[:, :, MLA_NOPE:].reshape(KV, -1), HEADS, HEAD_DIM)], axis=1)
    Wo = _pad_heads_rows(w_o, HEADS, HEAD_DIM)
    tabs = _rope_tables(L, MLA_ROPE, MLA_ROPE_THETA, MLA_NOPE)
    R = MLA_ROPE // 2
    pa = _mm(h, Wa, "nn", tag + "_proj")
    cqn, ckvn = _rms_fwd(pa, g_q, g_kv, tag + "_rms")
    q0 = _mm(cqn, Wuq, "nn", tag + "_uq")
    qr = _rope(q0, HEADS, tabs, R, tag + "_rope_q")
    kv0 = _mm(ckvn, Wukv, "nn", tag + "_ukv")
    kk = _rope(kv0, HEADS, tabs, R, tag + "_rope_k", shared=pa, shared_tile=(Q + KV) // LANES)
    cfg = _Dense(L, False)
    scale = dqk ** -0.5
    o, lse2 = _flash_fwd(cfg, qr, 0, kk, 0, kv0, HEADS, HEADS, DENSE_HS, scale, tag + "_attn")
    mix = _mm(o, Wo, "nn", tag + "_out")
    return mix, (h, Wa, Wuq, Wukv, Wo, g_q, g_kv, tabs, pa, cqn, ckvn, qr, kk, kv0, o, lse2)


def _mla_bwd(dmix, res, tag):
    h, Wa, Wuq, Wukv, Wo, g_q, g_kv, tabs, pa, cqn, ckvn, qr, kk, kv0, o, lse2 = res
    L = h.shape[0]
    Q, KV = MLA_Q_LORA, MLA_KV_LORA
    dqk = MLA_NOPE + MLA_ROPE
    R = MLA_ROPE // 2
    cfg = _Dense(L, False)
    scale = dqk ** -0.5
    dWo = _mm(o, dmix, "tn", tag + "_dwo")
    do = _mm(dmix, Wo, "nt", tag + "_do", out_dtype=BF16)
    delta = _delta(do, o, tag + "_delta")
    dk, dv, dqr = _flash_bwd(cfg, qr, 0, kk, 0, kv0, HEADS, do, lse2, delta, HEADS, DENSE_HS, scale, tag + "_bwd")
    dq0 = _rope(dqr, HEADS, tabs, R, tag + "_rope_q_bwd", transpose=True, out_dtype=BF16)
    dWuq = _mm(cqn, dq0, "tn", tag + "_dwuq")
    dcqn = _mm(dq0, Wuq, "nt", tag + "_dcq")
    dkv = jnp.concatenate([dk, dv], axis=1).astype(BF16)
    dWukv = _mm(ckvn, dkv, "tn", tag + "_dwukv")
    dckvn = _mm(dkv, Wukv, "nt", tag + "_dckv")
    dkr = _rope_shared_bwd(dk, HEADS, tabs, R, tag + "_rope_k_bwd")
    dpa, dgq, dgkv = _rms_bwd(pa, dcqn, dckvn, dkr, g_q, g_kv, tag + "_rms_bwd")
    dWa = _mm(h, dpa, "tn", tag + "_dw")
    dh = _mm(dpa, Wa, "nt", tag + "_dh")
    hp = HEADS * LANES
    dw_a = jnp.concatenate([dWa[:, :Q + KV], dWa[:, Q + KV + MLA_NOPE:Q + KV + dqk]], axis=1)
    dk_n = dWukv[:, :hp].reshape(KV, HEADS, LANES)[:, :, :MLA_NOPE]
    dv_n = dWukv[:, hp:].reshape(KV, HEADS, LANES)[:, :, :HEAD_DIM]
    dw_ukv = jnp.concatenate([dk_n, dv_n], axis=2).reshape(KV, HEADS * (MLA_NOPE + HEAD_DIM))
    return dh, dict(w_a=dw_a, g_q=dgq, g_kv=dgkv, w_uq=_unpad_heads_cols(dWuq, HEADS, dqk), w_ukv=dw_ukv,
                    w_o=_unpad_heads_rows(dWo, HEADS, HEAD_DIM))


MATMUL_WEIGHTS = ("fox_w_in", "fox_w_o", "swa_w_in", "swa_w_o", "mla_w_a", "mla_w_uq", "mla_w_ukv", "mla_w_o",
                  "ffn_w_in", "ffn_w_out")


def _local_step(x, target, w):
    w = {k: (_bf(v) if k in MATMUL_WEIGHTS else v) for k, v in w.items()}
    h = jnp.concatenate([jnp.zeros((PAD, D_MODEL), F32), w["meta_tokens"], x], axis=0)
    hb = h.astype(BF16)
    saved = []
    for i in range(DEPTH):
        kind, j = i % 3, i // 3
        tag = "l%d" % i
        if kind == 0:
            mix, mres = _fox_fwd(hb, w["fox_w_in"][j], w["fox_b_f"][j], w["fox_w_o"][j], tag + "_fox")
        elif kind == 1:
            mix, mres = _swa_fwd(hb, w["swa_w_in"][j], w["swa_sinks"][j], w["swa_w_o"][j], tag + "_swa")
        else:
            mix, mres = _mla_fwd(hb, w["mla_w_a"][j], w["mla_g_q"][j], w["mla_g_kv"][j], w["mla_w_uq"][j],
                                 w["mla_w_ukv"][j], w["mla_w_o"][j], tag + "_mla")
        h1, h1b, xh1, rs1 = _ln_fwd(h, mix, w["ln1_g"][i], w["ln1_b"][i], tag + "_ln1")
        u = _mm(h1b, w["ffn_w_in"][i], "nn", tag + "_ffn_in")
        y, a = _conv_glu_fwd(u, w["ffn_conv_w"][i], w["ffn_conv_b"][i], tag + "_conv")
        ffn = _mm(a, w["ffn_w_out"][i], "nn", tag + "_ffn_out")
        h2, h2b, xh2, rs2 = _ln_fwd(h1, ffn, w["ln2_g"][i], w["ln2_b"][i], tag + "_ln2")
        saved.append((mres, xh1, rs1, h1b, u, y, a, xh2, rs2))
        h, hb = h2, h2b
    loss, dh = _loss_head(h, target, "loss_head")

    g = {k: [None] * v.shape[0] for k, v in w.items() if k != "meta_tokens"}
    ga = None
    for i in reversed(range(DEPTH)):
        kind, j = i % 3, i // 3
        tag = "l%d" % i
        mres, xh1, rs1, h1b, u, y, a, xh2, rs2 = saved[i]
        dz2, dz2b, g["ln2_g"][i], g["ln2_b"][i] = _ln_bwd(ga, dh, xh2, rs2, w["ln2_g"][i], tag + "_ln2_bwd")
        g["ffn_w_out"][i] = _mm(a, dz2b, "tn", tag + "_dw_out")
        da = _mm(dz2b, w["ffn_w_out"][i], "nt", tag + "_da")
        du, g["ffn_conv_w"][i], g["ffn_conv_b"][i] = _conv_glu_bwd(y, da, u, w["ffn_conv_w"][i], tag + "_conv_bwd")
        g["ffn_w_in"][i] = _mm(h1b, du, "tn", tag + "_dw_in")
        dh1 = _mm(du, w["ffn_w_in"][i], "nt", tag + "_dh1")
        dz1, dz1b, g["ln1_g"][i], g["ln1_b"][i] = _ln_bwd(dz2, dh1, xh1, rs1, w["ln1_g"][i], tag + "_ln1_bwd")
        if kind == 0:
            dh, mg = _fox_bwd(dz1b, mres, tag + "_fox")
            pre = "fox_"
        elif kind == 1:
            dh, mg = _swa_bwd(dz1b, mres, tag + "_swa")
            pre = "swa_"
        else:
            dh, mg = _mla_bwd(dz1b, mres, tag + "_mla")
            pre = "mla_"
        for k, v in mg.items():
            g[pre + k][j] = v
        ga = dz1
    dh0 = _axpy(ga, dh, "dh0")
    grads = {k: jnp.stack(v) for k, v in g.items()}
    grads["meta_tokens"] = dh0[PAD:BLOCK]
    return loss, dh0, grads


SHARDED = (("meta_tokens", 1), ("fox_w_in", 2), ("fox_w_o", 1), ("swa_w_in", 2), ("swa_w_o", 1), ("mla_w_a", 1),
           ("mla_g_q", 1), ("mla_g_kv", 1), ("mla_w_uq", 2), ("mla_w_ukv", 2), ("mla_w_o", 1), ("ffn_w_in", 2),
           ("ffn_conv_w", 2), ("ffn_w_out", 1))
REPLICATED = ("ln1_g", "ln1_b", "ln2_g", "ln2_b", "fox_b_f", "swa_sinks", "ffn_conv_b")
WEIGHTS = ("meta_tokens", "ln1_g", "ln1_b", "ln2_g", "ln2_b", "fox_w_in", "fox_b_f", "fox_w_o", "swa_w_in",
           "swa_sinks", "swa_w_o", "mla_w_a", "mla_g_q", "mla_g_kv", "mla_w_uq", "mla_w_ukv", "mla_w_o", "ffn_w_in",
           "ffn_conv_w", "ffn_conv_b", "ffn_w_out")


def _rows(n):
    return -(-n // ROW)


def _pack(arrs, multiple):
    parts = []
    for a in arrs:
        n = math.prod(a.shape)
        parts.append(jnp.pad(a.reshape(-1), (0, _rows(n) * ROW - n)).reshape(-1, ROW))
    total = sum(p.shape[0] for p in parts)
    pad = -total % multiple
    if pad:
        parts.append(jnp.zeros((pad, ROW), parts[0].dtype))
    return jnp.concatenate(parts, axis=0)


def _unpack(flat, shapes):
    out, r = [], 0
    for s in shapes:
        n = math.prod(s)
        out.append(flat[r:r + _rows(n)].reshape(-1)[:n].reshape(s))
        r += _rows(n)
    return out


def _pack_bf16(w, names):
    return _pack([_bf(w[n]) if n in MATMUL_WEIGHTS else lax.bitcast_convert_type(w[n], BF16) for n in names], 2 * ROW)


def _unpack_bf16(flat, names, shapes):
    sh = [s if n in MATMUL_WEIGHTS else s + (2,) for n, s in zip(names, shapes)]
    parts = _unpack(flat, sh)
    return [p if n in MATMUL_WEIGHTS else lax.bitcast_convert_type(p, F32) for n, p in zip(names, parts)]


HBM_SPEC = pl.BlockSpec(memory_space=pltpu.HBM)


def _place():
    x, y, c = lax.axis_index("x"), lax.axis_index("y"), lax.axis_index("c")
    chips = [(1 - x, y), (x, 1 - y), (1 - x, 1 - y)]
    return x, y, c, chips


def _gather_weights(shard):
    R = shard.shape[0]
    Rh = R // 2

    def body(s_ref, o_ref, send_sems, recv_sems, local_sem):
        x, y, c, chips = _place()
        sib = (x, y, 1 - c)

        def half(k, hc):
            return o_ref.at[k, pl.ds(hc * Rh, Rh), :]

        def copy(j, src, dst, to):
            return pltpu.make_async_remote_copy(src_ref=src, dst_ref=dst, send_sem=send_sems.at[j],
                                                recv_sem=recv_sems.at[j], device_id=to, device_id_type=MESH)

        me = 2 * x + y
        mine = pltpu.make_async_copy(s_ref, o_ref.at[me], local_sem)
        mine.start()
        first = [copy(j, s_ref.at[pl.ds(c * Rh, Rh), :], half(me, c), (tx, ty, c)) for j, (tx, ty) in enumerate(chips)]
        for cp in first:
            cp.start()
        passed = []
        for j, (tx, ty) in enumerate(chips):
            k = 2 * tx + ty
            copy(j, half(k, c), half(k, c), (tx, ty, c)).wait_recv()
            fw = copy(3 + j, half(k, c), half(k, c), sib)
            fw.start()
            passed.append(fw)
        for j, (tx, ty) in enumerate(chips):
            k = 2 * tx + ty
            copy(3 + j, half(k, 1 - c), half(k, 1 - c), sib).wait_recv()
        for cp in first + passed:
            cp.wait_send()
        mine.wait()

    return pl.pallas_call(
        body, name="gather_weights", out_shape=jax.ShapeDtypeStruct((N_CHIPS, R, ROW), shard.dtype),
        in_specs=[HBM_SPEC], out_specs=HBM_SPEC,
        scratch_shapes=[pltpu.SemaphoreType.DMA((6,)), pltpu.SemaphoreType.DMA((6,)), pltpu.SemaphoreType.DMA],
    )(shard)


def _swap_halves(G):
    R = G.shape[1]
    Rh = R // 2

    def body(g_ref, a_ref, send_sem, recv_sem):
        x, y, c, _ = _place()
        cp = pltpu.make_async_remote_copy(src_ref=g_ref.at[:, pl.ds((1 - c) * Rh, Rh), :], dst_ref=a_ref,
                                          send_sem=send_sem, recv_sem=recv_sem, device_id=(x, y, 1 - c),
                                          device_id_type=MESH)
        cp.start()
        cp.wait()

    return pl.pallas_call(
        body, name="reduce_swap_halves", out_shape=jax.ShapeDtypeStruct((N_CHIPS, Rh, ROW), G.dtype),
        in_specs=[HBM_SPEC], out_specs=HBM_SPEC,
        scratch_shapes=[pltpu.SemaphoreType.DMA, pltpu.SemaphoreType.DMA],
    )(G)


def _exchange_chips(P):
    def body(p_ref, b_ref, send_sems, recv_sems, local_sem):
        x, y, c, chips = _place()
        me = 2 * x + y

        def copy(j, src, dst, to):
            return pltpu.make_async_remote_copy(src_ref=src, dst_ref=dst, send_sem=send_sems.at[j],
                                                recv_sem=recv_sems.at[j], device_id=to, device_id_type=MESH)

        mine = pltpu.make_async_copy(p_ref.at[me], b_ref.at[me], local_sem)
        mine.start()
        sends = [copy(j, p_ref.at[2 * tx + ty], b_ref.at[me], (tx, ty, c)) for j, (tx, ty) in enumerate(chips)]
        for cp in sends:
            cp.start()
        for j, (tx, ty) in enumerate(chips):
            k = 2 * tx + ty
            copy(j, p_ref.at[k], b_ref.at[k], (tx, ty, c)).wait_recv()
        for cp in sends:
            cp.wait_send()
        mine.wait()

    return pl.pallas_call(
        body, name="reduce_exchange_chips", out_shape=jax.ShapeDtypeStruct(P.shape, P.dtype),
        in_specs=[HBM_SPEC], out_specs=HBM_SPEC,
        scratch_shapes=[pltpu.SemaphoreType.DMA((3,)), pltpu.SemaphoreType.DMA((3,)), pltpu.SemaphoreType.DMA],
    )(P)


def _join_halves(Fh):
    Rh = Fh.shape[0]

    def body(f_ref, o_ref, send_sem, recv_sem, local_sem):
        x, y, c, _ = _place()
        mine = pltpu.make_async_copy(f_ref, o_ref.at[pl.ds(c * Rh, Rh), :], local_sem)
        mine.start()
        cp = pltpu.make_async_remote_copy(src_ref=f_ref, dst_ref=o_ref.at[pl.ds(c * Rh, Rh), :], send_sem=send_sem,
                                          recv_sem=recv_sem, device_id=(x, y, 1 - c), device_id_type=MESH)
        cp.start()
        pltpu.make_async_remote_copy(src_ref=f_ref, dst_ref=o_ref.at[pl.ds((1 - c) * Rh, Rh), :], send_sem=send_sem,
                                     recv_sem=recv_sem, device_id=(x, y, 1 - c), device_id_type=MESH).wait_recv()
        cp.wait_send()
        mine.wait()

    return pl.pallas_call(
        body, name="reduce_join_halves", out_shape=jax.ShapeDtypeStruct((2 * Rh, ROW), Fh.dtype),
        in_specs=[HBM_SPEC], out_specs=HBM_SPEC,
        scratch_shapes=[pltpu.SemaphoreType.DMA, pltpu.SemaphoreType.DMA, pltpu.SemaphoreType.DMA],
    )(Fh)


def _gather_small(v):
    m_per = v.shape[0]

    def body(x_ref, out_ref, send_sems, recv_sems, local_sem):
        x, y, c, chips = _place()
        me, sibling = (x, y, c), (x, y, 1 - c)

        def rows(px, py, pc):
            return out_ref.at[pl.ds((4 * px + 2 * py + pc) * m_per, m_per), :]

        def copy(k, block, to, src=None):
            return pltpu.make_async_remote_copy(src_ref=rows(*block) if src is None else src, dst_ref=rows(*block),
                                                send_sem=send_sems.at[k], recv_sem=recv_sems.at[k], device_id=to,
                                                device_id_type=MESH)

        mine = pltpu.make_async_copy(x_ref, rows(*me), local_sem)
        mine.start()
        first = [copy(0, me, sibling, src=x_ref)]
        first += [copy(1 + j, me, (*chip, c), src=x_ref) for j, chip in enumerate(chips)]
        for cp in first:
            cp.start()
        passed = [copy(4 + j, (*chip, c), sibling) for j, chip in enumerate(chips)]
        for j, chip in enumerate(chips):
            copy(1 + j, (*chip, c), me).wait_recv()
            passed[j].start()
        copy(0, sibling, me).wait_recv()
        for j, chip in enumerate(chips):
            copy(4 + j, (*chip, 1 - c), me).wait_recv()
        for cp in first + passed:
            cp.wait_send()
        mine.wait()

    return pl.pallas_call(
        body, name="gather_small", out_shape=jax.ShapeDtypeStruct((N_DEV * m_per, ROW), v.dtype),
        in_specs=[pl.BlockSpec(memory_space=pltpu.VMEM)], out_specs=pl.BlockSpec(memory_space=pltpu.VMEM),
        scratch_shapes=[pltpu.SemaphoreType.DMA((7,)), pltpu.SemaphoreType.DMA((7,)), pltpu.SemaphoreType.DMA],
    )(v)


def _sum_slots(a, n, name):
    M = a.shape[0] // n
    tr = _pick(M, (512, 256, 128, 64, 40, 8))
    nb = M // tr

    def body(*refs):
        acc = refs[0][...]
        for r in refs[1:-1]:
            acc = acc + r[...]
        refs[-1][...] = acc

    specs = [pl.BlockSpec((tr, ROW), functools.partial(lambda i, k: (k * nb + i, 0), k=k)) for k in range(n)]
    return pl.pallas_call(body, name=name, grid=(nb,), in_specs=specs,
                          out_specs=pl.BlockSpec((tr, ROW), lambda i: (i, 0)),
                          out_shape=jax.ShapeDtypeStruct((M, ROW), F32), compiler_params=_params("parallel"))(*([a] * n))


def _add(a, b, name):
    M = a.shape[0]
    tr = _pick(M, (512, 256, 128, 64, 40, 8))

    def body(a_ref, b_ref, o_ref):
        o_ref[...] = a_ref[...] + b_ref[...]

    row = pl.BlockSpec((tr, ROW), lambda i: (i, 0))
    return pl.pallas_call(body, name=name, grid=(M // tr,), in_specs=[row, row], out_specs=row,
                          out_shape=jax.ShapeDtypeStruct((M, ROW), F32), compiler_params=_params("parallel"))(a, b)


def _adamw(g, w, m, v, name):
    shp = w.shape
    N = shp[-1]
    M = math.prod(shp[:-1])
    g, w, m, v = (a.reshape(M, N) for a in (g, w, m, v))
    tr = _pick(M, tuple(t for t in (512, 256, 128, 64, 40, 32, 16, 8) if t * N <= 256 * 1024))
    c1 = 1.0 - ADAM_B1 ** ADAM_STEP
    c2 = 1.0 - ADAM_B2 ** ADAM_STEP

    def body(g_ref, w_ref, m_ref, v_ref, d_ref, nm_ref, nv_ref):
        gg = g_ref[...]
        nm = ADAM_B1 * m_ref[...] + (1.0 - ADAM_B1) * gg
        nv = ADAM_B2 * v_ref[...] + (1.0 - ADAM_B2) * (gg * gg)
        nm_ref[...] = nm
        nv_ref[...] = nv
        d_ref[...] = -ADAM_LR * ((nm / c1) / (jnp.sqrt(nv / c2) + ADAM_EPS) + ADAM_WD * w_ref[...])

    row = pl.BlockSpec((tr, N), lambda i: (i, 0))
    shape = jax.ShapeDtypeStruct((M, N), F32)
    outs = pl.pallas_call(body, name=name, grid=(M // tr,), in_specs=[row] * 4, out_specs=[row] * 3,
                          out_shape=[shape] * 3, compiler_params=_params("parallel"))(g, w, m, v)
    return [o.reshape(shp) for o in outs]


def kernel(x, meta_tokens, ln1_g, ln1_b, ln2_g, ln2_b, fox_w_in, fox_b_f, fox_w_o, swa_w_in, swa_sinks, swa_w_o, mla_w_a, mla_g_q, mla_g_kv, mla_w_uq, mla_w_ukv, mla_w_o, ffn_w_in, ffn_conv_w, ffn_conv_b, ffn_w_out, loss_target, m_meta_tokens, m_ln1_g, m_ln1_b, m_ln2_g, m_ln2_b, m_fox_w_in, m_fox_b_f, m_fox_w_o, m_swa_w_in, m_swa_sinks, m_swa_w_o, m_mla_w_a, m_mla_g_q, m_mla_g_kv, m_mla_w_uq, m_mla_w_ukv, m_mla_w_o, m_ffn_w_in, m_ffn_conv_w, m_ffn_conv_b, m_ffn_w_out, v_meta_tokens, v_ln1_g, v_ln1_b, v_ln2_g, v_ln2_b, v_fox_w_in, v_fox_b_f, v_fox_w_o, v_swa_w_in, v_swa_sinks, v_swa_w_o, v_mla_w_a, v_mla_g_q, v_mla_g_kv, v_mla_w_uq, v_mla_w_ukv, v_mla_w_o, v_ffn_w_in, v_ffn_conv_w, v_ffn_conv_b, v_ffn_w_out):
    given = dict(locals())
    w = {n: given[n] for n in WEIGHTS}
    m = {n: given["m_" + n] for n in WEIGHTS}
    v = {n: given["v_" + n] for n in WEIGHTS}
    sh_names = [n for n, _ in SHARDED]
    sh_shapes = [w[n].shape for n in sh_names]

    gathered = _gather_weights(_pack_bf16(w, sh_names))
    full = dict(w)
    per_chip = [_unpack_bf16(gathered[k], sh_names, sh_shapes) for k in range(N_CHIPS)]
    for t, (n, ax) in enumerate(SHARDED):
        full[n] = jnp.concatenate([per_chip[k][t] for k in range(N_CHIPS)], axis=ax)

    loss_part, dh0, grads = _local_step(x[0], loss_target[0], full)
    loss = lax.psum(jnp.sum(loss_part), ("x", "y", "c"))
    grad_x = dh0[BLOCK:][None]

    split = {n: jnp.split(grads[n], N_CHIPS, axis=ax) for n, ax in SHARDED}
    G = jnp.stack([_pack([split[n][k] for n in sh_names], 2 * ROW) for k in range(N_CHIPS)])
    Rh = G.shape[1] // 2
    c = lax.axis_index("c")
    mine = lax.dynamic_slice_in_dim(G, c * Rh, Rh, axis=1)
    P = _add(mine.reshape(N_CHIPS * Rh, ROW), _swap_halves(G).reshape(N_CHIPS * Rh, ROW), "reduce_pair_sum")
    B = _exchange_chips(P.reshape(N_CHIPS, Rh, ROW))
    Fh = _sum_slots(B.reshape(N_CHIPS * Rh, ROW), N_CHIPS, "reduce_chip_sum")
    Fg = _join_halves(Fh)
    out = {}
    for n, g_n in zip(sh_names, _unpack(Fg, sh_shapes)):
        out["grad", n] = g_n
        out["delta", n], out["new_m", n], out["new_v", n] = _adamw(g_n, w[n], m[n], v[n], "adamw_" + n)

    rp_shapes = [w[n].shape for n in REPLICATED]
    small = _gather_small(_pack([grads[n] for n in REPLICATED], 8))
    g_r = _sum_slots(small, N_DEV, "reduce_small_sum")
    d_r, m_r, v_r = _adamw(g_r, _pack([w[n] for n in REPLICATED], 8), _pack([m[n] for n in REPLICATED], 8),
                           _pack([v[n] for n in REPLICATED], 8), "adamw_replicated")

    for kind, fr in (("grad", g_r), ("delta", d_r), ("new_m", m_r), ("new_v", v_r)):
        for n, a in zip(REPLICATED, _unpack(fr, rp_shapes)):
            out[kind, n] = a
    return (loss, grad_x, *[out[k, n] for k in ("grad", "delta", "new_m", "new_v") for n in WEIGHTS])
```

```python
import functools
import math

import numpy as np
import jax
import jax.numpy as jnp
from jax import lax
from jax.experimental import pallas as pl
from jax.experimental.pallas import tpu as pltpu

F32 = jnp.float32
BF16 = jnp.bfloat16

D_MODEL = 1024
DEPTH = 4
BLOCK = 128
N_META = 16
PAD = BLOCK - N_META
NEG = -1e30
ALPHA = (2.0 * DEPTH) ** 0.25
LN_EPS = 1e-5
RMS_EPS = 1e-6
HEADS = 16
HEAD_DIM = 64
LANES = 128
SWA_KV = 2
SWA_G = HEADS // SWA_KV
WINDOW = 128
ROPE_THETA = 500000.0
ROPE_DIM = 16
MLA_Q_LORA = 384
MLA_KV_LORA = 256
MLA_NOPE = 64
MLA_ROPE = 32
MLA_ROPE_THETA = 10000.0
D_FF = 2816
ADAM_LR = 0.001
ADAM_B1 = 0.9
ADAM_B2 = 0.999
ADAM_EPS = 1e-08
ADAM_WD = 0.01
ADAM_STEP = 10
N_CHIPS = 4
N_DEV = 8
ROW = 1024
VMEM_LIMIT = 48 * 1024 * 1024
MESH = pl.DeviceIdType.MESH
LOG2E = 1.4426950408889634
DENSE_HS = 2
DENSE_HS_FWD = 4

NN = (((1,), (0,)), ((), ()))
NT = (((1,), (1,)), ((), ()))
TN = (((0,), (0,)), ((), ()))


def _pick(n, cands):
    for c in cands:
        if n % c == 0:
            return c
    return n


def _params(*sem):
    return pltpu.CompilerParams(dimension_semantics=sem, vmem_limit_bytes=VMEM_LIMIT)


def _bf(x):
    return x if x.dtype == BF16 else x.astype(BF16)


def _mm(a, b, mode, name, out_dtype=F32):
    if mode == "nn":
        (M, K), (_, N) = a.shape, b.shape
    elif mode == "nt":
        (M, K), (N, _) = a.shape, b.shape
    else:
        (K, M), (_, N) = a.shape, b.shape
    tm = _pick(M, (1664, 1408, 1024, 640, 512, 384, 256, 128))
    tn = _pick(N, (640, 512, 384, 1408, 256, 128))
    tk = K if (K <= 1024 and mode != "tn") else _pick(K, (640, 512, 384, 1408, 256, 128))
    nk = K // tk
    dn = {"nn": NN, "nt": NT, "tn": TN}[mode]

    def body(a_ref, b_ref, o_ref, *acc):
        part = lax.dot_general(_bf(a_ref[...]), _bf(b_ref[...]), dn, preferred_element_type=F32)
        if nk == 1:
            o_ref[...] = part.astype(out_dtype)
            return
        acc_ref, = acc
        k = pl.program_id(2)

        @pl.when(k == 0)
        def _():
            acc_ref[...] = part

        @pl.when(k > 0)
        def _():
            acc_ref[...] += part

        @pl.when(k == nk - 1)
        def _():
            o_ref[...] = acc_ref[...].astype(out_dtype)

    if mode == "tn":
        a_spec = pl.BlockSpec((tk, tm), lambda i, j, k: (k, i))
    else:
        a_spec = pl.BlockSpec((tm, tk), lambda i, j, k: (i, k))
    if mode == "nt":
        b_spec = pl.BlockSpec((tn, tk), lambda i, j, k: (j, k))
    else:
        b_spec = pl.BlockSpec((tk, tn), lambda i, j, k: (k, j))
    return pl.pallas_call(
        body, name=name, grid=(M // tm, N // tn, nk),
        in_specs=[a_spec, b_spec],
        out_specs=pl.BlockSpec((tm, tn), lambda i, j, k: (i, j)),
        out_shape=jax.ShapeDtypeStruct((M, N), out_dtype),
        scratch_shapes=[pltpu.VMEM((tm, tn), F32)] if nk > 1 else [],
        compiler_params=_params("parallel", "parallel", "arbitrary"),
    )(a, b)


def _ln_fwd(h, mix, g, b, name):
    L = h.shape[0]
    tr = 128

    def body(h_ref, m_ref, g_ref, b_ref, o_ref, ob_ref, xh_ref, rs_ref):
        z = ALPHA * h_ref[...] + m_ref[...]
        mu = jnp.mean(z, axis=1, keepdims=True)
        zc = z - mu
        var = jnp.mean(zc * zc, axis=1, keepdims=True)
        rstd = lax.rsqrt(var + LN_EPS)
        xh = zc * rstd
        xh_ref[...] = xh
        rs_ref[...] = rstd
        out = xh * g_ref[...] + b_ref[...]
        o_ref[...] = out
        ob_ref[...] = out.astype(BF16)

    row = pl.BlockSpec((tr, D_MODEL), lambda i: (i, 0))
    vec = pl.BlockSpec((1, D_MODEL), lambda i: (0, 0))
    return pl.pallas_call(
        body, name=name, grid=(L // tr,),
        in_specs=[row, row, vec, vec],
        out_specs=[row, row, row, pl.BlockSpec((tr, 1), lambda i: (i, 0))],
        out_shape=[jax.ShapeDtypeStruct((L, D_MODEL), F32), jax.ShapeDtypeStruct((L, D_MODEL), BF16),
                   jax.ShapeDtypeStruct((L, D_MODEL), F32), jax.ShapeDtypeStruct((L, 1), F32)],
        compiler_params=_params("parallel"),
    )(h, mix, g.reshape(1, D_MODEL), b.reshape(1, D_MODEL))


def _ln_bwd(ga, gb, xhat, rstd, g, name):
    L = xhat.shape[0]
    tr = 128
    two = ga is not None

    def body(*refs):
        if two:
            ga_ref, gb_ref, xh_ref, rs_ref, g_ref, dz_ref, dzb_ref, dg_ref, db_ref = refs
            dy = ALPHA * ga_ref[...] + gb_ref[...]
        else:
            gb_ref, xh_ref, rs_ref, g_ref, dz_ref, dzb_ref, dg_ref, db_ref = refs
            dy = gb_ref[...]
        xh = xh_ref[...]
        dxh = dy * g_ref[...]
        c1 = jnp.mean(dxh, axis=1, keepdims=True)
        c2 = jnp.mean(dxh * xh, axis=1, keepdims=True)
        dz = rs_ref[...] * (dxh - c1 - xh * c2)
        dz_ref[...] = dz
        dzb_ref[...] = dz.astype(BF16)

        @pl.when(pl.program_id(0) == 0)
        def _():
            dg_ref[...] = jnp.zeros_like(dg_ref)
            db_ref[...] = jnp.zeros_like(db_ref)

        dg_ref[...] += jnp.sum(dy * xh, axis=0, keepdims=True)
        db_ref[...] += jnp.sum(dy, axis=0, keepdims=True)

    row = pl.BlockSpec((tr, D_MODEL), lambda i: (i, 0))
    vec = pl.BlockSpec((1, D_MODEL), lambda i: (0, 0))
    ins = ([ga] if two else []) + [gb, xhat, rstd, g.reshape(1, D_MODEL)]
    specs = ([row] if two else []) + [row, row, pl.BlockSpec((tr, 1), lambda i: (i, 0)), vec]
    dz, dzb, dg, db = pl.pallas_call(
        body, name=name, grid=(L // tr,),
        in_specs=specs, out_specs=[row, row, vec, vec],
        out_shape=[jax.ShapeDtypeStruct((L, D_MODEL), F32), jax.ShapeDtypeStruct((L, D_MODEL), BF16),
                   jax.ShapeDtypeStruct((1, D_MODEL), F32), jax.ShapeDtypeStruct((1, D_MODEL), F32)],
        compiler_params=_params("arbitrary"),
    )(*ins)
    return dz, dzb, dg[0], db[0]


def _axpy(a, b, name):
    L, N = a.shape
    tr = 128

    def body(a_ref, b_ref, o_ref):
        o_ref[...] = ALPHA * a_ref[...] + b_ref[...]

    row = pl.BlockSpec((tr, N), lambda i: (i, 0))
    return pl.pallas_call(body, name=name, grid=(L // tr,), in_specs=[row, row], out_specs=row,
                          out_shape=jax.ShapeDtypeStruct((L, N), F32), compiler_params=_params("parallel"))(a, b)


def _shift_down(cur, prev8, n):
    rows = lax.broadcasted_iota(jnp.int32, cur.shape, 0)
    out = pltpu.roll(cur, n, 0)
    for r in range(n):
        out = jnp.where(rows == r, prev8[8 - n + r:8 - n + r + 1, :], out)
    return out


def _shift_up(cur, next8, n):
    tr = cur.shape[0]
    rows = lax.broadcasted_iota(jnp.int32, cur.shape, 0)
    out = pltpu.roll(cur, tr - n, 0)
    for r in range(n):
        out = jnp.where(rows == tr - n + r, next8[r:r + 1, :], out)
    return out


def _silu(x):
    return x / (1.0 + jnp.exp(-x))


def _conv(cur, prev8, cw_ref, cb_ref):
    y = cb_ref[...] + _shift_down(cur, prev8, 2) * cw_ref[0:1, :]
    y = y + _shift_down(cur, prev8, 1) * cw_ref[1:2, :]
    return y + cur * cw_ref[2:3, :]


def _valid_rows(i, tr, u_ref, up_ref):
    rows = i * tr + lax.broadcasted_iota(jnp.int32, u_ref.shape, 0)
    prow = i * tr - 8 + lax.broadcasted_iota(jnp.int32, up_ref.shape, 0)
    return jnp.where(rows >= PAD, u_ref[...], 0.0), jnp.where(prow >= PAD, up_ref[...], 0.0), rows


def _conv_glu_fwd(u, cw, cb, name):
    L, F2 = u.shape
    F = F2 // 2
    tr = 128

    def body(u_ref, up_ref, cw_ref, cb_ref, a_ref):
        cur, prev, _ = _valid_rows(pl.program_id(0), tr, u_ref, up_ref)
        y = _conv(cur, prev, cw_ref, cb_ref)
        a_ref[...] = (_silu(y[:, :F]) * y[:, F:]).astype(BF16)

    return pl.pallas_call(
        body, name=name, grid=(L // tr,),
        in_specs=[pl.BlockSpec((tr, F2), lambda i: (i, 0)),
                  pl.BlockSpec((8, F2), lambda i: (jnp.maximum(i * (tr // 8) - 1, 0), 0)),
                  pl.BlockSpec((3, F2), lambda i: (0, 0)),
                  pl.BlockSpec((1, F2), lambda i: (0, 0))],
        out_specs=pl.BlockSpec((tr, F), lambda i: (i, 0)),
        out_shape=jax.ShapeDtypeStruct((L, F), BF16),
        compiler_params=_params("parallel"),
    )(u, u, cw, cb.reshape(1, F2))


def _conv_glu_bwd(da, u, cw, cb, name):
    L, F2 = u.shape
    F = F2 // 2
    tr = 128
    nb = L // tr

    def dy_of(yv, dav):
        g, val = yv[:, :F], yv[:, F:]
        sg = 1.0 / (1.0 + jnp.exp(-g))
        dg = dav * val * (sg * (1.0 + g * (1.0 - sg)))
        dv = dav * (g * sg)
        return jnp.concatenate([dg, dv], axis=1)

    def body(da_ref, dan_ref, u_ref, up_ref, un_ref, cw_ref, cb_ref, du_ref, dcw_ref, dcb_ref):
        i = pl.program_id(0)
        cur, prev, rows = _valid_rows(i, tr, u_ref, up_ref)
        dy = dy_of(_conv(cur, prev, cw_ref, cb_ref), da_ref[...])
        yn = _conv(un_ref[...], cur[tr - 8:tr, :], cw_ref, cb_ref)
        dyn = jnp.where(i < nb - 1, dy_of(yn, dan_ref[...]), 0.0)
        du = dy * cw_ref[2:3, :] + _shift_up(dy, dyn, 1) * cw_ref[1:2, :] + _shift_up(dy, dyn, 2) * cw_ref[0:1, :]
        du_ref[...] = jnp.where(rows >= PAD, du, 0.0).astype(BF16)

        @pl.when(i == 0)
        def _():
            dcw_ref[...] = jnp.zeros_like(dcw_ref)
            dcb_ref[...] = jnp.zeros_like(dcb_ref)

        dcw_ref[0:1, :] += jnp.sum(dy * _shift_down(cur, prev, 2), axis=0, keepdims=True)
        dcw_ref[1:2, :] += jnp.sum(dy * _shift_down(cur, prev, 1), axis=0, keepdims=True)
        dcw_ref[2:3, :] += jnp.sum(dy * cur, axis=0, keepdims=True)
        dcb_ref[...] += jnp.sum(dy, axis=0, keepdims=True)

    nxt = lambda i: (jnp.minimum((i + 1) * (tr // 8), L // 8 - 1), 0)
    prv = lambda i: (jnp.maximum(i * (tr // 8) - 1, 0), 0)
    du, dcw, dcb = pl.pallas_call(
        body, name=name, grid=(nb,),
        in_specs=[pl.BlockSpec((tr, F), lambda i: (i, 0)), pl.BlockSpec((8, F), nxt),
                  pl.BlockSpec((tr, F2), lambda i: (i, 0)), pl.BlockSpec((8, F2), prv), pl.BlockSpec((8, F2), nxt),
                  pl.BlockSpec((3, F2), lambda i: (0, 0)), pl.BlockSpec((1, F2), lambda i: (0, 0))],
        out_specs=[pl.BlockSpec((tr, F2), lambda i: (i, 0)), pl.BlockSpec((3, F2), lambda i: (0, 0)),
                   pl.BlockSpec((1, F2), lambda i: (0, 0))],
        out_shape=[jax.ShapeDtypeStruct((L, F2), BF16), jax.ShapeDtypeStruct((3, F2), F32),
                   jax.ShapeDtypeStruct((1, F2), F32)],
        compiler_params=_params("arbitrary"),
    )(da, da, u, u, u, cw, cb.reshape(1, F2))
    return du, dcw, dcb[0]


def _dense_mask(qpos, kpos):
    return (kpos <= qpos) & (kpos >= PAD)


def _tables(pairs):
    qt, kt, ft = [], [], []
    for grp in pairs:
        for n, (qb, kb, msk) in enumerate(grp):
            qt.append(qb)
            kt.append(kb)
            ft.append((1 if n == 0 else 0) | (2 if n == len(grp) - 1 else 0) | (4 if msk else 0))
    return tuple(jnp.asarray(np.asarray(t, np.int32)) for t in (qt, kt, ft))


class _Dense:
    mask = staticmethod(_dense_mask)

    def __init__(self, L, pad_in_cr2):
        self.T = T = 640 if L % 640 == 0 else 128
        nb = L // T
        m = lambda qb, kb: kb == qb or (kb == 0 and not pad_in_cr2) or (qb * T < PAD)
        self.q_major = _tables([[(qb, kb, m(qb, kb)) for kb in range(qb + 1)] for qb in range(nb)])
        self.k_major = _tables([[(qb, kb, m(qb, kb)) for qb in range(kb, nb)] for kb in range(nb)])


def _positions(T, qb, kb):
    qpos = qb * T + lax.broadcasted_iota(jnp.int32, (T, T), 0)
    kpos = kb * T + lax.broadcasted_iota(jnp.int32, (T, T), 1)
    return qpos, kpos


def _scores(q, k, c, cr, masked, mask, T, qb, kb, backward):
    s = lax.dot_general(q, k, NT, preferred_element_type=F32) * c
    if cr is not None:
        s = s - cr
    live = None
    if masked:
        qpos, kpos = _positions(T, qb, kb)
        live = mask(qpos, kpos)
        if backward:
            live = live & (qpos >= PAD)
        s = jnp.where(live, s, NEG)
    return s, live


def _prob(s, lse, live):
    p = jnp.exp2(s - lse)
    return p if live is None else jnp.where(live, p, 0.0)


def _both(flag, fn):
    pl.when(flag != 0)(lambda: fn(True))
    pl.when(flag == 0)(lambda: fn(False))


def _flash_fwd(cfg, qa, q_off, ka, k_off, va, v_off, H, hs, scale, name, cr2=None):
    L = qa.shape[0]
    T = cfg.T
    qt, kt, ft = cfg.q_major
    npairs = qt.shape[0]
    c = scale * LOG2E
    decay = cr2 is not None
    W = hs * LANES

    def body(qt_ref, kt_ref, ft_ref, *refs):
        it = iter(refs)
        q_ref, k_ref, v_ref = next(it), next(it), next(it)
        cr_ref = next(it) if decay else None
        o_ref, lse_ref, m_sc, l_sc, acc_sc = next(it), next(it), next(it), next(it), next(it)
        n = pl.program_id(1)
        qb, kb, f = qt_ref[n], kt_ref[n], ft_ref[n]

        @pl.when((f & 1) != 0)
        def _():
            m_sc[...] = jnp.full(m_sc.shape, NEG, F32)
            l_sc[...] = jnp.zeros_like(l_sc)
            acc_sc[...] = jnp.zeros_like(acc_sc)

        def step(masked):
            for i in range(hs):
                cols = slice(i * LANES, (i + 1) * LANES)
                cr = cr_ref[i] if decay else None
                s, _ = _scores(_bf(q_ref[:, cols]), _bf(k_ref[:, cols]), c, cr, masked, cfg.mask, T, qb, kb, False)
                m_prev = m_sc[i]
                m_new = jnp.maximum(m_prev, jnp.max(s, axis=1, keepdims=True))
                alpha = jnp.exp2(m_prev - m_new)
                p = jnp.exp2(s - m_new)
                l_sc[i] = alpha * l_sc[i] + jnp.sum(p, axis=1, keepdims=True)
                acc_sc[i] = alpha * acc_sc[i] + lax.dot_general(p.astype(BF16), _bf(v_ref[:, cols]), NN,
                                                                preferred_element_type=F32)
                m_sc[i] = m_new

        _both(f & 4, step)

        @pl.when((f & 2) != 0)
        def _():
            for i in range(hs):
                cols = slice(i * LANES, (i + 1) * LANES)
                l = l_sc[i]
                o_ref[:, cols] = acc_sc[i] / l
                lse_ref[:, cols] = jnp.broadcast_to(m_sc[i] + jnp.log(l) * LOG2E, (T, LANES))

    qrow = lambda off: pl.BlockSpec((T, W), lambda h, n, qt, kt, ft: (qt[n], off // hs + h))
    krow = lambda off: pl.BlockSpec((T, W), lambda h, n, qt, kt, ft: (kt[n], off // hs + h))
    ins, specs = [qa, ka, va], [qrow(q_off), krow(k_off), krow(v_off)]
    if decay:
        ins.append(cr2)
        specs.append(pl.BlockSpec((hs, 1, T), lambda h, n, qt, kt, ft: (h, 0, kt[n])))
    full = jax.ShapeDtypeStruct((L, H * LANES), F32)
    return pl.pallas_call(
        body, name=name, out_shape=[full, full],
        grid_spec=pltpu.PrefetchScalarGridSpec(
            num_scalar_prefetch=3, grid=(H // hs, npairs), in_specs=specs, out_specs=[qrow(0), qrow(0)],
            scratch_shapes=[pltpu.VMEM((hs, T, 1), F32), pltpu.VMEM((hs, T, 1), F32), pltpu.VMEM((hs, T, LANES), F32)]),
        compiler_params=_params("parallel", "arbitrary"),
    )(qt, kt, ft, *ins)


def _flash_bwd(cfg, qa, q_off, ka, k_off, va, v_off, do, lse2, delta, H, hs, scale, name, cr2=None, lane_sums=False):
    L = qa.shape[0]
    T = cfg.T
    qt, kt, ft = cfg.k_major
    npairs = qt.shape[0]
    c = scale * LOG2E
    decay = cr2 is not None
    W = hs * LANES

    def body(qt_ref, kt_ref, ft_ref, *refs):
        it = iter(refs)
        q_ref, k_ref, v_ref = next(it), next(it), next(it)
        cr_ref = next(it) if decay else None
        do_ref, lse_ref, dl_ref, dk_ref, dv_ref, dq_ref, dk_sc, dv_sc = (next(it) for _ in range(8))
        n = pl.program_id(1)
        qb, kb, f = qt_ref[n], kt_ref[n], ft_ref[n]

        @pl.when(n == 0)
        def _():
            dq_ref[...] = jnp.zeros_like(dq_ref)

        @pl.when((f & 1) != 0)
        def _():
            dk_sc[...] = jnp.zeros_like(dk_sc)
            dv_sc[...] = jnp.zeros_like(dv_sc)

        def step(masked):
            last = lax.broadcasted_iota(jnp.int32, (T, LANES), 1) == LANES - 1
            rows = pl.ds(pl.multiple_of(qb * T, T), T)
            for i in range(hs):
                cols = slice(i * LANES, (i + 1) * LANES)
                q, k, v, dob = _bf(q_ref[:, cols]), _bf(k_ref[:, cols]), _bf(v_ref[:, cols]), _bf(do_ref[:, cols])
                k1 = jnp.where(last, 1.0, k_ref[:, cols]).astype(BF16) if lane_sums else k
                q1 = jnp.where(last, 1.0, q_ref[:, cols]).astype(BF16) if lane_sums else q
                s, live = _scores(q, k, c, cr_ref[i] if decay else None, masked, cfg.mask, T, qb, kb, True)
                p = _prob(s, lse_ref[:, i * LANES:i * LANES + 1], live)
                dv_sc[i] += lax.dot_general(p.astype(BF16), dob, TN, preferred_element_type=F32)
                dp = lax.dot_general(dob, v, NT, preferred_element_type=F32)
                dsb = (p * (dp - dl_ref[:, i * LANES:i * LANES + 1])).astype(BF16)
                dk_sc[i] += lax.dot_general(dsb, q1, TN, preferred_element_type=F32)
                dq_ref[rows, cols] += lax.dot_general(dsb, k1, NN, preferred_element_type=F32)

        _both(f & 4, step)

        @pl.when((f & 2) != 0)
        def _():
            for i in range(hs):
                cols = slice(i * LANES, (i + 1) * LANES)
                dk_ref[:, cols] = dk_sc[i] * scale
                dv_ref[:, cols] = dv_sc[i]

        @pl.when(n == npairs - 1)
        def _():
            dq_ref[...] = dq_ref[...] * scale

    qrow = lambda off: pl.BlockSpec((T, W), lambda h, n, qt, kt, ft: (qt[n], off // hs + h))
    krow = lambda off: pl.BlockSpec((T, W), lambda h, n, qt, kt, ft: (kt[n], off // hs + h))
    ins, specs = [qa, ka, va], [qrow(q_off), krow(k_off), krow(v_off)]
    if decay:
        ins.append(cr2)
        specs.append(pl.BlockSpec((hs, 1, T), lambda h, n, qt, kt, ft: (h, 0, kt[n])))
    ins += [do, lse2, delta]
    specs += [qrow(0), qrow(0), qrow(0)]
    full = jax.ShapeDtypeStruct((L, H * LANES), F32)
    return pl.pallas_call(
        body, name=name, out_shape=[full, full, full],
        grid_spec=pltpu.PrefetchScalarGridSpec(
            num_scalar_prefetch=3, grid=(H // hs, npairs), in_specs=specs,
            out_specs=[krow(0), krow(0), pl.BlockSpec((L, W), lambda h, n, qt, kt, ft: (0, h))],
            scratch_shapes=[pltpu.VMEM((hs, T, LANES), F32), pltpu.VMEM((hs, T, LANES), F32)]),
        compiler_params=_params("parallel", "arbitrary"),
    )(qt, kt, ft, *ins)


def _swa_parts(qb, q_ref, km_ref, kp_ref, kc_ref, vm_ref, vp_ref, vc_ref, c):
    G, B = SWA_G, BLOCK
    q = jnp.concatenate([_bf(q_ref[:, i * LANES:(i + 1) * LANES]) for i in range(G)], axis=0)
    kc = jnp.concatenate([_bf(km_ref[...]), _bf(kp_ref[...]), _bf(kc_ref[...])], axis=0)
    vc = jnp.concatenate([_bf(vm_ref[...]), _bf(vp_ref[...]), _bf(vc_ref[...])], axis=0)
    s = lax.dot_general(q, kc, NT, preferred_element_type=F32) * c
    row = lax.broadcasted_iota(jnp.int32, (G * B, 3 * B), 0)
    col = lax.broadcasted_iota(jnp.int32, (G * B, 3 * B), 1)
    qpos = qb * B + (row & (B - 1))
    kpos = jnp.where(col < B, col, jnp.where(col < 2 * B, (qb - 1) * B + col - B, qb * B + col - 2 * B))
    d = qpos - kpos
    live = ((col < B) & (kpos >= PAD) & (kpos <= qpos)) | ((col >= B) & (kpos >= B) & (d >= 0) & (d < WINDOW))
    return q, kc, vc, jnp.where(live, s, NEG), live


def _stack_col(ref):
    return jnp.concatenate([ref[:, i * LANES:i * LANES + 1] for i in range(SWA_G)], axis=0)


def _swa_specs(nqk):
    G, B = SWA_G, BLOCK
    qrow = pl.BlockSpec((B, G * LANES), lambda hk, qb: (qb, hk))
    kv = lambda off, blk: pl.BlockSpec((B, LANES), lambda hk, qb: (blk(qb), off + hk))
    zero, prev, cur = (lambda qb: 0), (lambda qb: jnp.maximum(qb - 1, 0)), (lambda qb: qb)
    keys = [kv(HEADS, zero), kv(HEADS, prev), kv(HEADS, cur)]
    vals = [kv(nqk, zero), kv(nqk, prev), kv(nqk, cur)]
    return qrow, keys, vals


def _swa_attn_fwd(qk, proj, sink2, scale, name):
    L = qk.shape[0]
    G, B = SWA_G, BLOCK
    nqk = HEADS + SWA_KV
    c = scale * LOG2E

    def body(q_ref, km_ref, kp_ref, kc_ref, vm_ref, vp_ref, vc_ref, sink_ref, o_ref, lse_ref):
        qb = pl.program_id(1)
        q, kc, vc, s, live = _swa_parts(qb, q_ref, km_ref, kp_ref, kc_ref, vm_ref, vp_ref, vc_ref, c)
        sink = jnp.concatenate([jnp.broadcast_to(sink_ref[:, i * LANES:i * LANES + 1], (B, 1)) for i in range(G)], axis=0)
        m = jnp.maximum(jnp.max(s, axis=1, keepdims=True), sink)
        p = jnp.exp2(s - m)
        l = jnp.sum(p, axis=1, keepdims=True) + jnp.exp2(sink - m)
        o = lax.dot_general(p.astype(BF16), vc, NN, preferred_element_type=F32) / l
        lse = m + jnp.log(l) * LOG2E
        for i in range(G):
            o_ref[:, i * LANES:(i + 1) * LANES] = o[i * B:(i + 1) * B]
            lse_ref[:, i * LANES:(i + 1) * LANES] = jnp.broadcast_to(lse[i * B:(i + 1) * B], (B, LANES))

    qrow, keys, vals = _swa_specs(nqk)
    shape = jax.ShapeDtypeStruct((L, HEADS * LANES), F32)
    return pl.pallas_call(
        body, name=name, grid=(SWA_KV, L // B),
        in_specs=[qrow] + keys + vals + [pl.BlockSpec((1, G * LANES), lambda hk, qb: (0, hk))],
        out_specs=[qrow, qrow], out_shape=[shape, shape],
        compiler_params=_params("parallel", "parallel"),
    )(qk, qk, qk, qk, proj, proj, proj, sink2)


def _swa_attn_bwd(qk, proj, do, lse2, delta, scale, name):
    L = qk.shape[0]
    G, B = SWA_G, BLOCK
    nqk = HEADS + SWA_KV
    c = scale * LOG2E

    def body(q_ref, km_ref, kp_ref, kc_ref, vm_ref, vp_ref, vc_ref, do_ref, lse_ref, dl_ref, dq_ref, dk_ref, dv_ref):
        qb = pl.program_id(1)

        @pl.when(qb == 0)
        def _():
            dk_ref[...] = jnp.zeros_like(dk_ref)
            dv_ref[...] = jnp.zeros_like(dv_ref)

        q, kc, vc, s, live = _swa_parts(qb, q_ref, km_ref, kp_ref, kc_ref, vm_ref, vp_ref, vc_ref, c)
        dob = jnp.concatenate([_bf(do_ref[:, i * LANES:(i + 1) * LANES]) for i in range(G)], axis=0)
        p = jnp.where(live, jnp.exp2(s - _stack_col(lse_ref)), 0.0)
        dp = lax.dot_general(dob, vc, NT, preferred_element_type=F32)
        dsb = (p * (dp - _stack_col(dl_ref))).astype(BF16)
        dq = lax.dot_general(dsb, kc, NN, preferred_element_type=F32) * scale
        for i in range(G):
            dq_ref[:, i * LANES:(i + 1) * LANES] = dq[i * B:(i + 1) * B]
        dkc = lax.dot_general(dsb, q, TN, preferred_element_type=F32) * scale
        dvc = lax.dot_general(p.astype(BF16), dob, TN, preferred_element_type=F32)
        starts = (0, pl.multiple_of(jnp.maximum(qb - 1, 0) * B, B), pl.multiple_of(qb * B, B))
        for n, st in enumerate(starts):
            dk_ref[pl.ds(st, B), :] += dkc[n * B:(n + 1) * B]
            dv_ref[pl.ds(st, B), :] += dvc[n * B:(n + 1) * B]

    qrow, keys, vals = _swa_specs(nqk)
    res = pl.BlockSpec((L, LANES), lambda hk, qb: (0, hk))
    return pl.pallas_call(
        body, name=name, grid=(SWA_KV, L // B),
        in_specs=[qrow] + keys + vals + [qrow, qrow, qrow],
        out_specs=[qrow, res, res],
        out_shape=[jax.ShapeDtypeStruct((L, HEADS * LANES), F32), jax.ShapeDtypeStruct((L, SWA_KV * LANES), F32),
                   jax.ShapeDtypeStruct((L, SWA_KV * LANES), F32)],
        compiler_params=_params("parallel", "arbitrary"),
    )(qk, qk, qk, qk, proj, proj, proj, do, lse2, delta)


def _delta(do, o, name):
    L, HW = do.shape
    T = _pick(L, (640, 128))

    def body(do_ref, o_ref, d_ref):
        d_ref[...] = jnp.broadcast_to(jnp.sum(do_ref[...].astype(F32) * o_ref[...], axis=1, keepdims=True), (T, LANES))

    spec = pl.BlockSpec((T, LANES), lambda h, i: (i, h))
    return pl.pallas_call(body, name=name, grid=(HW // LANES, L // T), in_specs=[spec, spec], out_specs=spec,
                          out_shape=jax.ShapeDtypeStruct((L, HW), F32),
                          compiler_params=_params("parallel", "parallel"))(do, o)


def _sink_grad(lse2, delta, sink2, name):
    L, HW = lse2.shape
    tr = 128

    def body(lse_ref, dl_ref, s_ref, o_ref):
        @pl.when(pl.program_id(0) == 0)
        def _():
            o_ref[...] = jnp.zeros_like(o_ref)

        o_ref[...] -= jnp.sum(jnp.exp2(s_ref[...] - lse_ref[...]) * dl_ref[...], axis=0, keepdims=True)

    row = pl.BlockSpec((tr, HW), lambda i: (i, 0))
    vec = pl.BlockSpec((1, HW), lambda i: (0, 0))
    return pl.pallas_call(body, name=name, grid=(L // tr,), in_specs=[row, row, vec], out_specs=vec,
                          out_shape=jax.ShapeDtypeStruct((1, HW), F32), compiler_params=_params("arbitrary"))(
        lse2, delta, sink2)


def _tri(lower):
    r = lax.broadcasted_iota(jnp.int32, (BLOCK, BLOCK), 0)
    c = lax.broadcasted_iota(jnp.int32, (BLOCK, BLOCK), 1)
    return jnp.where((c <= r) if lower else (c >= r), 1.0, 0.0).astype(F32)


def _gate_cumsum(proj, fg_tile, b_pad, name):
    L = proj.shape[0]

    def body(fg_ref, b_ref, c_ref, carry):
        @pl.when(pl.program_id(0) == 0)
        def _():
            carry[...] = jnp.zeros_like(carry)

        x = fg_ref[...] + b_ref[...]
        lf = jnp.minimum(x, 0.0) - jnp.log(1.0 + jnp.exp(-jnp.abs(x)))
        c = jnp.dot(_tri(True), lf, precision=lax.Precision.HIGHEST, preferred_element_type=F32) + carry[...]
        c_ref[...] = c
        carry[...] = c[BLOCK - 1:BLOCK, :]

    return pl.pallas_call(
        body, name=name, grid=(L // BLOCK,),
        in_specs=[pl.BlockSpec((BLOCK, LANES), lambda i: (i, fg_tile)), pl.BlockSpec((1, LANES), lambda i: (0, 0))],
        out_specs=pl.BlockSpec((BLOCK, LANES), lambda i: (i, 0)),
        out_shape=jax.ShapeDtypeStruct((L, LANES), F32),
        scratch_shapes=[pltpu.VMEM((1, LANES), F32)],
        compiler_params=_params("arbitrary"),
    )(proj, b_pad)


def _gate_cumsum_bwd(dc, proj, fg_tile, b_pad, name):
    L = proj.shape[0]
    nb = L // BLOCK

    def body(dc_ref, fg_ref, b_ref, dfg_ref, db_ref, carry):
        @pl.when(pl.program_id(0) == 0)
        def _():
            carry[...] = jnp.zeros_like(carry)
            db_ref[...] = jnp.zeros_like(db_ref)

        dlf = jnp.dot(_tri(False), dc_ref[...], precision=lax.Precision.HIGHEST,
                      preferred_element_type=F32) + carry[...]
        carry[...] = dlf[0:1, :]
        x = fg_ref[...] + b_ref[...]
        lanes = lax.broadcasted_iota(jnp.int32, (BLOCK, LANES), 1)
        rows = (nb - 1 - pl.program_id(0)) * BLOCK + lax.broadcasted_iota(jnp.int32, (BLOCK, LANES), 0)
        dfg = jnp.where((lanes < HEADS) & (rows >= PAD), dlf / (1.0 + jnp.exp(x)), 0.0)
        dfg_ref[...] = jnp.concatenate([dfg, jnp.zeros_like(dfg)], axis=1)
        db_ref[...] += jnp.sum(dfg, axis=0, keepdims=True)

    dfg, db = pl.pallas_call(
        body, name=name, grid=(nb,),
        in_specs=[pl.BlockSpec((BLOCK, LANES), lambda i: (nb - 1 - i, 0)),
                  pl.BlockSpec((BLOCK, LANES), lambda i: (nb - 1 - i, fg_tile)),
                  pl.BlockSpec((1, LANES), lambda i: (0, 0))],
        out_specs=[pl.BlockSpec((BLOCK, 2 * LANES), lambda i: (nb - 1 - i, 0)),
                   pl.BlockSpec((1, LANES), lambda i: (0, 0))],
        out_shape=[jax.ShapeDtypeStruct((L, 2 * LANES), F32), jax.ShapeDtypeStruct((1, LANES), F32)],
        scratch_shapes=[pltpu.VMEM((1, LANES), F32)],
        compiler_params=_params("arbitrary"),
    )(dc, proj, b_pad)
    return dfg, db[0]


def _rope_tables(L, dim, theta, lane0):
    half = dim // 2
    pos = (jnp.arange(L) - PAD).astype(F32)
    inv = theta ** (-jnp.arange(0, dim, 2, dtype=F32) / dim)
    ang = pos[:, None] * inv[None, :]
    cos, sin = jnp.cos(ang), jnp.sin(ang)
    C = jnp.ones((L, LANES), F32).at[:, lane0:lane0 + half].set(cos).at[:, lane0 + half:lane0 + dim].set(cos)
    S1 = jnp.zeros((L, LANES), F32).at[:, lane0:lane0 + half].set(-sin)
    S2 = jnp.zeros((L, LANES), F32).at[:, lane0 + half:lane0 + dim].set(sin)
    return C, S1, S2


def _rot(x, C, S1, S2, R):
    return x * C + pltpu.roll(x, LANES - R, 1) * S1 + pltpu.roll(x, R, 1) * S2


def _rot_t(dy, C, S1, S2, R):
    return dy * C + pltpu.roll(dy * S1, R, 1) + pltpu.roll(dy * S2, LANES - R, 1)


def _rope(x, nt, tabs, R, name, transpose=False, shared=None, shared_tile=0, out_dtype=F32):
    L = x.shape[0]
    T = BLOCK
    fn = _rot_t if transpose else _rot

    def body(*refs):
        if shared is None:
            x_ref, c_ref, s1_ref, s2_ref, o_ref = refs
        else:
            x_ref, sh_ref, c_ref, s1_ref, s2_ref, o_ref = refs
            rs = fn(sh_ref[...], c_ref[...], s1_ref[...], s2_ref[...], R)
        for h in range(nt):
            cols = slice(h * LANES, (h + 1) * LANES)
            if shared is None:
                o_ref[:, cols] = fn(x_ref[:, cols], c_ref[...], s1_ref[...], s2_ref[...], R).astype(out_dtype)
            else:
                o_ref[:, cols] = (x_ref[:, cols] + rs).astype(out_dtype)

    wide = pl.BlockSpec((T, nt * LANES), lambda i: (i, 0))
    tab = pl.BlockSpec((T, LANES), lambda i: (i, 0))
    ins, specs = [x], [wide]
    if shared is not None:
        ins.append(shared)
        specs.append(pl.BlockSpec((T, LANES), lambda i: (i, shared_tile)))
    return pl.pallas_call(body, name=name, grid=(L // T,), in_specs=specs + [tab, tab, tab], out_specs=wide,
                          out_shape=jax.ShapeDtypeStruct((L, nt * LANES), out_dtype),
                          compiler_params=_params("parallel"))(*ins, *tabs)


def _rope_shared_bwd(dk, nt, tabs, R, name):
    L = dk.shape[0]
    tr = 128

    def body(dk_ref, c_ref, s1_ref, s2_ref, o_ref):
        acc = dk_ref[:, 0:LANES]
        for h in range(1, nt):
            acc = acc + dk_ref[:, h * LANES:(h + 1) * LANES]
        o_ref[...] = _rot_t(acc, c_ref[...], s1_ref[...], s2_ref[...], R)

    tab = pl.BlockSpec((tr, LANES), lambda i: (i, 0))
    return pl.pallas_call(body, name=name, grid=(L // tr,),
                          in_specs=[pl.BlockSpec((tr, nt * LANES), lambda i: (i, 0)), tab, tab, tab], out_specs=tab,
                          out_shape=jax.ShapeDtypeStruct((L, LANES), F32), compiler_params=_params("parallel"))(
        dk, *tabs)


def _rms_fwd(pa, gq, gkv, name):
    L = pa.shape[0]
    tr = 128
    Q, KV = MLA_Q_LORA, MLA_KV_LORA

    def body(pa_ref, gq_ref, gkv_ref, q_ref, kv_ref):
        for lo, n, g_ref, o_ref in ((0, Q, gq_ref, q_ref), (Q, KV, gkv_ref, kv_ref)):
            x = pa_ref[:, lo:lo + n]
            r = lax.rsqrt(jnp.mean(x * x, axis=1, keepdims=True) + RMS_EPS)
            o_ref[...] = (x * r * g_ref[...]).astype(BF16)

    return pl.pallas_call(
        body, name=name, grid=(L // tr,),
        in_specs=[pl.BlockSpec((tr, pa.shape[1]), lambda i: (i, 0)), pl.BlockSpec((1, Q), lambda i: (0, 0)),
                  pl.BlockSpec((1, KV), lambda i: (0, 0))],
        out_specs=[pl.BlockSpec((tr, Q), lambda i: (i, 0)), pl.BlockSpec((tr, KV), lambda i: (i, 0))],
        out_shape=[jax.ShapeDtypeStruct((L, Q), BF16), jax.ShapeDtypeStruct((L, KV), BF16)],
        compiler_params=_params("parallel"),
    )(pa, gq.reshape(1, Q), gkv.reshape(1, KV))


def _rms_bwd(pa, dq, dkv, dkr, gq, gkv, name):
    L, W = pa.shape
    tr = 128
    Q, KV = MLA_Q_LORA, MLA_KV_LORA

    def body(pa_ref, dq_ref, dkv_ref, dkr_ref, gq_ref, gkv_ref, dpa_ref, dgq_ref, dgkv_ref):
        @pl.when(pl.program_id(0) == 0)
        def _():
            dgq_ref[...] = jnp.zeros_like(dgq_ref)
            dgkv_ref[...] = jnp.zeros_like(dgkv_ref)

        for lo, n, g_ref, dy_ref, dg_ref in ((0, Q, gq_ref, dq_ref, dgq_ref), (Q, KV, gkv_ref, dkv_ref, dgkv_ref)):
            x = pa_ref[:, lo:lo + n]
            r = lax.rsqrt(jnp.mean(x * x, axis=1, keepdims=True) + RMS_EPS)
            xh = x * r
            dy = dy_ref[...]
            dxh = dy * g_ref[...]
            dpa_ref[:, lo:lo + n] = (r * (dxh - xh * jnp.mean(dxh * xh, axis=1, keepdims=True))).astype(BF16)
            dg_ref[...] += jnp.sum(dy * xh, axis=0, keepdims=True)
        dpa_ref[:, Q + KV:W] = dkr_ref[...].astype(BF16)

    vq = pl.BlockSpec((1, Q), lambda i: (0, 0))
    vkv = pl.BlockSpec((1, KV), lambda i: (0, 0))
    dpa, dgq, dgkv = pl.pallas_call(
        body, name=name, grid=(L // tr,),
        in_specs=[pl.BlockSpec((tr, W), lambda i: (i, 0)), pl.BlockSpec((tr, Q), lambda i: (i, 0)),
                  pl.BlockSpec((tr, KV), lambda i: (i, 0)), pl.BlockSpec((tr, LANES), lambda i: (i, 0)), vq, vkv],
        out_specs=[pl.BlockSpec((tr, W), lambda i: (i, 0)), vq, vkv],
        out_shape=[jax.ShapeDtypeStruct((L, W), BF16), jax.ShapeDtypeStruct((1, Q), F32),
                   jax.ShapeDtypeStruct((1, KV), F32)],
        compiler_params=_params("arbitrary"),
    )(pa, dq, dkv, dkr, gq.reshape(1, Q), gkv.reshape(1, KV))
    return dpa, dgq[0], dgkv[0]


def _loss_head(h, target, name):
    L = h.shape[0]
    tr = BLOCK
    inv = 1.0 / D_MODEL

    def body(h_ref, t_ref, loss_ref, dh_ref):
        i = pl.program_id(0)

        @pl.when(i == 0)
        def _():
            loss_ref[...] = jnp.zeros_like(loss_ref)
            dh_ref[...] = jnp.zeros_like(dh_ref)

        @pl.when(i > 0)
        def _():
            e = h_ref[...] - t_ref[...]
            dh_ref[...] = e * inv
            loss_ref[...] += jnp.sum((e * e).reshape(tr // 8, 8, D_MODEL), axis=0) * (0.5 * inv)

    row = pl.BlockSpec((tr, D_MODEL), lambda i: (i, 0))
    loss, dh = pl.pallas_call(
        body, name=name, grid=(L // tr,),
        in_specs=[row, pl.BlockSpec((tr, D_MODEL), lambda i: (jnp.maximum(i - 1, 0), 0))],
        out_specs=[pl.BlockSpec((8, D_MODEL), lambda i: (0, 0)), row],
        out_shape=[jax.ShapeDtypeStruct((8, D_MODEL), F32), jax.ShapeDtypeStruct((L, D_MODEL), F32)],
        compiler_params=_params("arbitrary"),
    )(h, target)
    return loss, dh


def _pad_heads_cols(w, nh, d, dp=LANES):
    K = w.shape[0]
    return jnp.pad(w.reshape(K, nh, d), ((0, 0), (0, 0), (0, dp - d))).reshape(K, nh * dp)


def _unpad_heads_cols(w, nh, d, dp=LANES):
    K = w.shape[0]
    return w.reshape(K, nh, dp)[:, :, :d].reshape(K, nh * d)


def _pad_heads_rows(w, nh, d):
    N = w.shape[1]
    return jnp.pad(w.reshape(nh, d, N), ((0, 0), (0, LANES - d), (0, 0))).reshape(nh * LANES, N)


def _unpad_heads_rows(w, nh, d):
    N = w.shape[1]
    return w.reshape(nh, LANES, N)[:, :d, :].reshape(nh * d, N)


def _fox_fwd(h, w_in, b_f, w_o, tag):
    L = h.shape[0]
    hd = HEADS * HEAD_DIM
    W = jnp.concatenate([_pad_heads_cols(w_in[:, i * hd:(i + 1) * hd], HEADS, HEAD_DIM) for i in range(3)]
                        + [jnp.pad(w_in[:, 3 * hd:], ((0, 0), (0, 2 * LANES - HEADS)))], axis=1)
    Wo = _pad_heads_rows(w_o, HEADS, HEAD_DIM)
    b_pad = jnp.pad(b_f, (0, LANES - HEADS)).reshape(1, LANES)
    proj = _mm(h, W, "nn", tag + "_proj")
    c = _gate_cumsum(proj, 3 * HEADS, b_pad, tag + "_cumsum")
    dead = (jnp.arange(L) < PAD)[:, None]
    cr2 = jnp.where(dead, -NEG, c[:, :HEADS] * LOG2E).T.reshape(HEADS, 1, L)
    cfg = _Dense(L, True)
    scale = HEAD_DIM ** -0.5
    o, lse2 = _flash_fwd(cfg, proj, 0, proj, HEADS, proj, 2 * HEADS, HEADS, DENSE_HS_FWD, scale, tag + "_attn", cr2=cr2)
    mix = _mm(o, Wo, "nn", tag + "_out")
    return mix, (h, W, Wo, b_pad, proj, cr2, o, lse2)


def _fox_bwd(dmix, res, tag):
    h, W, Wo, b_pad, proj, cr2, o, lse2 = res
    L = h.shape[0]
    cfg = _Dense(L, True)
    scale = HEAD_DIM ** -0.5
    dWo = _mm(o, dmix, "tn", tag + "_dwo")
    do = _mm(dmix, Wo, "nt", tag + "_do", out_dtype=BF16)
    qkv = (proj, 0, proj, HEADS, proj, 2 * HEADS)
    delta = _delta(do, o, tag + "_delta")
    dk, dv, dq = _flash_bwd(cfg, *qkv, do, lse2, delta, HEADS, DENSE_HS, scale, tag + "_bwd", cr2=cr2, lane_sums=True)
    dc = jnp.pad((dq[:, LANES - 1::LANES] - dk[:, LANES - 1::LANES]) * (1.0 / scale), ((0, 0), (0, LANES - HEADS)))
    dfg, db = _gate_cumsum_bwd(dc, proj, 3 * HEADS, b_pad, tag + "_cumsum_bwd")
    dproj = jnp.concatenate([dq.astype(BF16), dk.astype(BF16), dv.astype(BF16), dfg.astype(BF16)], axis=1)
    dW = _mm(h, dproj, "tn", tag + "_dw")
    dh = _mm(dproj, W, "nt", tag + "_dh")
    hp = HEADS * LANES
    dw_in = jnp.concatenate([_unpad_heads_cols(dW[:, i * hp:(i + 1) * hp], HEADS, HEAD_DIM) for i in range(3)]
                            + [dW[:, 3 * hp:3 * hp + HEADS]], axis=1)
    return dh, dict(w_in=dw_in, b_f=db[:HEADS], w_o=_unpad_heads_rows(dWo, HEADS, HEAD_DIM))


def _swa_fwd(h, w_in, sinks, w_o, tag):
    L = h.shape[0]
    qd, kd = HEADS * HEAD_DIM, SWA_KV * HEAD_DIM
    W = jnp.concatenate([_pad_heads_cols(w_in[:, :qd], HEADS, HEAD_DIM),
                         _pad_heads_cols(w_in[:, qd:qd + kd], SWA_KV, HEAD_DIM),
                         _pad_heads_cols(w_in[:, qd + kd:], SWA_KV, HEAD_DIM)], axis=1)
    Wo = _pad_heads_rows(w_o, HEADS, HEAD_DIM)
    sink2 = jnp.repeat(sinks * LOG2E, LANES).reshape(1, HEADS * LANES)
    tabs = _rope_tables(L, ROPE_DIM, ROPE_THETA, 0)
    proj = _mm(h, W, "nn", tag + "_proj")
    nqk = HEADS + SWA_KV
    qk = _rope(proj, nqk, tabs, ROPE_DIM // 2, tag + "_rope")
    scale = HEAD_DIM ** -0.5
    o, lse2 = _swa_attn_fwd(qk, proj, sink2, scale, tag + "_attn")
    mix = _mm(o, Wo, "nn", tag + "_out")
    return mix, (h, W, Wo, sink2, tabs, proj, qk, o, lse2)


def _swa_bwd(dmix, res, tag):
    h, W, Wo, sink2, tabs, proj, qk, o, lse2 = res
    L = h.shape[0]
    nqk = HEADS + SWA_KV
    scale = HEAD_DIM ** -0.5
    dWo = _mm(o, dmix, "tn", tag + "_dwo")
    do = _mm(dmix, Wo, "nt", tag + "_do", out_dtype=BF16)
    delta = _delta(do, o, tag + "_delta")
    dsink = _sink_grad(lse2, delta, sink2, tag + "_dsink")[0, ::LANES]
    dq, dk, dv = _swa_attn_bwd(qk, proj, do, lse2, delta, scale, tag + "_bwd")
    dqk = _rope(jnp.concatenate([dq, dk], axis=1), nqk, tabs, ROPE_DIM // 2, tag + "_rope_bwd", transpose=True,
                out_dtype=BF16)
    dproj = jnp.concatenate([dqk, dv.astype(BF16)], axis=1)
    dW = _mm(h, dproj, "tn", tag + "_dw")
    dh = _mm(dproj, W, "nt", tag + "_dh")
    hp = HEADS * LANES
    dw_in = jnp.concatenate([_unpad_heads_cols(dW[:, :hp], HEADS, HEAD_DIM),
                             _unpad_heads_cols(dW[:, hp:hp + SWA_KV * LANES], SWA_KV, HEAD_DIM),
                             _unpad_heads_cols(dW[:, hp + SWA_KV * LANES:], SWA_KV, HEAD_DIM)], axis=1)
    return dh, dict(w_in=dw_in, sinks=dsink, w_o=_unpad_heads_rows(dWo, HEADS, HEAD_DIM))


def _mla_fwd(h, w_a, g_q, g_kv, w_uq, w_ukv, w_o, tag):
    L = h.shape[0]
    Q, KV = MLA_Q_LORA, MLA_KV_LORA
    dqk = MLA_NOPE + MLA_ROPE
    kr_w = jnp.pad(w_a[:, Q + KV:], ((0, 0), (MLA_NOPE, LANES - dqk)))
    Wa = jnp.concatenate([w_a[:, :Q + KV], kr_w], axis=1)
    Wuq = _pad_heads_cols(w_uq, HEADS, dqk)
    ukv = w_ukv.reshape(KV, HEADS, MLA_NOPE + HEAD_DIM)
    Wukv = jnp.concatenate([_pad_heads_cols(ukv[:, :, :MLA_NOPE].reshape(KV, -1), HEADS, MLA_NOPE),
                            _pad_heads_cols(ukv[:, :, MLA_NOPE:].reshape(KV, -1), HEADS, HEAD_DIM)], axis=1)
    Wo = _pad_heads_rows(w_o, HEADS, HEAD_DIM)
    tabs = _rope_tables(L, MLA_ROPE, MLA_ROPE_THETA, MLA_NOPE)
    R = MLA_ROPE // 2
    pa = _mm(h, Wa, "nn", tag + "_proj")
    cqn, ckvn = _rms_fwd(pa, g_q, g_kv, tag + "_rms")
    q0 = _mm(cqn, Wuq, "nn", tag + "_uq")
    qr = _rope(q0, HEADS, tabs, R, tag + "_rope_q")
    kv0 = _mm(ckvn, Wukv, "nn", tag + "_ukv")
    kk = _rope(kv0, HEADS, tabs, R, tag + "_rope_k", shared=pa, shared_tile=(Q + KV) // LANES)
    cfg = _Dense(L, False)
    scale = dqk ** -0.5
    o, lse2 = _flash_fwd(cfg, qr, 0, kk, 0, kv0, HEADS, HEADS, DENSE_HS_FWD, scale, tag + "_attn")
    mix = _mm(o, Wo, "nn", tag + "_out")
    return mix, (h, Wa, Wuq, Wukv, Wo, g_q, g_kv, tabs, pa, cqn, ckvn, qr, kk, kv0, o, lse2)


def _mla_bwd(dmix, res, tag):
    h, Wa, Wuq, Wukv, Wo, g_q, g_kv, tabs, pa, cqn, ckvn, qr, kk, kv0, o, lse2 = res
    L = h.shape[0]
    Q, KV = MLA_Q_LORA, MLA_KV_LORA
    dqk = MLA_NOPE + MLA_ROPE
    R = MLA_ROPE // 2
    cfg = _Dense(L, False)
    scale = dqk ** -0.5
    dWo = _mm(o, dmix, "tn", tag + "_dwo")
    do = _mm(dmix, Wo, "nt", tag + "_do", out_dtype=BF16)
    delta = _delta(do, o, tag + "_delta")
    dk, dv, dqr = _flash_bwd(cfg, qr, 0, kk, 0, kv0, HEADS, do, lse2, delta, HEADS, DENSE_HS, scale, tag + "_bwd")
    dq0 = _rope(dqr, HEADS, tabs, R, tag + "_rope_q_bwd", transpose=True, out_dtype=BF16)
    dWuq = _mm(cqn, dq0, "tn", tag + "_dwuq")
    dcqn = _mm(dq0, Wuq, "nt", tag + "_dcq")
    dkv = jnp.concatenate([dk, dv], axis=1).astype(BF16)
    dWukv = _mm(ckvn, dkv, "tn", tag + "_dwukv")
    dckvn = _mm(dkv, Wukv, "nt", tag + "_dckv")
    dkr = _rope_shared_bwd(dk, HEADS, tabs, R, tag + "_rope_k_bwd")
    dpa, dgq, dgkv = _rms_bwd(pa, dcqn, dckvn, dkr, g_q, g_kv, tag + "_rms_bwd")
    dWa = _mm(h, dpa, "tn", tag + "_dw")
    dh = _mm(dpa, Wa, "nt", tag + "_dh")
    hp = HEADS * LANES
    dw_a = jnp.concatenate([dWa[:, :Q + KV], dWa[:, Q + KV + MLA_NOPE:Q + KV + dqk]], axis=1)
    dk_n = dWukv[:, :hp].reshape(KV, HEADS, LANES)[:, :, :MLA_NOPE]
    dv_n = dWukv[:, hp:].reshape(KV, HEADS, LANES)[:, :, :HEAD_DIM]
    dw_ukv = jnp.concatenate([dk_n, dv_n], axis=2).reshape(KV, HEADS * (MLA_NOPE + HEAD_DIM))
    return dh, dict(w_a=dw_a, g_q=dgq, g_kv=dgkv, w_uq=_unpad_heads_cols(dWuq, HEADS, dqk), w_ukv=dw_ukv,
                    w_o=_unpad_heads_rows(dWo, HEADS, HEAD_DIM))


MATMUL_WEIGHTS = ("fox_w_in", "fox_w_o", "swa_w_in", "swa_w_o", "mla_w_a", "mla_w_uq", "mla_w_ukv", "mla_w_o",
                  "ffn_w_in", "ffn_w_out")


def _local_step(x, target, w):
    w = {k: (_bf(v) if k in MATMUL_WEIGHTS else v) for k, v in w.items()}
    h = jnp.concatenate([jnp.zeros((PAD, D_MODEL), F32), w["meta_tokens"], x], axis=0)
    hb = h.astype(BF16)
    saved = []
    for i in range(DEPTH):
        kind, j = i % 3, i // 3
        tag = "l%d" % i
        if kind == 0:
            mix, mres = _fox_fwd(hb, w["fox_w_in"][j], w["fox_b_f"][j], w["fox_w_o"][j], tag + "_fox")
        elif kind == 1:
            mix, mres = _swa_fwd(hb, w["swa_w_in"][j], w["swa_sinks"][j], w["swa_w_o"][j], tag + "_swa")
        else:
            mix, mres = _mla_fwd(hb, w["mla_w_a"][j], w["mla_g_q"][j], w["mla_g_kv"][j], w["mla_w_uq"][j],
                                 w["mla_w_ukv"][j], w["mla_w_o"][j], tag + "_mla")
        h1, h1b, xh1, rs1 = _ln_fwd(h, mix, w["ln1_g"][i], w["ln1_b"][i], tag + "_ln1")
        u = _mm(h1b, w["ffn_w_in"][i], "nn", tag + "_ffn_in")
        a = _conv_glu_fwd(u, w["ffn_conv_w"][i], w["ffn_conv_b"][i], tag + "_conv")
        ffn = _mm(a, w["ffn_w_out"][i], "nn", tag + "_ffn_out")
        h2, h2b, xh2, rs2 = _ln_fwd(h1, ffn, w["ln2_g"][i], w["ln2_b"][i], tag + "_ln2")
        saved.append((mres, xh1, rs1, h1b, u, a, xh2, rs2))
        h, hb = h2, h2b
    loss, dh = _loss_head(h, target, "loss_head")

    g = {k: [None] * v.shape[0] for k, v in w.items() if k != "meta_tokens"}
    ga = None
    for i in reversed(range(DEPTH)):
        kind, j = i % 3, i // 3
        tag = "l%d" % i
        mres, xh1, rs1, h1b, u, a, xh2, rs2 = saved[i]
        dz2, dz2b, g["ln2_g"][i], g["ln2_b"][i] = _ln_bwd(ga, dh, xh2, rs2, w["ln2_g"][i], tag + "_ln2_bwd")
        g["ffn_w_out"][i] = _mm(a, dz2b, "tn", tag + "_dw_out")
        da = _mm(dz2b, w["ffn_w_out"][i], "nt", tag + "_da")
        du, g["ffn_conv_w"][i], g["ffn_conv_b"][i] = _conv_glu_bwd(da, u, w["ffn_conv_w"][i], w["ffn_conv_b"][i],
                                                                   tag + "_conv_bwd")
        g["ffn_w_in"][i] = _mm(h1b, du, "tn", tag + "_dw_in")
        dh1 = _mm(du, w["ffn_w_in"][i], "nt", tag + "_dh1")
        dz1, dz1b, g["ln1_g"][i], g["ln1_b"][i] = _ln_bwd(dz2, dh1, xh1, rs1, w["ln1_g"][i], tag + "_ln1_bwd")
        if kind == 0:
            dh, mg = _fox_bwd(dz1b, mres, tag + "_fox")
            pre = "fox_"
        elif kind == 1:
            dh, mg = _swa_bwd(dz1b, mres, tag + "_swa")
            pre = "swa_"
        else:
            dh, mg = _mla_bwd(dz1b, mres, tag + "_mla")
            pre = "mla_"
        for k, v in mg.items():
            g[pre + k][j] = v
        ga = dz1
    dh0 = _axpy(ga, dh, "dh0")
    grads = {k: jnp.stack(v) for k, v in g.items()}
    grads["meta_tokens"] = dh0[PAD:BLOCK]
    return loss, dh0, grads


SHARDED = (("meta_tokens", 1), ("fox_w_in", 2), ("fox_w_o", 1), ("swa_w_in", 2), ("swa_w_o", 1), ("mla_w_a", 1),
           ("mla_g_q", 1), ("mla_g_kv", 1), ("mla_w_uq", 2), ("mla_w_ukv", 2), ("mla_w_o", 1), ("ffn_w_in", 2),
           ("ffn_conv_w", 2), ("ffn_w_out", 1))
REPLICATED = ("ln1_g", "ln1_b", "ln2_g", "ln2_b", "fox_b_f", "swa_sinks", "ffn_conv_b")
WEIGHTS = ("meta_tokens", "ln1_g", "ln1_b", "ln2_g", "ln2_b", "fox_w_in", "fox_b_f", "fox_w_o", "swa_w_in",
           "swa_sinks", "swa_w_o", "mla_w_a", "mla_g_q", "mla_g_kv", "mla_w_uq", "mla_w_ukv", "mla_w_o", "ffn_w_in",
           "ffn_conv_w", "ffn_conv_b", "ffn_w_out")


def _rows(n):
    return -(-n // ROW)


def _pack(arrs, multiple):
    parts = []
    for a in arrs:
        n = math.prod(a.shape)
        parts.append(jnp.pad(a.reshape(-1), (0, _rows(n) * ROW - n)).reshape(-1, ROW))
    total = sum(p.shape[0] for p in parts)
    pad = -total % multiple
    if pad:
        parts.append(jnp.zeros((pad, ROW), parts[0].dtype))
    return jnp.concatenate(parts, axis=0)


def _unpack(flat, shapes):
    out, r = [], 0
    for s in shapes:
        n = math.prod(s)
        out.append(flat[r:r + _rows(n)].reshape(-1)[:n].reshape(s))
        r += _rows(n)
    return out


def _pack_bf16(w, names):
    return _pack([_bf(w[n]) if n in MATMUL_WEIGHTS else lax.bitcast_convert_type(w[n], BF16) for n in names], 2 * ROW)


def _unpack_bf16(flat, names, shapes):
    sh = [s if n in MATMUL_WEIGHTS else s + (2,) for n, s in zip(names, shapes)]
    parts = _unpack(flat, sh)
    return [p if n in MATMUL_WEIGHTS else lax.bitcast_convert_type(p, F32) for n, p in zip(names, parts)]


HBM_SPEC = pl.BlockSpec(memory_space=pltpu.HBM)


def _place():
    x, y, c = lax.axis_index("x"), lax.axis_index("y"), lax.axis_index("c")
    chips = [(1 - x, y), (x, 1 - y), (1 - x, 1 - y)]
    return x, y, c, chips


def _gather_weights(shard):
    R = shard.shape[0]
    Rh = R // 2

    def body(s_ref, o_ref, send_sems, recv_sems, local_sem):
        x, y, c, chips = _place()
        sib = (x, y, 1 - c)

        def half(k, hc):
            return o_ref.at[k, pl.ds(hc * Rh, Rh), :]

        def copy(j, src, dst, to):
            return pltpu.make_async_remote_copy(src_ref=src, dst_ref=dst, send_sem=send_sems.at[j],
                                                recv_sem=recv_sems.at[j], device_id=to, device_id_type=MESH)

        me = 2 * x + y
        mine = pltpu.make_async_copy(s_ref, o_ref.at[me], local_sem)
        mine.start()
        first = [copy(j, s_ref.at[pl.ds(c * Rh, Rh), :], half(me, c), (tx, ty, c)) for j, (tx, ty) in enumerate(chips)]
        for cp in first:
            cp.start()
        passed = []
        for j, (tx, ty) in enumerate(chips):
            k = 2 * tx + ty
            copy(j, half(k, c), half(k, c), (tx, ty, c)).wait_recv()
            fw = copy(3 + j, half(k, c), half(k, c), sib)
            fw.start()
            passed.append(fw)
        for j, (tx, ty) in enumerate(chips):
            k = 2 * tx + ty
            copy(3 + j, half(k, 1 - c), half(k, 1 - c), sib).wait_recv()
        for cp in first + passed:
            cp.wait_send()
        mine.wait()

    return pl.pallas_call(
        body, name="gather_weights", out_shape=jax.ShapeDtypeStruct((N_CHIPS, R, ROW), shard.dtype),
        in_specs=[HBM_SPEC], out_specs=HBM_SPEC,
        scratch_shapes=[pltpu.SemaphoreType.DMA((6,)), pltpu.SemaphoreType.DMA((6,)), pltpu.SemaphoreType.DMA],
    )(shard)


def _swap_halves(G):
    R = G.shape[1]
    Rh = R // 2

    def body(g_ref, a_ref, send_sem, recv_sem):
        x, y, c, _ = _place()
        cp = pltpu.make_async_remote_copy(src_ref=g_ref.at[:, pl.ds((1 - c) * Rh, Rh), :], dst_ref=a_ref,
                                          send_sem=send_sem, recv_sem=recv_sem, device_id=(x, y, 1 - c),
                                          device_id_type=MESH)
        cp.start()
        cp.wait()

    return pl.pallas_call(
        body, name="reduce_swap_halves", out_shape=jax.ShapeDtypeStruct((N_CHIPS, Rh, ROW), G.dtype),
        in_specs=[HBM_SPEC], out_specs=HBM_SPEC,
        scratch_shapes=[pltpu.SemaphoreType.DMA, pltpu.SemaphoreType.DMA],
    )(G)


def _exchange_chips(P):
    def body(p_ref, b_ref, send_sems, recv_sems, local_sem):
        x, y, c, chips = _place()
        me = 2 * x + y

        def copy(j, src, dst, to):
            return pltpu.make_async_remote_copy(src_ref=src, dst_ref=dst, send_sem=send_sems.at[j],
                                                recv_sem=recv_sems.at[j], device_id=to, device_id_type=MESH)

        mine = pltpu.make_async_copy(p_ref.at[me], b_ref.at[me], local_sem)
        mine.start()
        sends = [copy(j, p_ref.at[2 * tx + ty], b_ref.at[me], (tx, ty, c)) for j, (tx, ty) in enumerate(chips)]
        for cp in sends:
            cp.start()
        for j, (tx, ty) in enumerate(chips):
            k = 2 * tx + ty
            copy(j, p_ref.at[k], b_ref.at[k], (tx, ty, c)).wait_recv()
        for cp in sends:
            cp.wait_send()
        mine.wait()

    return pl.pallas_call(
        body, name="reduce_exchange_chips", out_shape=jax.ShapeDtypeStruct(P.shape, P.dtype),
        in_specs=[HBM_SPEC], out_specs=HBM_SPEC,
        scratch_shapes=[pltpu.SemaphoreType.DMA((3,)), pltpu.SemaphoreType.DMA((3,)), pltpu.SemaphoreType.DMA],
    )(P)


def _join_halves(Fh):
    Rh = Fh.shape[0]

    def body(f_ref, o_ref, send_sem, recv_sem, local_sem):
        x, y, c, _ = _place()
        mine = pltpu.make_async_copy(f_ref, o_ref.at[pl.ds(c * Rh, Rh), :], local_sem)
        mine.start()
        cp = pltpu.make_async_remote_copy(src_ref=f_ref, dst_ref=o_ref.at[pl.ds(c * Rh, Rh), :], send_sem=send_sem,
                                          recv_sem=recv_sem, device_id=(x, y, 1 - c), device_id_type=MESH)
        cp.start()
        pltpu.make_async_remote_copy(src_ref=f_ref, dst_ref=o_ref.at[pl.ds((1 - c) * Rh, Rh), :], send_sem=send_sem,
                                     recv_sem=recv_sem, device_id=(x, y, 1 - c), device_id_type=MESH).wait_recv()
        cp.wait_send()
        mine.wait()

    return pl.pallas_call(
        body, name="reduce_join_halves", out_shape=jax.ShapeDtypeStruct((2 * Rh, ROW), Fh.dtype),
        in_specs=[HBM_SPEC], out_specs=HBM_SPEC,
        scratch_shapes=[pltpu.SemaphoreType.DMA, pltpu.SemaphoreType.DMA, pltpu.SemaphoreType.DMA],
    )(Fh)


def _gather_small(v):
    m_per = v.shape[0]

    def body(x_ref, out_ref, send_sems, recv_sems, local_sem):
        x, y, c, chips = _place()
        me, sibling = (x, y, c), (x, y, 1 - c)

        def rows(px, py, pc):
            return out_ref.at[pl.ds((4 * px + 2 * py + pc) * m_per, m_per), :]

        def copy(k, block, to, src=None):
            return pltpu.make_async_remote_copy(src_ref=rows(*block) if src is None else src, dst_ref=rows(*block),
                                                send_sem=send_sems.at[k], recv_sem=recv_sems.at[k], device_id=to,
                                                device_id_type=MESH)

        mine = pltpu.make_async_copy(x_ref, rows(*me), local_sem)
        mine.start()
        first = [copy(0, me, sibling, src=x_ref)]
        first += [copy(1 + j, me, (*chip, c), src=x_ref) for j, chip in enumerate(chips)]
        for cp in first:
            cp.start()
        passed = [copy(4 + j, (*chip, c), sibling) for j, chip in enumerate(chips)]
        for j, chip in enumerate(chips):
            copy(1 + j, (*chip, c), me).wait_recv()
            passed[j].start()
        copy(0, sibling, me).wait_recv()
        for j, chip in enumerate(chips):
            copy(4 + j, (*chip, 1 - c), me).wait_recv()
        for cp in first + passed:
            cp.wait_send()
        mine.wait()

    return pl.pallas_call(
        body, name="gather_small", out_shape=jax.ShapeDtypeStruct((N_DEV * m_per, ROW), v.dtype),
        in_specs=[pl.BlockSpec(memory_space=pltpu.VMEM)], out_specs=pl.BlockSpec(memory_space=pltpu.VMEM),
        scratch_shapes=[pltpu.SemaphoreType.DMA((7,)), pltpu.SemaphoreType.DMA((7,)), pltpu.SemaphoreType.DMA],
    )(v)


def _sum_slots(a, n, name):
    M = a.shape[0] // n
    tr = _pick(M, (512, 256, 128, 64, 40, 8))
    nb = M // tr

    def body(*refs):
        acc = refs[0][...].astype(F32)
        for r in refs[1:-1]:
            acc = acc + r[...].astype(F32)
        refs[-1][...] = acc

    specs = [pl.BlockSpec((tr, ROW), functools.partial(lambda i, k: (k * nb + i, 0), k=k)) for k in range(n)]
    return pl.pallas_call(body, name=name, grid=(nb,), in_specs=specs,
                          out_specs=pl.BlockSpec((tr, ROW), lambda i: (i, 0)),
                          out_shape=jax.ShapeDtypeStruct((M, ROW), F32), compiler_params=_params("parallel"))(*([a] * n))


def _add(a, b, name, out_dtype):
    M = a.shape[0]
    tr = _pick(M, (512, 256, 128, 64, 40, 8))

    def body(a_ref, b_ref, o_ref):
        o_ref[...] = (a_ref[...] + b_ref[...]).astype(out_dtype)

    row = pl.BlockSpec((tr, ROW), lambda i: (i, 0))
    return pl.pallas_call(body, name=name, grid=(M // tr,), in_specs=[row, row], out_specs=row,
                          out_shape=jax.ShapeDtypeStruct((M, ROW), out_dtype), compiler_params=_params("parallel"))(a, b)


def _adamw(g, w, m, v, name):
    shp = w.shape
    N = shp[-1]
    M = math.prod(shp[:-1])
    g, w, m, v = (a.reshape(M, N) for a in (g, w, m, v))
    tr = _pick(M, tuple(t for t in (512, 256, 128, 64, 40, 32, 16, 8) if t * N <= 256 * 1024))
    c1 = 1.0 - ADAM_B1 ** ADAM_STEP
    c2 = 1.0 - ADAM_B2 ** ADAM_STEP

    def body(g_ref, w_ref, m_ref, v_ref, d_ref, nm_ref, nv_ref):
        gg = g_ref[...]
        nm = ADAM_B1 * m_ref[...] + (1.0 - ADAM_B1) * gg
        nv = ADAM_B2 * v_ref[...] + (1.0 - ADAM_B2) * (gg * gg)
        nm_ref[...] = nm
        nv_ref[...] = nv
        d_ref[...] = -ADAM_LR * ((nm / c1) / (jnp.sqrt(nv / c2) + ADAM_EPS) + ADAM_WD * w_ref[...])

    row = pl.BlockSpec((tr, N), lambda i: (i, 0))
    shape = jax.ShapeDtypeStruct((M, N), F32)
    outs = pl.pallas_call(body, name=name, grid=(M // tr,), in_specs=[row] * 4, out_specs=[row] * 3,
                          out_shape=[shape] * 3, compiler_params=_params("parallel"))(g, w, m, v)
    return [o.reshape(shp) for o in outs]


def kernel(x, meta_tokens, ln1_g, ln1_b, ln2_g, ln2_b, fox_w_in, fox_b_f, fox_w_o, swa_w_in, swa_sinks, swa_w_o, mla_w_a, mla_g_q, mla_g_kv, mla_w_uq, mla_w_ukv, mla_w_o, ffn_w_in, ffn_conv_w, ffn_conv_b, ffn_w_out, loss_target, m_meta_tokens, m_ln1_g, m_ln1_b, m_ln2_g, m_ln2_b, m_fox_w_in, m_fox_b_f, m_fox_w_o, m_swa_w_in, m_swa_sinks, m_swa_w_o, m_mla_w_a, m_mla_g_q, m_mla_g_kv, m_mla_w_uq, m_mla_w_ukv, m_mla_w_o, m_ffn_w_in, m_ffn_conv_w, m_ffn_conv_b, m_ffn_w_out, v_meta_tokens, v_ln1_g, v_ln1_b, v_ln2_g, v_ln2_b, v_fox_w_in, v_fox_b_f, v_fox_w_o, v_swa_w_in, v_swa_sinks, v_swa_w_o, v_mla_w_a, v_mla_g_q, v_mla_g_kv, v_mla_w_uq, v_mla_w_ukv, v_mla_w_o, v_ffn_w_in, v_ffn_conv_w, v_ffn_conv_b, v_ffn_w_out):
    given = dict(locals())
    w = {n: given[n] for n in WEIGHTS}
    m = {n: given["m_" + n] for n in WEIGHTS}
    v = {n: given["v_" + n] for n in WEIGHTS}
    sh_names = [n for n, _ in SHARDED]
    sh_shapes = [w[n].shape for n in sh_names]

    gathered = _gather_weights(_pack_bf16(w, sh_names))
    full = dict(w)
    per_chip = [_unpack_bf16(gathered[k], sh_names, sh_shapes) for k in range(N_CHIPS)]
    for t, (n, ax) in enumerate(SHARDED):
        full[n] = jnp.concatenate([per_chip[k][t] for k in range(N_CHIPS)], axis=ax)

    loss_part, dh0, grads = _local_step(x[0], loss_target[0], full)
    loss = lax.psum(jnp.sum(loss_part), ("x", "y", "c"))
    grad_x = dh0[BLOCK:][None]

    split = {n: jnp.split(grads[n], N_CHIPS, axis=ax) for n, ax in SHARDED}
    G = jnp.stack([_pack([split[n][k] for n in sh_names], 2 * ROW) for k in range(N_CHIPS)])
    Rh = G.shape[1] // 2
    c = lax.axis_index("c")
    mine = lax.dynamic_slice_in_dim(G, c * Rh, Rh, axis=1)
    P = _add(mine.reshape(N_CHIPS * Rh, ROW), _swap_halves(G).reshape(N_CHIPS * Rh, ROW), "reduce_pair_sum", BF16)
    B = _exchange_chips(P.reshape(N_CHIPS, Rh, ROW))
    Fh = _sum_slots(B.reshape(N_CHIPS * Rh, ROW), N_CHIPS, "reduce_chip_sum")
    Fg = _join_halves(Fh)
    out = {}
    for n, g_n in zip(sh_names, _unpack(Fg, sh_shapes)):
        out["grad", n] = g_n
        out["delta", n], out["new_m", n], out["new_v", n] = _adamw(g_n, w[n], m[n], v[n], "adamw_" + n)

    rp_shapes = [w[n].shape for n in REPLICATED]
    small = _gather_small(_pack([grads[n] for n in REPLICATED], 8))
    g_r = _sum_slots(small, N_DEV, "reduce_small_sum")
    d_r, m_r, v_r = _adamw(g_r, _pack([w[n] for n in REPLICATED], 8), _pack([m[n] for n in REPLICATED], 8),
                           _pack([v[n] for n in REPLICATED], 8), "adamw_replicated")

    for kind, fr in (("grad", g_r), ("delta", d_r), ("new_m", m_r), ("new_v", v_r)):
        for n, a in zip(REPLICATED, _unpack(fr, rp_shapes)):
            out[kind, n] = a
    return (loss, grad_x, *[out[k, n] for k in ("grad", "delta", "new_m", "new_v") for n in WEIGHTS])
```

```python
import functools
import math

import numpy as np
import jax
import jax.numpy as jnp
from jax import lax
from jax.experimental import pallas as pl
from jax.experimental.pallas import tpu as pltpu

F32 = jnp.float32
BF16 = jnp.bfloat16

D_MODEL = 1024
DEPTH = 4
BLOCK = 128
N_META = 16
PAD = BLOCK - N_META
NEG = -1e30
ALPHA = (2.0 * DEPTH) ** 0.25
LN_EPS = 1e-5
RMS_EPS = 1e-6
HEADS = 16
HEAD_DIM = 64
LANES = 128
SWA_KV = 2
SWA_G = HEADS // SWA_KV
WINDOW = 128
ROPE_THETA = 500000.0
ROPE_DIM = 16
MLA_Q_LORA = 384
MLA_KV_LORA = 256
MLA_NOPE = 64
MLA_ROPE = 32
MLA_ROPE_THETA = 10000.0
D_FF = 2816
ADAM_LR = 0.001
ADAM_B1 = 0.9
ADAM_B2 = 0.999
ADAM_EPS = 1e-08
ADAM_WD = 0.01
ADAM_STEP = 10
N_CHIPS = 4
N_DEV = 8
ROW = 1024
VMEM_LIMIT = 48 * 1024 * 1024
MESH = pl.DeviceIdType.MESH
LOG2E = 1.4426950408889634
DENSE_HS = 2
DENSE_HS_FWD = 4

NN = (((1,), (0,)), ((), ()))
NT = (((1,), (1,)), ((), ()))
TN = (((0,), (0,)), ((), ()))


def _pick(n, cands):
    for c in cands:
        if n % c == 0:
            return c
    return n


def _params(*sem):
    return pltpu.CompilerParams(dimension_semantics=sem, vmem_limit_bytes=VMEM_LIMIT)


def _bf(x):
    return x if x.dtype == BF16 else x.astype(BF16)


def _mm(a, b, mode, name, out_dtype=F32):
    if mode == "nn":
        (M, K), (_, N) = a.shape, b.shape
    elif mode == "nt":
        (M, K), (N, _) = a.shape, b.shape
    else:
        (K, M), (_, N) = a.shape, b.shape
    tm = _pick(M, (1664, 1408, 1024, 640, 512, 384, 256, 128))
    tn = _pick(N, (640, 512, 384, 1408, 256, 128))
    tk = K if (K <= 1024 and mode != "tn") else _pick(K, (640, 512, 384, 1408, 256, 128))
    nk = K // tk
    dn = {"nn": NN, "nt": NT, "tn": TN}[mode]

    def body(a_ref, b_ref, o_ref, *acc):
        part = lax.dot_general(_bf(a_ref[...]), _bf(b_ref[...]), dn, preferred_element_type=F32)
        if nk == 1:
            o_ref[...] = part.astype(out_dtype)
            return
        acc_ref, = acc
        k = pl.program_id(2)

        @pl.when(k == 0)
        def _():
            acc_ref[...] = part

        @pl.when(k > 0)
        def _():
            acc_ref[...] += part

        @pl.when(k == nk - 1)
        def _():
            o_ref[...] = acc_ref[...].astype(out_dtype)

    if mode == "tn":
        a_spec = pl.BlockSpec((tk, tm), lambda i, j, k: (k, i))
    else:
        a_spec = pl.BlockSpec((tm, tk), lambda i, j, k: (i, k))
    if mode == "nt":
        b_spec = pl.BlockSpec((tn, tk), lambda i, j, k: (j, k))
    else:
        b_spec = pl.BlockSpec((tk, tn), lambda i, j, k: (k, j))
    return pl.pallas_call(
        body, name=name, grid=(M // tm, N // tn, nk),
        in_specs=[a_spec, b_spec],
        out_specs=pl.BlockSpec((tm, tn), lambda i, j, k: (i, j)),
        out_shape=jax.ShapeDtypeStruct((M, N), out_dtype),
        scratch_shapes=[pltpu.VMEM((tm, tn), F32)] if nk > 1 else [],
        compiler_params=_params("parallel", "parallel", "arbitrary"),
    )(a, b)


def _ln_fwd(h, mix, g, b, name):
    L = h.shape[0]
    tr = 128

    def body(h_ref, m_ref, g_ref, b_ref, o_ref, ob_ref, xh_ref, rs_ref):
        z = ALPHA * h_ref[...] + m_ref[...]
        mu = jnp.mean(z, axis=1, keepdims=True)
        zc = z - mu
        var = jnp.mean(zc * zc, axis=1, keepdims=True)
        rstd = lax.rsqrt(var + LN_EPS)
        xh = zc * rstd
        xh_ref[...] = xh
        rs_ref[...] = rstd
        out = xh * g_ref[...] + b_ref[...]
        o_ref[...] = out
        ob_ref[...] = out.astype(BF16)

    row = pl.BlockSpec((tr, D_MODEL), lambda i: (i, 0))
    vec = pl.BlockSpec((1, D_MODEL), lambda i: (0, 0))
    return pl.pallas_call(
        body, name=name, grid=(L // tr,),
        in_specs=[row, row, vec, vec],
        out_specs=[row, row, row, pl.BlockSpec((tr, 1), lambda i: (i, 0))],
        out_shape=[jax.ShapeDtypeStruct((L, D_MODEL), F32), jax.ShapeDtypeStruct((L, D_MODEL), BF16),
                   jax.ShapeDtypeStruct((L, D_MODEL), F32), jax.ShapeDtypeStruct((L, 1), F32)],
        compiler_params=_params("parallel"),
    )(h, mix, g.reshape(1, D_MODEL), b.reshape(1, D_MODEL))


def _ln_bwd(ga, gb, xhat, rstd, g, name):
    L = xhat.shape[0]
    tr = 128
    two = ga is not None

    def body(*refs):
        if two:
            ga_ref, gb_ref, xh_ref, rs_ref, g_ref, dz_ref, dzb_ref, dg_ref, db_ref = refs
            dy = ALPHA * ga_ref[...] + gb_ref[...]
        else:
            gb_ref, xh_ref, rs_ref, g_ref, dz_ref, dzb_ref, dg_ref, db_ref = refs
            dy = gb_ref[...]
        xh = xh_ref[...]
        dxh = dy * g_ref[...]
        c1 = jnp.mean(dxh, axis=1, keepdims=True)
        c2 = jnp.mean(dxh * xh, axis=1, keepdims=True)
        dz = rs_ref[...] * (dxh - c1 - xh * c2)
        dz_ref[...] = dz
        dzb_ref[...] = dz.astype(BF16)

        @pl.when(pl.program_id(0) == 0)
        def _():
            dg_ref[...] = jnp.zeros_like(dg_ref)
            db_ref[...] = jnp.zeros_like(db_ref)

        dg_ref[...] += jnp.sum(dy * xh, axis=0, keepdims=True)
        db_ref[...] += jnp.sum(dy, axis=0, keepdims=True)

    row = pl.BlockSpec((tr, D_MODEL), lambda i: (i, 0))
    vec = pl.BlockSpec((1, D_MODEL), lambda i: (0, 0))
    ins = ([ga] if two else []) + [gb, xhat, rstd, g.reshape(1, D_MODEL)]
    specs = ([row] if two else []) + [row, row, pl.BlockSpec((tr, 1), lambda i: (i, 0)), vec]
    dz, dzb, dg, db = pl.pallas_call(
        body, name=name, grid=(L // tr,),
        in_specs=specs, out_specs=[row, row, vec, vec],
        out_shape=[jax.ShapeDtypeStruct((L, D_MODEL), F32), jax.ShapeDtypeStruct((L, D_MODEL), BF16),
                   jax.ShapeDtypeStruct((1, D_MODEL), F32), jax.ShapeDtypeStruct((1, D_MODEL), F32)],
        compiler_params=_params("arbitrary"),
    )(*ins)
    return dz, dzb, dg[0], db[0]


def _axpy(a, b, name):
    L, N = a.shape
    tr = 128

    def body(a_ref, b_ref, o_ref):
        o_ref[...] = ALPHA * a_ref[...] + b_ref[...]

    row = pl.BlockSpec((tr, N), lambda i: (i, 0))
    return pl.pallas_call(body, name=name, grid=(L // tr,), in_specs=[row, row], out_specs=row,
                          out_shape=jax.ShapeDtypeStruct((L, N), F32), compiler_params=_params("parallel"))(a, b)


def _shift_down(cur, prev8, n):
    rows = lax.broadcasted_iota(jnp.int32, cur.shape, 0)
    out = pltpu.roll(cur, n, 0)
    for r in range(n):
        out = jnp.where(rows == r, prev8[8 - n + r:8 - n + r + 1, :], out)
    return out


def _shift_up(cur, next8, n):
    tr = cur.shape[0]
    rows = lax.broadcasted_iota(jnp.int32, cur.shape, 0)
    out = pltpu.roll(cur, tr - n, 0)
    for r in range(n):
        out = jnp.where(rows == tr - n + r, next8[r:r + 1, :], out)
    return out


def _silu(x):
    return x / (1.0 + jnp.exp(-x))


def _conv(cur, prev8, cw_ref, cb_ref):
    y = cb_ref[...] + _shift_down(cur, prev8, 2) * cw_ref[0:1, :]
    y = y + _shift_down(cur, prev8, 1) * cw_ref[1:2, :]
    return y + cur * cw_ref[2:3, :]


def _valid_rows(i, tr, u_ref, up_ref):
    rows = i * tr + lax.broadcasted_iota(jnp.int32, u_ref.shape, 0)
    prow = i * tr - 8 + lax.broadcasted_iota(jnp.int32, up_ref.shape, 0)
    return jnp.where(rows >= PAD, u_ref[...], 0.0), jnp.where(prow >= PAD, up_ref[...], 0.0), rows


def _conv_glu_fwd(u, cw, cb, name):
    L, F2 = u.shape
    F = F2 // 2
    tr = 128

    def body(u_ref, up_ref, cw_ref, cb_ref, a_ref):
        cur, prev, _ = _valid_rows(pl.program_id(0), tr, u_ref, up_ref)
        y = _conv(cur, prev, cw_ref, cb_ref)
        a_ref[...] = (_silu(y[:, :F]) * y[:, F:]).astype(BF16)

    return pl.pallas_call(
        body, name=name, grid=(L // tr,),
        in_specs=[pl.BlockSpec((tr, F2), lambda i: (i, 0)),
                  pl.BlockSpec((8, F2), lambda i: (jnp.maximum(i * (tr // 8) - 1, 0), 0)),
                  pl.BlockSpec((3, F2), lambda i: (0, 0)),
                  pl.BlockSpec((1, F2), lambda i: (0, 0))],
        out_specs=pl.BlockSpec((tr, F), lambda i: (i, 0)),
        out_shape=jax.ShapeDtypeStruct((L, F), BF16),
        compiler_params=_params("parallel"),
    )(u, u, cw, cb.reshape(1, F2))


def _conv_glu_bwd(da, u, cw, cb, name):
    L, F2 = u.shape
    F = F2 // 2
    tr = 128
    nb = L // tr

    def dy_of(yv, dav):
        g, val = yv[:, :F], yv[:, F:]
        sg = 1.0 / (1.0 + jnp.exp(-g))
        dg = dav * val * (sg * (1.0 + g * (1.0 - sg)))
        dv = dav * (g * sg)
        return jnp.concatenate([dg, dv], axis=1)

    def body(da_ref, dan_ref, u_ref, up_ref, un_ref, cw_ref, cb_ref, du_ref, dcw_ref, dcb_ref):
        i = pl.program_id(0)
        cur, prev, rows = _valid_rows(i, tr, u_ref, up_ref)
        dy = dy_of(_conv(cur, prev, cw_ref, cb_ref), da_ref[...])
        yn = _conv(un_ref[...], cur[tr - 8:tr, :], cw_ref, cb_ref)
        dyn = jnp.where(i < nb - 1, dy_of(yn, dan_ref[...]), 0.0)
        du = dy * cw_ref[2:3, :] + _shift_up(dy, dyn, 1) * cw_ref[1:2, :] + _shift_up(dy, dyn, 2) * cw_ref[0:1, :]
        du_ref[...] = jnp.where(rows >= PAD, du, 0.0).astype(BF16)

        @pl.when(i == 0)
        def _():
            dcw_ref[...] = jnp.zeros_like(dcw_ref)
            dcb_ref[...] = jnp.zeros_like(dcb_ref)

        dcw_ref[0:1, :] += jnp.sum(dy * _shift_down(cur, prev, 2), axis=0, keepdims=True)
        dcw_ref[1:2, :] += jnp.sum(dy * _shift_down(cur, prev, 1), axis=0, keepdims=True)
        dcw_ref[2:3, :] += jnp.sum(dy * cur, axis=0, keepdims=True)
        dcb_ref[...] += jnp.sum(dy, axis=0, keepdims=True)

    nxt = lambda i: (jnp.minimum((i + 1) * (tr // 8), L // 8 - 1), 0)
    prv = lambda i: (jnp.maximum(i * (tr // 8) - 1, 0), 0)
    du, dcw, dcb = pl.pallas_call(
        body, name=name, grid=(nb,),
        in_specs=[pl.BlockSpec((tr, F), lambda i: (i, 0)), pl.BlockSpec((8, F), nxt),
                  pl.BlockSpec((tr, F2), lambda i: (i, 0)), pl.BlockSpec((8, F2), prv), pl.BlockSpec((8, F2), nxt),
                  pl.BlockSpec((3, F2), lambda i: (0, 0)), pl.BlockSpec((1, F2), lambda i: (0, 0))],
        out_specs=[pl.BlockSpec((tr, F2), lambda i: (i, 0)), pl.BlockSpec((3, F2), lambda i: (0, 0)),
                   pl.BlockSpec((1, F2), lambda i: (0, 0))],
        out_shape=[jax.ShapeDtypeStruct((L, F2), BF16), jax.ShapeDtypeStruct((3, F2), F32),
                   jax.ShapeDtypeStruct((1, F2), F32)],
        compiler_params=_params("arbitrary"),
    )(da, da, u, u, u, cw, cb.reshape(1, F2))
    return du, dcw, dcb[0]


def _dense_mask(qpos, kpos):
    return (kpos <= qpos) & (kpos >= PAD)


def _tables(pairs):
    qt, kt, ft = [], [], []
    for grp in pairs:
        for n, (qb, kb, msk) in enumerate(grp):
            qt.append(qb)
            kt.append(kb)
            ft.append((1 if n == 0 else 0) | (2 if n == len(grp) - 1 else 0) | (4 if msk else 0))
    return tuple(jnp.asarray(np.asarray(t, np.int32)) for t in (qt, kt, ft))


class _Dense:
    mask = staticmethod(_dense_mask)

    def __init__(self, L, pad_in_cr2):
        self.T = T = 640 if L % 640 == 0 else 128
        nb = L // T
        m = lambda qb, kb: kb == qb or (kb == 0 and not pad_in_cr2) or (qb * T < PAD)
        self.q_major = _tables([[(qb, kb, m(qb, kb)) for kb in range(qb + 1)] for qb in range(nb)])
        self.k_major = _tables([[(qb, kb, m(qb, kb)) for qb in range(kb, nb)] for kb in range(nb)])


def _positions(T, qb, kb):
    qpos = qb * T + lax.broadcasted_iota(jnp.int32, (T, T), 0)
    kpos = kb * T + lax.broadcasted_iota(jnp.int32, (T, T), 1)
    return qpos, kpos


def _scores(q, k, c, cr, masked, mask, T, qb, kb, backward):
    s = lax.dot_general(q, k, NT, preferred_element_type=F32) * c
    if cr is not None:
        s = s - cr
    live = None
    if masked:
        qpos, kpos = _positions(T, qb, kb)
        live = mask(qpos, kpos)
        if backward:
            live = live & (qpos >= PAD)
        s = jnp.where(live, s, NEG)
    return s, live


def _prob(s, lse, live):
    p = jnp.exp2(s - lse)
    return p if live is None else jnp.where(live, p, 0.0)


def _both(flag, fn):
    pl.when(flag != 0)(lambda: fn(True))
    pl.when(flag == 0)(lambda: fn(False))


def _flash_fwd(cfg, qa, q_off, ka, k_off, va, v_off, H, hs, scale, name, cr2=None):
    L = qa.shape[0]
    T = cfg.T
    qt, kt, ft = cfg.q_major
    npairs = qt.shape[0]
    c = scale * LOG2E
    decay = cr2 is not None
    W = hs * LANES

    def body(qt_ref, kt_ref, ft_ref, *refs):
        it = iter(refs)
        q_ref, k_ref, v_ref = next(it), next(it), next(it)
        cr_ref = next(it) if decay else None
        o_ref, lse_ref, m_sc, l_sc, acc_sc = next(it), next(it), next(it), next(it), next(it)
        n = pl.program_id(1)
        qb, kb, f = qt_ref[n], kt_ref[n], ft_ref[n]

        @pl.when((f & 1) != 0)
        def _():
            m_sc[...] = jnp.full(m_sc.shape, NEG, F32)
            l_sc[...] = jnp.zeros_like(l_sc)
            acc_sc[...] = jnp.zeros_like(acc_sc)

        def step(masked):
            for i in range(hs):
                cols = slice(i * LANES, (i + 1) * LANES)
                cr = cr_ref[i] if decay else None
                s, _ = _scores(_bf(q_ref[:, cols]), _bf(k_ref[:, cols]), c, cr, masked, cfg.mask, T, qb, kb, False)
                m_prev = m_sc[i]
                m_new = jnp.maximum(m_prev, jnp.max(s, axis=1, keepdims=True))
                alpha = jnp.exp2(m_prev - m_new)
                p = jnp.exp2(s - m_new)
                l_sc[i] = alpha * l_sc[i] + jnp.sum(p, axis=1, keepdims=True)
                acc_sc[i] = alpha * acc_sc[i] + lax.dot_general(p.astype(BF16), _bf(v_ref[:, cols]), NN,
                                                                preferred_element_type=F32)
                m_sc[i] = m_new

        _both(f & 4, step)

        @pl.when((f & 2) != 0)
        def _():
            for i in range(hs):
                cols = slice(i * LANES, (i + 1) * LANES)
                l = l_sc[i]
                o_ref[:, cols] = acc_sc[i] / l
                lse_ref[:, cols] = jnp.broadcast_to(m_sc[i] + jnp.log(l) * LOG2E, (T, LANES))

    qrow = lambda off: pl.BlockSpec((T, W), lambda h, n, qt, kt, ft: (qt[n], off // hs + h))
    krow = lambda off: pl.BlockSpec((T, W), lambda h, n, qt, kt, ft: (kt[n], off // hs + h))
    ins, specs = [qa, ka, va], [qrow(q_off), krow(k_off), krow(v_off)]
    if decay:
        ins.append(cr2)
        specs.append(pl.BlockSpec((hs, 1, T), lambda h, n, qt, kt, ft: (h, 0, kt[n])))
    full = jax.ShapeDtypeStruct((L, H * LANES), F32)
    return pl.pallas_call(
        body, name=name, out_shape=[full, full],
        grid_spec=pltpu.PrefetchScalarGridSpec(
            num_scalar_prefetch=3, grid=(H // hs, npairs), in_specs=specs, out_specs=[qrow(0), qrow(0)],
            scratch_shapes=[pltpu.VMEM((hs, T, 1), F32), pltpu.VMEM((hs, T, 1), F32), pltpu.VMEM((hs, T, LANES), F32)]),
        compiler_params=_params("parallel", "arbitrary"),
    )(qt, kt, ft, *ins)


def _flash_bwd(cfg, qa, q_off, ka, k_off, va, v_off, do, lse2, delta, H, hs, scale, name, cr2=None, lane_sums=False):
    L = qa.shape[0]
    T = cfg.T
    qt, kt, ft = cfg.k_major
    npairs = qt.shape[0]
    c = scale * LOG2E
    decay = cr2 is not None
    W = hs * LANES

    def body(qt_ref, kt_ref, ft_ref, *refs):
        it = iter(refs)
        q_ref, k_ref, v_ref = next(it), next(it), next(it)
        cr_ref = next(it) if decay else None
        do_ref, lse_ref, dl_ref, dk_ref, dv_ref, dq_ref, dk_sc, dv_sc = (next(it) for _ in range(8))
        n = pl.program_id(1)
        qb, kb, f = qt_ref[n], kt_ref[n], ft_ref[n]

        @pl.when(n == 0)
        def _():
            dq_ref[...] = jnp.zeros_like(dq_ref)

        @pl.when((f & 1) != 0)
        def _():
            dk_sc[...] = jnp.zeros_like(dk_sc)
            dv_sc[...] = jnp.zeros_like(dv_sc)

        def step(masked):
            last = lax.broadcasted_iota(jnp.int32, (T, LANES), 1) == LANES - 1
            rows = pl.ds(pl.multiple_of(qb * T, T), T)
            for i in range(hs):
                cols = slice(i * LANES, (i + 1) * LANES)
                q, k, v, dob = _bf(q_ref[:, cols]), _bf(k_ref[:, cols]), _bf(v_ref[:, cols]), _bf(do_ref[:, cols])
                k1 = jnp.where(last, 1.0, k_ref[:, cols]).astype(BF16) if lane_sums else k
                q1 = jnp.where(last, 1.0, q_ref[:, cols]).astype(BF16) if lane_sums else q
                s, live = _scores(q, k, c, cr_ref[i] if decay else None, masked, cfg.mask, T, qb, kb, True)
                p = _prob(s, lse_ref[:, i * LANES:i * LANES + 1], live)
                dv_sc[i] += lax.dot_general(p.astype(BF16), dob, TN, preferred_element_type=F32)
                dp = lax.dot_general(dob, v, NT, preferred_element_type=F32)
                dsb = (p * (dp - dl_ref[:, i * LANES:i * LANES + 1])).astype(BF16)
                dk_sc[i] += lax.dot_general(dsb, q1, TN, preferred_element_type=F32)
                dq_ref[rows, cols] += lax.dot_general(dsb, k1, NN, preferred_element_type=F32)

        _both(f & 4, step)

        @pl.when((f & 2) != 0)
        def _():
            for i in range(hs):
                cols = slice(i * LANES, (i + 1) * LANES)
                dk_ref[:, cols] = dk_sc[i] * scale
                dv_ref[:, cols] = dv_sc[i]

        @pl.when(n == npairs - 1)
        def _():
            dq_ref[...] = dq_ref[...] * scale

    qrow = lambda off: pl.BlockSpec((T, W), lambda h, n, qt, kt, ft: (qt[n], off // hs + h))
    krow = lambda off: pl.BlockSpec((T, W), lambda h, n, qt, kt, ft: (kt[n], off // hs + h))
    ins, specs = [qa, ka, va], [qrow(q_off), krow(k_off), krow(v_off)]
    if decay:
        ins.append(cr2)
        specs.append(pl.BlockSpec((hs, 1, T), lambda h, n, qt, kt, ft: (h, 0, kt[n])))
    ins += [do, lse2, delta]
    specs += [qrow(0), qrow(0), qrow(0)]
    full = jax.ShapeDtypeStruct((L, H * LANES), F32)
    return pl.pallas_call(
        body, name=name, out_shape=[full, full, full],
        grid_spec=pltpu.PrefetchScalarGridSpec(
            num_scalar_prefetch=3, grid=(H // hs, npairs), in_specs=specs,
            out_specs=[krow(0), krow(0), pl.BlockSpec((L, W), lambda h, n, qt, kt, ft: (0, h))],
            scratch_shapes=[pltpu.VMEM((hs, T, LANES), F32), pltpu.VMEM((hs, T, LANES), F32)]),
        compiler_params=_params("parallel", "arbitrary"),
    )(qt, kt, ft, *ins)


def _swa_parts(qb, q_ref, km_ref, kp_ref, kc_ref, vm_ref, vp_ref, vc_ref, c):
    G, B = SWA_G, BLOCK
    q = jnp.concatenate([_bf(q_ref[:, i * LANES:(i + 1) * LANES]) for i in range(G)], axis=0)
    kc = jnp.concatenate([_bf(km_ref[...]), _bf(kp_ref[...]), _bf(kc_ref[...])], axis=0)
    vc = jnp.concatenate([_bf(vm_ref[...]), _bf(vp_ref[...]), _bf(vc_ref[...])], axis=0)
    s = lax.dot_general(q, kc, NT, preferred_element_type=F32) * c
    row = lax.broadcasted_iota(jnp.int32, (G * B, 3 * B), 0)
    col = lax.broadcasted_iota(jnp.int32, (G * B, 3 * B), 1)
    qpos = qb * B + (row & (B - 1))
    kpos = jnp.where(col < B, col, jnp.where(col < 2 * B, (qb - 1) * B + col - B, qb * B + col - 2 * B))
    d = qpos - kpos
    live = ((col < B) & (kpos >= PAD) & (kpos <= qpos)) | ((col >= B) & (kpos >= B) & (d >= 0) & (d < WINDOW))
    return q, kc, vc, jnp.where(live, s, NEG), live


def _stack_col(ref):
    return jnp.concatenate([ref[:, i * LANES:i * LANES + 1] for i in range(SWA_G)], axis=0)


def _swa_specs(nqk):
    G, B = SWA_G, BLOCK
    qrow = pl.BlockSpec((B, G * LANES), lambda hk, qb: (qb, hk))
    kv = lambda off, blk: pl.BlockSpec((B, LANES), lambda hk, qb: (blk(qb), off + hk))
    zero, prev, cur = (lambda qb: 0), (lambda qb: jnp.maximum(qb - 1, 0)), (lambda qb: qb)
    keys = [kv(HEADS, zero), kv(HEADS, prev), kv(HEADS, cur)]
    vals = [kv(nqk, zero), kv(nqk, prev), kv(nqk, cur)]
    return qrow, keys, vals


def _swa_attn_fwd(qk, proj, sink2, scale, name):
    L = qk.shape[0]
    G, B = SWA_G, BLOCK
    nqk = HEADS + SWA_KV
    c = scale * LOG2E

    def body(q_ref, km_ref, kp_ref, kc_ref, vm_ref, vp_ref, vc_ref, sink_ref, o_ref, lse_ref):
        qb = pl.program_id(1)
        q, kc, vc, s, live = _swa_parts(qb, q_ref, km_ref, kp_ref, kc_ref, vm_ref, vp_ref, vc_ref, c)
        sink = jnp.concatenate([jnp.broadcast_to(sink_ref[:, i * LANES:i * LANES + 1], (B, 1)) for i in range(G)], axis=0)
        m = jnp.maximum(jnp.max(s, axis=1, keepdims=True), sink)
        p = jnp.exp2(s - m)
        l = jnp.sum(p, axis=1, keepdims=True) + jnp.exp2(sink - m)
        o = lax.dot_general(p.astype(BF16), vc, NN, preferred_element_type=F32) / l
        lse = m + jnp.log(l) * LOG2E
        for i in range(G):
            o_ref[:, i * LANES:(i + 1) * LANES] = o[i * B:(i + 1) * B]
            lse_ref[:, i * LANES:(i + 1) * LANES] = jnp.broadcast_to(lse[i * B:(i + 1) * B], (B, LANES))

    qrow, keys, vals = _swa_specs(nqk)
    shape = jax.ShapeDtypeStruct((L, HEADS * LANES), F32)
    return pl.pallas_call(
        body, name=name, grid=(SWA_KV, L // B),
        in_specs=[qrow] + keys + vals + [pl.BlockSpec((1, G * LANES), lambda hk, qb: (0, hk))],
        out_specs=[qrow, qrow], out_shape=[shape, shape],
        compiler_params=_params("parallel", "parallel"),
    )(qk, qk, qk, qk, proj, proj, proj, sink2)


def _swa_attn_bwd(qk, proj, do, lse2, delta, scale, name):
    L = qk.shape[0]
    G, B = SWA_G, BLOCK
    nqk = HEADS + SWA_KV
    c = scale * LOG2E

    def body(q_ref, km_ref, kp_ref, kc_ref, vm_ref, vp_ref, vc_ref, do_ref, lse_ref, dl_ref, dq_ref, dk_ref, dv_ref):
        qb = pl.program_id(1)

        @pl.when(qb == 0)
        def _():
            dk_ref[...] = jnp.zeros_like(dk_ref)
            dv_ref[...] = jnp.zeros_like(dv_ref)

        q, kc, vc, s, live = _swa_parts(qb, q_ref, km_ref, kp_ref, kc_ref, vm_ref, vp_ref, vc_ref, c)
        dob = jnp.concatenate([_bf(do_ref[:, i * LANES:(i + 1) * LANES]) for i in range(G)], axis=0)
        p = jnp.where(live, jnp.exp2(s - _stack_col(lse_ref)), 0.0)
        dp = lax.dot_general(dob, vc, NT, preferred_element_type=F32)
        dsb = (p * (dp - _stack_col(dl_ref))).astype(BF16)
        dq = lax.dot_general(dsb, kc, NN, preferred_element_type=F32) * scale
        for i in range(G):
            dq_ref[:, i * LANES:(i + 1) * LANES] = dq[i * B:(i + 1) * B]
        dkc = lax.dot_general(dsb, q, TN, preferred_element_type=F32) * scale
        dvc = lax.dot_general(p.astype(BF16), dob, TN, preferred_element_type=F32)
        starts = (0, pl.multiple_of(jnp.maximum(qb - 1, 0) * B, B), pl.multiple_of(qb * B, B))
        for n, st in enumerate(starts):
            dk_ref[pl.ds(st, B), :] += dkc[n * B:(n + 1) * B]
            dv_ref[pl.ds(st, B), :] += dvc[n * B:(n + 1) * B]

    qrow, keys, vals = _swa_specs(nqk)
    res = pl.BlockSpec((L, LANES), lambda hk, qb: (0, hk))
    return pl.pallas_call(
        body, name=name, grid=(SWA_KV, L // B),
        in_specs=[qrow] + keys + vals + [qrow, qrow, qrow],
        out_specs=[qrow, res, res],
        out_shape=[jax.ShapeDtypeStruct((L, HEADS * LANES), F32), jax.ShapeDtypeStruct((L, SWA_KV * LANES), F32),
                   jax.ShapeDtypeStruct((L, SWA_KV * LANES), F32)],
        compiler_params=_params("parallel", "arbitrary"),
    )(qk, qk, qk, qk, proj, proj, proj, do, lse2, delta)


def _delta(do, o, name):
    L, HW = do.shape
    tr = BLOCK

    def body(do_ref, o_ref, d_ref):
        for h in range(HW // LANES):
            cols = slice(h * LANES, (h + 1) * LANES)
            d = jnp.sum(do_ref[:, cols].astype(F32) * o_ref[:, cols], axis=1, keepdims=True)
            d_ref[:, cols] = jnp.broadcast_to(d, (tr, LANES))

    spec = pl.BlockSpec((tr, HW), lambda i: (i, 0))
    return pl.pallas_call(body, name=name, grid=(L // tr,), in_specs=[spec, spec], out_specs=spec,
                          out_shape=jax.ShapeDtypeStruct((L, HW), F32), compiler_params=_params("parallel"))(do, o)


def _sink_grad(lse2, delta, sink2, name):
    L, HW = lse2.shape
    tr = 128

    def body(lse_ref, dl_ref, s_ref, o_ref):
        @pl.when(pl.program_id(0) == 0)
        def _():
            o_ref[...] = jnp.zeros_like(o_ref)

        o_ref[...] -= jnp.sum(jnp.exp2(s_ref[...] - lse_ref[...]) * dl_ref[...], axis=0, keepdims=True)

    row = pl.BlockSpec((tr, HW), lambda i: (i, 0))
    vec = pl.BlockSpec((1, HW), lambda i: (0, 0))
    return pl.pallas_call(body, name=name, grid=(L // tr,), in_specs=[row, row, vec], out_specs=vec,
                          out_shape=jax.ShapeDtypeStruct((1, HW), F32), compiler_params=_params("arbitrary"))(
        lse2, delta, sink2)


def _tri(lower):
    r = lax.broadcasted_iota(jnp.int32, (BLOCK, BLOCK), 0)
    c = lax.broadcasted_iota(jnp.int32, (BLOCK, BLOCK), 1)
    return jnp.where((c <= r) if lower else (c >= r), 1.0, 0.0).astype(F32)


def _gate_cumsum(proj, fg_tile, b_pad, name):
    L = proj.shape[0]

    def body(fg_ref, b_ref, c_ref, carry):
        @pl.when(pl.program_id(0) == 0)
        def _():
            carry[...] = jnp.zeros_like(carry)

        x = fg_ref[...] + b_ref[...]
        lf = jnp.minimum(x, 0.0) - jnp.log(1.0 + jnp.exp(-jnp.abs(x)))
        c = jnp.dot(_tri(True), lf, precision=lax.Precision.HIGHEST, preferred_element_type=F32) + carry[...]
        c_ref[...] = c
        carry[...] = c[BLOCK - 1:BLOCK, :]

    return pl.pallas_call(
        body, name=name, grid=(L // BLOCK,),
        in_specs=[pl.BlockSpec((BLOCK, LANES), lambda i: (i, fg_tile)), pl.BlockSpec((1, LANES), lambda i: (0, 0))],
        out_specs=pl.BlockSpec((BLOCK, LANES), lambda i: (i, 0)),
        out_shape=jax.ShapeDtypeStruct((L, LANES), F32),
        scratch_shapes=[pltpu.VMEM((1, LANES), F32)],
        compiler_params=_params("arbitrary"),
    )(proj, b_pad)


def _gate_cumsum_bwd(dc, proj, fg_tile, b_pad, name):
    L = proj.shape[0]
    nb = L // BLOCK

    def body(dc_ref, fg_ref, b_ref, dfg_ref, db_ref, carry):
        @pl.when(pl.program_id(0) == 0)
        def _():
            carry[...] = jnp.zeros_like(carry)
            db_ref[...] = jnp.zeros_like(db_ref)

        dlf = jnp.dot(_tri(False), dc_ref[...], precision=lax.Precision.HIGHEST,
                      preferred_element_type=F32) + carry[...]
        carry[...] = dlf[0:1, :]
        x = fg_ref[...] + b_ref[...]
        lanes = lax.broadcasted_iota(jnp.int32, (BLOCK, LANES), 1)
        rows = (nb - 1 - pl.program_id(0)) * BLOCK + lax.broadcasted_iota(jnp.int32, (BLOCK, LANES), 0)
        dfg = jnp.where((lanes < HEADS) & (rows >= PAD), dlf / (1.0 + jnp.exp(x)), 0.0)
        dfg_ref[...] = jnp.concatenate([dfg, jnp.zeros_like(dfg)], axis=1)
        db_ref[...] += jnp.sum(dfg, axis=0, keepdims=True)

    dfg, db = pl.pallas_call(
        body, name=name, grid=(nb,),
        in_specs=[pl.BlockSpec((BLOCK, LANES), lambda i: (nb - 1 - i, 0)),
                  pl.BlockSpec((BLOCK, LANES), lambda i: (nb - 1 - i, fg_tile)),
                  pl.BlockSpec((1, LANES), lambda i: (0, 0))],
        out_specs=[pl.BlockSpec((BLOCK, 2 * LANES), lambda i: (nb - 1 - i, 0)),
                   pl.BlockSpec((1, LANES), lambda i: (0, 0))],
        out_shape=[jax.ShapeDtypeStruct((L, 2 * LANES), F32), jax.ShapeDtypeStruct((1, LANES), F32)],
        scratch_shapes=[pltpu.VMEM((1, LANES), F32)],
        compiler_params=_params("arbitrary"),
    )(dc, proj, b_pad)
    return dfg, db[0]


def _rope_tables(L, dim, theta, lane0):
    half = dim // 2
    pos = (jnp.arange(L) - PAD).astype(F32)
    inv = theta ** (-jnp.arange(0, dim, 2, dtype=F32) / dim)
    ang = pos[:, None] * inv[None, :]
    cos, sin = jnp.cos(ang), jnp.sin(ang)
    C = jnp.ones((L, LANES), F32).at[:, lane0:lane0 + half].set(cos).at[:, lane0 + half:lane0 + dim].set(cos)
    S1 = jnp.zeros((L, LANES), F32).at[:, lane0:lane0 + half].set(-sin)
    S2 = jnp.zeros((L, LANES), F32).at[:, lane0 + half:lane0 + dim].set(sin)
    return C, S1, S2


def _rot(x, C, S1, S2, R):
    return x * C + pltpu.roll(x, LANES - R, 1) * S1 + pltpu.roll(x, R, 1) * S2


def _rot_t(dy, C, S1, S2, R):
    return dy * C + pltpu.roll(dy * S1, R, 1) + pltpu.roll(dy * S2, LANES - R, 1)


def _rope(x, nt, tabs, R, name, transpose=False, shared=None, shared_tile=0, out_dtype=F32):
    L = x.shape[0]
    T = BLOCK
    fn = _rot_t if transpose else _rot

    def body(*refs):
        if shared is None:
            x_ref, c_ref, s1_ref, s2_ref, o_ref = refs
        else:
            x_ref, sh_ref, c_ref, s1_ref, s2_ref, o_ref = refs
            rs = fn(sh_ref[...], c_ref[...], s1_ref[...], s2_ref[...], R)
        for h in range(nt):
            cols = slice(h * LANES, (h + 1) * LANES)
            if shared is None:
                o_ref[:, cols] = fn(x_ref[:, cols], c_ref[...], s1_ref[...], s2_ref[...], R).astype(out_dtype)
            else:
                o_ref[:, cols] = (x_ref[:, cols] + rs).astype(out_dtype)

    wide = pl.BlockSpec((T, nt * LANES), lambda i: (i, 0))
    tab = pl.BlockSpec((T, LANES), lambda i: (i, 0))
    ins, specs = [x], [wide]
    if shared is not None:
        ins.append(shared)
        specs.append(pl.BlockSpec((T, LANES), lambda i: (i, shared_tile)))
    return pl.pallas_call(body, name=name, grid=(L // T,), in_specs=specs + [tab, tab, tab], out_specs=wide,
                          out_shape=jax.ShapeDtypeStruct((L, nt * LANES), out_dtype),
                          compiler_params=_params("parallel"))(*ins, *tabs)


def _rope_shared_bwd(dk, nt, tabs, R, name):
    L = dk.shape[0]
    tr = 128

    def body(dk_ref, c_ref, s1_ref, s2_ref, o_ref):
        acc = dk_ref[:, 0:LANES]
        for h in range(1, nt):
            acc = acc + dk_ref[:, h * LANES:(h + 1) * LANES]
        o_ref[...] = _rot_t(acc, c_ref[...], s1_ref[...], s2_ref[...], R)

    tab = pl.BlockSpec((tr, LANES), lambda i: (i, 0))
    return pl.pallas_call(body, name=name, grid=(L // tr,),
                          in_specs=[pl.BlockSpec((tr, nt * LANES), lambda i: (i, 0)), tab, tab, tab], out_specs=tab,
                          out_shape=jax.ShapeDtypeStruct((L, LANES), F32), compiler_params=_params("parallel"))(
        dk, *tabs)


def _rms_fwd(pa, gq, gkv, name):
    L = pa.shape[0]
    tr = 128
    Q, KV = MLA_Q_LORA, MLA_KV_LORA

    def body(pa_ref, gq_ref, gkv_ref, q_ref, kv_ref):
        for lo, n, g_ref, o_ref in ((0, Q, gq_ref, q_ref), (Q, KV, gkv_ref, kv_ref)):
            x = pa_ref[:, lo:lo + n]
            r = lax.rsqrt(jnp.mean(x * x, axis=1, keepdims=True) + RMS_EPS)
            o_ref[...] = (x * r * g_ref[...]).astype(BF16)

    return pl.pallas_call(
        body, name=name, grid=(L // tr,),
        in_specs=[pl.BlockSpec((tr, pa.shape[1]), lambda i: (i, 0)), pl.BlockSpec((1, Q), lambda i: (0, 0)),
                  pl.BlockSpec((1, KV), lambda i: (0, 0))],
        out_specs=[pl.BlockSpec((tr, Q), lambda i: (i, 0)), pl.BlockSpec((tr, KV), lambda i: (i, 0))],
        out_shape=[jax.ShapeDtypeStruct((L, Q), BF16), jax.ShapeDtypeStruct((L, KV), BF16)],
        compiler_params=_params("parallel"),
    )(pa, gq.reshape(1, Q), gkv.reshape(1, KV))


def _rms_bwd(pa, dq, dkv, dkr, gq, gkv, name):
    L, W = pa.shape
    tr = 128
    Q, KV = MLA_Q_LORA, MLA_KV_LORA

    def body(pa_ref, dq_ref, dkv_ref, dkr_ref, gq_ref, gkv_ref, dpa_ref, dgq_ref, dgkv_ref):
        @pl.when(pl.program_id(0) == 0)
        def _():
            dgq_ref[...] = jnp.zeros_like(dgq_ref)
            dgkv_ref[...] = jnp.zeros_like(dgkv_ref)

        for lo, n, g_ref, dy_ref, dg_ref in ((0, Q, gq_ref, dq_ref, dgq_ref), (Q, KV, gkv_ref, dkv_ref, dgkv_ref)):
            x = pa_ref[:, lo:lo + n]
            r = lax.rsqrt(jnp.mean(x * x, axis=1, keepdims=True) + RMS_EPS)
            xh = x * r
            dy = dy_ref[...]
            dxh = dy * g_ref[...]
            dpa_ref[:, lo:lo + n] = (r * (dxh - xh * jnp.mean(dxh * xh, axis=1, keepdims=True))).astype(BF16)
            dg_ref[...] += jnp.sum(dy * xh, axis=0, keepdims=True)
        dpa_ref[:, Q + KV:W] = dkr_ref[...].astype(BF16)

    vq = pl.BlockSpec((1, Q), lambda i: (0, 0))
    vkv = pl.BlockSpec((1, KV), lambda i: (0, 0))
    dpa, dgq, dgkv = pl.pallas_call(
        body, name=name, grid=(L // tr,),
        in_specs=[pl.BlockSpec((tr, W), lambda i: (i, 0)), pl.BlockSpec((tr, Q), lambda i: (i, 0)),
                  pl.BlockSpec((tr, KV), lambda i: (i, 0)), pl.BlockSpec((tr, LANES), lambda i: (i, 0)), vq, vkv],
        out_specs=[pl.BlockSpec((tr, W), lambda i: (i, 0)), vq, vkv],
        out_shape=[jax.ShapeDtypeStruct((L, W), BF16), jax.ShapeDtypeStruct((1, Q), F32),
                   jax.ShapeDtypeStruct((1, KV), F32)],
        compiler_params=_params("arbitrary"),
    )(pa, dq, dkv, dkr, gq.reshape(1, Q), gkv.reshape(1, KV))
    return dpa, dgq[0], dgkv[0]


def _loss_head(h, target, name):
    L = h.shape[0]
    tr = BLOCK
    inv = 1.0 / D_MODEL

    def body(h_ref, t_ref, loss_ref, dh_ref):
        i = pl.program_id(0)

        @pl.when(i == 0)
        def _():
            loss_ref[...] = jnp.zeros_like(loss_ref)
            dh_ref[...] = jnp.zeros_like(dh_ref)

        @pl.when(i > 0)
        def _():
            e = h_ref[...] - t_ref[...]
            dh_ref[...] = e * inv
            loss_ref[...] += jnp.sum((e * e).reshape(tr // 8, 8, D_MODEL), axis=0) * (0.5 * inv)

    row = pl.BlockSpec((tr, D_MODEL), lambda i: (i, 0))
    loss, dh = pl.pallas_call(
        body, name=name, grid=(L // tr,),
        in_specs=[row, pl.BlockSpec((tr, D_MODEL), lambda i: (jnp.maximum(i - 1, 0), 0))],
        out_specs=[pl.BlockSpec((8, D_MODEL), lambda i: (0, 0)), row],
        out_shape=[jax.ShapeDtypeStruct((8, D_MODEL), F32), jax.ShapeDtypeStruct((L, D_MODEL), F32)],
        compiler_params=_params("arbitrary"),
    )(h, target)
    return loss, dh


def _pad_heads_cols(w, nh, d, dp=LANES):
    K = w.shape[0]
    return jnp.pad(w.reshape(K, nh, d), ((0, 0), (0, 0), (0, dp - d))).reshape(K, nh * dp)


def _unpad_heads_cols(w, nh, d, dp=LANES):
    K = w.shape[0]
    return w.reshape(K, nh, dp)[:, :, :d].reshape(K, nh * d)


def _pad_heads_rows(w, nh, d):
    N = w.shape[1]
    return jnp.pad(w.reshape(nh, d, N), ((0, 0), (0, LANES - d), (0, 0))).reshape(nh * LANES, N)


def _unpad_heads_rows(w, nh, d):
    N = w.shape[1]
    return w.reshape(nh, LANES, N)[:, :d, :].reshape(nh * d, N)


def _fox_fwd(h, w_in, b_f, w_o, tag):
    L = h.shape[0]
    hd = HEADS * HEAD_DIM
    W = jnp.concatenate([_pad_heads_cols(w_in[:, i * hd:(i + 1) * hd], HEADS, HEAD_DIM) for i in range(3)]
                        + [jnp.pad(w_in[:, 3 * hd:], ((0, 0), (0, 2 * LANES - HEADS)))], axis=1)
    Wo = _pad_heads_rows(w_o, HEADS, HEAD_DIM)
    b_pad = jnp.pad(b_f, (0, LANES - HEADS)).reshape(1, LANES)
    proj = _mm(h, W, "nn", tag + "_proj")
    c = _gate_cumsum(proj, 3 * HEADS, b_pad, tag + "_cumsum")
    dead = (jnp.arange(L) < PAD)[:, None]
    cr2 = jnp.where(dead, -NEG, c[:, :HEADS] * LOG2E).T.reshape(HEADS, 1, L)
    cfg = _Dense(L, True)
    scale = HEAD_DIM ** -0.5
    o, lse2 = _flash_fwd(cfg, proj, 0, proj, HEADS, proj, 2 * HEADS, HEADS, DENSE_HS_FWD, scale, tag + "_attn", cr2=cr2)
    mix = _mm(o, Wo, "nn", tag + "_out")
    return mix, (h, W, Wo, b_pad, proj, cr2, o, lse2)


def _fox_bwd(dmix, res, tag):
    h, W, Wo, b_pad, proj, cr2, o, lse2 = res
    L = h.shape[0]
    cfg = _Dense(L, True)
    scale = HEAD_DIM ** -0.5
    dWo = _mm(o, dmix, "tn", tag + "_dwo")
    do = _mm(dmix, Wo, "nt", tag + "_do", out_dtype=BF16)
    qkv = (proj, 0, proj, HEADS, proj, 2 * HEADS)
    delta = _delta(do, o, tag + "_delta")
    dk, dv, dq = _flash_bwd(cfg, *qkv, do, lse2, delta, HEADS, DENSE_HS, scale, tag + "_bwd", cr2=cr2, lane_sums=True)
    dc = jnp.pad((dq[:, LANES - 1::LANES] - dk[:, LANES - 1::LANES]) * (1.0 / scale), ((0, 0), (0, LANES - HEADS)))
    dfg, db = _gate_cumsum_bwd(dc, proj, 3 * HEADS, b_pad, tag + "_cumsum_bwd")
    dproj = jnp.concatenate([dq.astype(BF16), dk.astype(BF16), dv.astype(BF16), dfg.astype(BF16)], axis=1)
    dW = _mm(h, dproj, "tn", tag + "_dw")
    dh = _mm(dproj, W, "nt", tag + "_dh")
    hp = HEADS * LANES
    dw_in = jnp.concatenate([_unpad_heads_cols(dW[:, i * hp:(i + 1) * hp], HEADS, HEAD_DIM) for i in range(3)]
                            + [dW[:, 3 * hp:3 * hp + HEADS]], axis=1)
    return dh, dict(w_in=dw_in, b_f=db[:HEADS], w_o=_unpad_heads_rows(dWo, HEADS, HEAD_DIM))


def _swa_fwd(h, w_in, sinks, w_o, tag):
    L = h.shape[0]
    qd, kd = HEADS * HEAD_DIM, SWA_KV * HEAD_DIM
    W = jnp.concatenate([_pad_heads_cols(w_in[:, :qd], HEADS, HEAD_DIM),
                         _pad_heads_cols(w_in[:, qd:qd + kd], SWA_KV, HEAD_DIM),
                         _pad_heads_cols(w_in[:, qd + kd:], SWA_KV, HEAD_DIM)], axis=1)
    Wo = _pad_heads_rows(w_o, HEADS, HEAD_DIM)
    sink2 = jnp.repeat(sinks * LOG2E, LANES).reshape(1, HEADS * LANES)
    tabs = _rope_tables(L, ROPE_DIM, ROPE_THETA, 0)
    proj = _mm(h, W, "nn", tag + "_proj")
    nqk = HEADS + SWA_KV
    qk = _rope(proj, nqk, tabs, ROPE_DIM // 2, tag + "_rope")
    scale = HEAD_DIM ** -0.5
    o, lse2 = _swa_attn_fwd(qk, proj, sink2, scale, tag + "_attn")
    mix = _mm(o, Wo, "nn", tag + "_out")
    return mix, (h, W, Wo, sink2, tabs, proj, qk, o, lse2)


def _swa_bwd(dmix, res, tag):
    h, W, Wo, sink2, tabs, proj, qk, o, lse2 = res
    L = h.shape[0]
    nqk = HEADS + SWA_KV
    scale = HEAD_DIM ** -0.5
    dWo = _mm(o, dmix, "tn", tag + "_dwo")
    do = _mm(dmix, Wo, "nt", tag + "_do", out_dtype=BF16)
    delta = _delta(do, o, tag + "_delta")
    dsink = _sink_grad(lse2, delta, sink2, tag + "_dsink")[0, ::LANES]
    dq, dk, dv = _swa_attn_bwd(qk, proj, do, lse2, delta, scale, tag + "_bwd")
    dqk = _rope(jnp.concatenate([dq, dk], axis=1), nqk, tabs, ROPE_DIM // 2, tag + "_rope_bwd", transpose=True,
                out_dtype=BF16)
    dproj = jnp.concatenate([dqk, dv.astype(BF16)], axis=1)
    dW = _mm(h, dproj, "tn", tag + "_dw")
    dh = _mm(dproj, W, "nt", tag + "_dh")
    hp = HEADS * LANES
    dw_in = jnp.concatenate([_unpad_heads_cols(dW[:, :hp], HEADS, HEAD_DIM),
                             _unpad_heads_cols(dW[:, hp:hp + SWA_KV * LANES], SWA_KV, HEAD_DIM),
                             _unpad_heads_cols(dW[:, hp + SWA_KV * LANES:], SWA_KV, HEAD_DIM)], axis=1)
    return dh, dict(w_in=dw_in, sinks=dsink, w_o=_unpad_heads_rows(dWo, HEADS, HEAD_DIM))


def _mla_fwd(h, w_a, g_q, g_kv, w_uq, w_ukv, w_o, tag):
    L = h.shape[0]
    Q, KV = MLA_Q_LORA, MLA_KV_LORA
    dqk = MLA_NOPE + MLA_ROPE
    kr_w = jnp.pad(w_a[:, Q + KV:], ((0, 0), (MLA_NOPE, LANES - dqk)))
    Wa = jnp.concatenate([w_a[:, :Q + KV], kr_w], axis=1)
    Wuq = _pad_heads_cols(w_uq, HEADS, dqk)
    ukv = w_ukv.reshape(KV, HEADS, MLA_NOPE + HEAD_DIM)
    Wukv = jnp.concatenate([_pad_heads_cols(ukv[:, :, :MLA_NOPE].reshape(KV, -1), HEADS, MLA_NOPE),
                            _pad_heads_cols(ukv[:, :, MLA_NOPE:].reshape(KV, -1), HEADS, HEAD_DIM)], axis=1)
    Wo = _pad_heads_rows(w_o, HEADS, HEAD_DIM)
    tabs = _rope_tables(L, MLA_ROPE, MLA_ROPE_THETA, MLA_NOPE)
    R = MLA_ROPE // 2
    pa = _mm(h, Wa, "nn", tag + "_proj")
    cqn, ckvn = _rms_fwd(pa, g_q, g_kv, tag + "_rms")
    q0 = _mm(cqn, Wuq, "nn", tag + "_uq")
    qr = _rope(q0, HEADS, tabs, R, tag + "_rope_q")
    kv0 = _mm(ckvn, Wukv, "nn", tag + "_ukv")
    kk = _rope(kv0, HEADS, tabs, R, tag + "_rope_k", shared=pa, shared_tile=(Q + KV) // LANES)
    cfg = _Dense(L, False)
    scale = dqk ** -0.5
    o, lse2 = _flash_fwd(cfg, qr, 0, kk, 0, kv0, HEADS, HEADS, DENSE_HS_FWD, scale, tag + "_attn")
    mix = _mm(o, Wo, "nn", tag + "_out")
    return mix, (h, Wa, Wuq, Wukv, Wo, g_q, g_kv, tabs, pa, cqn, ckvn, qr, kk, kv0, o, lse2)


def _mla_bwd(dmix, res, tag):
    h, Wa, Wuq, Wukv, Wo, g_q, g_kv, tabs, pa, cqn, ckvn, qr, kk, kv0, o, lse2 = res
    L = h.shape[0]
    Q, KV = MLA_Q_LORA, MLA_KV_LORA
    dqk = MLA_NOPE + MLA_ROPE
    R = MLA_ROPE // 2
    cfg = _Dense(L, False)
    scale = dqk ** -0.5
    dWo = _mm(o, dmix, "tn", tag + "_dwo")
    do = _mm(dmix, Wo, "nt", tag + "_do", out_dtype=BF16)
    delta = _delta(do, o, tag + "_delta")
    dk, dv, dqr = _flash_bwd(cfg, qr, 0, kk, 0, kv0, HEADS, do, lse2, delta, HEADS, DENSE_HS, scale, tag + "_bwd")
    dq0 = _rope(dqr, HEADS, tabs, R, tag + "_rope_q_bwd", transpose=True, out_dtype=BF16)
    dWuq = _mm(cqn, dq0, "tn", tag + "_dwuq")
    dcqn = _mm(dq0, Wuq, "nt", tag + "_dcq")
    dkv = jnp.concatenate([dk, dv], axis=1).astype(BF16)
    dWukv = _mm(ckvn, dkv, "tn", tag + "_dwukv")
    dckvn = _mm(dkv, Wukv, "nt", tag + "_dckv")
    dkr = _rope_shared_bwd(dk, HEADS, tabs, R, tag + "_rope_k_bwd")
    dpa, dgq, dgkv = _rms_bwd(pa, dcqn, dckvn, dkr, g_q, g_kv, tag + "_rms_bwd")
    dWa = _mm(h, dpa, "tn", tag + "_dw")
    dh = _mm(dpa, Wa, "nt", tag + "_dh")
    hp = HEADS * LANES
    dw_a = jnp.concatenate([dWa[:, :Q + KV], dWa[:, Q + KV + MLA_NOPE:Q + KV + dqk]], axis=1)
    dk_n = dWukv[:, :hp].reshape(KV, HEADS, LANES)[:, :, :MLA_NOPE]
    dv_n = dWukv[:, hp:].reshape(KV, HEADS, LANES)[:, :, :HEAD_DIM]
    dw_ukv = jnp.concatenate([dk_n, dv_n], axis=2).reshape(KV, HEADS * (MLA_NOPE + HEAD_DIM))
    return dh, dict(w_a=dw_a, g_q=dgq, g_kv=dgkv, w_uq=_unpad_heads_cols(dWuq, HEADS, dqk), w_ukv=dw_ukv,
                    w_o=_unpad_heads_rows(dWo, HEADS, HEAD_DIM))


MATMUL_WEIGHTS = ("fox_w_in", "fox_w_o", "swa_w_in", "swa_w_o", "mla_w_a", "mla_w_uq", "mla_w_ukv", "mla_w_o",
                  "ffn_w_in", "ffn_w_out")


def _local_step(x, target, w):
    w = {k: (_bf(v) if k in MATMUL_WEIGHTS else v) for k, v in w.items()}
    h = jnp.concatenate([jnp.zeros((PAD, D_MODEL), F32), w["meta_tokens"], x], axis=0)
    hb = h.astype(BF16)
    saved = []
    for i in range(DEPTH):
        kind, j = i % 3, i // 3
        tag = "l%d" % i
        if kind == 0:
            mix, mres = _fox_fwd(hb, w["fox_w_in"][j], w["fox_b_f"][j], w["fox_w_o"][j], tag + "_fox")
        elif kind == 1:
            mix, mres = _swa_fwd(hb, w["swa_w_in"][j], w["swa_sinks"][j], w["swa_w_o"][j], tag + "_swa")
        else:
            mix, mres = _mla_fwd(hb, w["mla_w_a"][j], w["mla_g_q"][j], w["mla_g_kv"][j], w["mla_w_uq"][j],
                                 w["mla_w_ukv"][j], w["mla_w_o"][j], tag + "_mla")
        h1, h1b, xh1, rs1 = _ln_fwd(h, mix, w["ln1_g"][i], w["ln1_b"][i], tag + "_ln1")
        u = _mm(h1b, w["ffn_w_in"][i], "nn", tag + "_ffn_in")
        a = _conv_glu_fwd(u, w["ffn_conv_w"][i], w["ffn_conv_b"][i], tag + "_conv")
        ffn = _mm(a, w["ffn_w_out"][i], "nn", tag + "_ffn_out")
        h2, h2b, xh2, rs2 = _ln_fwd(h1, ffn, w["ln2_g"][i], w["ln2_b"][i], tag + "_ln2")
        saved.append((mres, xh1, rs1, h1b, u, a, xh2, rs2))
        h, hb = h2, h2b
    loss, dh = _loss_head(h, target, "loss_head")

    g = {k: [None] * v.shape[0] for k, v in w.items() if k != "meta_tokens"}
    ga = None
    for i in reversed(range(DEPTH)):
        kind, j = i % 3, i // 3
        tag = "l%d" % i
        mres, xh1, rs1, h1b, u, a, xh2, rs2 = saved[i]
        dz2, dz2b, g["ln2_g"][i], g["ln2_b"][i] = _ln_bwd(ga, dh, xh2, rs2, w["ln2_g"][i], tag + "_ln2_bwd")
        g["ffn_w_out"][i] = _mm(a, dz2b, "tn", tag + "_dw_out")
        da = _mm(dz2b, w["ffn_w_out"][i], "nt", tag + "_da")
        du, g["ffn_conv_w"][i], g["ffn_conv_b"][i] = _conv_glu_bwd(da, u, w["ffn_conv_w"][i], w["ffn_conv_b"][i],
                                                                   tag + "_conv_bwd")
        g["ffn_w_in"][i] = _mm(h1b, du, "tn", tag + "_dw_in")
        dh1 = _mm(du, w["ffn_w_in"][i], "nt", tag + "_dh1")
        dz1, dz1b, g["ln1_g"][i], g["ln1_b"][i] = _ln_bwd(dz2, dh1, xh1, rs1, w["ln1_g"][i], tag + "_ln1_bwd")
        if kind == 0:
            dh, mg = _fox_bwd(dz1b, mres, tag + "_fox")
            pre = "fox_"
        elif kind == 1:
            dh, mg = _swa_bwd(dz1b, mres, tag + "_swa")
            pre = "swa_"
        else:
            dh, mg = _mla_bwd(dz1b, mres, tag + "_mla")
            pre = "mla_"
        for k, v in mg.items():
            g[pre + k][j] = v
        ga = dz1
    dh0 = _axpy(ga, dh, "dh0")
    grads = {k: jnp.stack(v) for k, v in g.items()}
    grads["meta_tokens"] = dh0[PAD:BLOCK]
    return loss, dh0, grads


SHARDED = (("meta_tokens", 1), ("fox_w_in", 2), ("fox_w_o", 1), ("swa_w_in", 2), ("swa_w_o", 1), ("mla_w_a", 1),
           ("mla_g_q", 1), ("mla_g_kv", 1), ("mla_w_uq", 2), ("mla_w_ukv", 2), ("mla_w_o", 1), ("ffn_w_in", 2),
           ("ffn_conv_w", 2), ("ffn_w_out", 1))
REPLICATED = ("ln1_g", "ln1_b", "ln2_g", "ln2_b", "fox_b_f", "swa_sinks", "ffn_conv_b")
WEIGHTS = ("meta_tokens", "ln1_g", "ln1_b", "ln2_g", "ln2_b", "fox_w_in", "fox_b_f", "fox_w_o", "swa_w_in",
           "swa_sinks", "swa_w_o", "mla_w_a", "mla_g_q", "mla_g_kv", "mla_w_uq", "mla_w_ukv", "mla_w_o", "ffn_w_in",
           "ffn_conv_w", "ffn_conv_b", "ffn_w_out")


def _rows(n):
    return -(-n // ROW)


def _pack(arrs, multiple):
    parts = []
    for a in arrs:
        n = math.prod(a.shape)
        parts.append(jnp.pad(a.reshape(-1), (0, _rows(n) * ROW - n)).reshape(-1, ROW))
    total = sum(p.shape[0] for p in parts)
    pad = -total % multiple
    if pad:
        parts.append(jnp.zeros((pad, ROW), parts[0].dtype))
    return jnp.concatenate(parts, axis=0)


def _unpack(flat, shapes):
    out, r = [], 0
    for s in shapes:
        n = math.prod(s)
        out.append(flat[r:r + _rows(n)].reshape(-1)[:n].reshape(s))
        r += _rows(n)
    return out


def _pack_bf16(w, names):
    return _pack([_bf(w[n]) if n in MATMUL_WEIGHTS else lax.bitcast_convert_type(w[n], BF16) for n in names], 2 * ROW)


def _unpack_bf16(flat, names, shapes):
    sh = [s if n in MATMUL_WEIGHTS else s + (2,) for n, s in zip(names, shapes)]
    parts = _unpack(flat, sh)
    return [p if n in MATMUL_WEIGHTS else lax.bitcast_convert_type(p, F32) for n, p in zip(names, parts)]


HBM_SPEC = pl.BlockSpec(memory_space=pltpu.HBM)


def _place():
    x, y, c = lax.axis_index("x"), lax.axis_index("y"), lax.axis_index("c")
    chips = [(1 - x, y), (x, 1 - y), (1 - x, 1 - y)]
    return x, y, c, chips


def _gather_weights(shard):
    R = shard.shape[0]
    Rh = R // 2

    def body(s_ref, o_ref, send_sems, recv_sems):
        x, y, c, chips = _place()
        sib = (x, y, 1 - c)

        def half(k, hc):
            return o_ref.at[k, pl.ds(hc * Rh, Rh), :]

        def copy(j, src, dst, to):
            return pltpu.make_async_remote_copy(src_ref=src, dst_ref=dst, send_sem=send_sems.at[j],
                                                recv_sem=recv_sems.at[j], device_id=to, device_id_type=MESH)

        me = 2 * x + y
        first = [copy(j, s_ref.at[pl.ds(c * Rh, Rh), :], half(me, c), (tx, ty, c)) for j, (tx, ty) in enumerate(chips)]
        for cp in first:
            cp.start()
        passed = []
        for j, (tx, ty) in enumerate(chips):
            k = 2 * tx + ty
            copy(j, half(k, c), half(k, c), (tx, ty, c)).wait_recv()
            fw = copy(3 + j, half(k, c), half(k, c), sib)
            fw.start()
            passed.append(fw)
        for j, (tx, ty) in enumerate(chips):
            k = 2 * tx + ty
            copy(3 + j, half(k, 1 - c), half(k, 1 - c), sib).wait_recv()
        for cp in first + passed:
            cp.wait_send()

    return pl.pallas_call(
        body, name="gather_weights", out_shape=jax.ShapeDtypeStruct((N_CHIPS, R, ROW), shard.dtype),
        in_specs=[HBM_SPEC], out_specs=HBM_SPEC,
        scratch_shapes=[pltpu.SemaphoreType.DMA((6,)), pltpu.SemaphoreType.DMA((6,))],
    )(shard)


def _swap_halves(G):
    R = G.shape[1]
    Rh = R // 2

    def body(g_ref, a_ref, send_sem, recv_sem):
        x, y, c, _ = _place()
        cp = pltpu.make_async_remote_copy(src_ref=g_ref.at[:, pl.ds((1 - c) * Rh, Rh), :], dst_ref=a_ref,
                                          send_sem=send_sem, recv_sem=recv_sem, device_id=(x, y, 1 - c),
                                          device_id_type=MESH)
        cp.start()
        cp.wait()

    return pl.pallas_call(
        body, name="reduce_swap_halves", out_shape=jax.ShapeDtypeStruct((N_CHIPS, Rh, ROW), G.dtype),
        in_specs=[HBM_SPEC], out_specs=HBM_SPEC,
        scratch_shapes=[pltpu.SemaphoreType.DMA, pltpu.SemaphoreType.DMA],
    )(G)


def _exchange_chips(P):
    def body(p_ref, b_ref, send_sems, recv_sems):
        x, y, c, chips = _place()
        me = 2 * x + y

        def copy(j, src, dst, to):
            return pltpu.make_async_remote_copy(src_ref=src, dst_ref=dst, send_sem=send_sems.at[j],
                                                recv_sem=recv_sems.at[j], device_id=to, device_id_type=MESH)

        sends = [copy(j, p_ref.at[2 * tx + ty], b_ref.at[me], (tx, ty, c)) for j, (tx, ty) in enumerate(chips)]
        for cp in sends:
            cp.start()
        for j, (tx, ty) in enumerate(chips):
            k = 2 * tx + ty
            copy(j, p_ref.at[k], b_ref.at[k], (tx, ty, c)).wait_recv()
        for cp in sends:
            cp.wait_send()

    return pl.pallas_call(
        body, name="reduce_exchange_chips", out_shape=jax.ShapeDtypeStruct(P.shape, P.dtype),
        in_specs=[HBM_SPEC], out_specs=HBM_SPEC,
        scratch_shapes=[pltpu.SemaphoreType.DMA((3,)), pltpu.SemaphoreType.DMA((3,))],
    )(P)


def _swap_reduced(Fh):
    def body(f_ref, o_ref, send_sem, recv_sem):
        x, y, c, _ = _place()
        cp = pltpu.make_async_remote_copy(src_ref=f_ref, dst_ref=o_ref, send_sem=send_sem, recv_sem=recv_sem,
                                          device_id=(x, y, 1 - c), device_id_type=MESH)
        cp.start()
        cp.wait()

    return pl.pallas_call(
        body, name="reduce_swap_reduced", out_shape=jax.ShapeDtypeStruct(Fh.shape, Fh.dtype),
        in_specs=[HBM_SPEC], out_specs=HBM_SPEC,
        scratch_shapes=[pltpu.SemaphoreType.DMA, pltpu.SemaphoreType.DMA],
    )(Fh)


def _gather_small(v):
    m_per = v.shape[0]

    def body(x_ref, out_ref, send_sems, recv_sems, local_sem):
        x, y, c, chips = _place()
        me, sibling = (x, y, c), (x, y, 1 - c)

        def rows(px, py, pc):
            return out_ref.at[pl.ds((4 * px + 2 * py + pc) * m_per, m_per), :]

        def copy(k, block, to, src=None):
            return pltpu.make_async_remote_copy(src_ref=rows(*block) if src is None else src, dst_ref=rows(*block),
                                                send_sem=send_sems.at[k], recv_sem=recv_sems.at[k], device_id=to,
                                                device_id_type=MESH)

        mine = pltpu.make_async_copy(x_ref, rows(*me), local_sem)
        mine.start()
        first = [copy(0, me, sibling, src=x_ref)]
        first += [copy(1 + j, me, (*chip, c), src=x_ref) for j, chip in enumerate(chips)]
        for cp in first:
            cp.start()
        passed = [copy(4 + j, (*chip, c), sibling) for j, chip in enumerate(chips)]
        for j, chip in enumerate(chips):
            copy(1 + j, (*chip, c), me).wait_recv()
            passed[j].start()
        copy(0, sibling, me).wait_recv()
        for j, chip in enumerate(chips):
            copy(4 + j, (*chip, 1 - c), me).wait_recv()
        for cp in first + passed:
            cp.wait_send()
        mine.wait()

    return pl.pallas_call(
        body, name="gather_small", out_shape=jax.ShapeDtypeStruct((N_DEV * m_per, ROW), v.dtype),
        in_specs=[pl.BlockSpec(memory_space=pltpu.VMEM)], out_specs=pl.BlockSpec(memory_space=pltpu.VMEM),
        scratch_shapes=[pltpu.SemaphoreType.DMA((7,)), pltpu.SemaphoreType.DMA((7,)), pltpu.SemaphoreType.DMA],
    )(v)


def _sum_slots(a, n, name):
    M = a.shape[0] // n
    tr = _pick(M, (512, 256, 128, 64, 40, 8))
    nb = M // tr

    def body(*refs):
        acc = refs[0][...].astype(F32)
        for r in refs[1:-1]:
            acc = acc + r[...].astype(F32)
        refs[-1][...] = acc

    specs = [pl.BlockSpec((tr, ROW), functools.partial(lambda i, k: (k * nb + i, 0), k=k)) for k in range(n)]
    return pl.pallas_call(body, name=name, grid=(nb,), in_specs=specs,
                          out_specs=pl.BlockSpec((tr, ROW), lambda i: (i, 0)),
                          out_shape=jax.ShapeDtypeStruct((M, ROW), F32), compiler_params=_params("parallel"))(*([a] * n))


def _add(a, b, name, out_dtype):
    M = a.shape[0]
    tr = _pick(M, (512, 256, 128, 64, 40, 8))

    def body(a_ref, b_ref, o_ref):
        o_ref[...] = (a_ref[...] + b_ref[...]).astype(out_dtype)

    row = pl.BlockSpec((tr, ROW), lambda i: (i, 0))
    return pl.pallas_call(body, name=name, grid=(M // tr,), in_specs=[row, row], out_specs=row,
                          out_shape=jax.ShapeDtypeStruct((M, ROW), out_dtype), compiler_params=_params("parallel"))(a, b)


def _adamw(g, w, m, v, name):
    shp = w.shape
    N = shp[-1]
    M = math.prod(shp[:-1])
    g, w, m, v = (a.reshape(M, N) for a in (g, w, m, v))
    tr = _pick(M, tuple(t for t in (512, 256, 128, 64, 40, 32, 16, 8) if t * N <= 256 * 1024))
    c1 = 1.0 - ADAM_B1 ** ADAM_STEP
    c2 = 1.0 - ADAM_B2 ** ADAM_STEP

    def body(g_ref, w_ref, m_ref, v_ref, d_ref, nm_ref, nv_ref):
        gg = g_ref[...]
        nm = ADAM_B1 * m_ref[...] + (1.0 - ADAM_B1) * gg
        nv = ADAM_B2 * v_ref[...] + (1.0 - ADAM_B2) * (gg * gg)
        nm_ref[...] = nm
        nv_ref[...] = nv
        d_ref[...] = -ADAM_LR * ((nm / c1) / (jnp.sqrt(nv / c2) + ADAM_EPS) + ADAM_WD * w_ref[...])

    row = pl.BlockSpec((tr, N), lambda i: (i, 0))
    shape = jax.ShapeDtypeStruct((M, N), F32)
    outs = pl.pallas_call(body, name=name, grid=(M // tr,), in_specs=[row] * 4, out_specs=[row] * 3,
                          out_shape=[shape] * 3, compiler_params=_params("parallel"))(g, w, m, v)
    return [o.reshape(shp) for o in outs]


def kernel(x, meta_tokens, ln1_g, ln1_b, ln2_g, ln2_b, fox_w_in, fox_b_f, fox_w_o, swa_w_in, swa_sinks, swa_w_o, mla_w_a, mla_g_q, mla_g_kv, mla_w_uq, mla_w_ukv, mla_w_o, ffn_w_in, ffn_conv_w, ffn_conv_b, ffn_w_out, loss_target, m_meta_tokens, m_ln1_g, m_ln1_b, m_ln2_g, m_ln2_b, m_fox_w_in, m_fox_b_f, m_fox_w_o, m_swa_w_in, m_swa_sinks, m_swa_w_o, m_mla_w_a, m_mla_g_q, m_mla_g_kv, m_mla_w_uq, m_mla_w_ukv, m_mla_w_o, m_ffn_w_in, m_ffn_conv_w, m_ffn_conv_b, m_ffn_w_out, v_meta_tokens, v_ln1_g, v_ln1_b, v_ln2_g, v_ln2_b, v_fox_w_in, v_fox_b_f, v_fox_w_o, v_swa_w_in, v_swa_sinks, v_swa_w_o, v_mla_w_a, v_mla_g_q, v_mla_g_kv, v_mla_w_uq, v_mla_w_ukv, v_mla_w_o, v_ffn_w_in, v_ffn_conv_w, v_ffn_conv_b, v_ffn_w_out):
    given = dict(locals())
    w = {n: given[n] for n in WEIGHTS}
    m = {n: given["m_" + n] for n in WEIGHTS}
    v = {n: given["v_" + n] for n in WEIGHTS}
    sh_names = [n for n, _ in SHARDED]
    sh_shapes = [w[n].shape for n in sh_names]

    me = 2 * lax.axis_index("x") + lax.axis_index("y")
    c = lax.axis_index("c")
    packed = _pack_bf16(w, sh_names)
    gathered = lax.dynamic_update_index_in_dim(_gather_weights(packed), packed, me, 0)
    full = dict(w)
    per_chip = [_unpack_bf16(gathered[k], sh_names, sh_shapes) for k in range(N_CHIPS)]
    for t, (n, ax) in enumerate(SHARDED):
        full[n] = jnp.concatenate([per_chip[k][t] for k in range(N_CHIPS)], axis=ax)

    loss_part, dh0, grads = _local_step(x[0], loss_target[0], full)
    loss = lax.psum(jnp.sum(loss_part), ("x", "y", "c"))
    grad_x = dh0[BLOCK:][None]

    split = {n: jnp.split(grads[n], N_CHIPS, axis=ax) for n, ax in SHARDED}
    G = jnp.stack([_pack([split[n][k] for n in sh_names], 2 * ROW) for k in range(N_CHIPS)])
    Rh = G.shape[1] // 2
    mine = lax.dynamic_slice_in_dim(G, c * Rh, Rh, axis=1)
    P = _add(mine.reshape(N_CHIPS * Rh, ROW), _swap_halves(G).reshape(N_CHIPS * Rh, ROW), "reduce_pair_sum", BF16)
    P = P.reshape(N_CHIPS, Rh, ROW)
    B = lax.dynamic_update_index_in_dim(_exchange_chips(P), lax.dynamic_index_in_dim(P, me, 0, keepdims=False), me, 0)
    Fh = _sum_slots(B.reshape(N_CHIPS * Rh, ROW), N_CHIPS, "reduce_chip_sum")
    other = _swap_reduced(Fh)
    Fg = jnp.concatenate([jnp.where(c == 0, Fh, other), jnp.where(c == 0, other, Fh)], axis=0)
    out = {}
    for n, g_n in zip(sh_names, _unpack(Fg, sh_shapes)):
        out["grad", n] = g_n
        out["delta", n], out["new_m", n], out["new_v", n] = _adamw(g_n, w[n], m[n], v[n], "adamw_" + n)

    rp_shapes = [w[n].shape for n in REPLICATED]
    small = _gather_small(_pack([grads[n] for n in REPLICATED], 8))
    g_r = _sum_slots(small, N_DEV, "reduce_small_sum")
    d_r, m_r, v_r = _adamw(g_r, _pack([w[n] for n in REPLICATED], 8), _pack([m[n] for n in REPLICATED], 8),
                           _pack([v[n] for n in REPLICATED], 8), "adamw_replicated")

    for kind, fr in (("grad", g_r), ("delta", d_r), ("new_m", m_r), ("new_v", v_r)):
        for n, a in zip(REPLICATED, _unpack(fr, rp_shapes)):
            out[kind, n] = a
    return (loss, grad_x, *[out[k, n] for k in ("grad", "delta", "new_m", "new_v") for n in WEIGHTS])
```

```python
import functools
import math

import numpy as np
import jax
import jax.numpy as jnp
from jax import lax
from jax.experimental import pallas as pl
from jax.experimental.pallas import tpu as pltpu

F32 = jnp.float32
BF16 = jnp.bfloat16

D_MODEL = 1024
DEPTH = 4
BLOCK = 128
N_META = 16
PAD = BLOCK - N_META
NEG = -1e30
ALPHA = (2.0 * DEPTH) ** 0.25
LN_EPS = 1e-5
RMS_EPS = 1e-6
HEADS = 16
HEAD_DIM = 64
LANES = 128
SWA_KV = 2
SWA_G = HEADS // SWA_KV
WINDOW = 128
ROPE_THETA = 500000.0
ROPE_DIM = 16
MLA_Q_LORA = 384
MLA_KV_LORA = 256
MLA_NOPE = 64
MLA_ROPE = 32
MLA_ROPE_THETA = 10000.0
D_FF = 2816
ADAM_LR = 0.001
ADAM_B1 = 0.9
ADAM_B2 = 0.999
ADAM_EPS = 1e-08
ADAM_WD = 0.01
ADAM_STEP = 10
N_CHIPS = 4
N_DEV = 8
ROW = 1024
VMEM_LIMIT = 48 * 1024 * 1024
MESH = pl.DeviceIdType.MESH
LOG2E = 1.4426950408889634
DENSE_HS = 2
DENSE_HS_FWD = 4

NN = (((1,), (0,)), ((), ()))
NT = (((1,), (1,)), ((), ()))
TN = (((0,), (0,)), ((), ()))


def _pick(n, cands):
    for c in cands:
        if n % c == 0:
            return c
    return n


def _params(*sem):
    return pltpu.CompilerParams(dimension_semantics=sem, vmem_limit_bytes=VMEM_LIMIT)


def _bf(x):
    return x if x.dtype == BF16 else x.astype(BF16)


def _mm_tn(a, b, name):
    (K, M), (_, N) = a.shape, b.shape
    tm = _pick(M, (512, 384, 256, 128))
    tn = _pick(N, (640, 512, 384, 256, 128))
    tk = _pick(K, (640, 512, 384, 256, 128))
    nk = K // tk

    def body(a_ref, b_ref, o_ref, at_sc, acc_ref):
        j, k = pl.program_id(1), pl.program_id(2)

        @pl.when(j == 0)
        def _():
            at_sc[k] = a_ref[...].astype(F32).T.astype(BF16)

        part = lax.dot_general(at_sc[k], _bf(b_ref[...]), NN, preferred_element_type=F32)

        @pl.when(k == 0)
        def _():
            acc_ref[...] = part

        @pl.when(k > 0)
        def _():
            acc_ref[...] += part

        @pl.when(k == nk - 1)
        def _():
            o_ref[...] = acc_ref[...]

    return pl.pallas_call(
        body, name=name, grid=(M // tm, N // tn, nk),
        in_specs=[pl.BlockSpec((tk, tm), lambda i, j, k: (jnp.where(j == 0, k, nk - 1), i)),
                  pl.BlockSpec((tk, tn), lambda i, j, k: (k, j))],
        out_specs=pl.BlockSpec((tm, tn), lambda i, j, k: (i, j)),
        out_shape=jax.ShapeDtypeStruct((M, N), F32),
        scratch_shapes=[pltpu.VMEM((nk, tm, tk), BF16), pltpu.VMEM((tm, tn), F32)],
        compiler_params=_params("parallel", "arbitrary", "arbitrary"),
    )(a, b)


def _mm(a, b, mode, name, out_dtype=F32):
    if mode == "tn":
        return _mm_tn(a, b, name)
    if mode == "nn":
        (M, K), (_, N) = a.shape, b.shape
    else:
        (M, K), (N, _) = a.shape, b.shape
    tm = _pick(M, (1664, 1408, 1024, 640, 512, 384, 256, 128))
    tn = _pick(N, (640, 512, 384, 1408, 256, 128))
    tk = K if K <= 1024 else _pick(K, (640, 512, 384, 1408, 256, 128))
    nk = K // tk
    dn = {"nn": NN, "nt": NT}[mode]

    def body(a_ref, b_ref, o_ref, *acc):
        part = lax.dot_general(_bf(a_ref[...]), _bf(b_ref[...]), dn, preferred_element_type=F32)
        if nk == 1:
            o_ref[...] = part.astype(out_dtype)
            return
        acc_ref, = acc
        k = pl.program_id(2)

        @pl.when(k == 0)
        def _():
            acc_ref[...] = part

        @pl.when(k > 0)
        def _():
            acc_ref[...] += part

        @pl.when(k == nk - 1)
        def _():
            o_ref[...] = acc_ref[...].astype(out_dtype)

    a_spec = pl.BlockSpec((tm, tk), lambda i, j, k: (i, k))
    if mode == "nt":
        b_spec = pl.BlockSpec((tn, tk), lambda i, j, k: (j, k))
    else:
        b_spec = pl.BlockSpec((tk, tn), lambda i, j, k: (k, j))
    return pl.pallas_call(
        body, name=name, grid=(M // tm, N // tn, nk),
        in_specs=[a_spec, b_spec],
        out_specs=pl.BlockSpec((tm, tn), lambda i, j, k: (i, j)),
        out_shape=jax.ShapeDtypeStruct((M, N), out_dtype),
        scratch_shapes=[pltpu.VMEM((tm, tn), F32)] if nk > 1 else [],
        compiler_params=_params("parallel", "parallel", "arbitrary"),
    )(a, b)


def _ln_fwd(h, mix, g, b, name):
    L = h.shape[0]
    tr = 128

    def body(h_ref, m_ref, g_ref, b_ref, o_ref, ob_ref, xh_ref, rs_ref):
        z = ALPHA * h_ref[...] + m_ref[...]
        mu = jnp.mean(z, axis=1, keepdims=True)
        zc = z - mu
        var = jnp.mean(zc * zc, axis=1, keepdims=True)
        rstd = lax.rsqrt(var + LN_EPS)
        xh = zc * rstd
        xh_ref[...] = xh
        rs_ref[...] = rstd
        out = xh * g_ref[...] + b_ref[...]
        o_ref[...] = out
        ob_ref[...] = out.astype(BF16)

    row = pl.BlockSpec((tr, D_MODEL), lambda i: (i, 0))
    vec = pl.BlockSpec((1, D_MODEL), lambda i: (0, 0))
    return pl.pallas_call(
        body, name=name, grid=(L // tr,),
        in_specs=[row, row, vec, vec],
        out_specs=[row, row, row, pl.BlockSpec((tr, 1), lambda i: (i, 0))],
        out_shape=[jax.ShapeDtypeStruct((L, D_MODEL), F32), jax.ShapeDtypeStruct((L, D_MODEL), BF16),
                   jax.ShapeDtypeStruct((L, D_MODEL), F32), jax.ShapeDtypeStruct((L, 1), F32)],
        compiler_params=_params("parallel"),
    )(h, mix, g.reshape(1, D_MODEL), b.reshape(1, D_MODEL))


def _ln_bwd(ga, gb, xhat, rstd, g, name):
    L = xhat.shape[0]
    tr = 128
    two = ga is not None

    def body(*refs):
        if two:
            ga_ref, gb_ref, xh_ref, rs_ref, g_ref, dz_ref, dzb_ref, dg_ref, db_ref = refs
            dy = ALPHA * ga_ref[...] + gb_ref[...]
        else:
            gb_ref, xh_ref, rs_ref, g_ref, dz_ref, dzb_ref, dg_ref, db_ref = refs
            dy = gb_ref[...]
        xh = xh_ref[...]
        dxh = dy * g_ref[...]
        c1 = jnp.mean(dxh, axis=1, keepdims=True)
        c2 = jnp.mean(dxh * xh, axis=1, keepdims=True)
        dz = rs_ref[...] * (dxh - c1 - xh * c2)
        dz_ref[...] = dz
        dzb_ref[...] = dz.astype(BF16)

        @pl.when(pl.program_id(0) == 0)
        def _():
            dg_ref[...] = jnp.zeros_like(dg_ref)
            db_ref[...] = jnp.zeros_like(db_ref)

        dg_ref[...] += jnp.sum(dy * xh, axis=0, keepdims=True)
        db_ref[...] += jnp.sum(dy, axis=0, keepdims=True)

    row = pl.BlockSpec((tr, D_MODEL), lambda i: (i, 0))
    vec = pl.BlockSpec((1, D_MODEL), lambda i: (0, 0))
    ins = ([ga] if two else []) + [gb, xhat, rstd, g.reshape(1, D_MODEL)]
    specs = ([row] if two else []) + [row, row, pl.BlockSpec((tr, 1), lambda i: (i, 0)), vec]
    dz, dzb, dg, db = pl.pallas_call(
        body, name=name, grid=(L // tr,),
        in_specs=specs, out_specs=[row, row, vec, vec],
        out_shape=[jax.ShapeDtypeStruct((L, D_MODEL), F32), jax.ShapeDtypeStruct((L, D_MODEL), BF16),
                   jax.ShapeDtypeStruct((1, D_MODEL), F32), jax.ShapeDtypeStruct((1, D_MODEL), F32)],
        compiler_params=_params("arbitrary"),
    )(*ins)
    return dz, dzb, dg[0], db[0]


def _axpy(a, b, name):
    L, N = a.shape
    tr = 128

    def body(a_ref, b_ref, o_ref):
        o_ref[...] = ALPHA * a_ref[...] + b_ref[...]

    row = pl.BlockSpec((tr, N), lambda i: (i, 0))
    return pl.pallas_call(body, name=name, grid=(L // tr,), in_specs=[row, row], out_specs=row,
                          out_shape=jax.ShapeDtypeStruct((L, N), F32), compiler_params=_params("parallel"))(a, b)


def _shift_down(cur, prev8, n):
    rows = lax.broadcasted_iota(jnp.int32, cur.shape, 0)
    out = pltpu.roll(cur, n, 0)
    for r in range(n):
        out = jnp.where(rows == r, prev8[8 - n + r:8 - n + r + 1, :], out)
    return out


def _shift_up(cur, next8, n):
    tr = cur.shape[0]
    rows = lax.broadcasted_iota(jnp.int32, cur.shape, 0)
    out = pltpu.roll(cur, tr - n, 0)
    for r in range(n):
        out = jnp.where(rows == tr - n + r, next8[r:r + 1, :], out)
    return out


def _silu(x):
    return x / (1.0 + jnp.exp(-x))


def _conv(cur, prev8, cw_ref, cb_ref):
    y = cb_ref[...] + _shift_down(cur, prev8, 2) * cw_ref[0:1, :]
    y = y + _shift_down(cur, prev8, 1) * cw_ref[1:2, :]
    return y + cur * cw_ref[2:3, :]


def _valid_rows(i, tr, u_ref, up_ref):
    rows = i * tr + lax.broadcasted_iota(jnp.int32, u_ref.shape, 0)
    prow = i * tr - 8 + lax.broadcasted_iota(jnp.int32, up_ref.shape, 0)
    return jnp.where(rows >= PAD, u_ref[...], 0.0), jnp.where(prow >= PAD, up_ref[...], 0.0), rows


def _conv_glu_fwd(u, cw, cb, name):
    L, F2 = u.shape
    F = F2 // 2
    tr = 128

    def body(u_ref, up_ref, cw_ref, cb_ref, a_ref):
        cur, prev, _ = _valid_rows(pl.program_id(0), tr, u_ref, up_ref)
        y = _conv(cur, prev, cw_ref, cb_ref)
        a_ref[...] = (_silu(y[:, :F]) * y[:, F:]).astype(BF16)

    return pl.pallas_call(
        body, name=name, grid=(L // tr,),
        in_specs=[pl.BlockSpec((tr, F2), lambda i: (i, 0)),
                  pl.BlockSpec((8, F2), lambda i: (jnp.maximum(i * (tr // 8) - 1, 0), 0)),
                  pl.BlockSpec((3, F2), lambda i: (0, 0)),
                  pl.BlockSpec((1, F2), lambda i: (0, 0))],
        out_specs=pl.BlockSpec((tr, F), lambda i: (i, 0)),
        out_shape=jax.ShapeDtypeStruct((L, F), BF16),
        compiler_params=_params("parallel"),
    )(u, u, cw, cb.reshape(1, F2))


def _conv_glu_bwd(da, u, cw, cb, name):
    L, F2 = u.shape
    F = F2 // 2
    tr = 128
    nb = L // tr

    def dy_of(yv, dav):
        g, val = yv[:, :F], yv[:, F:]
        sg = 1.0 / (1.0 + jnp.exp(-g))
        dg = dav * val * (sg * (1.0 + g * (1.0 - sg)))
        dv = dav * (g * sg)
        return jnp.concatenate([dg, dv], axis=1)

    def body(da_ref, dan_ref, u_ref, up_ref, un_ref, cw_ref, cb_ref, du_ref, dcw_ref, dcb_ref):
        i = pl.program_id(0)
        cur, prev, rows = _valid_rows(i, tr, u_ref, up_ref)
        dy = dy_of(_conv(cur, prev, cw_ref, cb_ref), da_ref[...])
        yn = _conv(un_ref[...], cur[tr - 8:tr, :], cw_ref, cb_ref)
        dyn = jnp.where(i < nb - 1, dy_of(yn, dan_ref[...]), 0.0)
        du = dy * cw_ref[2:3, :] + _shift_up(dy, dyn, 1) * cw_ref[1:2, :] + _shift_up(dy, dyn, 2) * cw_ref[0:1, :]
        du_ref[...] = jnp.where(rows >= PAD, du, 0.0).astype(BF16)

        @pl.when(i == 0)
        def _():
            dcw_ref[...] = jnp.zeros_like(dcw_ref)
            dcb_ref[...] = jnp.zeros_like(dcb_ref)

        dcw_ref[0:1, :] += jnp.sum(dy * _shift_down(cur, prev, 2), axis=0, keepdims=True)
        dcw_ref[1:2, :] += jnp.sum(dy * _shift_down(cur, prev, 1), axis=0, keepdims=True)
        dcw_ref[2:3, :] += jnp.sum(dy * cur, axis=0, keepdims=True)
        dcb_ref[...] += jnp.sum(dy, axis=0, keepdims=True)

    nxt = lambda i: (jnp.minimum((i + 1) * (tr // 8), L // 8 - 1), 0)
    prv = lambda i: (jnp.maximum(i * (tr // 8) - 1, 0), 0)
    du, dcw, dcb = pl.pallas_call(
        body, name=name, grid=(nb,),
        in_specs=[pl.BlockSpec((tr, F), lambda i: (i, 0)), pl.BlockSpec((8, F), nxt),
                  pl.BlockSpec((tr, F2), lambda i: (i, 0)), pl.BlockSpec((8, F2), prv), pl.BlockSpec((8, F2), nxt),
                  pl.BlockSpec((3, F2), lambda i: (0, 0)), pl.BlockSpec((1, F2), lambda i: (0, 0))],
        out_specs=[pl.BlockSpec((tr, F2), lambda i: (i, 0)), pl.BlockSpec((3, F2), lambda i: (0, 0)),
                   pl.BlockSpec((1, F2), lambda i: (0, 0))],
        out_shape=[jax.ShapeDtypeStruct((L, F2), BF16), jax.ShapeDtypeStruct((3, F2), F32),
                   jax.ShapeDtypeStruct((1, F2), F32)],
        compiler_params=_params("arbitrary"),
    )(da, da, u, u, u, cw, cb.reshape(1, F2))
    return du, dcw, dcb[0]


def _dense_mask(qpos, kpos):
    return (kpos <= qpos) & (kpos >= PAD)


def _tables(pairs):
    qt, kt, ft = [], [], []
    for grp in pairs:
        for n, (qb, kb, msk) in enumerate(grp):
            qt.append(qb)
            kt.append(kb)
            ft.append((1 if n == 0 else 0) | (2 if n == len(grp) - 1 else 0) | (4 if msk else 0))
    return tuple(jnp.asarray(np.asarray(t, np.int32)) for t in (qt, kt, ft))


class _Dense:
    mask = staticmethod(_dense_mask)

    def __init__(self, L, pad_in_cr2):
        self.T = T = 640 if L % 640 == 0 else 128
        nb = L // T
        m = lambda qb, kb: kb == qb or (kb == 0 and not pad_in_cr2) or (qb * T < PAD)
        self.q_major = _tables([[(qb, kb, m(qb, kb)) for kb in range(qb + 1)] for qb in range(nb)])
        self.k_major = _tables([[(qb, kb, m(qb, kb)) for qb in range(kb, nb)] for kb in range(nb)])


def _positions(T, qb, kb):
    qpos = qb * T + lax.broadcasted_iota(jnp.int32, (T, T), 0)
    kpos = kb * T + lax.broadcasted_iota(jnp.int32, (T, T), 1)
    return qpos, kpos


def _scores(q, k, c, cr, masked, mask, T, qb, kb, backward):
    s = lax.dot_general(q, k, NT, preferred_element_type=F32) * c
    if cr is not None:
        s = s - cr
    live = None
    if masked:
        qpos, kpos = _positions(T, qb, kb)
        live = mask(qpos, kpos)
        if backward:
            live = live & (qpos >= PAD)
        s = jnp.where(live, s, NEG)
    return s, live


def _prob(s, lse, live):
    p = jnp.exp2(s - lse)
    return p if live is None else jnp.where(live, p, 0.0)


def _both(flag, fn):
    pl.when(flag != 0)(lambda: fn(True))
    pl.when(flag == 0)(lambda: fn(False))


def _flash_fwd(cfg, qa, q_off, ka, k_off, va, v_off, H, hs, scale, name, cr2=None):
    L = qa.shape[0]
    T = cfg.T
    qt, kt, ft = cfg.q_major
    npairs = qt.shape[0]
    c = scale * LOG2E
    decay = cr2 is not None
    W = hs * LANES

    def body(qt_ref, kt_ref, ft_ref, *refs):
        it = iter(refs)
        q_ref, k_ref, v_ref = next(it), next(it), next(it)
        cr_ref = next(it) if decay else None
        o_ref, lse_ref, m_sc, l_sc, acc_sc = next(it), next(it), next(it), next(it), next(it)
        n = pl.program_id(1)
        qb, kb, f = qt_ref[n], kt_ref[n], ft_ref[n]

        @pl.when((f & 1) != 0)
        def _():
            m_sc[...] = jnp.full(m_sc.shape, NEG, F32)
            l_sc[...] = jnp.zeros_like(l_sc)
            acc_sc[...] = jnp.zeros_like(acc_sc)

        def step(masked):
            for i in range(hs):
                cols = slice(i * LANES, (i + 1) * LANES)
                cr = cr_ref[i] if decay else None
                s, _ = _scores(_bf(q_ref[:, cols]), _bf(k_ref[:, cols]), c, cr, masked, cfg.mask, T, qb, kb, False)
                m_prev = m_sc[i]
                m_new = jnp.maximum(m_prev, jnp.max(s, axis=1, keepdims=True))
                alpha = jnp.exp2(m_prev - m_new)
                p = jnp.exp2(s - m_new)
                l_sc[i] = alpha * l_sc[i] + jnp.sum(p, axis=1, keepdims=True)
                acc_sc[i] = alpha * acc_sc[i] + lax.dot_general(p.astype(BF16), _bf(v_ref[:, cols]), NN,
                                                                preferred_element_type=F32)
                m_sc[i] = m_new

        _both(f & 4, step)

        @pl.when((f & 2) != 0)
        def _():
            for i in range(hs):
                cols = slice(i * LANES, (i + 1) * LANES)
                l = l_sc[i]
                o_ref[:, cols] = acc_sc[i] / l
                lse_ref[:, cols] = jnp.broadcast_to(m_sc[i] + jnp.log(l) * LOG2E, (T, LANES))

    qrow = lambda off: pl.BlockSpec((T, W), lambda h, n, qt, kt, ft: (qt[n], off // hs + h))
    krow = lambda off: pl.BlockSpec((T, W), lambda h, n, qt, kt, ft: (kt[n], off // hs + h))
    ins, specs = [qa, ka, va], [qrow(q_off), krow(k_off), krow(v_off)]
    if decay:
        ins.append(cr2)
        specs.append(pl.BlockSpec((hs, 1, T), lambda h, n, qt, kt, ft: (h, 0, kt[n])))
    full = jax.ShapeDtypeStruct((L, H * LANES), F32)
    return pl.pallas_call(
        body, name=name, out_shape=[full, full],
        grid_spec=pltpu.PrefetchScalarGridSpec(
            num_scalar_prefetch=3, grid=(H // hs, npairs), in_specs=specs, out_specs=[qrow(0), qrow(0)],
            scratch_shapes=[pltpu.VMEM((hs, T, 1), F32), pltpu.VMEM((hs, T, 1), F32), pltpu.VMEM((hs, T, LANES), F32)]),
        compiler_params=_params("parallel", "arbitrary"),
    )(qt, kt, ft, *ins)


def _flash_bwd(cfg, qa, q_off, ka, k_off, va, v_off, do, lse2, delta, H, hs, scale, name, cr2=None, lane_sums=False):
    L = qa.shape[0]
    T = cfg.T
    qt, kt, ft = cfg.k_major
    npairs = qt.shape[0]
    c = scale * LOG2E
    decay = cr2 is not None
    W = hs * LANES

    def body(qt_ref, kt_ref, ft_ref, *refs):
        it = iter(refs)
        q_ref, k_ref, v_ref = next(it), next(it), next(it)
        cr_ref = next(it) if decay else None
        do_ref, lse_ref, dl_ref, dk_ref, dv_ref, dq_ref, dk_sc, dv_sc = (next(it) for _ in range(8))
        n = pl.program_id(1)
        qb, kb, f = qt_ref[n], kt_ref[n], ft_ref[n]

        @pl.when(n == 0)
        def _():
            dq_ref[...] = jnp.zeros_like(dq_ref)

        @pl.when((f & 1) != 0)
        def _():
            dk_sc[...] = jnp.zeros_like(dk_sc)
            dv_sc[...] = jnp.zeros_like(dv_sc)

        def step(masked):
            last = lax.broadcasted_iota(jnp.int32, (T, LANES), 1) == LANES - 1
            rows = pl.ds(pl.multiple_of(qb * T, T), T)
            for i in range(hs):
                cols = slice(i * LANES, (i + 1) * LANES)
                q, k, v, dob = _bf(q_ref[:, cols]), _bf(k_ref[:, cols]), _bf(v_ref[:, cols]), _bf(do_ref[:, cols])
                k1 = jnp.where(last, 1.0, k_ref[:, cols]).astype(BF16) if lane_sums else k
                q1 = jnp.where(last, 1.0, q_ref[:, cols]).astype(BF16) if lane_sums else q
                s, live = _scores(q, k, c, cr_ref[i] if decay else None, masked, cfg.mask, T, qb, kb, True)
                p = _prob(s, lse_ref[:, i * LANES:i * LANES + 1], live)
                dv_sc[i] += lax.dot_general(p.astype(BF16), dob, TN, preferred_element_type=F32)
                dp = lax.dot_general(dob, v, NT, preferred_element_type=F32)
                dsb = (p * (dp - dl_ref[:, i * LANES:i * LANES + 1])).astype(BF16)
                dk_sc[i] += lax.dot_general(dsb, q1, TN, preferred_element_type=F32)
                dq_ref[rows, cols] += lax.dot_general(dsb, k1, NN, preferred_element_type=F32)

        _both(f & 4, step)

        @pl.when((f & 2) != 0)
        def _():
            for i in range(hs):
                cols = slice(i * LANES, (i + 1) * LANES)
                dk_ref[:, cols] = dk_sc[i] * scale
                dv_ref[:, cols] = dv_sc[i]

        @pl.when(n == npairs - 1)
        def _():
            dq_ref[...] = dq_ref[...] * scale

    qrow = lambda off: pl.BlockSpec((T, W), lambda h, n, qt, kt, ft: (qt[n], off // hs + h))
    krow = lambda off: pl.BlockSpec((T, W), lambda h, n, qt, kt, ft: (kt[n], off // hs + h))
    ins, specs = [qa, ka, va], [qrow(q_off), krow(k_off), krow(v_off)]
    if decay:
        ins.append(cr2)
        specs.append(pl.BlockSpec((hs, 1, T), lambda h, n, qt, kt, ft: (h, 0, kt[n])))
    ins += [do, lse2, delta]
    specs += [qrow(0), qrow(0), qrow(0)]
    full = jax.ShapeDtypeStruct((L, H * LANES), F32)
    return pl.pallas_call(
        body, name=name, out_shape=[full, full, full],
        grid_spec=pltpu.PrefetchScalarGridSpec(
            num_scalar_prefetch=3, grid=(H // hs, npairs), in_specs=specs,
            out_specs=[krow(0), krow(0), pl.BlockSpec((L, W), lambda h, n, qt, kt, ft: (0, h))],
            scratch_shapes=[pltpu.VMEM((hs, T, LANES), F32), pltpu.VMEM((hs, T, LANES), F32)]),
        compiler_params=_params("parallel", "arbitrary"),
    )(qt, kt, ft, *ins)


def _swa_parts(qb, q_ref, km_ref, kp_ref, kc_ref, vm_ref, vp_ref, vc_ref, c):
    G, B = SWA_G, BLOCK
    q = jnp.concatenate([_bf(q_ref[:, i * LANES:(i + 1) * LANES]) for i in range(G)], axis=0)
    kc = jnp.concatenate([_bf(km_ref[...]), _bf(kp_ref[...]), _bf(kc_ref[...])], axis=0)
    vc = jnp.concatenate([_bf(vm_ref[...]), _bf(vp_ref[...]), _bf(vc_ref[...])], axis=0)
    s = lax.dot_general(q, kc, NT, preferred_element_type=F32) * c
    row = lax.broadcasted_iota(jnp.int32, (G * B, 3 * B), 0)
    col = lax.broadcasted_iota(jnp.int32, (G * B, 3 * B), 1)
    qpos = qb * B + (row & (B - 1))
    kpos = jnp.where(col < B, col, jnp.where(col < 2 * B, (qb - 1) * B + col - B, qb * B + col - 2 * B))
    d = qpos - kpos
    live = ((col < B) & (kpos >= PAD) & (kpos <= qpos)) | ((col >= B) & (kpos >= B) & (d >= 0) & (d < WINDOW))
    return q, kc, vc, jnp.where(live, s, NEG), live


def _stack_col(ref):
    return jnp.concatenate([ref[:, i * LANES:i * LANES + 1] for i in range(SWA_G)], axis=0)


def _swa_specs(nqk):
    G, B = SWA_G, BLOCK
    qrow = pl.BlockSpec((B, G * LANES), lambda hk, qb: (qb, hk))
    kv = lambda off, blk: pl.BlockSpec((B, LANES), lambda hk, qb: (blk(qb), off + hk))
    zero, prev, cur = (lambda qb: 0), (lambda qb: jnp.maximum(qb - 1, 0)), (lambda qb: qb)
    keys = [kv(HEADS, zero), kv(HEADS, prev), kv(HEADS, cur)]
    vals = [kv(nqk, zero), kv(nqk, prev), kv(nqk, cur)]
    return qrow, keys, vals


def _swa_attn_fwd(qk, proj, sink2, scale, name):
    L = qk.shape[0]
    G, B = SWA_G, BLOCK
    nqk = HEADS + SWA_KV
    c = scale * LOG2E

    def body(q_ref, km_ref, kp_ref, kc_ref, vm_ref, vp_ref, vc_ref, sink_ref, o_ref, lse_ref):
        qb = pl.program_id(1)
        q, kc, vc, s, live = _swa_parts(qb, q_ref, km_ref, kp_ref, kc_ref, vm_ref, vp_ref, vc_ref, c)
        sink = jnp.concatenate([jnp.broadcast_to(sink_ref[:, i * LANES:i * LANES + 1], (B, 1)) for i in range(G)], axis=0)
        m = jnp.maximum(jnp.max(s, axis=1, keepdims=True), sink)
        p = jnp.exp2(s - m)
        l = jnp.sum(p, axis=1, keepdims=True) + jnp.exp2(sink - m)
        o = lax.dot_general(p.astype(BF16), vc, NN, preferred_element_type=F32) / l
        lse = m + jnp.log(l) * LOG2E
        for i in range(G):
            o_ref[:, i * LANES:(i + 1) * LANES] = o[i * B:(i + 1) * B]
            lse_ref[:, i * LANES:(i + 1) * LANES] = jnp.broadcast_to(lse[i * B:(i + 1) * B], (B, LANES))

    qrow, keys, vals = _swa_specs(nqk)
    shape = jax.ShapeDtypeStruct((L, HEADS * LANES), F32)
    return pl.pallas_call(
        body, name=name, grid=(SWA_KV, L // B),
        in_specs=[qrow] + keys + vals + [pl.BlockSpec((1, G * LANES), lambda hk, qb: (0, hk))],
        out_specs=[qrow, qrow], out_shape=[shape, shape],
        compiler_params=_params("parallel", "parallel"),
    )(qk, qk, qk, qk, proj, proj, proj, sink2)


def _swa_attn_bwd(qk, proj, do, lse2, delta, scale, name):
    L = qk.shape[0]
    G, B = SWA_G, BLOCK
    nqk = HEADS + SWA_KV
    c = scale * LOG2E

    def body(q_ref, km_ref, kp_ref, kc_ref, vm_ref, vp_ref, vc_ref, do_ref, lse_ref, dl_ref, dq_ref, dk_ref, dv_ref):
        qb = pl.program_id(1)

        @pl.when(qb == 0)
        def _():
            dk_ref[...] = jnp.zeros_like(dk_ref)
            dv_ref[...] = jnp.zeros_like(dv_ref)

        q, kc, vc, s, live = _swa_parts(qb, q_ref, km_ref, kp_ref, kc_ref, vm_ref, vp_ref, vc_ref, c)
        dob = jnp.concatenate([_bf(do_ref[:, i * LANES:(i + 1) * LANES]) for i in range(G)], axis=0)
        p = jnp.where(live, jnp.exp2(s - _stack_col(lse_ref)), 0.0)
        dp = lax.dot_general(dob, vc, NT, preferred_element_type=F32)
        dsb = (p * (dp - _stack_col(dl_ref))).astype(BF16)
        dq = lax.dot_general(dsb, kc, NN, preferred_element_type=F32) * scale
        for i in range(G):
            dq_ref[:, i * LANES:(i + 1) * LANES] = dq[i * B:(i + 1) * B]
        dkc = lax.dot_general(dsb, q, TN, preferred_element_type=F32) * scale
        dvc = lax.dot_general(p.astype(BF16), dob, TN, preferred_element_type=F32)
        starts = (0, pl.multiple_of(jnp.maximum(qb - 1, 0) * B, B), pl.multiple_of(qb * B, B))
        for n, st in enumerate(starts):
            dk_ref[pl.ds(st, B), :] += dkc[n * B:(n + 1) * B]
            dv_ref[pl.ds(st, B), :] += dvc[n * B:(n + 1) * B]

    qrow, keys, vals = _swa_specs(nqk)
    res = pl.BlockSpec((L, LANES), lambda hk, qb: (0, hk))
    return pl.pallas_call(
        body, name=name, grid=(SWA_KV, L // B),
        in_specs=[qrow] + keys + vals + [qrow, qrow, qrow],
        out_specs=[qrow, res, res],
        out_shape=[jax.ShapeDtypeStruct((L, HEADS * LANES), F32), jax.ShapeDtypeStruct((L, SWA_KV * LANES), F32),
                   jax.ShapeDtypeStruct((L, SWA_KV * LANES), F32)],
        compiler_params=_params("parallel", "arbitrary"),
    )(qk, qk, qk, qk, proj, proj, proj, do, lse2, delta)


def _delta(do, o, name):
    L, HW = do.shape
    tr = BLOCK

    def body(do_ref, o_ref, d_ref):
        for h in range(HW // LANES):
            cols = slice(h * LANES, (h + 1) * LANES)
            d = jnp.sum(do_ref[:, cols].astype(F32) * o_ref[:, cols], axis=1, keepdims=True)
            d_ref[:, cols] = jnp.broadcast_to(d, (tr, LANES))

    spec = pl.BlockSpec((tr, HW), lambda i: (i, 0))
    return pl.pallas_call(body, name=name, grid=(L // tr,), in_specs=[spec, spec], out_specs=spec,
                          out_shape=jax.ShapeDtypeStruct((L, HW), F32), compiler_params=_params("parallel"))(do, o)


def _sink_grad(lse2, delta, sink2, name):
    L, HW = lse2.shape
    tr = 128

    def body(lse_ref, dl_ref, s_ref, o_ref):
        @pl.when(pl.program_id(0) == 0)
        def _():
            o_ref[...] = jnp.zeros_like(o_ref)

        o_ref[...] -= jnp.sum(jnp.exp2(s_ref[...] - lse_ref[...]) * dl_ref[...], axis=0, keepdims=True)

    row = pl.BlockSpec((tr, HW), lambda i: (i, 0))
    vec = pl.BlockSpec((1, HW), lambda i: (0, 0))
    return pl.pallas_call(body, name=name, grid=(L // tr,), in_specs=[row, row, vec], out_specs=vec,
                          out_shape=jax.ShapeDtypeStruct((1, HW), F32), compiler_params=_params("arbitrary"))(
        lse2, delta, sink2)


def _tri(lower):
    r = lax.broadcasted_iota(jnp.int32, (BLOCK, BLOCK), 0)
    c = lax.broadcasted_iota(jnp.int32, (BLOCK, BLOCK), 1)
    return jnp.where((c <= r) if lower else (c >= r), 1.0, 0.0).astype(F32)


def _gate_cumsum(proj, fg_tile, b_pad, name):
    L = proj.shape[0]

    def body(fg_ref, b_ref, c_ref, carry):
        @pl.when(pl.program_id(0) == 0)
        def _():
            carry[...] = jnp.zeros_like(carry)

        x = fg_ref[...] + b_ref[...]
        lf = jnp.minimum(x, 0.0) - jnp.log(1.0 + jnp.exp(-jnp.abs(x)))
        c = jnp.dot(_tri(True), lf, precision=lax.Precision.HIGHEST, preferred_element_type=F32) + carry[...]
        c_ref[...] = c
        carry[...] = c[BLOCK - 1:BLOCK, :]

    return pl.pallas_call(
        body, name=name, grid=(L // BLOCK,),
        in_specs=[pl.BlockSpec((BLOCK, LANES), lambda i: (i, fg_tile)), pl.BlockSpec((1, LANES), lambda i: (0, 0))],
        out_specs=pl.BlockSpec((BLOCK, LANES), lambda i: (i, 0)),
        out_shape=jax.ShapeDtypeStruct((L, LANES), F32),
        scratch_shapes=[pltpu.VMEM((1, LANES), F32)],
        compiler_params=_params("arbitrary"),
    )(proj, b_pad)


def _gate_cumsum_bwd(dc, proj, fg_tile, b_pad, name):
    L = proj.shape[0]
    nb = L // BLOCK

    def body(dc_ref, fg_ref, b_ref, dfg_ref, db_ref, carry):
        @pl.when(pl.program_id(0) == 0)
        def _():
            carry[...] = jnp.zeros_like(carry)
            db_ref[...] = jnp.zeros_like(db_ref)

        dlf = jnp.dot(_tri(False), dc_ref[...], precision=lax.Precision.HIGHEST,
                      preferred_element_type=F32) + carry[...]
        carry[...] = dlf[0:1, :]
        x = fg_ref[...] + b_ref[...]
        lanes = lax.broadcasted_iota(jnp.int32, (BLOCK, LANES), 1)
        rows = (nb - 1 - pl.program_id(0)) * BLOCK + lax.broadcasted_iota(jnp.int32, (BLOCK, LANES), 0)
        dfg = jnp.where((lanes < HEADS) & (rows >= PAD), dlf / (1.0 + jnp.exp(x)), 0.0)
        dfg_ref[...] = jnp.concatenate([dfg, jnp.zeros_like(dfg)], axis=1)
        db_ref[...] += jnp.sum(dfg, axis=0, keepdims=True)

    dfg, db = pl.pallas_call(
        body, name=name, grid=(nb,),
        in_specs=[pl.BlockSpec((BLOCK, LANES), lambda i: (nb - 1 - i, 0)),
                  pl.BlockSpec((BLOCK, LANES), lambda i: (nb - 1 - i, fg_tile)),
                  pl.BlockSpec((1, LANES), lambda i: (0, 0))],
        out_specs=[pl.BlockSpec((BLOCK, 2 * LANES), lambda i: (nb - 1 - i, 0)),
                   pl.BlockSpec((1, LANES), lambda i: (0, 0))],
        out_shape=[jax.ShapeDtypeStruct((L, 2 * LANES), F32), jax.ShapeDtypeStruct((1, LANES), F32)],
        scratch_shapes=[pltpu.VMEM((1, LANES), F32)],
        compiler_params=_params("arbitrary"),
    )(dc, proj, b_pad)
    return dfg, db[0]


def _rope_tables(L, dim, theta, lane0):
    half = dim // 2
    pos = (jnp.arange(L) - PAD).astype(F32)
    inv = theta ** (-jnp.arange(0, dim, 2, dtype=F32) / dim)
    ang = pos[:, None] * inv[None, :]
    cos, sin = jnp.cos(ang), jnp.sin(ang)
    C = jnp.ones((L, LANES), F32).at[:, lane0:lane0 + half].set(cos).at[:, lane0 + half:lane0 + dim].set(cos)
    S1 = jnp.zeros((L, LANES), F32).at[:, lane0:lane0 + half].set(-sin)
    S2 = jnp.zeros((L, LANES), F32).at[:, lane0 + half:lane0 + dim].set(sin)
    return C, S1, S2


def _rot(x, C, S1, S2, R):
    return x * C + pltpu.roll(x, LANES - R, 1) * S1 + pltpu.roll(x, R, 1) * S2


def _rot_t(dy, C, S1, S2, R):
    return dy * C + pltpu.roll(dy * S1, R, 1) + pltpu.roll(dy * S2, LANES - R, 1)


def _rope(x, nt, tabs, R, name, transpose=False, shared=None, shared_tile=0, out_dtype=F32):
    L = x.shape[0]
    T = BLOCK
    fn = _rot_t if transpose else _rot

    def body(*refs):
        if shared is None:
            x_ref, c_ref, s1_ref, s2_ref, o_ref = refs
        else:
            x_ref, sh_ref, c_ref, s1_ref, s2_ref, o_ref = refs
            rs = fn(sh_ref[...], c_ref[...], s1_ref[...], s2_ref[...], R)
        for h in range(nt):
            cols = slice(h * LANES, (h + 1) * LANES)
            if shared is None:
                o_ref[:, cols] = fn(x_ref[:, cols], c_ref[...], s1_ref[...], s2_ref[...], R).astype(out_dtype)
            else:
                o_ref[:, cols] = (x_ref[:, cols] + rs).astype(out_dtype)

    wide = pl.BlockSpec((T, nt * LANES), lambda i: (i, 0))
    tab = pl.BlockSpec((T, LANES), lambda i: (i, 0))
    ins, specs = [x], [wide]
    if shared is not None:
        ins.append(shared)
        specs.append(pl.BlockSpec((T, LANES), lambda i: (i, shared_tile)))
    return pl.pallas_call(body, name=name, grid=(L // T,), in_specs=specs + [tab, tab, tab], out_specs=wide,
                          out_shape=jax.ShapeDtypeStruct((L, nt * LANES), out_dtype),
                          compiler_params=_params("parallel"))(*ins, *tabs)


def _rope_shared_bwd(dk, nt, tabs, R, name):
    L = dk.shape[0]
    tr = 128

    def body(dk_ref, c_ref, s1_ref, s2_ref, o_ref):
        acc = dk_ref[:, 0:LANES]
        for h in range(1, nt):
            acc = acc + dk_ref[:, h * LANES:(h + 1) * LANES]
        o_ref[...] = _rot_t(acc, c_ref[...], s1_ref[...], s2_ref[...], R)

    tab = pl.BlockSpec((tr, LANES), lambda i: (i, 0))
    return pl.pallas_call(body, name=name, grid=(L // tr,),
                          in_specs=[pl.BlockSpec((tr, nt * LANES), lambda i: (i, 0)), tab, tab, tab], out_specs=tab,
                          out_shape=jax.ShapeDtypeStruct((L, LANES), F32), compiler_params=_params("parallel"))(
        dk, *tabs)


def _rms_fwd(pa, gq, gkv, name):
    L = pa.shape[0]
    tr = 128
    Q, KV = MLA_Q_LORA, MLA_KV_LORA

    def body(pa_ref, gq_ref, gkv_ref, q_ref, kv_ref):
        for lo, n, g_ref, o_ref in ((0, Q, gq_ref, q_ref), (Q, KV, gkv_ref, kv_ref)):
            x = pa_ref[:, lo:lo + n]
            r = lax.rsqrt(jnp.mean(x * x, axis=1, keepdims=True) + RMS_EPS)
            o_ref[...] = (x * r * g_ref[...]).astype(BF16)

    return pl.pallas_call(
        body, name=name, grid=(L // tr,),
        in_specs=[pl.BlockSpec((tr, pa.shape[1]), lambda i: (i, 0)), pl.BlockSpec((1, Q), lambda i: (0, 0)),
                  pl.BlockSpec((1, KV), lambda i: (0, 0))],
        out_specs=[pl.BlockSpec((tr, Q), lambda i: (i, 0)), pl.BlockSpec((tr, KV), lambda i: (i, 0))],
        out_shape=[jax.ShapeDtypeStruct((L, Q), BF16), jax.ShapeDtypeStruct((L, KV), BF16)],
        compiler_params=_params("parallel"),
    )(pa, gq.reshape(1, Q), gkv.reshape(1, KV))


def _rms_bwd(pa, dq, dkv, dkr, gq, gkv, name):
    L, W = pa.shape
    tr = 128
    Q, KV = MLA_Q_LORA, MLA_KV_LORA

    def body(pa_ref, dq_ref, dkv_ref, dkr_ref, gq_ref, gkv_ref, dpa_ref, dgq_ref, dgkv_ref):
        @pl.when(pl.program_id(0) == 0)
        def _():
            dgq_ref[...] = jnp.zeros_like(dgq_ref)
            dgkv_ref[...] = jnp.zeros_like(dgkv_ref)

        for lo, n, g_ref, dy_ref, dg_ref in ((0, Q, gq_ref, dq_ref, dgq_ref), (Q, KV, gkv_ref, dkv_ref, dgkv_ref)):
            x = pa_ref[:, lo:lo + n]
            r = lax.rsqrt(jnp.mean(x * x, axis=1, keepdims=True) + RMS_EPS)
            xh = x * r
            dy = dy_ref[...]
            dxh = dy * g_ref[...]
            dpa_ref[:, lo:lo + n] = (r * (dxh - xh * jnp.mean(dxh * xh, axis=1, keepdims=True))).astype(BF16)
            dg_ref[...] += jnp.sum(dy * xh, axis=0, keepdims=True)
        dpa_ref[:, Q + KV:W] = dkr_ref[...].astype(BF16)

    vq = pl.BlockSpec((1, Q), lambda i: (0, 0))
    vkv = pl.BlockSpec((1, KV), lambda i: (0, 0))
    dpa, dgq, dgkv = pl.pallas_call(
        body, name=name, grid=(L // tr,),
        in_specs=[pl.BlockSpec((tr, W), lambda i: (i, 0)), pl.BlockSpec((tr, Q), lambda i: (i, 0)),
                  pl.BlockSpec((tr, KV), lambda i: (i, 0)), pl.BlockSpec((tr, LANES), lambda i: (i, 0)), vq, vkv],
        out_specs=[pl.BlockSpec((tr, W), lambda i: (i, 0)), vq, vkv],
        out_shape=[jax.ShapeDtypeStruct((L, W), BF16), jax.ShapeDtypeStruct((1, Q), F32),
                   jax.ShapeDtypeStruct((1, KV), F32)],
        compiler_params=_params("arbitrary"),
    )(pa, dq, dkv, dkr, gq.reshape(1, Q), gkv.reshape(1, KV))
    return dpa, dgq[0], dgkv[0]


def _loss_head(h, target, name):
    L = h.shape[0]
    tr = BLOCK
    inv = 1.0 / D_MODEL

    def body(h_ref, t_ref, loss_ref, dh_ref):
        i = pl.program_id(0)

        @pl.when(i == 0)
        def _():
            loss_ref[...] = jnp.zeros_like(loss_ref)
            dh_ref[...] = jnp.zeros_like(dh_ref)

        @pl.when(i > 0)
        def _():
            e = h_ref[...] - t_ref[...]
            dh_ref[...] = e * inv
            loss_ref[...] += jnp.sum((e * e).reshape(tr // 8, 8, D_MODEL), axis=0) * (0.5 * inv)

    row = pl.BlockSpec((tr, D_MODEL), lambda i: (i, 0))
    loss, dh = pl.pallas_call(
        body, name=name, grid=(L // tr,),
        in_specs=[row, pl.BlockSpec((tr, D_MODEL), lambda i: (jnp.maximum(i - 1, 0), 0))],
        out_specs=[pl.BlockSpec((8, D_MODEL), lambda i: (0, 0)), row],
        out_shape=[jax.ShapeDtypeStruct((8, D_MODEL), F32), jax.ShapeDtypeStruct((L, D_MODEL), F32)],
        compiler_params=_params("arbitrary"),
    )(h, target)
    return loss, dh


def _pad_heads_cols(w, nh, d, dp=LANES):
    K = w.shape[0]
    return jnp.pad(w.reshape(K, nh, d), ((0, 0), (0, 0), (0, dp - d))).reshape(K, nh * dp)


def _unpad_heads_cols(w, nh, d, dp=LANES):
    K = w.shape[0]
    return w.reshape(K, nh, dp)[:, :, :d].reshape(K, nh * d)


def _pad_heads_rows(w, nh, d):
    N = w.shape[1]
    return jnp.pad(w.reshape(nh, d, N), ((0, 0), (0, LANES - d), (0, 0))).reshape(nh * LANES, N)


def _unpad_heads_rows(w, nh, d):
    N = w.shape[1]
    return w.reshape(nh, LANES, N)[:, :d, :].reshape(nh * d, N)


def _fox_fwd(h, w_in, b_f, w_o, tag):
    L = h.shape[0]
    hd = HEADS * HEAD_DIM
    W = jnp.concatenate([_pad_heads_cols(w_in[:, i * hd:(i + 1) * hd], HEADS, HEAD_DIM) for i in range(3)]
                        + [jnp.pad(w_in[:, 3 * hd:], ((0, 0), (0, 2 * LANES - HEADS)))], axis=1)
    Wo = _pad_heads_rows(w_o, HEADS, HEAD_DIM)
    b_pad = jnp.pad(b_f, (0, LANES - HEADS)).reshape(1, LANES)
    proj = _mm(h, W, "nn", tag + "_proj")
    c = _gate_cumsum(proj, 3 * HEADS, b_pad, tag + "_cumsum")
    dead = (jnp.arange(L) < PAD)[:, None]
    cr2 = jnp.where(dead, -NEG, c[:, :HEADS] * LOG2E).T.reshape(HEADS, 1, L)
    cfg = _Dense(L, True)
    scale = HEAD_DIM ** -0.5
    o, lse2 = _flash_fwd(cfg, proj, 0, proj, HEADS, proj, 2 * HEADS, HEADS, DENSE_HS_FWD, scale, tag + "_attn", cr2=cr2)
    mix = _mm(o, Wo, "nn", tag + "_out")
    return mix, (h, W, Wo, b_pad, proj, cr2, o, lse2)


def _fox_bwd(dmix, res, tag):
    h, W, Wo, b_pad, proj, cr2, o, lse2 = res
    L = h.shape[0]
    cfg = _Dense(L, True)
    scale = HEAD_DIM ** -0.5
    dWo = _mm(o, dmix, "tn", tag + "_dwo")
    do = _mm(dmix, Wo, "nt", tag + "_do", out_dtype=BF16)
    qkv = (proj, 0, proj, HEADS, proj, 2 * HEADS)
    delta = _delta(do, o, tag + "_delta")
    dk, dv, dq = _flash_bwd(cfg, *qkv, do, lse2, delta, HEADS, DENSE_HS, scale, tag + "_bwd", cr2=cr2, lane_sums=True)
    dc = jnp.pad((dq[:, LANES - 1::LANES] - dk[:, LANES - 1::LANES]) * (1.0 / scale), ((0, 0), (0, LANES - HEADS)))
    dfg, db = _gate_cumsum_bwd(dc, proj, 3 * HEADS, b_pad, tag + "_cumsum_bwd")
    dproj = jnp.concatenate([dq.astype(BF16), dk.astype(BF16), dv.astype(BF16), dfg.astype(BF16)], axis=1)
    dW = _mm(h, dproj, "tn", tag + "_dw")
    dh = _mm(dproj, W, "nt", tag + "_dh")
    hp = HEADS * LANES
    dw_in = jnp.concatenate([_unpad_heads_cols(dW[:, i * hp:(i + 1) * hp], HEADS, HEAD_DIM) for i in range(3)]
                            + [dW[:, 3 * hp:3 * hp + HEADS]], axis=1)
    return dh, dict(w_in=dw_in, b_f=db[:HEADS], w_o=_unpad_heads_rows(dWo, HEADS, HEAD_DIM))


def _swa_fwd(h, w_in, sinks, w_o, tag):
    L = h.shape[0]
    qd, kd = HEADS * HEAD_DIM, SWA_KV * HEAD_DIM
    W = jnp.concatenate([_pad_heads_cols(w_in[:, :qd], HEADS, HEAD_DIM),
                         _pad_heads_cols(w_in[:, qd:qd + kd], SWA_KV, HEAD_DIM),
                         _pad_heads_cols(w_in[:, qd + kd:], SWA_KV, HEAD_DIM)], axis=1)
    Wo = _pad_heads_rows(w_o, HEADS, HEAD_DIM)
    sink2 = jnp.repeat(sinks * LOG2E, LANES).reshape(1, HEADS * LANES)
    tabs = _rope_tables(L, ROPE_DIM, ROPE_THETA, 0)
    proj = _mm(h, W, "nn", tag + "_proj")
    nqk = HEADS + SWA_KV
    qk = _rope(proj, nqk, tabs, ROPE_DIM // 2, tag + "_rope")
    scale = HEAD_DIM ** -0.5
    o, lse2 = _swa_attn_fwd(qk, proj, sink2, scale, tag + "_attn")
    mix = _mm(o, Wo, "nn", tag + "_out")
    return mix, (h, W, Wo, sink2, tabs, proj, qk, o, lse2)


def _swa_bwd(dmix, res, tag):
    h, W, Wo, sink2, tabs, proj, qk, o, lse2 = res
    L = h.shape[0]
    nqk = HEADS + SWA_KV
    scale = HEAD_DIM ** -0.5
    dWo = _mm(o, dmix, "tn", tag + "_dwo")
    do = _mm(dmix, Wo, "nt", tag + "_do", out_dtype=BF16)
    delta = _delta(do, o, tag + "_delta")
    dsink = _sink_grad(lse2, delta, sink2, tag + "_dsink")[0, ::LANES]
    dq, dk, dv = _swa_attn_bwd(qk, proj, do, lse2, delta, scale, tag + "_bwd")
    dqk = _rope(jnp.concatenate([dq, dk], axis=1), nqk, tabs, ROPE_DIM // 2, tag + "_rope_bwd", transpose=True,
                out_dtype=BF16)
    dproj = jnp.concatenate([dqk, dv.astype(BF16)], axis=1)
    dW = _mm(h, dproj, "tn", tag + "_dw")
    dh = _mm(dproj, W, "nt", tag + "_dh")
    hp = HEADS * LANES
    dw_in = jnp.concatenate([_unpad_heads_cols(dW[:, :hp], HEADS, HEAD_DIM),
                             _unpad_heads_cols(dW[:, hp:hp + SWA_KV * LANES], SWA_KV, HEAD_DIM),
                             _unpad_heads_cols(dW[:, hp + SWA_KV * LANES:], SWA_KV, HEAD_DIM)], axis=1)
    return dh, dict(w_in=dw_in, sinks=dsink, w_o=_unpad_heads_rows(dWo, HEADS, HEAD_DIM))


def _mla_fwd(h, w_a, g_q, g_kv, w_uq, w_ukv, w_o, tag):
    L = h.shape[0]
    Q, KV = MLA_Q_LORA, MLA_KV_LORA
    dqk = MLA_NOPE + MLA_ROPE
    kr_w = jnp.pad(w_a[:, Q + KV:], ((0, 0), (MLA_NOPE, LANES - dqk)))
    Wa = jnp.concatenate([w_a[:, :Q + KV], kr_w], axis=1)
    Wuq = _pad_heads_cols(w_uq, HEADS, dqk)
    ukv = w_ukv.reshape(KV, HEADS, MLA_NOPE + HEAD_DIM)
    Wukv = jnp.concatenate([_pad_heads_cols(ukv[:, :, :MLA_NOPE].reshape(KV, -1), HEADS, MLA_NOPE),
                            _pad_heads_cols(ukv[:, :, MLA_NOPE:].reshape(KV, -1), HEADS, HEAD_DIM)], axis=1)
    Wo = _pad_heads_rows(w_o, HEADS, HEAD_DIM)
    tabs = _rope_tables(L, MLA_ROPE, MLA_ROPE_THETA, MLA_NOPE)
    R = MLA_ROPE // 2
    pa = _mm(h, Wa, "nn", tag + "_proj")
    cqn, ckvn = _rms_fwd(pa, g_q, g_kv, tag + "_rms")
    q0 = _mm(cqn, Wuq, "nn", tag + "_uq")
    qr = _rope(q0, HEADS, tabs, R, tag + "_rope_q")
    kv0 = _mm(ckvn, Wukv, "nn", tag + "_ukv")
    kk = _rope(kv0, HEADS, tabs, R, tag + "_rope_k", shared=pa, shared_tile=(Q + KV) // LANES)
    cfg = _Dense(L, False)
    scale = dqk ** -0.5
    o, lse2 = _flash_fwd(cfg, qr, 0, kk, 0, kv0, HEADS, HEADS, DENSE_HS_FWD, scale, tag + "_attn")
    mix = _mm(o, Wo, "nn", tag + "_out")
    return mix, (h, Wa, Wuq, Wukv, Wo, g_q, g_kv, tabs, pa, cqn, ckvn, qr, kk, kv0, o, lse2)


def _mla_bwd(dmix, res, tag):
    h, Wa, Wuq, Wukv, Wo, g_q, g_kv, tabs, pa, cqn, ckvn, qr, kk, kv0, o, lse2 = res
    L = h.shape[0]
    Q, KV = MLA_Q_LORA, MLA_KV_LORA
    dqk = MLA_NOPE + MLA_ROPE
    R = MLA_ROPE // 2
    cfg = _Dense(L, False)
    scale = dqk ** -0.5
    dWo = _mm(o, dmix, "tn", tag + "_dwo")
    do = _mm(dmix, Wo, "nt", tag + "_do", out_dtype=BF16)
    delta = _delta(do, o, tag + "_delta")
    dk, dv, dqr = _flash_bwd(cfg, qr, 0, kk, 0, kv0, HEADS, do, lse2, delta, HEADS, DENSE_HS, scale, tag + "_bwd")
    dq0 = _rope(dqr, HEADS, tabs, R, tag + "_rope_q_bwd", transpose=True, out_dtype=BF16)
    dWuq = _mm(cqn, dq0, "tn", tag + "_dwuq")
    dcqn = _mm(dq0, Wuq, "nt", tag + "_dcq")
    dkv = jnp.concatenate([dk, dv], axis=1).astype(BF16)
    dWukv = _mm(ckvn, dkv, "tn", tag + "_dwukv")
    dckvn = _mm(dkv, Wukv, "nt", tag + "_dckv")
    dkr = _rope_shared_bwd(dk, HEADS, tabs, R, tag + "_rope_k_bwd")
    dpa, dgq, dgkv = _rms_bwd(pa, dcqn, dckvn, dkr, g_q, g_kv, tag + "_rms_bwd")
    dWa = _mm(h, dpa, "tn", tag + "_dw")
    dh = _mm(dpa, Wa, "nt", tag + "_dh")
    hp = HEADS * LANES
    dw_a = jnp.concatenate([dWa[:, :Q + KV], dWa[:, Q + KV + MLA_NOPE:Q + KV + dqk]], axis=1)
    dk_n = dWukv[:, :hp].reshape(KV, HEADS, LANES)[:, :, :MLA_NOPE]
    dv_n = dWukv[:, hp:].reshape(KV, HEADS, LANES)[:, :, :HEAD_DIM]
    dw_ukv = jnp.concatenate([dk_n, dv_n], axis=2).reshape(KV, HEADS * (MLA_NOPE + HEAD_DIM))
    return dh, dict(w_a=dw_a, g_q=dgq, g_kv=dgkv, w_uq=_unpad_heads_cols(dWuq, HEADS, dqk), w_ukv=dw_ukv,
                    w_o=_unpad_heads_rows(dWo, HEADS, HEAD_DIM))


MATMUL_WEIGHTS = ("fox_w_in", "fox_w_o", "swa_w_in", "swa_w_o", "mla_w_a", "mla_w_uq", "mla_w_ukv", "mla_w_o",
                  "ffn_w_in", "ffn_w_out")


def _local_step(x, target, w):
    w = {k: (_bf(v) if k in MATMUL_WEIGHTS else v) for k, v in w.items()}
    h = jnp.concatenate([jnp.zeros((PAD, D_MODEL), F32), w["meta_tokens"], x], axis=0)
    hb = h.astype(BF16)
    saved = []
    for i in range(DEPTH):
        kind, j = i % 3, i // 3
        tag = "l%d" % i
        if kind == 0:
            mix, mres = _fox_fwd(hb, w["fox_w_in"][j], w["fox_b_f"][j], w["fox_w_o"][j], tag + "_fox")
        elif kind == 1:
            mix, mres = _swa_fwd(hb, w["swa_w_in"][j], w["swa_sinks"][j], w["swa_w_o"][j], tag + "_swa")
        else:
            mix, mres = _mla_fwd(hb, w["mla_w_a"][j], w["mla_g_q"][j], w["mla_g_kv"][j], w["mla_w_uq"][j],
                                 w["mla_w_ukv"][j], w["mla_w_o"][j], tag + "_mla")
        h1, h1b, xh1, rs1 = _ln_fwd(h, mix, w["ln1_g"][i], w["ln1_b"][i], tag + "_ln1")
        u = _mm(h1b, w["ffn_w_in"][i], "nn", tag + "_ffn_in")
        a = _conv_glu_fwd(u, w["ffn_conv_w"][i], w["ffn_conv_b"][i], tag + "_conv")
        ffn = _mm(a, w["ffn_w_out"][i], "nn", tag + "_ffn_out")
        h2, h2b, xh2, rs2 = _ln_fwd(h1, ffn, w["ln2_g"][i], w["ln2_b"][i], tag + "_ln2")
        saved.append((mres, xh1, rs1, h1b, u, a, xh2, rs2))
        h, hb = h2, h2b
    loss, dh = _loss_head(h, target, "loss_head")

    g = {k: [None] * v.shape[0] for k, v in w.items() if k != "meta_tokens"}
    ga = None
    for i in reversed(range(DEPTH)):
        kind, j = i % 3, i // 3
        tag = "l%d" % i
        mres, xh1, rs1, h1b, u, a, xh2, rs2 = saved[i]
        dz2, dz2b, g["ln2_g"][i], g["ln2_b"][i] = _ln_bwd(ga, dh, xh2, rs2, w["ln2_g"][i], tag + "_ln2_bwd")
        g["ffn_w_out"][i] = _mm(a, dz2b, "tn", tag + "_dw_out")
        da = _mm(dz2b, w["ffn_w_out"][i], "nt", tag + "_da")
        du, g["ffn_conv_w"][i], g["ffn_conv_b"][i] = _conv_glu_bwd(da, u, w["ffn_conv_w"][i], w["ffn_conv_b"][i],
                                                                   tag + "_conv_bwd")
        g["ffn_w_in"][i] = _mm(h1b, du, "tn", tag + "_dw_in")
        dh1 = _mm(du, w["ffn_w_in"][i], "nt", tag + "_dh1")
        dz1, dz1b, g["ln1_g"][i], g["ln1_b"][i] = _ln_bwd(dz2, dh1, xh1, rs1, w["ln1_g"][i], tag + "_ln1_bwd")
        if kind == 0:
            dh, mg = _fox_bwd(dz1b, mres, tag + "_fox")
            pre = "fox_"
        elif kind == 1:
            dh, mg = _swa_bwd(dz1b, mres, tag + "_swa")
            pre = "swa_"
        else:
            dh, mg = _mla_bwd(dz1b, mres, tag + "_mla")
            pre = "mla_"
        for k, v in mg.items():
            g[pre + k][j] = v
        ga = dz1
    dh0 = _axpy(ga, dh, "dh0")
    grads = {k: jnp.stack(v) for k, v in g.items()}
    grads["meta_tokens"] = dh0[PAD:BLOCK]
    return loss, dh0, grads


SHARDED = (("meta_tokens", 1), ("fox_w_in", 2), ("fox_w_o", 1), ("swa_w_in", 2), ("swa_w_o", 1), ("mla_w_a", 1),
           ("mla_g_q", 1), ("mla_g_kv", 1), ("mla_w_uq", 2), ("mla_w_ukv", 2), ("mla_w_o", 1), ("ffn_w_in", 2),
           ("ffn_conv_w", 2), ("ffn_w_out", 1))
REPLICATED = ("ln1_g", "ln1_b", "ln2_g", "ln2_b", "fox_b_f", "swa_sinks", "ffn_conv_b")
WEIGHTS = ("meta_tokens", "ln1_g", "ln1_b", "ln2_g", "ln2_b", "fox_w_in", "fox_b_f", "fox_w_o", "swa_w_in",
           "swa_sinks", "swa_w_o", "mla_w_a", "mla_g_q", "mla_g_kv", "mla_w_uq", "mla_w_ukv", "mla_w_o", "ffn_w_in",
           "ffn_conv_w", "ffn_conv_b", "ffn_w_out")


def _rows(n):
    return -(-n // ROW)


def _pack(arrs, multiple):
    parts = []
    for a in arrs:
        n = math.prod(a.shape)
        parts.append(jnp.pad(a.reshape(-1), (0, _rows(n) * ROW - n)).reshape(-1, ROW))
    total = sum(p.shape[0] for p in parts)
    pad = -total % multiple
    if pad:
        parts.append(jnp.zeros((pad, ROW), parts[0].dtype))
    return jnp.concatenate(parts, axis=0)


def _unpack(flat, shapes):
    out, r = [], 0
    for s in shapes:
        n = math.prod(s)
        out.append(flat[r:r + _rows(n)].reshape(-1)[:n].reshape(s))
        r += _rows(n)
    return out


def _pack_bf16(w, names):
    return _pack([_bf(w[n]) if n in MATMUL_WEIGHTS else lax.bitcast_convert_type(w[n], BF16) for n in names], 2 * ROW)


def _unpack_bf16(flat, names, shapes):
    sh = [s if n in MATMUL_WEIGHTS else s + (2,) for n, s in zip(names, shapes)]
    parts = _unpack(flat, sh)
    return [p if n in MATMUL_WEIGHTS else lax.bitcast_convert_type(p, F32) for n, p in zip(names, parts)]


HBM_SPEC = pl.BlockSpec(memory_space=pltpu.HBM)


def _place():
    x, y, c = lax.axis_index("x"), lax.axis_index("y"), lax.axis_index("c")
    chips = [(1 - x, y), (x, 1 - y), (1 - x, 1 - y)]
    return x, y, c, chips


def _gather_weights(shard):
    R = shard.shape[0]
    Rh = R // 2

    def body(s_ref, o_ref, send_sems, recv_sems):
        x, y, c, chips = _place()
        sib = (x, y, 1 - c)

        def half(k, hc):
            return o_ref.at[k, pl.ds(hc * Rh, Rh), :]

        def copy(j, src, dst, to):
            return pltpu.make_async_remote_copy(src_ref=src, dst_ref=dst, send_sem=send_sems.at[j],
                                                recv_sem=recv_sems.at[j], device_id=to, device_id_type=MESH)

        me = 2 * x + y
        first = [copy(j, s_ref.at[pl.ds(c * Rh, Rh), :], half(me, c), (tx, ty, c)) for j, (tx, ty) in enumerate(chips)]
        for cp in first:
            cp.start()
        passed = []
        for j, (tx, ty) in enumerate(chips):
            k = 2 * tx + ty
            copy(j, half(k, c), half(k, c), (tx, ty, c)).wait_recv()
            fw = copy(3 + j, half(k, c), half(k, c), sib)
            fw.start()
            passed.append(fw)
        for j, (tx, ty) in enumerate(chips):
            k = 2 * tx + ty
            copy(3 + j, half(k, 1 - c), half(k, 1 - c), sib).wait_recv()
        for cp in first + passed:
            cp.wait_send()

    return pl.pallas_call(
        body, name="gather_weights", out_shape=jax.ShapeDtypeStruct((N_CHIPS, R, ROW), shard.dtype),
        in_specs=[HBM_SPEC], out_specs=HBM_SPEC,
        scratch_shapes=[pltpu.SemaphoreType.DMA((6,)), pltpu.SemaphoreType.DMA((6,))],
    )(shard)


def _swap_halves(G):
    R = G.shape[1]
    Rh = R // 2

    def body(g_ref, a_ref, send_sem, recv_sem):
        x, y, c, _ = _place()
        cp = pltpu.make_async_remote_copy(src_ref=g_ref.at[:, pl.ds((1 - c) * Rh, Rh), :], dst_ref=a_ref,
                                          send_sem=send_sem, recv_sem=recv_sem, device_id=(x, y, 1 - c),
                                          device_id_type=MESH)
        cp.start()
        cp.wait()

    return pl.pallas_call(
        body, name="reduce_swap_halves", out_shape=jax.ShapeDtypeStruct((N_CHIPS, Rh, ROW), G.dtype),
        in_specs=[HBM_SPEC], out_specs=HBM_SPEC,
        scratch_shapes=[pltpu.SemaphoreType.DMA, pltpu.SemaphoreType.DMA],
    )(G)


def _exchange_chips(P):
    def body(p_ref, b_ref, send_sems, recv_sems):
        x, y, c, chips = _place()
        me = 2 * x + y

        def copy(j, src, dst, to):
            return pltpu.make_async_remote_copy(src_ref=src, dst_ref=dst, send_sem=send_sems.at[j],
                                                recv_sem=recv_sems.at[j], device_id=to, device_id_type=MESH)

        sends = [copy(j, p_ref.at[2 * tx + ty], b_ref.at[me], (tx, ty, c)) for j, (tx, ty) in enumerate(chips)]
        for cp in sends:
            cp.start()
        for j, (tx, ty) in enumerate(chips):
            k = 2 * tx + ty
            copy(j, p_ref.at[k], b_ref.at[k], (tx, ty, c)).wait_recv()
        for cp in sends:
            cp.wait_send()

    return pl.pallas_call(
        body, name="reduce_exchange_chips", out_shape=jax.ShapeDtypeStruct(P.shape, P.dtype),
        in_specs=[HBM_SPEC], out_specs=HBM_SPEC,
        scratch_shapes=[pltpu.SemaphoreType.DMA((3,)), pltpu.SemaphoreType.DMA((3,))],
    )(P)


def _swap_reduced(Fh):
    def body(f_ref, o_ref, send_sem, recv_sem):
        x, y, c, _ = _place()
        cp = pltpu.make_async_remote_copy(src_ref=f_ref, dst_ref=o_ref, send_sem=send_sem, recv_sem=recv_sem,
                                          device_id=(x, y, 1 - c), device_id_type=MESH)
        cp.start()
        cp.wait()

    return pl.pallas_call(
        body, name="reduce_swap_reduced", out_shape=jax.ShapeDtypeStruct(Fh.shape, Fh.dtype),
        in_specs=[HBM_SPEC], out_specs=HBM_SPEC,
        scratch_shapes=[pltpu.SemaphoreType.DMA, pltpu.SemaphoreType.DMA],
    )(Fh)


def _gather_small(v):
    m_per = v.shape[0]

    def body(x_ref, out_ref, send_sems, recv_sems, local_sem):
        x, y, c, chips = _place()
        me, sibling = (x, y, c), (x, y, 1 - c)

        def rows(px, py, pc):
            return out_ref.at[pl.ds((4 * px + 2 * py + pc) * m_per, m_per), :]

        def copy(k, block, to, src=None):
            return pltpu.make_async_remote_copy(src_ref=rows(*block) if src is None else src, dst_ref=rows(*block),
                                                send_sem=send_sems.at[k], recv_sem=recv_sems.at[k], device_id=to,
                                                device_id_type=MESH)

        mine = pltpu.make_async_copy(x_ref, rows(*me), local_sem)
        mine.start()
        first = [copy(0, me, sibling, src=x_ref)]
        first += [copy(1 + j, me, (*chip, c), src=x_ref) for j, chip in enumerate(chips)]
        for cp in first:
            cp.start()
        passed = [copy(4 + j, (*chip, c), sibling) for j, chip in enumerate(chips)]
        for j, chip in enumerate(chips):
            copy(1 + j, (*chip, c), me).wait_recv()
            passed[j].start()
        copy(0, sibling, me).wait_recv()
        for j, chip in enumerate(chips):
            copy(4 + j, (*chip, 1 - c), me).wait_recv()
        for cp in first + passed:
            cp.wait_send()
        mine.wait()

    return pl.pallas_call(
        body, name="gather_small", out_shape=jax.ShapeDtypeStruct((N_DEV * m_per, ROW), v.dtype),
        in_specs=[pl.BlockSpec(memory_space=pltpu.VMEM)], out_specs=pl.BlockSpec(memory_space=pltpu.VMEM),
        scratch_shapes=[pltpu.SemaphoreType.DMA((7,)), pltpu.SemaphoreType.DMA((7,)), pltpu.SemaphoreType.DMA],
    )(v)


def _sum_slots(a, n, name):
    M = a.shape[0] // n
    tr = _pick(M, (512, 256, 128, 64, 40, 8))
    nb = M // tr

    def body(*refs):
        acc = refs[0][...].astype(F32)
        for r in refs[1:-1]:
            acc = acc + r[...].astype(F32)
        refs[-1][...] = acc

    specs = [pl.BlockSpec((tr, ROW), functools.partial(lambda i, k: (k * nb + i, 0), k=k)) for k in range(n)]
    return pl.pallas_call(body, name=name, grid=(nb,), in_specs=specs,
                          out_specs=pl.BlockSpec((tr, ROW), lambda i: (i, 0)),
                          out_shape=jax.ShapeDtypeStruct((M, ROW), F32), compiler_params=_params("parallel"))(*([a] * n))


def _add(a, b, name, out_dtype):
    M = a.shape[0]
    tr = _pick(M, (512, 256, 128, 64, 40, 8))

    def body(a_ref, b_ref, o_ref):
        o_ref[...] = (a_ref[...] + b_ref[...]).astype(out_dtype)

    row = pl.BlockSpec((tr, ROW), lambda i: (i, 0))
    return pl.pallas_call(body, name=name, grid=(M // tr,), in_specs=[row, row], out_specs=row,
                          out_shape=jax.ShapeDtypeStruct((M, ROW), out_dtype), compiler_params=_params("parallel"))(a, b)


def _adamw(g, w, m, v, name):
    shp = w.shape
    N = shp[-1]
    M = math.prod(shp[:-1])
    g, w, m, v = (a.reshape(M, N) for a in (g, w, m, v))
    tr = _pick(M, tuple(t for t in (512, 256, 128, 64, 40, 32, 16, 8) if t * N <= 256 * 1024))
    c1 = 1.0 - ADAM_B1 ** ADAM_STEP
    c2 = 1.0 - ADAM_B2 ** ADAM_STEP

    def body(g_ref, w_ref, m_ref, v_ref, d_ref, nm_ref, nv_ref):
        gg = g_ref[...]
        nm = ADAM_B1 * m_ref[...] + (1.0 - ADAM_B1) * gg
        nv = ADAM_B2 * v_ref[...] + (1.0 - ADAM_B2) * (gg * gg)
        nm_ref[...] = nm
        nv_ref[...] = nv
        d_ref[...] = -ADAM_LR * ((nm / c1) / (jnp.sqrt(nv / c2) + ADAM_EPS) + ADAM_WD * w_ref[...])

    row = pl.BlockSpec((tr, N), lambda i: (i, 0))
    shape = jax.ShapeDtypeStruct((M, N), F32)
    outs = pl.pallas_call(body, name=name, grid=(M // tr,), in_specs=[row] * 4, out_specs=[row] * 3,
                          out_shape=[shape] * 3, compiler_params=_params("parallel"))(g, w, m, v)
    return [o.reshape(shp) for o in outs]


def kernel(x, meta_tokens, ln1_g, ln1_b, ln2_g, ln2_b, fox_w_in, fox_b_f, fox_w_o, swa_w_in, swa_sinks, swa_w_o, mla_w_a, mla_g_q, mla_g_kv, mla_w_uq, mla_w_ukv, mla_w_o, ffn_w_in, ffn_conv_w, ffn_conv_b, ffn_w_out, loss_target, m_meta_tokens, m_ln1_g, m_ln1_b, m_ln2_g, m_ln2_b, m_fox_w_in, m_fox_b_f, m_fox_w_o, m_swa_w_in, m_swa_sinks, m_swa_w_o, m_mla_w_a, m_mla_g_q, m_mla_g_kv, m_mla_w_uq, m_mla_w_ukv, m_mla_w_o, m_ffn_w_in, m_ffn_conv_w, m_ffn_conv_b, m_ffn_w_out, v_meta_tokens, v_ln1_g, v_ln1_b, v_ln2_g, v_ln2_b, v_fox_w_in, v_fox_b_f, v_fox_w_o, v_swa_w_in, v_swa_sinks, v_swa_w_o, v_mla_w_a, v_mla_g_q, v_mla_g_kv, v_mla_w_uq, v_mla_w_ukv, v_mla_w_o, v_ffn_w_in, v_ffn_conv_w, v_ffn_conv_b, v_ffn_w_out):
    given = dict(locals())
    w = {n: given[n] for n in WEIGHTS}
    m = {n: given["m_" + n] for n in WEIGHTS}
    v = {n: given["v_" + n] for n in WEIGHTS}
    sh_names = [n for n, _ in SHARDED]
    sh_shapes = [w[n].shape for n in sh_names]

    me = 2 * lax.axis_index("x") + lax.axis_index("y")
    c = lax.axis_index("c")
    packed = _pack_bf16(w, sh_names)
    gathered = lax.dynamic_update_index_in_dim(_gather_weights(packed), packed, me, 0)
    full = dict(w)
    per_chip = [_unpack_bf16(gathered[k], sh_names, sh_shapes) for k in range(N_CHIPS)]
    for t, (n, ax) in enumerate(SHARDED):
        full[n] = jnp.concatenate([per_chip[k][t] for k in range(N_CHIPS)], axis=ax)

    loss_part, dh0, grads = _local_step(x[0], loss_target[0], full)
    loss = lax.psum(jnp.sum(loss_part), ("x", "y", "c"))
    grad_x = dh0[BLOCK:][None]

    split = {n: jnp.split(grads[n], N_CHIPS, axis=ax) for n, ax in SHARDED}
    G = jnp.stack([_pack([split[n][k] for n in sh_names], 2 * ROW) for k in range(N_CHIPS)])
    Rh = G.shape[1] // 2
    mine = lax.dynamic_slice_in_dim(G, c * Rh, Rh, axis=1)
    P = _add(mine.reshape(N_CHIPS * Rh, ROW), _swap_halves(G).reshape(N_CHIPS * Rh, ROW), "reduce_pair_sum", BF16)
    P = P.reshape(N_CHIPS, Rh, ROW)
    B = lax.dynamic_update_index_in_dim(_exchange_chips(P), lax.dynamic_index_in_dim(P, me, 0, keepdims=False), me, 0)
    Fh = _sum_slots(B.reshape(N_CHIPS * Rh, ROW), N_CHIPS, "reduce_chip_sum")
    other = _swap_reduced(Fh)
    Fg = jnp.concatenate([jnp.where(c == 0, Fh, other), jnp.where(c == 0, other, Fh)], axis=0)
    out = {}
    for n, g_n in zip(sh_names, _unpack(Fg, sh_shapes)):
        out["grad", n] = g_n
        out["delta", n], out["new_m", n], out["new_v", n] = _adamw(g_n, w[n], m[n], v[n], "adamw_" + n)

    rp_shapes = [w[n].shape for n in REPLICATED]
    small = _gather_small(_pack([grads[n] for n in REPLICATED], 8))
    g_r = _sum_slots(small, N_DEV, "reduce_small_sum")
    d_r, m_r, v_r = _adamw(g_r, _pack([w[n] for n in REPLICATED], 8), _pack([m[n] for n in REPLICATED], 8),
                           _pack([v[n] for n in REPLICATED], 8), "adamw_replicated")

    for kind, fr in (("grad", g_r), ("delta", d_r), ("new_m", m_r), ("new_v", v_r)):
        for n, a in zip(REPLICATED, _unpack(fr, rp_shapes)):
            out[kind, n] = a
    return (loss, grad_x, *[out[k, n] for k in ("grad", "delta", "new_m", "new_v") for n in WEIGHTS])
```

```python
import functools
import math

import numpy as np
import jax
import jax.numpy as jnp
from jax import lax
from jax.experimental import pallas as pl
from jax.experimental.pallas import tpu as pltpu

F32 = jnp.float32
BF16 = jnp.bfloat16

D_MODEL = 1024
DEPTH = 4
BLOCK = 128
N_META = 16
PAD = BLOCK - N_META
NEG = -1e30
ALPHA = (2.0 * DEPTH) ** 0.25
LN_EPS = 1e-5
RMS_EPS = 1e-6
HEADS = 16
HEAD_DIM = 64
LANES = 128
SWA_KV = 2
SWA_G = HEADS // SWA_KV
WINDOW = 128
ROPE_THETA = 500000.0
ROPE_DIM = 16
MLA_Q_LORA = 384
MLA_KV_LORA = 256
MLA_NOPE = 64
MLA_ROPE = 32
MLA_ROPE_THETA = 10000.0
D_FF = 2816
ADAM_LR = 0.001
ADAM_B1 = 0.9
ADAM_B2 = 0.999
ADAM_EPS = 1e-08
ADAM_WD = 0.01
ADAM_STEP = 10
N_CHIPS = 4
N_DEV = 8
ROW = 1024
VMEM_LIMIT = 48 * 1024 * 1024
MESH = pl.DeviceIdType.MESH
LOG2E = 1.4426950408889634
DENSE_HS = 2
DENSE_HS_FWD = 4

NN = (((1,), (0,)), ((), ()))
NT = (((1,), (1,)), ((), ()))
TN = (((0,), (0,)), ((), ()))


def _pick(n, cands):
    for c in cands:
        if n % c == 0:
            return c
    return n


def _params(*sem):
    return pltpu.CompilerParams(dimension_semantics=sem, vmem_limit_bytes=VMEM_LIMIT)


def _bf(x):
    return x if x.dtype == BF16 else x.astype(BF16)


def _mm(a, b, mode, name, out_dtype=F32):
    if mode == "nn":
        (M, K), (_, N) = a.shape, b.shape
    elif mode == "nt":
        (M, K), (N, _) = a.shape, b.shape
    else:
        (K, M), (_, N) = a.shape, b.shape
    tm = _pick(M, (1664, 1408, 1024, 640, 512, 384, 256, 128))
    tn = _pick(N, (640, 512, 384, 1408, 256, 128))
    tk = K if (K <= 1024 and mode != "tn") else _pick(K, (640, 512, 384, 1408, 256, 128))
    nk = K // tk
    dn = {"nn": NN, "nt": NT, "tn": TN}[mode]

    def body(a_ref, b_ref, o_ref, *acc):
        part = lax.dot_general(_bf(a_ref[...]), _bf(b_ref[...]), dn, preferred_element_type=F32)
        if nk == 1:
            o_ref[...] = part.astype(out_dtype)
            return
        acc_ref, = acc
        k = pl.program_id(2)

        @pl.when(k == 0)
        def _():
            acc_ref[...] = part

        @pl.when(k > 0)
        def _():
            acc_ref[...] += part

        @pl.when(k == nk - 1)
        def _():
            o_ref[...] = acc_ref[...].astype(out_dtype)

    if mode == "tn":
        a_spec = pl.BlockSpec((tk, tm), lambda i, j, k: (k, i))
    else:
        a_spec = pl.BlockSpec((tm, tk), lambda i, j, k: (i, k))
    if mode == "nt":
        b_spec = pl.BlockSpec((tn, tk), lambda i, j, k: (j, k))
    else:
        b_spec = pl.BlockSpec((tk, tn), lambda i, j, k: (k, j))
    return pl.pallas_call(
        body, name=name, grid=(M // tm, N // tn, nk),
        in_specs=[a_spec, b_spec],
        out_specs=pl.BlockSpec((tm, tn), lambda i, j, k: (i, j)),
        out_shape=jax.ShapeDtypeStruct((M, N), out_dtype),
        scratch_shapes=[pltpu.VMEM((tm, tn), F32)] if nk > 1 else [],
        compiler_params=_params("parallel", "parallel", "arbitrary"),
    )(a, b)


def _ln_fwd(h, mix, g, b, name):
    L = h.shape[0]
    tr = 128

    def body(h_ref, m_ref, g_ref, b_ref, o_ref, ob_ref, xh_ref, rs_ref):
        z = ALPHA * h_ref[...] + m_ref[...]
        mu = jnp.mean(z, axis=1, keepdims=True)
        zc = z - mu
        var = jnp.mean(zc * zc, axis=1, keepdims=True)
        rstd = lax.rsqrt(var + LN_EPS)
        xh = zc * rstd
        xh_ref[...] = xh
        rs_ref[...] = rstd
        out = xh * g_ref[...] + b_ref[...]
        o_ref[...] = out
        ob_ref[...] = out.astype(BF16)

    row = pl.BlockSpec((tr, D_MODEL), lambda i: (i, 0))
    vec = pl.BlockSpec((1, D_MODEL), lambda i: (0, 0))
    return pl.pallas_call(
        body, name=name, grid=(L // tr,),
        in_specs=[row, row, vec, vec],
        out_specs=[row, row, row, pl.BlockSpec((tr, 1), lambda i: (i, 0))],
        out_shape=[jax.ShapeDtypeStruct((L, D_MODEL), F32), jax.ShapeDtypeStruct((L, D_MODEL), BF16),
                   jax.ShapeDtypeStruct((L, D_MODEL), F32), jax.ShapeDtypeStruct((L, 1), F32)],
        compiler_params=_params("parallel"),
    )(h, mix, g.reshape(1, D_MODEL), b.reshape(1, D_MODEL))


def _ln_bwd(ga, gb, xhat, rstd, g, name):
    L = xhat.shape[0]
    tr = 128
    two = ga is not None

    def body(*refs):
        if two:
            ga_ref, gb_ref, xh_ref, rs_ref, g_ref, dz_ref, dzb_ref, dg_ref, db_ref = refs
            dy = ALPHA * ga_ref[...] + gb_ref[...]
        else:
            gb_ref, xh_ref, rs_ref, g_ref, dz_ref, dzb_ref, dg_ref, db_ref = refs
            dy = gb_ref[...]
        xh = xh_ref[...]
        dxh = dy * g_ref[...]
        c1 = jnp.mean(dxh, axis=1, keepdims=True)
        c2 = jnp.mean(dxh * xh, axis=1, keepdims=True)
        dz = rs_ref[...] * (dxh - c1 - xh * c2)
        dz_ref[...] = dz
        dzb_ref[...] = dz.astype(BF16)

        @pl.when(pl.program_id(0) == 0)
        def _():
            dg_ref[...] = jnp.zeros_like(dg_ref)
            db_ref[...] = jnp.zeros_like(db_ref)

        dg_ref[...] += jnp.sum(dy * xh, axis=0, keepdims=True)
        db_ref[...] += jnp.sum(dy, axis=0, keepdims=True)

    row = pl.BlockSpec((tr, D_MODEL), lambda i: (i, 0))
    vec = pl.BlockSpec((1, D_MODEL), lambda i: (0, 0))
    ins = ([ga] if two else []) + [gb, xhat, rstd, g.reshape(1, D_MODEL)]
    specs = ([row] if two else []) + [row, row, pl.BlockSpec((tr, 1), lambda i: (i, 0)), vec]
    dz, dzb, dg, db = pl.pallas_call(
        body, name=name, grid=(L // tr,),
        in_specs=specs, out_specs=[row, row, vec, vec],
        out_shape=[jax.ShapeDtypeStruct((L, D_MODEL), F32), jax.ShapeDtypeStruct((L, D_MODEL), BF16),
                   jax.ShapeDtypeStruct((1, D_MODEL), F32), jax.ShapeDtypeStruct((1, D_MODEL), F32)],
        compiler_params=_params("arbitrary"),
    )(*ins)
    return dz, dzb, dg[0], db[0]


def _axpy(a, b, name):
    L, N = a.shape
    tr = 128

    def body(a_ref, b_ref, o_ref):
        o_ref[...] = ALPHA * a_ref[...] + b_ref[...]

    row = pl.BlockSpec((tr, N), lambda i: (i, 0))
    return pl.pallas_call(body, name=name, grid=(L // tr,), in_specs=[row, row], out_specs=row,
                          out_shape=jax.ShapeDtypeStruct((L, N), F32), compiler_params=_params("parallel"))(a, b)


def _shift_down(cur, prev8, n):
    rows = lax.broadcasted_iota(jnp.int32, cur.shape, 0)
    out = pltpu.roll(cur, n, 0)
    for r in range(n):
        out = jnp.where(rows == r, prev8[8 - n + r:8 - n + r + 1, :], out)
    return out


def _shift_up(cur, next8, n):
    tr = cur.shape[0]
    rows = lax.broadcasted_iota(jnp.int32, cur.shape, 0)
    out = pltpu.roll(cur, tr - n, 0)
    for r in range(n):
        out = jnp.where(rows == tr - n + r, next8[r:r + 1, :], out)
    return out


def _silu(x):
    return x / (1.0 + jnp.exp(-x))


def _conv(cur, prev8, cw_ref, cb_ref):
    y = cb_ref[...] + _shift_down(cur, prev8, 2) * cw_ref[0:1, :]
    y = y + _shift_down(cur, prev8, 1) * cw_ref[1:2, :]
    return y + cur * cw_ref[2:3, :]


def _valid_rows(i, tr, u_ref, up_ref):
    rows = i * tr + lax.broadcasted_iota(jnp.int32, u_ref.shape, 0)
    prow = i * tr - 8 + lax.broadcasted_iota(jnp.int32, up_ref.shape, 0)
    return jnp.where(rows >= PAD, u_ref[...], 0.0), jnp.where(prow >= PAD, up_ref[...], 0.0), rows


def _conv_glu_fwd(u, cw, cb, name):
    L, F2 = u.shape
    F = F2 // 2
    tr = 128

    def body(u_ref, up_ref, cw_ref, cb_ref, a_ref):
        cur, prev, _ = _valid_rows(pl.program_id(0), tr, u_ref, up_ref)
        y = _conv(cur, prev, cw_ref, cb_ref)
        a_ref[...] = (_silu(y[:, :F]) * y[:, F:]).astype(BF16)

    return pl.pallas_call(
        body, name=name, grid=(L // tr,),
        in_specs=[pl.BlockSpec((tr, F2), lambda i: (i, 0)),
                  pl.BlockSpec((8, F2), lambda i: (jnp.maximum(i * (tr // 8) - 1, 0), 0)),
                  pl.BlockSpec((3, F2), lambda i: (0, 0)),
                  pl.BlockSpec((1, F2), lambda i: (0, 0))],
        out_specs=pl.BlockSpec((tr, F), lambda i: (i, 0)),
        out_shape=jax.ShapeDtypeStruct((L, F), BF16),
        compiler_params=_params("parallel"),
    )(u, u, cw, cb.reshape(1, F2))


def _conv_glu_bwd(da, u, cw, cb, name):
    L, F2 = u.shape
    F = F2 // 2
    tr = 128
    nb = L // tr

    def dy_of(yv, dav):
        g, val = yv[:, :F], yv[:, F:]
        sg = 1.0 / (1.0 + jnp.exp(-g))
        dg = dav * val * (sg * (1.0 + g * (1.0 - sg)))
        dv = dav * (g * sg)
        return jnp.concatenate([dg, dv], axis=1)

    def body(da_ref, dan_ref, u_ref, up_ref, un_ref, cw_ref, cb_ref, du_ref, dcw_ref, dcb_ref):
        i = pl.program_id(0)
        cur, prev, rows = _valid_rows(i, tr, u_ref, up_ref)
        dy = dy_of(_conv(cur, prev, cw_ref, cb_ref), da_ref[...])
        yn = _conv(un_ref[...], cur[tr - 8:tr, :], cw_ref, cb_ref)
        dyn = jnp.where(i < nb - 1, dy_of(yn, dan_ref[...]), 0.0)
        du = dy * cw_ref[2:3, :] + _shift_up(dy, dyn, 1) * cw_ref[1:2, :] + _shift_up(dy, dyn, 2) * cw_ref[0:1, :]
        du_ref[...] = jnp.where(rows >= PAD, du, 0.0).astype(BF16)

        @pl.when(i == 0)
        def _():
            dcw_ref[...] = jnp.zeros_like(dcw_ref)
            dcb_ref[...] = jnp.zeros_like(dcb_ref)

        dcw_ref[0:1, :] += jnp.sum(dy * _shift_down(cur, prev, 2), axis=0, keepdims=True)
        dcw_ref[1:2, :] += jnp.sum(dy * _shift_down(cur, prev, 1), axis=0, keepdims=True)
        dcw_ref[2:3, :] += jnp.sum(dy * cur, axis=0, keepdims=True)
        dcb_ref[...] += jnp.sum(dy, axis=0, keepdims=True)

    nxt = lambda i: (jnp.minimum((i + 1) * (tr // 8), L // 8 - 1), 0)
    prv = lambda i: (jnp.maximum(i * (tr // 8) - 1, 0), 0)
    du, dcw, dcb = pl.pallas_call(
        body, name=name, grid=(nb,),
        in_specs=[pl.BlockSpec((tr, F), lambda i: (i, 0)), pl.BlockSpec((8, F), nxt),
                  pl.BlockSpec((tr, F2), lambda i: (i, 0)), pl.BlockSpec((8, F2), prv), pl.BlockSpec((8, F2), nxt),
                  pl.BlockSpec((3, F2), lambda i: (0, 0)), pl.BlockSpec((1, F2), lambda i: (0, 0))],
        out_specs=[pl.BlockSpec((tr, F2), lambda i: (i, 0)), pl.BlockSpec((3, F2), lambda i: (0, 0)),
                   pl.BlockSpec((1, F2), lambda i: (0, 0))],
        out_shape=[jax.ShapeDtypeStruct((L, F2), BF16), jax.ShapeDtypeStruct((3, F2), F32),
                   jax.ShapeDtypeStruct((1, F2), F32)],
        compiler_params=_params("arbitrary"),
    )(da, da, u, u, u, cw, cb.reshape(1, F2))
    return du, dcw, dcb[0]


def _dense_mask(qpos, kpos):
    return (kpos <= qpos) & (kpos >= PAD)


def _tables(pairs):
    qt, kt, ft = [], [], []
    for grp in pairs:
        for n, (qb, kb, msk) in enumerate(grp):
            qt.append(qb)
            kt.append(kb)
            ft.append((1 if n == 0 else 0) | (2 if n == len(grp) - 1 else 0) | (4 if msk else 0))
    return tuple(jnp.asarray(np.asarray(t, np.int32)) for t in (qt, kt, ft))


class _Dense:
    mask = staticmethod(_dense_mask)

    def __init__(self, L, pad_in_cr2):
        self.T = T = 640 if L % 640 == 0 else 128
        nb = L // T
        m = lambda qb, kb: kb == qb or (kb == 0 and not pad_in_cr2) or (qb * T < PAD)
        self.q_major = _tables([[(qb, kb, m(qb, kb)) for kb in range(qb + 1)] for qb in range(nb)])
        self.k_major = _tables([[(qb, kb, m(qb, kb)) for qb in range(kb, nb)] for kb in range(nb)])


CHUNK = 32


def _chunk_scores(raw, r, c, cr, masked, mask, T, qb, kb):
    s = raw * c
    if cr is not None:
        s = s - cr
    live = None
    if masked:
        qpos = qb * T + r * CHUNK + lax.broadcasted_iota(jnp.int32, (CHUNK, T), 0)
        kpos = kb * T + lax.broadcasted_iota(jnp.int32, (CHUNK, T), 1)
        live = mask(qpos, kpos) & (qpos >= PAD)
        s = jnp.where(live, s, NEG)
    return s, live


def _prob(s, lse, live):
    p = jnp.exp2(s - lse)
    return p if live is None else jnp.where(live, p, 0.0)


def _both(flag, fn):
    pl.when(flag != 0)(lambda: fn(True))
    pl.when(flag == 0)(lambda: fn(False))


def _flash_fwd(cfg, qa, q_off, ka, k_off, va, v_off, H, hs, scale, name, cr2=None):
    L = qa.shape[0]
    T = cfg.T
    qt, kt, ft = cfg.q_major
    npairs = qt.shape[0]
    c = scale * LOG2E
    decay = cr2 is not None
    W = hs * LANES

    def body(qt_ref, kt_ref, ft_ref, *refs):
        it = iter(refs)
        q_ref, k_ref, v_ref = next(it), next(it), next(it)
        cr_ref = next(it) if decay else None
        o_ref, lse_ref, m_sc, l_sc, acc_sc = next(it), next(it), next(it), next(it), next(it)
        n = pl.program_id(1)
        qb, kb, f = qt_ref[n], kt_ref[n], ft_ref[n]

        @pl.when((f & 1) != 0)
        def _():
            m_sc[...] = jnp.full(m_sc.shape, NEG, F32)
            l_sc[...] = jnp.zeros_like(l_sc)
            acc_sc[...] = jnp.zeros_like(acc_sc)

        def step(masked):
            for i in range(hs):
                cols = slice(i * LANES, (i + 1) * LANES)
                s = lax.dot_general(_bf(q_ref[:, cols]), _bf(k_ref[:, cols]), NT, preferred_element_type=F32) * c
                if decay:
                    s = s - cr_ref[i]
                if masked:
                    qpos = qb * T + lax.broadcasted_iota(jnp.int32, (T, T), 0)
                    kpos = kb * T + lax.broadcasted_iota(jnp.int32, (T, T), 1)
                    s = jnp.where(cfg.mask(qpos, kpos), s, NEG)
                m_prev = m_sc[i]
                m_new = jnp.maximum(m_prev, jnp.max(s, axis=1, keepdims=True))
                alpha = jnp.exp2(m_prev - m_new)
                p = jnp.exp2(s - m_new)
                l_sc[i] = alpha * l_sc[i] + jnp.sum(p, axis=1, keepdims=True)
                acc_sc[i] = alpha * acc_sc[i] + lax.dot_general(p.astype(BF16), _bf(v_ref[:, cols]), NN,
                                                                preferred_element_type=F32)
                m_sc[i] = m_new

        _both(f & 4, step)

        @pl.when((f & 2) != 0)
        def _():
            for i in range(hs):
                cols = slice(i * LANES, (i + 1) * LANES)
                l = l_sc[i]
                o_ref[:, cols] = acc_sc[i] / l
                lse_ref[:, cols] = jnp.broadcast_to(m_sc[i] + jnp.log(l) * LOG2E, (T, LANES))

    qrow = lambda off: pl.BlockSpec((T, W), lambda h, n, qt, kt, ft: (qt[n], off // hs + h))
    krow = lambda off: pl.BlockSpec((T, W), lambda h, n, qt, kt, ft: (kt[n], off // hs + h))
    ins, specs = [qa, ka, va], [qrow(q_off), krow(k_off), krow(v_off)]
    if decay:
        ins.append(cr2)
        specs.append(pl.BlockSpec((hs, 1, T), lambda h, n, qt, kt, ft: (h, 0, kt[n])))
    full = jax.ShapeDtypeStruct((L, H * LANES), F32)
    return pl.pallas_call(
        body, name=name, out_shape=[full, full],
        grid_spec=pltpu.PrefetchScalarGridSpec(
            num_scalar_prefetch=3, grid=(H // hs, npairs), in_specs=specs, out_specs=[qrow(0), qrow(0)],
            scratch_shapes=[pltpu.VMEM((hs, T, 1), F32), pltpu.VMEM((hs, T, 1), F32), pltpu.VMEM((hs, T, LANES), F32)]),
        compiler_params=_params("parallel", "arbitrary"),
    )(qt, kt, ft, *ins)


def _flash_bwd(cfg, qa, q_off, ka, k_off, va, v_off, do, lse2, delta, H, hs, scale, name, cr2=None, lane_sums=False):
    L = qa.shape[0]
    T = cfg.T
    qt, kt, ft = cfg.k_major
    npairs = qt.shape[0]
    c = scale * LOG2E
    decay = cr2 is not None
    W = hs * LANES

    def body(qt_ref, kt_ref, ft_ref, *refs):
        it = iter(refs)
        q_ref, k_ref, v_ref = next(it), next(it), next(it)
        cr_ref = next(it) if decay else None
        do_ref, lse_ref, dl_ref, dk_ref, dv_ref, dq_ref, dk_sc, dv_sc, s_sc, dp_sc, p_sc, ds_sc = (
            next(it) for _ in range(12))
        n = pl.program_id(1)
        qb, kb, f = qt_ref[n], kt_ref[n], ft_ref[n]

        @pl.when(n == 0)
        def _():
            dq_ref[...] = jnp.zeros_like(dq_ref)

        @pl.when((f & 1) != 0)
        def _():
            dk_sc[...] = jnp.zeros_like(dk_sc)
            dv_sc[...] = jnp.zeros_like(dv_sc)

        def step(masked):
            last = lax.broadcasted_iota(jnp.int32, (T, LANES), 1) == LANES - 1
            rows = pl.ds(pl.multiple_of(qb * T, T), T)
            for i in range(hs):
                cols = slice(i * LANES, (i + 1) * LANES)
                q, k, v, dob = _bf(q_ref[:, cols]), _bf(k_ref[:, cols]), _bf(v_ref[:, cols]), _bf(do_ref[:, cols])
                k1 = jnp.where(last, 1.0, k_ref[:, cols]).astype(BF16) if lane_sums else k
                q1 = jnp.where(last, 1.0, q_ref[:, cols]).astype(BF16) if lane_sums else q
                cr = cr_ref[i] if decay else None
                s_sc[i] = lax.dot_general(q, k, NT, preferred_element_type=F32)
                dp_sc[i] = lax.dot_general(dob, v, NT, preferred_element_type=F32)

                def chunk(r, carry, i=i, cr=cr):
                    rs = pl.ds(r * CHUNK, CHUNK)
                    s, live = _chunk_scores(s_sc[i, rs, :], r, c, cr, masked, cfg.mask, T, qb, kb)
                    p = _prob(s, lse_ref[rs, i * LANES:i * LANES + 1], live)
                    p_sc[i, rs, :] = p.astype(BF16)
                    ds_sc[i, rs, :] = (p * (dp_sc[i, rs, :] - dl_ref[rs, i * LANES:i * LANES + 1])).astype(BF16)
                    return carry

                for r in range(T // CHUNK):
                    chunk(r, 0)
                dv_sc[i] += lax.dot_general(p_sc[i], dob, TN, preferred_element_type=F32)
                dk_sc[i] += lax.dot_general(ds_sc[i], q1, TN, preferred_element_type=F32)
                dq_ref[rows, cols] += lax.dot_general(ds_sc[i], k1, NN, preferred_element_type=F32)

        _both(f & 4, step)

        @pl.when((f & 2) != 0)
        def _():
            for i in range(hs):
                cols = slice(i * LANES, (i + 1) * LANES)
                dk_ref[:, cols] = dk_sc[i] * scale
                dv_ref[:, cols] = dv_sc[i]

        @pl.when(n == npairs - 1)
        def _():
            dq_ref[...] = dq_ref[...] * scale

    qrow = lambda off: pl.BlockSpec((T, W), lambda h, n, qt, kt, ft: (qt[n], off // hs + h))
    krow = lambda off: pl.BlockSpec((T, W), lambda h, n, qt, kt, ft: (kt[n], off // hs + h))
    ins, specs = [qa, ka, va], [qrow(q_off), krow(k_off), krow(v_off)]
    if decay:
        ins.append(cr2)
        specs.append(pl.BlockSpec((hs, 1, T), lambda h, n, qt, kt, ft: (h, 0, kt[n])))
    ins += [do, lse2, delta]
    specs += [qrow(0), qrow(0), qrow(0)]
    full = jax.ShapeDtypeStruct((L, H * LANES), F32)
    return pl.pallas_call(
        body, name=name, out_shape=[full, full, full],
        grid_spec=pltpu.PrefetchScalarGridSpec(
            num_scalar_prefetch=3, grid=(H // hs, npairs), in_specs=specs,
            out_specs=[krow(0), krow(0), pl.BlockSpec((L, W), lambda h, n, qt, kt, ft: (0, h))],
            scratch_shapes=[pltpu.VMEM((hs, T, LANES), F32), pltpu.VMEM((hs, T, LANES), F32),
                            pltpu.VMEM((hs, T, T), F32), pltpu.VMEM((hs, T, T), F32),
                            pltpu.VMEM((hs, T, T), BF16), pltpu.VMEM((hs, T, T), BF16)]),
        compiler_params=_params("parallel", "arbitrary"),
    )(qt, kt, ft, *ins)


def _swa_parts(qb, q_ref, km_ref, kp_ref, kc_ref, vm_ref, vp_ref, vc_ref, c):
    G, B = SWA_G, BLOCK
    q = jnp.concatenate([_bf(q_ref[:, i * LANES:(i + 1) * LANES]) for i in range(G)], axis=0)
    kc = jnp.concatenate([_bf(km_ref[...]), _bf(kp_ref[...]), _bf(kc_ref[...])], axis=0)
    vc = jnp.concatenate([_bf(vm_ref[...]), _bf(vp_ref[...]), _bf(vc_ref[...])], axis=0)
    s = lax.dot_general(q, kc, NT, preferred_element_type=F32) * c
    row = lax.broadcasted_iota(jnp.int32, (G * B, 3 * B), 0)
    col = lax.broadcasted_iota(jnp.int32, (G * B, 3 * B), 1)
    qpos = qb * B + (row & (B - 1))
    kpos = jnp.where(col < B, col, jnp.where(col < 2 * B, (qb - 1) * B + col - B, qb * B + col - 2 * B))
    d = qpos - kpos
    live = ((col < B) & (kpos >= PAD) & (kpos <= qpos)) | ((col >= B) & (kpos >= B) & (d >= 0) & (d < WINDOW))
    return q, kc, vc, jnp.where(live, s, NEG), live


def _stack_col(ref):
    return jnp.concatenate([ref[:, i * LANES:i * LANES + 1] for i in range(SWA_G)], axis=0)


def _swa_specs(nqk):
    G, B = SWA_G, BLOCK
    qrow = pl.BlockSpec((B, G * LANES), lambda hk, qb: (qb, hk))
    kv = lambda off, blk: pl.BlockSpec((B, LANES), lambda hk, qb: (blk(qb), off + hk))
    zero, prev, cur = (lambda qb: 0), (lambda qb: jnp.maximum(qb - 1, 0)), (lambda qb: qb)
    keys = [kv(HEADS, zero), kv(HEADS, prev), kv(HEADS, cur)]
    vals = [kv(nqk, zero), kv(nqk, prev), kv(nqk, cur)]
    return qrow, keys, vals


def _swa_attn_fwd(qk, proj, sink2, scale, name):
    L = qk.shape[0]
    G, B = SWA_G, BLOCK
    nqk = HEADS + SWA_KV
    c = scale * LOG2E

    def body(q_ref, km_ref, kp_ref, kc_ref, vm_ref, vp_ref, vc_ref, sink_ref, o_ref, lse_ref):
        qb = pl.program_id(1)
        q, kc, vc, s, live = _swa_parts(qb, q_ref, km_ref, kp_ref, kc_ref, vm_ref, vp_ref, vc_ref, c)
        sink = jnp.concatenate([jnp.broadcast_to(sink_ref[:, i * LANES:i * LANES + 1], (B, 1)) for i in range(G)], axis=0)
        m = jnp.maximum(jnp.max(s, axis=1, keepdims=True), sink)
        p = jnp.exp2(s - m)
        l = jnp.sum(p, axis=1, keepdims=True) + jnp.exp2(sink - m)
        o = lax.dot_general(p.astype(BF16), vc, NN, preferred_element_type=F32) / l
        lse = m + jnp.log(l) * LOG2E
        for i in range(G):
            o_ref[:, i * LANES:(i + 1) * LANES] = o[i * B:(i + 1) * B]
            lse_ref[:, i * LANES:(i + 1) * LANES] = jnp.broadcast_to(lse[i * B:(i + 1) * B], (B, LANES))

    qrow, keys, vals = _swa_specs(nqk)
    shape = jax.ShapeDtypeStruct((L, HEADS * LANES), F32)
    return pl.pallas_call(
        body, name=name, grid=(SWA_KV, L // B),
        in_specs=[qrow] + keys + vals + [pl.BlockSpec((1, G * LANES), lambda hk, qb: (0, hk))],
        out_specs=[qrow, qrow], out_shape=[shape, shape],
        compiler_params=_params("parallel", "parallel"),
    )(qk, qk, qk, qk, proj, proj, proj, sink2)


def _swa_attn_bwd(qk, proj, do, lse2, delta, scale, name):
    L = qk.shape[0]
    G, B = SWA_G, BLOCK
    nqk = HEADS + SWA_KV
    c = scale * LOG2E

    def body(q_ref, km_ref, kp_ref, kc_ref, vm_ref, vp_ref, vc_ref, do_ref, lse_ref, dl_ref, dq_ref, dk_ref, dv_ref):
        qb = pl.program_id(1)

        @pl.when(qb == 0)
        def _():
            dk_ref[...] = jnp.zeros_like(dk_ref)
            dv_ref[...] = jnp.zeros_like(dv_ref)

        q, kc, vc, s, live = _swa_parts(qb, q_ref, km_ref, kp_ref, kc_ref, vm_ref, vp_ref, vc_ref, c)
        dob = jnp.concatenate([_bf(do_ref[:, i * LANES:(i + 1) * LANES]) for i in range(G)], axis=0)
        p = jnp.where(live, jnp.exp2(s - _stack_col(lse_ref)), 0.0)
        dp = lax.dot_general(dob, vc, NT, preferred_element_type=F32)
        dsb = (p * (dp - _stack_col(dl_ref))).astype(BF16)
        dq = lax.dot_general(dsb, kc, NN, preferred_element_type=F32) * scale
        for i in range(G):
            dq_ref[:, i * LANES:(i + 1) * LANES] = dq[i * B:(i + 1) * B]
        dkc = lax.dot_general(dsb, q, TN, preferred_element_type=F32) * scale
        dvc = lax.dot_general(p.astype(BF16), dob, TN, preferred_element_type=F32)
        starts = (0, pl.multiple_of(jnp.maximum(qb - 1, 0) * B, B), pl.multiple_of(qb * B, B))
        for n, st in enumerate(starts):
            dk_ref[pl.ds(st, B), :] += dkc[n * B:(n + 1) * B]
            dv_ref[pl.ds(st, B), :] += dvc[n * B:(n + 1) * B]

    qrow, keys, vals = _swa_specs(nqk)
    res = pl.BlockSpec((L, LANES), lambda hk, qb: (0, hk))
    return pl.pallas_call(
        body, name=name, grid=(SWA_KV, L // B),
        in_specs=[qrow] + keys + vals + [qrow, qrow, qrow],
        out_specs=[qrow, res, res],
        out_shape=[jax.ShapeDtypeStruct((L, HEADS * LANES), F32), jax.ShapeDtypeStruct((L, SWA_KV * LANES), F32),
                   jax.ShapeDtypeStruct((L, SWA_KV * LANES), F32)],
        compiler_params=_params("parallel", "arbitrary"),
    )(qk, qk, qk, qk, proj, proj, proj, do, lse2, delta)


def _delta(do, o, name):
    L, HW = do.shape
    tr = BLOCK

    def body(do_ref, o_ref, d_ref):
        for h in range(HW // LANES):
            cols = slice(h * LANES, (h + 1) * LANES)
            d = jnp.sum(do_ref[:, cols].astype(F32) * o_ref[:, cols], axis=1, keepdims=True)
            d_ref[:, cols] = jnp.broadcast_to(d, (tr, LANES))

    spec = pl.BlockSpec((tr, HW), lambda i: (i, 0))
    return pl.pallas_call(body, name=name, grid=(L // tr,), in_specs=[spec, spec], out_specs=spec,
                          out_shape=jax.ShapeDtypeStruct((L, HW), F32), compiler_params=_params("parallel"))(do, o)


def _sink_grad(lse2, delta, sink2, name):
    L, HW = lse2.shape
    tr = 128

    def body(lse_ref, dl_ref, s_ref, o_ref):
        @pl.when(pl.program_id(0) == 0)
        def _():
            o_ref[...] = jnp.zeros_like(o_ref)

        o_ref[...] -= jnp.sum(jnp.exp2(s_ref[...] - lse_ref[...]) * dl_ref[...], axis=0, keepdims=True)

    row = pl.BlockSpec((tr, HW), lambda i: (i, 0))
    vec = pl.BlockSpec((1, HW), lambda i: (0, 0))
    return pl.pallas_call(body, name=name, grid=(L // tr,), in_specs=[row, row, vec], out_specs=vec,
                          out_shape=jax.ShapeDtypeStruct((1, HW), F32), compiler_params=_params("arbitrary"))(
        lse2, delta, sink2)


def _tri(lower):
    r = lax.broadcasted_iota(jnp.int32, (BLOCK, BLOCK), 0)
    c = lax.broadcasted_iota(jnp.int32, (BLOCK, BLOCK), 1)
    return jnp.where((c <= r) if lower else (c >= r), 1.0, 0.0).astype(F32)


def _gate_cumsum(proj, fg_tile, b_pad, name):
    L = proj.shape[0]

    def body(fg_ref, b_ref, c_ref, carry):
        @pl.when(pl.program_id(0) == 0)
        def _():
            carry[...] = jnp.zeros_like(carry)

        x = fg_ref[...] + b_ref[...]
        lf = jnp.minimum(x, 0.0) - jnp.log(1.0 + jnp.exp(-jnp.abs(x)))
        c = jnp.dot(_tri(True), lf, precision=lax.Precision.HIGHEST, preferred_element_type=F32) + carry[...]
        c_ref[...] = c
        carry[...] = c[BLOCK - 1:BLOCK, :]

    return pl.pallas_call(
        body, name=name, grid=(L // BLOCK,),
        in_specs=[pl.BlockSpec((BLOCK, LANES), lambda i: (i, fg_tile)), pl.BlockSpec((1, LANES), lambda i: (0, 0))],
        out_specs=pl.BlockSpec((BLOCK, LANES), lambda i: (i, 0)),
        out_shape=jax.ShapeDtypeStruct((L, LANES), F32),
        scratch_shapes=[pltpu.VMEM((1, LANES), F32)],
        compiler_params=_params("arbitrary"),
    )(proj, b_pad)


def _gate_cumsum_bwd(dc, proj, fg_tile, b_pad, name):
    L = proj.shape[0]
    nb = L // BLOCK

    def body(dc_ref, fg_ref, b_ref, dfg_ref, db_ref, carry):
        @pl.when(pl.program_id(0) == 0)
        def _():
            carry[...] = jnp.zeros_like(carry)
            db_ref[...] = jnp.zeros_like(db_ref)

        dlf = jnp.dot(_tri(False), dc_ref[...], precision=lax.Precision.HIGHEST,
                      preferred_element_type=F32) + carry[...]
        carry[...] = dlf[0:1, :]
        x = fg_ref[...] + b_ref[...]
        lanes = lax.broadcasted_iota(jnp.int32, (BLOCK, LANES), 1)
        rows = (nb - 1 - pl.program_id(0)) * BLOCK + lax.broadcasted_iota(jnp.int32, (BLOCK, LANES), 0)
        dfg = jnp.where((lanes < HEADS) & (rows >= PAD), dlf / (1.0 + jnp.exp(x)), 0.0)
        dfg_ref[...] = jnp.concatenate([dfg, jnp.zeros_like(dfg)], axis=1)
        db_ref[...] += jnp.sum(dfg, axis=0, keepdims=True)

    dfg, db = pl.pallas_call(
        body, name=name, grid=(nb,),
        in_specs=[pl.BlockSpec((BLOCK, LANES), lambda i: (nb - 1 - i, 0)),
                  pl.BlockSpec((BLOCK, LANES), lambda i: (nb - 1 - i, fg_tile)),
                  pl.BlockSpec((1, LANES), lambda i: (0, 0))],
        out_specs=[pl.BlockSpec((BLOCK, 2 * LANES), lambda i: (nb - 1 - i, 0)),
                   pl.BlockSpec((1, LANES), lambda i: (0, 0))],
        out_shape=[jax.ShapeDtypeStruct((L, 2 * LANES), F32), jax.ShapeDtypeStruct((1, LANES), F32)],
        scratch_shapes=[pltpu.VMEM((1, LANES), F32)],
        compiler_params=_params("arbitrary"),
    )(dc, proj, b_pad)
    return dfg, db[0]


def _rope_tables(L, dim, theta, lane0):
    half = dim // 2
    pos = (jnp.arange(L) - PAD).astype(F32)
    inv = theta ** (-jnp.arange(0, dim, 2, dtype=F32) / dim)
    ang = pos[:, None] * inv[None, :]
    cos, sin = jnp.cos(ang), jnp.sin(ang)
    C = jnp.ones((L, LANES), F32).at[:, lane0:lane0 + half].set(cos).at[:, lane0 + half:lane0 + dim].set(cos)
    S1 = jnp.zeros((L, LANES), F32).at[:, lane0:lane0 + half].set(-sin)
    S2 = jnp.zeros((L, LANES), F32).at[:, lane0 + half:lane0 + dim].set(sin)
    return C, S1, S2


def _rot(x, C, S1, S2, R):
    return x * C + pltpu.roll(x, LANES - R, 1) * S1 + pltpu.roll(x, R, 1) * S2


def _rot_t(dy, C, S1, S2, R):
    return dy * C + pltpu.roll(dy * S1, R, 1) + pltpu.roll(dy * S2, LANES - R, 1)


def _rope(x, nt, tabs, R, name, transpose=False, shared=None, shared_tile=0, out_dtype=F32):
    L = x.shape[0]
    T = BLOCK
    fn = _rot_t if transpose else _rot

    def body(*refs):
        if shared is None:
            x_ref, c_ref, s1_ref, s2_ref, o_ref = refs
        else:
            x_ref, sh_ref, c_ref, s1_ref, s2_ref, o_ref = refs
            rs = fn(sh_ref[...], c_ref[...], s1_ref[...], s2_ref[...], R)
        for h in range(nt):
            cols = slice(h * LANES, (h + 1) * LANES)
            if shared is None:
                o_ref[:, cols] = fn(x_ref[:, cols], c_ref[...], s1_ref[...], s2_ref[...], R).astype(out_dtype)
            else:
                o_ref[:, cols] = (x_ref[:, cols] + rs).astype(out_dtype)

    wide = pl.BlockSpec((T, nt * LANES), lambda i: (i, 0))
    tab = pl.BlockSpec((T, LANES), lambda i: (i, 0))
    ins, specs = [x], [wide]
    if shared is not None:
        ins.append(shared)
        specs.append(pl.BlockSpec((T, LANES), lambda i: (i, shared_tile)))
    return pl.pallas_call(body, name=name, grid=(L // T,), in_specs=specs + [tab, tab, tab], out_specs=wide,
                          out_shape=jax.ShapeDtypeStruct((L, nt * LANES), out_dtype),
                          compiler_params=_params("parallel"))(*ins, *tabs)


def _rope_shared_bwd(dk, nt, tabs, R, name):
    L = dk.shape[0]
    tr = 128

    def body(dk_ref, c_ref, s1_ref, s2_ref, o_ref):
        acc = dk_ref[:, 0:LANES]
        for h in range(1, nt):
            acc = acc + dk_ref[:, h * LANES:(h + 1) * LANES]
        o_ref[...] = _rot_t(acc, c_ref[...], s1_ref[...], s2_ref[...], R)

    tab = pl.BlockSpec((tr, LANES), lambda i: (i, 0))
    return pl.pallas_call(body, name=name, grid=(L // tr,),
                          in_specs=[pl.BlockSpec((tr, nt * LANES), lambda i: (i, 0)), tab, tab, tab], out_specs=tab,
                          out_shape=jax.ShapeDtypeStruct((L, LANES), F32), compiler_params=_params("parallel"))(
        dk, *tabs)


def _rms_fwd(pa, gq, gkv, name):
    L = pa.shape[0]
    tr = 128
    Q, KV = MLA_Q_LORA, MLA_KV_LORA

    def body(pa_ref, gq_ref, gkv_ref, q_ref, kv_ref):
        for lo, n, g_ref, o_ref in ((0, Q, gq_ref, q_ref), (Q, KV, gkv_ref, kv_ref)):
            x = pa_ref[:, lo:lo + n]
            r = lax.rsqrt(jnp.mean(x * x, axis=1, keepdims=True) + RMS_EPS)
            o_ref[...] = (x * r * g_ref[...]).astype(BF16)

    return pl.pallas_call(
        body, name=name, grid=(L // tr,),
        in_specs=[pl.BlockSpec((tr, pa.shape[1]), lambda i: (i, 0)), pl.BlockSpec((1, Q), lambda i: (0, 0)),
                  pl.BlockSpec((1, KV), lambda i: (0, 0))],
        out_specs=[pl.BlockSpec((tr, Q), lambda i: (i, 0)), pl.BlockSpec((tr, KV), lambda i: (i, 0))],
        out_shape=[jax.ShapeDtypeStruct((L, Q), BF16), jax.ShapeDtypeStruct((L, KV), BF16)],
        compiler_params=_params("parallel"),
    )(pa, gq.reshape(1, Q), gkv.reshape(1, KV))


def _rms_bwd(pa, dq, dkv, dkr, gq, gkv, name):
    L, W = pa.shape
    tr = 128
    Q, KV = MLA_Q_LORA, MLA_KV_LORA

    def body(pa_ref, dq_ref, dkv_ref, dkr_ref, gq_ref, gkv_ref, dpa_ref, dgq_ref, dgkv_ref):
        @pl.when(pl.program_id(0) == 0)
        def _():
            dgq_ref[...] = jnp.zeros_like(dgq_ref)
            dgkv_ref[...] = jnp.zeros_like(dgkv_ref)

        for lo, n, g_ref, dy_ref, dg_ref in ((0, Q, gq_ref, dq_ref, dgq_ref), (Q, KV, gkv_ref, dkv_ref, dgkv_ref)):
            x = pa_ref[:, lo:lo + n]
            r = lax.rsqrt(jnp.mean(x * x, axis=1, keepdims=True) + RMS_EPS)
            xh = x * r
            dy = dy_ref[...]
            dxh = dy * g_ref[...]
            dpa_ref[:, lo:lo + n] = (r * (dxh - xh * jnp.mean(dxh * xh, axis=1, keepdims=True))).astype(BF16)
            dg_ref[...] += jnp.sum(dy * xh, axis=0, keepdims=True)
        dpa_ref[:, Q + KV:W] = dkr_ref[...].astype(BF16)

    vq = pl.BlockSpec((1, Q), lambda i: (0, 0))
    vkv = pl.BlockSpec((1, KV), lambda i: (0, 0))
    dpa, dgq, dgkv = pl.pallas_call(
        body, name=name, grid=(L // tr,),
        in_specs=[pl.BlockSpec((tr, W), lambda i: (i, 0)), pl.BlockSpec((tr, Q), lambda i: (i, 0)),
                  pl.BlockSpec((tr, KV), lambda i: (i, 0)), pl.BlockSpec((tr, LANES), lambda i: (i, 0)), vq, vkv],
        out_specs=[pl.BlockSpec((tr, W), lambda i: (i, 0)), vq, vkv],
        out_shape=[jax.ShapeDtypeStruct((L, W), BF16), jax.ShapeDtypeStruct((1, Q), F32),
                   jax.ShapeDtypeStruct((1, KV), F32)],
        compiler_params=_params("arbitrary"),
    )(pa, dq, dkv, dkr, gq.reshape(1, Q), gkv.reshape(1, KV))
    return dpa, dgq[0], dgkv[0]


def _loss_head(h, target, name):
    L = h.shape[0]
    tr = BLOCK
    inv = 1.0 / D_MODEL

    def body(h_ref, t_ref, loss_ref, dh_ref):
        i = pl.program_id(0)

        @pl.when(i == 0)
        def _():
            loss_ref[...] = jnp.zeros_like(loss_ref)
            dh_ref[...] = jnp.zeros_like(dh_ref)

        @pl.when(i > 0)
        def _():
            e = h_ref[...] - t_ref[...]
            dh_ref[...] = e * inv
            loss_ref[...] += jnp.sum((e * e).reshape(tr // 8, 8, D_MODEL), axis=0) * (0.5 * inv)

    row = pl.BlockSpec((tr, D_MODEL), lambda i: (i, 0))
    loss, dh = pl.pallas_call(
        body, name=name, grid=(L // tr,),
        in_specs=[row, pl.BlockSpec((tr, D_MODEL), lambda i: (jnp.maximum(i - 1, 0), 0))],
        out_specs=[pl.BlockSpec((8, D_MODEL), lambda i: (0, 0)), row],
        out_shape=[jax.ShapeDtypeStruct((8, D_MODEL), F32), jax.ShapeDtypeStruct((L, D_MODEL), F32)],
        compiler_params=_params("arbitrary"),
    )(h, target)
    return loss, dh


def _pad_heads_cols(w, nh, d, dp=LANES):
    K = w.shape[0]
    return jnp.pad(w.reshape(K, nh, d), ((0, 0), (0, 0), (0, dp - d))).reshape(K, nh * dp)


def _unpad_heads_cols(w, nh, d, dp=LANES):
    K = w.shape[0]
    return w.reshape(K, nh, dp)[:, :, :d].reshape(K, nh * d)


def _pad_heads_rows(w, nh, d):
    N = w.shape[1]
    return jnp.pad(w.reshape(nh, d, N), ((0, 0), (0, LANES - d), (0, 0))).reshape(nh * LANES, N)


def _unpad_heads_rows(w, nh, d):
    N = w.shape[1]
    return w.reshape(nh, LANES, N)[:, :d, :].reshape(nh * d, N)


def _fox_fwd(h, w_in, b_f, w_o, tag):
    L = h.shape[0]
    hd = HEADS * HEAD_DIM
    W = jnp.concatenate([_pad_heads_cols(w_in[:, i * hd:(i + 1) * hd], HEADS, HEAD_DIM) for i in range(3)]
                        + [jnp.pad(w_in[:, 3 * hd:], ((0, 0), (0, 2 * LANES - HEADS)))], axis=1)
    Wo = _pad_heads_rows(w_o, HEADS, HEAD_DIM)
    b_pad = jnp.pad(b_f, (0, LANES - HEADS)).reshape(1, LANES)
    proj = _mm(h, W, "nn", tag + "_proj")
    c = _gate_cumsum(proj, 3 * HEADS, b_pad, tag + "_cumsum")
    dead = (jnp.arange(L) < PAD)[:, None]
    cr2 = jnp.where(dead, -NEG, c[:, :HEADS] * LOG2E).T.reshape(HEADS, 1, L)
    cfg = _Dense(L, True)
    scale = HEAD_DIM ** -0.5
    o, lse2 = _flash_fwd(cfg, proj, 0, proj, HEADS, proj, 2 * HEADS, HEADS, DENSE_HS_FWD, scale, tag + "_attn", cr2=cr2)
    mix = _mm(o, Wo, "nn", tag + "_out")
    return mix, (h, W, Wo, b_pad, proj, cr2, o, lse2)


def _fox_bwd(dmix, res, tag):
    h, W, Wo, b_pad, proj, cr2, o, lse2 = res
    L = h.shape[0]
    cfg = _Dense(L, True)
    scale = HEAD_DIM ** -0.5
    dWo = _mm(o, dmix, "tn", tag + "_dwo")
    do = _mm(dmix, Wo, "nt", tag + "_do", out_dtype=BF16)
    qkv = (proj, 0, proj, HEADS, proj, 2 * HEADS)
    delta = _delta(do, o, tag + "_delta")
    dk, dv, dq = _flash_bwd(cfg, *qkv, do, lse2, delta, HEADS, DENSE_HS, scale, tag + "_bwd", cr2=cr2, lane_sums=True)
    dc = jnp.pad((dq[:, LANES - 1::LANES] - dk[:, LANES - 1::LANES]) * (1.0 / scale), ((0, 0), (0, LANES - HEADS)))
    dfg, db = _gate_cumsum_bwd(dc, proj, 3 * HEADS, b_pad, tag + "_cumsum_bwd")
    dproj = jnp.concatenate([dq.astype(BF16), dk.astype(BF16), dv.astype(BF16), dfg.astype(BF16)], axis=1)
    dW = _mm(h, dproj, "tn", tag + "_dw")
    dh = _mm(dproj, W, "nt", tag + "_dh")
    hp = HEADS * LANES
    dw_in = jnp.concatenate([_unpad_heads_cols(dW[:, i * hp:(i + 1) * hp], HEADS, HEAD_DIM) for i in range(3)]
                            + [dW[:, 3 * hp:3 * hp + HEADS]], axis=1)
    return dh, dict(w_in=dw_in, b_f=db[:HEADS], w_o=_unpad_heads_rows(dWo, HEADS, HEAD_DIM))


def _swa_fwd(h, w_in, sinks, w_o, tag):
    L = h.shape[0]
    qd, kd = HEADS * HEAD_DIM, SWA_KV * HEAD_DIM
    W = jnp.concatenate([_pad_heads_cols(w_in[:, :qd], HEADS, HEAD_DIM),
                         _pad_heads_cols(w_in[:, qd:qd + kd], SWA_KV, HEAD_DIM),
                         _pad_heads_cols(w_in[:, qd + kd:], SWA_KV, HEAD_DIM)], axis=1)
    Wo = _pad_heads_rows(w_o, HEADS, HEAD_DIM)
    sink2 = jnp.repeat(sinks * LOG2E, LANES).reshape(1, HEADS * LANES)
    tabs = _rope_tables(L, ROPE_DIM, ROPE_THETA, 0)
    proj = _mm(h, W, "nn", tag + "_proj")
    nqk = HEADS + SWA_KV
    qk = _rope(proj, nqk, tabs, ROPE_DIM // 2, tag + "_rope")
    scale = HEAD_DIM ** -0.5
    o, lse2 = _swa_attn_fwd(qk, proj, sink2, scale, tag + "_attn")
    mix = _mm(o, Wo, "nn", tag + "_out")
    return mix, (h, W, Wo, sink2, tabs, proj, qk, o, lse2)


def _swa_bwd(dmix, res, tag):
    h, W, Wo, sink2, tabs, proj, qk, o, lse2 = res
    L = h.shape[0]
    nqk = HEADS + SWA_KV
    scale = HEAD_DIM ** -0.5
    dWo = _mm(o, dmix, "tn", tag + "_dwo")
    do = _mm(dmix, Wo, "nt", tag + "_do", out_dtype=BF16)
    delta = _delta(do, o, tag + "_delta")
    dsink = _sink_grad(lse2, delta, sink2, tag + "_dsink")[0, ::LANES]
    dq, dk, dv = _swa_attn_bwd(qk, proj, do, lse2, delta, scale, tag + "_bwd")
    dqk = _rope(jnp.concatenate([dq, dk], axis=1), nqk, tabs, ROPE_DIM // 2, tag + "_rope_bwd", transpose=True,
                out_dtype=BF16)
    dproj = jnp.concatenate([dqk, dv.astype(BF16)], axis=1)
    dW = _mm(h, dproj, "tn", tag + "_dw")
    dh = _mm(dproj, W, "nt", tag + "_dh")
    hp = HEADS * LANES
    dw_in = jnp.concatenate([_unpad_heads_cols(dW[:, :hp], HEADS, HEAD_DIM),
                             _unpad_heads_cols(dW[:, hp:hp + SWA_KV * LANES], SWA_KV, HEAD_DIM),
                             _unpad_heads_cols(dW[:, hp + SWA_KV * LANES:], SWA_KV, HEAD_DIM)], axis=1)
    return dh, dict(w_in=dw_in, sinks=dsink, w_o=_unpad_heads_rows(dWo, HEADS, HEAD_DIM))


def _mla_fwd(h, w_a, g_q, g_kv, w_uq, w_ukv, w_o, tag):
    L = h.shape[0]
    Q, KV = MLA_Q_LORA, MLA_KV_LORA
    dqk = MLA_NOPE + MLA_ROPE
    kr_w = jnp.pad(w_a[:, Q + KV:], ((0, 0), (MLA_NOPE, LANES - dqk)))
    Wa = jnp.concatenate([w_a[:, :Q + KV], kr_w], axis=1)
    Wuq = _pad_heads_cols(w_uq, HEADS, dqk)
    ukv = w_ukv.reshape(KV, HEADS, MLA_NOPE + HEAD_DIM)
    Wukv = jnp.concatenate([_pad_heads_cols(ukv[:, :, :MLA_NOPE].reshape(KV, -1), HEADS, MLA_NOPE),
                            _pad_heads_cols(ukv[:, :, MLA_NOPE:].reshape(KV, -1), HEADS, HEAD_DIM)], axis=1)
    Wo = _pad_heads_rows(w_o, HEADS, HEAD_DIM)
    tabs = _rope_tables(L, MLA_ROPE, MLA_ROPE_THETA, MLA_NOPE)
    R = MLA_ROPE // 2
    pa = _mm(h, Wa, "nn", tag + "_proj")
    cqn, ckvn = _rms_fwd(pa, g_q, g_kv, tag + "_rms")
    q0 = _mm(cqn, Wuq, "nn", tag + "_uq")
    qr = _rope(q0, HEADS, tabs, R, tag + "_rope_q")
    kv0 = _mm(ckvn, Wukv, "nn", tag + "_ukv")
    kk = _rope(kv0, HEADS, tabs, R, tag + "_rope_k", shared=pa, shared_tile=(Q + KV) // LANES)
    cfg = _Dense(L, False)
    scale = dqk ** -0.5
    o, lse2 = _flash_fwd(cfg, qr, 0, kk, 0, kv0, HEADS, HEADS, DENSE_HS_FWD, scale, tag + "_attn")
    mix = _mm(o, Wo, "nn", tag + "_out")
    return mix, (h, Wa, Wuq, Wukv, Wo, g_q, g_kv, tabs, pa, cqn, ckvn, qr, kk, kv0, o, lse2)


def _mla_bwd(dmix, res, tag):
    h, Wa, Wuq, Wukv, Wo, g_q, g_kv, tabs, pa, cqn, ckvn, qr, kk, kv0, o, lse2 = res
    L = h.shape[0]
    Q, KV = MLA_Q_LORA, MLA_KV_LORA
    dqk = MLA_NOPE + MLA_ROPE
    R = MLA_ROPE // 2
    cfg = _Dense(L, False)
    scale = dqk ** -0.5
    dWo = _mm(o, dmix, "tn", tag + "_dwo")
    do = _mm(dmix, Wo, "nt", tag + "_do", out_dtype=BF16)
    delta = _delta(do, o, tag + "_delta")
    dk, dv, dqr = _flash_bwd(cfg, qr, 0, kk, 0, kv0, HEADS, do, lse2, delta, HEADS, DENSE_HS, scale, tag + "_bwd")
    dq0 = _rope(dqr, HEADS, tabs, R, tag + "_rope_q_bwd", transpose=True, out_dtype=BF16)
    dWuq = _mm(cqn, dq0, "tn", tag + "_dwuq")
    dcqn = _mm(dq0, Wuq, "nt", tag + "_dcq")
    dkv = jnp.concatenate([dk, dv], axis=1).astype(BF16)
    dWukv = _mm(ckvn, dkv, "tn", tag + "_dwukv")
    dckvn = _mm(dkv, Wukv, "nt", tag + "_dckv")
    dkr = _rope_shared_bwd(dk, HEADS, tabs, R, tag + "_rope_k_bwd")
    dpa, dgq, dgkv = _rms_bwd(pa, dcqn, dckvn, dkr, g_q, g_kv, tag + "_rms_bwd")
    dWa = _mm(h, dpa, "tn", tag + "_dw")
    dh = _mm(dpa, Wa, "nt", tag + "_dh")
    hp = HEADS * LANES
    dw_a = jnp.concatenate([dWa[:, :Q + KV], dWa[:, Q + KV + MLA_NOPE:Q + KV + dqk]], axis=1)
    dk_n = dWukv[:, :hp].reshape(KV, HEADS, LANES)[:, :, :MLA_NOPE]
    dv_n = dWukv[:, hp:].reshape(KV, HEADS, LANES)[:, :, :HEAD_DIM]
    dw_ukv = jnp.concatenate([dk_n, dv_n], axis=2).reshape(KV, HEADS * (MLA_NOPE + HEAD_DIM))
    return dh, dict(w_a=dw_a, g_q=dgq, g_kv=dgkv, w_uq=_unpad_heads_cols(dWuq, HEADS, dqk), w_ukv=dw_ukv,
                    w_o=_unpad_heads_rows(dWo, HEADS, HEAD_DIM))


MATMUL_WEIGHTS = ("fox_w_in", "fox_w_o", "swa_w_in", "swa_w_o", "mla_w_a", "mla_w_uq", "mla_w_ukv", "mla_w_o",
                  "ffn_w_in", "ffn_w_out")


def _local_step(x, target, w):
    w = {k: (_bf(v) if k in MATMUL_WEIGHTS else v) for k, v in w.items()}
    h = jnp.concatenate([jnp.zeros((PAD, D_MODEL), F32), w["meta_tokens"], x], axis=0)
    hb = h.astype(BF16)
    saved = []
    for i in range(DEPTH):
        kind, j = i % 3, i // 3
        tag = "l%d" % i
        if kind == 0:
            mix, mres = _fox_fwd(hb, w["fox_w_in"][j], w["fox_b_f"][j], w["fox_w_o"][j], tag + "_fox")
        elif kind == 1:
            mix, mres = _swa_fwd(hb, w["swa_w_in"][j], w["swa_sinks"][j], w["swa_w_o"][j], tag + "_swa")
        else:
            mix, mres = _mla_fwd(hb, w["mla_w_a"][j], w["mla_g_q"][j], w["mla_g_kv"][j], w["mla_w_uq"][j],
                                 w["mla_w_ukv"][j], w["mla_w_o"][j], tag + "_mla")
        h1, h1b, xh1, rs1 = _ln_fwd(h, mix, w["ln1_g"][i], w["ln1_b"][i], tag + "_ln1")
        u = _mm(h1b, w["ffn_w_in"][i], "nn", tag + "_ffn_in")
        a = _conv_glu_fwd(u, w["ffn_conv_w"][i], w["ffn_conv_b"][i], tag + "_conv")
        ffn = _mm(a, w["ffn_w_out"][i], "nn", tag + "_ffn_out")
        h2, h2b, xh2, rs2 = _ln_fwd(h1, ffn, w["ln2_g"][i], w["ln2_b"][i], tag + "_ln2")
        saved.append((mres, xh1, rs1, h1b, u, a, xh2, rs2))
        h, hb = h2, h2b
    loss, dh = _loss_head(h, target, "loss_head")

    g = {k: [None] * v.shape[0] for k, v in w.items() if k != "meta_tokens"}
    ga = None
    for i in reversed(range(DEPTH)):
        kind, j = i % 3, i // 3
        tag = "l%d" % i
        mres, xh1, rs1, h1b, u, a, xh2, rs2 = saved[i]
        dz2, dz2b, g["ln2_g"][i], g["ln2_b"][i] = _ln_bwd(ga, dh, xh2, rs2, w["ln2_g"][i], tag + "_ln2_bwd")
        g["ffn_w_out"][i] = _mm(a, dz2b, "tn", tag + "_dw_out")
        da = _mm(dz2b, w["ffn_w_out"][i], "nt", tag + "_da")
        du, g["ffn_conv_w"][i], g["ffn_conv_b"][i] = _conv_glu_bwd(da, u, w["ffn_conv_w"][i], w["ffn_conv_b"][i],
                                                                   tag + "_conv_bwd")
        g["ffn_w_in"][i] = _mm(h1b, du, "tn", tag + "_dw_in")
        dh1 = _mm(du, w["ffn_w_in"][i], "nt", tag + "_dh1")
        dz1, dz1b, g["ln1_g"][i], g["ln1_b"][i] = _ln_bwd(dz2, dh1, xh1, rs1, w["ln1_g"][i], tag + "_ln1_bwd")
        if kind == 0:
            dh, mg = _fox_bwd(dz1b, mres, tag + "_fox")
            pre = "fox_"
        elif kind == 1:
            dh, mg = _swa_bwd(dz1b, mres, tag + "_swa")
            pre = "swa_"
        else:
            dh, mg = _mla_bwd(dz1b, mres, tag + "_mla")
            pre = "mla_"
        for k, v in mg.items():
            g[pre + k][j] = v
        ga = dz1
    dh0 = _axpy(ga, dh, "dh0")
    grads = {k: jnp.stack(v) for k, v in g.items()}
    grads["meta_tokens"] = dh0[PAD:BLOCK]
    return loss, dh0, grads


SHARDED = (("meta_tokens", 1), ("fox_w_in", 2), ("fox_w_o", 1), ("swa_w_in", 2), ("swa_w_o", 1), ("mla_w_a", 1),
           ("mla_g_q", 1), ("mla_g_kv", 1), ("mla_w_uq", 2), ("mla_w_ukv", 2), ("mla_w_o", 1), ("ffn_w_in", 2),
           ("ffn_conv_w", 2), ("ffn_w_out", 1))
REPLICATED = ("ln1_g", "ln1_b", "ln2_g", "ln2_b", "fox_b_f", "swa_sinks", "ffn_conv_b")
WEIGHTS = ("meta_tokens", "ln1_g", "ln1_b", "ln2_g", "ln2_b", "fox_w_in", "fox_b_f", "fox_w_o", "swa_w_in",
           "swa_sinks", "swa_w_o", "mla_w_a", "mla_g_q", "mla_g_kv", "mla_w_uq", "mla_w_ukv", "mla_w_o", "ffn_w_in",
           "ffn_conv_w", "ffn_conv_b", "ffn_w_out")


def _rows(n):
    return -(-n // ROW)


def _pack(arrs, multiple):
    parts = []
    for a in arrs:
        n = math.prod(a.shape)
        parts.append(jnp.pad(a.reshape(-1), (0, _rows(n) * ROW - n)).reshape(-1, ROW))
    total = sum(p.shape[0] for p in parts)
    pad = -total % multiple
    if pad:
        parts.append(jnp.zeros((pad, ROW), parts[0].dtype))
    return jnp.concatenate(parts, axis=0)


def _unpack(flat, shapes):
    out, r = [], 0
    for s in shapes:
        n = math.prod(s)
        out.append(flat[r:r + _rows(n)].reshape(-1)[:n].reshape(s))
        r += _rows(n)
    return out


def _pack_bf16(w, names):
    return _pack([_bf(w[n]) if n in MATMUL_WEIGHTS else lax.bitcast_convert_type(w[n], BF16) for n in names], 2 * ROW)


def _unpack_bf16(flat, names, shapes):
    sh = [s if n in MATMUL_WEIGHTS else s + (2,) for n, s in zip(names, shapes)]
    parts = _unpack(flat, sh)
    return [p if n in MATMUL_WEIGHTS else lax.bitcast_convert_type(p, F32) for n, p in zip(names, parts)]


HBM_SPEC = pl.BlockSpec(memory_space=pltpu.HBM)


def _place():
    x, y, c = lax.axis_index("x"), lax.axis_index("y"), lax.axis_index("c")
    chips = [(1 - x, y), (x, 1 - y), (1 - x, 1 - y)]
    return x, y, c, chips


def _gather_weights(shard):
    R = shard.shape[0]
    Rh = R // 2

    def body(s_ref, o_ref, send_sems, recv_sems):
        x, y, c, chips = _place()
        sib = (x, y, 1 - c)

        def half(k, hc):
            return o_ref.at[k, pl.ds(hc * Rh, Rh), :]

        def copy(j, src, dst, to):
            return pltpu.make_async_remote_copy(src_ref=src, dst_ref=dst, send_sem=send_sems.at[j],
                                                recv_sem=recv_sems.at[j], device_id=to, device_id_type=MESH)

        me = 2 * x + y
        first = [copy(j, s_ref.at[pl.ds(c * Rh, Rh), :], half(me, c), (tx, ty, c)) for j, (tx, ty) in enumerate(chips)]
        for cp in first:
            cp.start()
        passed = []
        for j, (tx, ty) in enumerate(chips):
            k = 2 * tx + ty
            copy(j, half(k, c), half(k, c), (tx, ty, c)).wait_recv()
            fw = copy(3 + j, half(k, c), half(k, c), sib)
            fw.start()
            passed.append(fw)
        for j, (tx, ty) in enumerate(chips):
            k = 2 * tx + ty
            copy(3 + j, half(k, 1 - c), half(k, 1 - c), sib).wait_recv()
        for cp in first + passed:
            cp.wait_send()

    return pl.pallas_call(
        body, name="gather_weights", out_shape=jax.ShapeDtypeStruct((N_CHIPS, R, ROW), shard.dtype),
        in_specs=[HBM_SPEC], out_specs=HBM_SPEC,
        scratch_shapes=[pltpu.SemaphoreType.DMA((6,)), pltpu.SemaphoreType.DMA((6,))],
    )(shard)


def _swap_halves(G):
    R = G.shape[1]
    Rh = R // 2

    def body(g_ref, a_ref, send_sem, recv_sem):
        x, y, c, _ = _place()
        cp = pltpu.make_async_remote_copy(src_ref=g_ref.at[:, pl.ds((1 - c) * Rh, Rh), :], dst_ref=a_ref,
                                          send_sem=send_sem, recv_sem=recv_sem, device_id=(x, y, 1 - c),
                                          device_id_type=MESH)
        cp.start()
        cp.wait()

    return pl.pallas_call(
        body, name="reduce_swap_halves", out_shape=jax.ShapeDtypeStruct((N_CHIPS, Rh, ROW), G.dtype),
        in_specs=[HBM_SPEC], out_specs=HBM_SPEC,
        scratch_shapes=[pltpu.SemaphoreType.DMA, pltpu.SemaphoreType.DMA],
    )(G)


def _exchange_chips(P):
    def body(p_ref, b_ref, send_sems, recv_sems):
        x, y, c, chips = _place()
        me = 2 * x + y

        def copy(j, src, dst, to):
            return pltpu.make_async_remote_copy(src_ref=src, dst_ref=dst, send_sem=send_sems.at[j],
                                                recv_sem=recv_sems.at[j], device_id=to, device_id_type=MESH)

        sends = [copy(j, p_ref.at[2 * tx + ty], b_ref.at[me], (tx, ty, c)) for j, (tx, ty) in enumerate(chips)]
        for cp in sends:
            cp.start()
        for j, (tx, ty) in enumerate(chips):
            k = 2 * tx + ty
            copy(j, p_ref.at[k], b_ref.at[k], (tx, ty, c)).wait_recv()
        for cp in sends:
            cp.wait_send()

    return pl.pallas_call(
        body, name="reduce_exchange_chips", out_shape=jax.ShapeDtypeStruct(P.shape, P.dtype),
        in_specs=[HBM_SPEC], out_specs=HBM_SPEC,
        scratch_shapes=[pltpu.SemaphoreType.DMA((3,)), pltpu.SemaphoreType.DMA((3,))],
    )(P)


def _swap_reduced(Fh):
    def body(f_ref, o_ref, send_sem, recv_sem):
        x, y, c, _ = _place()
        cp = pltpu.make_async_remote_copy(src_ref=f_ref, dst_ref=o_ref, send_sem=send_sem, recv_sem=recv_sem,
                                          device_id=(x, y, 1 - c), device_id_type=MESH)
        cp.start()
        cp.wait()

    return pl.pallas_call(
        body, name="reduce_swap_reduced", out_shape=jax.ShapeDtypeStruct(Fh.shape, Fh.dtype),
        in_specs=[HBM_SPEC], out_specs=HBM_SPEC,
        scratch_shapes=[pltpu.SemaphoreType.DMA, pltpu.SemaphoreType.DMA],
    )(Fh)


def _gather_small(v):
    m_per = v.shape[0]

    def body(x_ref, out_ref, send_sems, recv_sems, local_sem):
        x, y, c, chips = _place()
        me, sibling = (x, y, c), (x, y, 1 - c)

        def rows(px, py, pc):
            return out_ref.at[pl.ds((4 * px + 2 * py + pc) * m_per, m_per), :]

        def copy(k, block, to, src=None):
            return pltpu.make_async_remote_copy(src_ref=rows(*block) if src is None else src, dst_ref=rows(*block),
                                                send_sem=send_sems.at[k], recv_sem=recv_sems.at[k], device_id=to,
                                                device_id_type=MESH)

        mine = pltpu.make_async_copy(x_ref, rows(*me), local_sem)
        mine.start()
        first = [copy(0, me, sibling, src=x_ref)]
        first += [copy(1 + j, me, (*chip, c), src=x_ref) for j, chip in enumerate(chips)]
        for cp in first:
            cp.start()
        passed = [copy(4 + j, (*chip, c), sibling) for j, chip in enumerate(chips)]
        for j, chip in enumerate(chips):
            copy(1 + j, (*chip, c), me).wait_recv()
            passed[j].start()
        copy(0, sibling, me).wait_recv()
        for j, chip in enumerate(chips):
            copy(4 + j, (*chip, 1 - c), me).wait_recv()
        for cp in first + passed:
            cp.wait_send()
        mine.wait()

    return pl.pallas_call(
        body, name="gather_small", out_shape=jax.ShapeDtypeStruct((N_DEV * m_per, ROW), v.dtype),
        in_specs=[pl.BlockSpec(memory_space=pltpu.VMEM)], out_specs=pl.BlockSpec(memory_space=pltpu.VMEM),
        scratch_shapes=[pltpu.SemaphoreType.DMA((7,)), pltpu.SemaphoreType.DMA((7,)), pltpu.SemaphoreType.DMA],
    )(v)


def _sum_slots(a, n, name):
    M = a.shape[0] // n
    tr = _pick(M, (512, 256, 128, 64, 40, 8))
    nb = M // tr

    def body(*refs):
        acc = refs[0][...].astype(F32)
        for r in refs[1:-1]:
            acc = acc + r[...].astype(F32)
        refs[-1][...] = acc

    specs = [pl.BlockSpec((tr, ROW), functools.partial(lambda i, k: (k * nb + i, 0), k=k)) for k in range(n)]
    return pl.pallas_call(body, name=name, grid=(nb,), in_specs=specs,
                          out_specs=pl.BlockSpec((tr, ROW), lambda i: (i, 0)),
                          out_shape=jax.ShapeDtypeStruct((M, ROW), F32), compiler_params=_params("parallel"))(*([a] * n))


def _add(a, b, name, out_dtype):
    M = a.shape[0]
    tr = _pick(M, (512, 256, 128, 64, 40, 8))

    def body(a_ref, b_ref, o_ref):
        o_ref[...] = (a_ref[...] + b_ref[...]).astype(out_dtype)

    row = pl.BlockSpec((tr, ROW), lambda i: (i, 0))
    return pl.pallas_call(body, name=name, grid=(M // tr,), in_specs=[row, row], out_specs=row,
                          out_shape=jax.ShapeDtypeStruct((M, ROW), out_dtype), compiler_params=_params("parallel"))(a, b)


def _adamw(g, w, m, v, name):
    shp = w.shape
    N = shp[-1]
    M = math.prod(shp[:-1])
    g, w, m, v = (a.reshape(M, N) for a in (g, w, m, v))
    tr = _pick(M, tuple(t for t in (512, 256, 128, 64, 40, 32, 16, 8) if t * N <= 256 * 1024))
    c1 = 1.0 - ADAM_B1 ** ADAM_STEP
    c2 = 1.0 - ADAM_B2 ** ADAM_STEP

    def body(g_ref, w_ref, m_ref, v_ref, d_ref, nm_ref, nv_ref):
        gg = g_ref[...]
        nm = ADAM_B1 * m_ref[...] + (1.0 - ADAM_B1) * gg
        nv = ADAM_B2 * v_ref[...] + (1.0 - ADAM_B2) * (gg * gg)
        nm_ref[...] = nm
        nv_ref[...] = nv
        d_ref[...] = -ADAM_LR * ((nm / c1) / (jnp.sqrt(nv / c2) + ADAM_EPS) + ADAM_WD * w_ref[...])

    row = pl.BlockSpec((tr, N), lambda i: (i, 0))
    shape = jax.ShapeDtypeStruct((M, N), F32)
    outs = pl.pallas_call(body, name=name, grid=(M // tr,), in_specs=[row] * 4, out_specs=[row] * 3,
                          out_shape=[shape] * 3, compiler_params=_params("parallel"))(g, w, m, v)
    return [o.reshape(shp) for o in outs]


def kernel(x, meta_tokens, ln1_g, ln1_b, ln2_g, ln2_b, fox_w_in, fox_b_f, fox_w_o, swa_w_in, swa_sinks, swa_w_o, mla_w_a, mla_g_q, mla_g_kv, mla_w_uq, mla_w_ukv, mla_w_o, ffn_w_in, ffn_conv_w, ffn_conv_b, ffn_w_out, loss_target, m_meta_tokens, m_ln1_g, m_ln1_b, m_ln2_g, m_ln2_b, m_fox_w_in, m_fox_b_f, m_fox_w_o, m_swa_w_in, m_swa_sinks, m_swa_w_o, m_mla_w_a, m_mla_g_q, m_mla_g_kv, m_mla_w_uq, m_mla_w_ukv, m_mla_w_o, m_ffn_w_in, m_ffn_conv_w, m_ffn_conv_b, m_ffn_w_out, v_meta_tokens, v_ln1_g, v_ln1_b, v_ln2_g, v_ln2_b, v_fox_w_in, v_fox_b_f, v_fox_w_o, v_swa_w_in, v_swa_sinks, v_swa_w_o, v_mla_w_a, v_mla_g_q, v_mla_g_kv, v_mla_w_uq, v_mla_w_ukv, v_mla_w_o, v_ffn_w_in, v_ffn_conv_w, v_ffn_conv_b, v_ffn_w_out):
    given = dict(locals())
    w = {n: given[n] for n in WEIGHTS}
    m = {n: given["m_" + n] for n in WEIGHTS}
    v = {n: given["v_" + n] for n in WEIGHTS}
    sh_names = [n for n, _ in SHARDED]
    sh_shapes = [w[n].shape for n in sh_names]

    me = 2 * lax.axis_index("x") + lax.axis_index("y")
    c = lax.axis_index("c")
    packed = _pack_bf16(w, sh_names)
    gathered = lax.dynamic_update_index_in_dim(_gather_weights(packed), packed, me, 0)
    full = dict(w)
    per_chip = [_unpack_bf16(gathered[k], sh_names, sh_shapes) for k in range(N_CHIPS)]
    for t, (n, ax) in enumerate(SHARDED):
        full[n] = jnp.concatenate([per_chip[k][t] for k in range(N_CHIPS)], axis=ax)

    loss_part, dh0, grads = _local_step(x[0], loss_target[0], full)
    loss = lax.psum(jnp.sum(loss_part), ("x", "y", "c"))
    grad_x = dh0[BLOCK:][None]

    split = {n: jnp.split(grads[n], N_CHIPS, axis=ax) for n, ax in SHARDED}
    G = jnp.stack([_pack([split[n][k] for n in sh_names], 2 * ROW) for k in range(N_CHIPS)])
    Rh = G.shape[1] // 2
    mine = lax.dynamic_slice_in_dim(G, c * Rh, Rh, axis=1)
    P = _add(mine.reshape(N_CHIPS * Rh, ROW), _swap_halves(G).reshape(N_CHIPS * Rh, ROW), "reduce_pair_sum", BF16)
    P = P.reshape(N_CHIPS, Rh, ROW)
    B = lax.dynamic_update_index_in_dim(_exchange_chips(P), lax.dynamic_index_in_dim(P, me, 0, keepdims=False), me, 0)
    Fh = _sum_slots(B.reshape(N_CHIPS * Rh, ROW), N_CHIPS, "reduce_chip_sum")
    other = _swap_reduced(Fh)
    Fg = jnp.concatenate([jnp.where(c == 0, Fh, other), jnp.where(c == 0, other, Fh)], axis=0)
    out = {}
    for n, g_n in zip(sh_names, _unpack(Fg, sh_shapes)):
        out["grad", n] = g_n
        out["delta", n], out["new_m", n], out["new_v", n] = _adamw(g_n, w[n], m[n], v[n], "adamw_" + n)

    rp_shapes = [w[n].shape for n in REPLICATED]
    small = _gather_small(_pack([grads[n] for n in REPLICATED], 8))
    g_r = _sum_slots(small, N_DEV, "reduce_small_sum")
    d_r, m_r, v_r = _adamw(g_r, _pack([w[n] for n in REPLICATED], 8), _pack([m[n] for n in REPLICATED], 8),
                           _pack([v[n] for n in REPLICATED], 8), "adamw_replicated")

    for kind, fr in (("grad", g_r), ("delta", d_r), ("new_m", m_r), ("new_v", v_r)):
        for n, a in zip(REPLICATED, _unpack(fr, rp_shapes)):
            out[kind, n] = a
    return (loss, grad_x, *[out[k, n] for k in ("grad", "delta", "new_m", "new_v") for n in WEIGHTS])
```

```python
import functools
import math

import numpy as np
import jax
import jax.numpy as jnp
from jax import lax
from jax.experimental import pallas as pl
from jax.experimental.pallas import tpu as pltpu

F32 = jnp.float32
BF16 = jnp.bfloat16

D_MODEL = 1024
DEPTH = 4
BLOCK = 128
N_META = 16
PAD = BLOCK - N_META
NEG = -1e30
ALPHA = (2.0 * DEPTH) ** 0.25
LN_EPS = 1e-5
RMS_EPS = 1e-6
HEADS = 16
HEAD_DIM = 64
LANES = 128
SWA_KV = 2
SWA_G = HEADS // SWA_KV
WINDOW = 128
ROPE_THETA = 500000.0
ROPE_DIM = 16
MLA_Q_LORA = 384
MLA_KV_LORA = 256
MLA_NOPE = 64
MLA_ROPE = 32
MLA_ROPE_THETA = 10000.0
D_FF = 2816
ADAM_LR = 0.001
ADAM_B1 = 0.9
ADAM_B2 = 0.999
ADAM_EPS = 1e-08
ADAM_WD = 0.01
ADAM_STEP = 10
N_CHIPS = 4
N_DEV = 8
ROW = 1024
VMEM_LIMIT = 48 * 1024 * 1024
MESH = pl.DeviceIdType.MESH
LOG2E = 1.4426950408889634
DENSE_HS = 2
DENSE_HS_FWD = 4

NN = (((1,), (0,)), ((), ()))
NT = (((1,), (1,)), ((), ()))
TN = (((0,), (0,)), ((), ()))


def _pick(n, cands):
    for c in cands:
        if n % c == 0:
            return c
    return n


def _params(*sem):
    return pltpu.CompilerParams(dimension_semantics=sem, vmem_limit_bytes=VMEM_LIMIT)


def _bf(x):
    return x if x.dtype == BF16 else x.astype(BF16)


def _mm(a, b, mode, name, out_dtype=F32):
    if mode == "nn":
        (M, K), (_, N) = a.shape, b.shape
    elif mode == "nt":
        (M, K), (N, _) = a.shape, b.shape
    else:
        (K, M), (_, N) = a.shape, b.shape
    tm = _pick(M, (1664, 1408, 1024, 640, 512, 384, 256, 128))
    tn = _pick(N, (640, 512, 384, 1408, 256, 128))
    tk = K if (K <= 1024 and mode != "tn") else _pick(K, (640, 512, 384, 1408, 256, 128))
    nk = K // tk
    dn = {"nn": NN, "nt": NT, "tn": TN}[mode]

    def body(a_ref, b_ref, o_ref, *acc):
        part = lax.dot_general(_bf(a_ref[...]), _bf(b_ref[...]), dn, preferred_element_type=F32)
        if nk == 1:
            o_ref[...] = part.astype(out_dtype)
            return
        acc_ref, = acc
        k = pl.program_id(2)

        @pl.when(k == 0)
        def _():
            acc_ref[...] = part

        @pl.when(k > 0)
        def _():
            acc_ref[...] += part

        @pl.when(k == nk - 1)
        def _():
            o_ref[...] = acc_ref[...].astype(out_dtype)

    if mode == "tn":
        a_spec = pl.BlockSpec((tk, tm), lambda i, j, k: (k, i))
    else:
        a_spec = pl.BlockSpec((tm, tk), lambda i, j, k: (i, k))
    if mode == "nt":
        b_spec = pl.BlockSpec((tn, tk), lambda i, j, k: (j, k))
    else:
        b_spec = pl.BlockSpec((tk, tn), lambda i, j, k: (k, j))
    return pl.pallas_call(
        body, name=name, grid=(M // tm, N // tn, nk),
        in_specs=[a_spec, b_spec],
        out_specs=pl.BlockSpec((tm, tn), lambda i, j, k: (i, j)),
        out_shape=jax.ShapeDtypeStruct((M, N), out_dtype),
        scratch_shapes=[pltpu.VMEM((tm, tn), F32)] if nk > 1 else [],
        compiler_params=_params("parallel", "parallel", "arbitrary"),
    )(a, b)


def _ln_fwd(h, mix, g, b, name):
    L = h.shape[0]
    tr = 128

    def body(h_ref, m_ref, g_ref, b_ref, o_ref, ob_ref, xh_ref, rs_ref):
        z = ALPHA * h_ref[...] + m_ref[...]
        mu = jnp.mean(z, axis=1, keepdims=True)
        zc = z - mu
        var = jnp.mean(zc * zc, axis=1, keepdims=True)
        rstd = lax.rsqrt(var + LN_EPS)
        xh = zc * rstd
        xh_ref[...] = xh
        rs_ref[...] = rstd
        out = xh * g_ref[...] + b_ref[...]
        o_ref[...] = out
        ob_ref[...] = out.astype(BF16)

    row = pl.BlockSpec((tr, D_MODEL), lambda i: (i, 0))
    vec = pl.BlockSpec((1, D_MODEL), lambda i: (0, 0))
    return pl.pallas_call(
        body, name=name, grid=(L // tr,),
        in_specs=[row, row, vec, vec],
        out_specs=[row, row, row, pl.BlockSpec((tr, 1), lambda i: (i, 0))],
        out_shape=[jax.ShapeDtypeStruct((L, D_MODEL), F32), jax.ShapeDtypeStruct((L, D_MODEL), BF16),
                   jax.ShapeDtypeStruct((L, D_MODEL), F32), jax.ShapeDtypeStruct((L, 1), F32)],
        compiler_params=_params("parallel"),
    )(h, mix, g.reshape(1, D_MODEL), b.reshape(1, D_MODEL))


def _ln_bwd(ga, gb, xhat, rstd, g, name):
    L = xhat.shape[0]
    tr = 128
    two = ga is not None

    def body(*refs):
        if two:
            ga_ref, gb_ref, xh_ref, rs_ref, g_ref, dz_ref, dzb_ref, dg_ref, db_ref = refs
            dy = ALPHA * ga_ref[...] + gb_ref[...]
        else:
            gb_ref, xh_ref, rs_ref, g_ref, dz_ref, dzb_ref, dg_ref, db_ref = refs
            dy = gb_ref[...]
        xh = xh_ref[...]
        dxh = dy * g_ref[...]
        c1 = jnp.mean(dxh, axis=1, keepdims=True)
        c2 = jnp.mean(dxh * xh, axis=1, keepdims=True)
        dz = rs_ref[...] * (dxh - c1 - xh * c2)
        dz_ref[...] = dz
        dzb_ref[...] = dz.astype(BF16)

        @pl.when(pl.program_id(0) == 0)
        def _():
            dg_ref[...] = jnp.zeros_like(dg_ref)
            db_ref[...] = jnp.zeros_like(db_ref)

        dg_ref[...] += jnp.sum(dy * xh, axis=0, keepdims=True)
        db_ref[...] += jnp.sum(dy, axis=0, keepdims=True)

    row = pl.BlockSpec((tr, D_MODEL), lambda i: (i, 0))
    vec = pl.BlockSpec((1, D_MODEL), lambda i: (0, 0))
    ins = ([ga] if two else []) + [gb, xhat, rstd, g.reshape(1, D_MODEL)]
    specs = ([row] if two else []) + [row, row, pl.BlockSpec((tr, 1), lambda i: (i, 0)), vec]
    dz, dzb, dg, db = pl.pallas_call(
        body, name=name, grid=(L // tr,),
        in_specs=specs, out_specs=[row, row, vec, vec],
        out_shape=[jax.ShapeDtypeStruct((L, D_MODEL), F32), jax.ShapeDtypeStruct((L, D_MODEL), BF16),
                   jax.ShapeDtypeStruct((1, D_MODEL), F32), jax.ShapeDtypeStruct((1, D_MODEL), F32)],
        compiler_params=_params("arbitrary"),
    )(*ins)
    return dz, dzb, dg[0], db[0]


def _axpy(a, b, name):
    L, N = a.shape
    tr = 128

    def body(a_ref, b_ref, o_ref):
        o_ref[...] = ALPHA * a_ref[...] + b_ref[...]

    row = pl.BlockSpec((tr, N), lambda i: (i, 0))
    return pl.pallas_call(body, name=name, grid=(L // tr,), in_specs=[row, row], out_specs=row,
                          out_shape=jax.ShapeDtypeStruct((L, N), F32), compiler_params=_params("parallel"))(a, b)


def _shift_down(cur, prev8, n):
    out = pltpu.roll(cur, n, 0)
    rows = lax.broadcasted_iota(jnp.int32, (8, cur.shape[1]), 0)
    top = out[0:8]
    for r in range(n):
        top = jnp.where(rows == r, prev8[8 - n + r:8 - n + r + 1, :], top)
    return top if cur.shape[0] == 8 else jnp.concatenate([top, out[8:]], axis=0)


def _shift_up(cur, next8, n):
    tr = cur.shape[0]
    out = pltpu.roll(cur, tr - n, 0)
    rows = lax.broadcasted_iota(jnp.int32, (8, cur.shape[1]), 0)
    bottom = out[tr - 8:tr]
    for r in range(n):
        bottom = jnp.where(rows == 8 - n + r, next8[r:r + 1, :], bottom)
    return jnp.concatenate([out[:tr - 8], bottom], axis=0)


def _silu(x):
    return x / (1.0 + jnp.exp(-x))


def _conv(cur, prev8, cw_ref, cb_ref):
    d2, d1 = _shift_down(cur, prev8, 2), _shift_down(cur, prev8, 1)
    return cb_ref[...] + d2 * cw_ref[0:1, :] + d1 * cw_ref[1:2, :] + cur * cw_ref[2:3, :], d2, d1


def _valid_rows(i, tr, u_ref, up_ref):
    rows = i * tr + lax.broadcasted_iota(jnp.int32, u_ref.shape, 0)
    prow = i * tr - 8 + lax.broadcasted_iota(jnp.int32, up_ref.shape, 0)
    return jnp.where(rows >= PAD, u_ref[...], 0.0), jnp.where(prow >= PAD, up_ref[...], 0.0), rows


def _conv_glu_fwd(u, cw, cb, name):
    L, F2 = u.shape
    F = F2 // 2
    tr = 128

    def body(u_ref, up_ref, cw_ref, cb_ref, a_ref):
        cur, prev, _ = _valid_rows(pl.program_id(0), tr, u_ref, up_ref)
        y, _, _ = _conv(cur, prev, cw_ref, cb_ref)
        a_ref[...] = (_silu(y[:, :F]) * y[:, F:]).astype(BF16)

    return pl.pallas_call(
        body, name=name, grid=(L // tr,),
        in_specs=[pl.BlockSpec((tr, F2), lambda i: (i, 0)),
                  pl.BlockSpec((8, F2), lambda i: (jnp.maximum(i * (tr // 8) - 1, 0), 0)),
                  pl.BlockSpec((3, F2), lambda i: (0, 0)),
                  pl.BlockSpec((1, F2), lambda i: (0, 0))],
        out_specs=pl.BlockSpec((tr, F), lambda i: (i, 0)),
        out_shape=jax.ShapeDtypeStruct((L, F), BF16),
        compiler_params=_params("parallel"),
    )(u, u, cw, cb.reshape(1, F2))


def _conv_glu_bwd(da, u, cw, cb, name):
    L, F2 = u.shape
    F = F2 // 2
    tr = 128
    nb = L // tr

    def dy_of(yv, dav):
        g, val = yv[:, :F], yv[:, F:]
        sg = 1.0 / (1.0 + jnp.exp(-g))
        dg = dav * val * (sg * (1.0 + g * (1.0 - sg)))
        dv = dav * (g * sg)
        return jnp.concatenate([dg, dv], axis=1)

    def body(da_ref, dan_ref, u_ref, up_ref, un_ref, cw_ref, cb_ref, du_ref, dcw_ref, dcb_ref):
        i = pl.program_id(0)
        cur, prev, rows = _valid_rows(i, tr, u_ref, up_ref)
        y, d2, d1 = _conv(cur, prev, cw_ref, cb_ref)
        dy = dy_of(y, da_ref[...])
        yn, _, _ = _conv(un_ref[...], cur[tr - 8:tr, :], cw_ref, cb_ref)
        dyn = jnp.where(i < nb - 1, dy_of(yn, dan_ref[...]), 0.0)
        du = dy * cw_ref[2:3, :] + _shift_up(dy, dyn, 1) * cw_ref[1:2, :] + _shift_up(dy, dyn, 2) * cw_ref[0:1, :]
        du_ref[...] = jnp.where(rows >= PAD, du, 0.0).astype(BF16)

        @pl.when(i == 0)
        def _():
            dcw_ref[...] = jnp.zeros_like(dcw_ref)
            dcb_ref[...] = jnp.zeros_like(dcb_ref)

        dcw_ref[0:1, :] += jnp.sum(dy * d2, axis=0, keepdims=True)
        dcw_ref[1:2, :] += jnp.sum(dy * d1, axis=0, keepdims=True)
        dcw_ref[2:3, :] += jnp.sum(dy * cur, axis=0, keepdims=True)
        dcb_ref[...] += jnp.sum(dy, axis=0, keepdims=True)

    nxt = lambda i: (jnp.minimum((i + 1) * (tr // 8), L // 8 - 1), 0)
    prv = lambda i: (jnp.maximum(i * (tr // 8) - 1, 0), 0)
    du, dcw, dcb = pl.pallas_call(
        body, name=name, grid=(nb,),
        in_specs=[pl.BlockSpec((tr, F), lambda i: (i, 0)), pl.BlockSpec((8, F), nxt),
                  pl.BlockSpec((tr, F2), lambda i: (i, 0)), pl.BlockSpec((8, F2), prv), pl.BlockSpec((8, F2), nxt),
                  pl.BlockSpec((3, F2), lambda i: (0, 0)), pl.BlockSpec((1, F2), lambda i: (0, 0))],
        out_specs=[pl.BlockSpec((tr, F2), lambda i: (i, 0)), pl.BlockSpec((3, F2), lambda i: (0, 0)),
                   pl.BlockSpec((1, F2), lambda i: (0, 0))],
        out_shape=[jax.ShapeDtypeStruct((L, F2), BF16), jax.ShapeDtypeStruct((3, F2), F32),
                   jax.ShapeDtypeStruct((1, F2), F32)],
        compiler_params=_params("arbitrary"),
    )(da, da, u, u, u, cw, cb.reshape(1, F2))
    return du, dcw, dcb[0]


def _dense_mask(qpos, kpos):
    return (kpos <= qpos) & (kpos >= PAD)


def _tables(pairs):
    qt, kt, ft = [], [], []
    for grp in pairs:
        for n, (qb, kb, msk) in enumerate(grp):
            qt.append(qb)
            kt.append(kb)
            ft.append((1 if n == 0 else 0) | (2 if n == len(grp) - 1 else 0) | (4 if msk else 0))
    return tuple(jnp.asarray(np.asarray(t, np.int32)) for t in (qt, kt, ft))


class _Dense:
    mask = staticmethod(_dense_mask)

    def __init__(self, L, pad_in_cr2):
        self.T = T = 640 if L % 640 == 0 else 128
        nb = L // T
        m = lambda qb, kb: kb == qb or (kb == 0 and not pad_in_cr2) or (qb * T < PAD)
        self.q_major = _tables([[(qb, kb, m(qb, kb)) for kb in range(qb + 1)] for qb in range(nb)])
        self.k_major = _tables([[(qb, kb, m(qb, kb)) for qb in range(kb, nb)] for kb in range(nb)])


CHUNK = 32


def _chunk_scores(raw, r, c, cr, masked, mask, T, qb, kb):
    s = raw * c
    if cr is not None:
        s = s - cr
    live = None
    if masked:
        qpos = qb * T + r * CHUNK + lax.broadcasted_iota(jnp.int32, (CHUNK, T), 0)
        kpos = kb * T + lax.broadcasted_iota(jnp.int32, (CHUNK, T), 1)
        live = mask(qpos, kpos) & (qpos >= PAD)
        s = jnp.where(live, s, NEG)
    return s, live


def _prob(s, lse, live):
    p = jnp.exp2(s - lse)
    return p if live is None else jnp.where(live, p, 0.0)


def _both(flag, fn):
    pl.when(flag != 0)(lambda: fn(True))
    pl.when(flag == 0)(lambda: fn(False))


def _flash_fwd(cfg, qa, q_off, ka, k_off, va, v_off, H, hs, scale, name, cr2=None):
    L = qa.shape[0]
    T = cfg.T
    qt, kt, ft = cfg.q_major
    npairs = qt.shape[0]
    c = scale * LOG2E
    decay = cr2 is not None
    W = hs * LANES

    def body(qt_ref, kt_ref, ft_ref, *refs):
        it = iter(refs)
        q_ref, k_ref, v_ref = next(it), next(it), next(it)
        cr_ref = next(it) if decay else None
        o_ref, lse_ref, m_sc, l_sc, acc_sc = next(it), next(it), next(it), next(it), next(it)
        n = pl.program_id(1)
        qb, kb, f = qt_ref[n], kt_ref[n], ft_ref[n]

        @pl.when((f & 1) != 0)
        def _():
            m_sc[...] = jnp.full(m_sc.shape, NEG, F32)
            l_sc[...] = jnp.zeros_like(l_sc)
            acc_sc[...] = jnp.zeros_like(acc_sc)

        def step(masked):
            for i in range(hs):
                cols = slice(i * LANES, (i + 1) * LANES)
                s = lax.dot_general(_bf(q_ref[:, cols]), _bf(k_ref[:, cols]), NT, preferred_element_type=F32) * c
                if decay:
                    s = s - cr_ref[i]
                if masked:
                    qpos = qb * T + lax.broadcasted_iota(jnp.int32, (T, T), 0)
                    kpos = kb * T + lax.broadcasted_iota(jnp.int32, (T, T), 1)
                    s = jnp.where(cfg.mask(qpos, kpos), s, NEG)
                m_prev = m_sc[i]
                m_new = jnp.maximum(m_prev, jnp.max(s, axis=1, keepdims=True))
                alpha = jnp.exp2(m_prev - m_new)
                p = jnp.exp2(s - m_new)
                l_sc[i] = alpha * l_sc[i] + jnp.sum(p, axis=1, keepdims=True)
                acc_sc[i] = alpha * acc_sc[i] + lax.dot_general(p.astype(BF16), _bf(v_ref[:, cols]), NN,
                                                                preferred_element_type=F32)
                m_sc[i] = m_new

        _both(f & 4, step)

        @pl.when((f & 2) != 0)
        def _():
            for i in range(hs):
                cols = slice(i * LANES, (i + 1) * LANES)
                l = l_sc[i]
                o_ref[:, cols] = acc_sc[i] / l
                lse_ref[:, cols] = jnp.broadcast_to(m_sc[i] + jnp.log(l) * LOG2E, (T, LANES))

    qrow = lambda off: pl.BlockSpec((T, W), lambda h, n, qt, kt, ft: (qt[n], off // hs + h))
    krow = lambda off: pl.BlockSpec((T, W), lambda h, n, qt, kt, ft: (kt[n], off // hs + h))
    ins, specs = [qa, ka, va], [qrow(q_off), krow(k_off), krow(v_off)]
    if decay:
        ins.append(cr2)
        specs.append(pl.BlockSpec((hs, 1, T), lambda h, n, qt, kt, ft: (h, 0, kt[n])))
    full = jax.ShapeDtypeStruct((L, H * LANES), F32)
    return pl.pallas_call(
        body, name=name, out_shape=[full, full],
        grid_spec=pltpu.PrefetchScalarGridSpec(
            num_scalar_prefetch=3, grid=(H // hs, npairs), in_specs=specs, out_specs=[qrow(0), qrow(0)],
            scratch_shapes=[pltpu.VMEM((hs, T, 1), F32), pltpu.VMEM((hs, T, 1), F32), pltpu.VMEM((hs, T, LANES), F32)]),
        compiler_params=_params("parallel", "arbitrary"),
    )(qt, kt, ft, *ins)


def _flash_bwd(cfg, qa, q_off, ka, k_off, va, v_off, do, lse2, delta, H, hs, scale, name, cr2=None, lane_sums=False):
    L = qa.shape[0]
    T = cfg.T
    qt, kt, ft = cfg.k_major
    npairs = qt.shape[0]
    c = scale * LOG2E
    decay = cr2 is not None
    W = hs * LANES

    def body(qt_ref, kt_ref, ft_ref, *refs):
        it = iter(refs)
        q_ref, k_ref, v_ref = next(it), next(it), next(it)
        cr_ref = next(it) if decay else None
        do_ref, lse_ref, dl_ref, dk_ref, dv_ref, dq_ref, dk_sc, dv_sc, s_sc, dp_sc, p_sc, ds_sc = (
            next(it) for _ in range(12))
        n = pl.program_id(1)
        qb, kb, f = qt_ref[n], kt_ref[n], ft_ref[n]

        @pl.when(n == 0)
        def _():
            dq_ref[...] = jnp.zeros_like(dq_ref)

        @pl.when((f & 1) != 0)
        def _():
            dk_sc[...] = jnp.zeros_like(dk_sc)
            dv_sc[...] = jnp.zeros_like(dv_sc)

        def step(masked):
            last = lax.broadcasted_iota(jnp.int32, (T, LANES), 1) == LANES - 1
            rows = pl.ds(pl.multiple_of(qb * T, T), T)
            for i in range(hs):
                cols = slice(i * LANES, (i + 1) * LANES)
                q, k, v, dob = _bf(q_ref[:, cols]), _bf(k_ref[:, cols]), _bf(v_ref[:, cols]), _bf(do_ref[:, cols])
                k1 = jnp.where(last, 1.0, k_ref[:, cols]).astype(BF16) if lane_sums else k
                q1 = jnp.where(last, 1.0, q_ref[:, cols]).astype(BF16) if lane_sums else q
                cr = cr_ref[i] if decay else None
                s_sc[i] = lax.dot_general(q, k, NT, preferred_element_type=F32)
                dp_sc[i] = lax.dot_general(dob, v, NT, preferred_element_type=F32)

                def chunk(r, carry, i=i, cr=cr):
                    rs = pl.ds(r * CHUNK, CHUNK)
                    s, live = _chunk_scores(s_sc[i, rs, :], r, c, cr, masked, cfg.mask, T, qb, kb)
                    p = _prob(s, lse_ref[rs, i * LANES:i * LANES + 1], live)
                    p_sc[i, rs, :] = p.astype(BF16)
                    ds_sc[i, rs, :] = (p * (dp_sc[i, rs, :] - dl_ref[rs, i * LANES:i * LANES + 1])).astype(BF16)
                    return carry

                for r in range(T // CHUNK):
                    chunk(r, 0)
                dv_sc[i] += lax.dot_general(p_sc[i], dob, TN, preferred_element_type=F32)
                dk_sc[i] += lax.dot_general(ds_sc[i], q1, TN, preferred_element_type=F32)
                dq_ref[rows, cols] += lax.dot_general(ds_sc[i], k1, NN, preferred_element_type=F32)

        _both(f & 4, step)

        @pl.when((f & 2) != 0)
        def _():
            for i in range(hs):
                cols = slice(i * LANES, (i + 1) * LANES)
                dk_ref[:, cols] = dk_sc[i] * scale
                dv_ref[:, cols] = dv_sc[i]

        @pl.when(n == npairs - 1)
        def _():
            dq_ref[...] = dq_ref[...] * scale

    qrow = lambda off: pl.BlockSpec((T, W), lambda h, n, qt, kt, ft: (qt[n], off // hs + h))
    krow = lambda off: pl.BlockSpec((T, W), lambda h, n, qt, kt, ft: (kt[n], off // hs + h))
    ins, specs = [qa, ka, va], [qrow(q_off), krow(k_off), krow(v_off)]
    if decay:
        ins.append(cr2)
        specs.append(pl.BlockSpec((hs, 1, T), lambda h, n, qt, kt, ft: (h, 0, kt[n])))
    ins += [do, lse2, delta]
    specs += [qrow(0), qrow(0), qrow(0)]
    full = jax.ShapeDtypeStruct((L, H * LANES), F32)
    return pl.pallas_call(
        body, name=name, out_shape=[full, full, full],
        grid_spec=pltpu.PrefetchScalarGridSpec(
            num_scalar_prefetch=3, grid=(H // hs, npairs), in_specs=specs,
            out_specs=[krow(0), krow(0), pl.BlockSpec((L, W), lambda h, n, qt, kt, ft: (0, h))],
            scratch_shapes=[pltpu.VMEM((hs, T, LANES), F32), pltpu.VMEM((hs, T, LANES), F32),
                            pltpu.VMEM((hs, T, T), F32), pltpu.VMEM((hs, T, T), F32),
                            pltpu.VMEM((hs, T, T), BF16), pltpu.VMEM((hs, T, T), BF16)]),
        compiler_params=_params("parallel", "arbitrary"),
    )(qt, kt, ft, *ins)


def _swa_parts(qb, q_ref, km_ref, kp_ref, kc_ref, vm_ref, vp_ref, vc_ref, c):
    G, B = SWA_G, BLOCK
    q = jnp.concatenate([_bf(q_ref[:, i * LANES:(i + 1) * LANES]) for i in range(G)], axis=0)
    kc = jnp.concatenate([_bf(km_ref[...]), _bf(kp_ref[...]), _bf(kc_ref[...])], axis=0)
    vc = jnp.concatenate([_bf(vm_ref[...]), _bf(vp_ref[...]), _bf(vc_ref[...])], axis=0)
    s = lax.dot_general(q, kc, NT, preferred_element_type=F32) * c
    row = lax.broadcasted_iota(jnp.int32, (G * B, 3 * B), 0)
    col = lax.broadcasted_iota(jnp.int32, (G * B, 3 * B), 1)
    qpos = qb * B + (row & (B - 1))
    kpos = jnp.where(col < B, col, jnp.where(col < 2 * B, (qb - 1) * B + col - B, qb * B + col - 2 * B))
    d = qpos - kpos
    live = ((col < B) & (kpos >= PAD) & (kpos <= qpos)) | ((col >= B) & (kpos >= B) & (d >= 0) & (d < WINDOW))
    return q, kc, vc, jnp.where(live, s, NEG), live


def _stack_col(ref):
    return jnp.concatenate([ref[:, i * LANES:i * LANES + 1] for i in range(SWA_G)], axis=0)


def _swa_specs(nqk):
    G, B = SWA_G, BLOCK
    qrow = pl.BlockSpec((B, G * LANES), lambda hk, qb: (qb, hk))
    kv = lambda off, blk: pl.BlockSpec((B, LANES), lambda hk, qb: (blk(qb), off + hk))
    zero, prev, cur = (lambda qb: 0), (lambda qb: jnp.maximum(qb - 1, 0)), (lambda qb: qb)
    keys = [kv(HEADS, zero), kv(HEADS, prev), kv(HEADS, cur)]
    vals = [kv(nqk, zero), kv(nqk, prev), kv(nqk, cur)]
    return qrow, keys, vals


def _swa_attn_fwd(qk, proj, sink2, scale, name):
    L = qk.shape[0]
    G, B = SWA_G, BLOCK
    nqk = HEADS + SWA_KV
    c = scale * LOG2E

    def body(q_ref, km_ref, kp_ref, kc_ref, vm_ref, vp_ref, vc_ref, sink_ref, o_ref, lse_ref):
        qb = pl.program_id(1)
        q, kc, vc, s, live = _swa_parts(qb, q_ref, km_ref, kp_ref, kc_ref, vm_ref, vp_ref, vc_ref, c)
        sink = jnp.concatenate([jnp.broadcast_to(sink_ref[:, i * LANES:i * LANES + 1], (B, 1)) for i in range(G)], axis=0)
        m = jnp.maximum(jnp.max(s, axis=1, keepdims=True), sink)
        p = jnp.exp2(s - m)
        l = jnp.sum(p, axis=1, keepdims=True) + jnp.exp2(sink - m)
        o = lax.dot_general(p.astype(BF16), vc, NN, preferred_element_type=F32) / l
        lse = m + jnp.log(l) * LOG2E
        for i in range(G):
            o_ref[:, i * LANES:(i + 1) * LANES] = o[i * B:(i + 1) * B]
            lse_ref[:, i * LANES:(i + 1) * LANES] = jnp.broadcast_to(lse[i * B:(i + 1) * B], (B, LANES))

    qrow, keys, vals = _swa_specs(nqk)
    shape = jax.ShapeDtypeStruct((L, HEADS * LANES), F32)
    return pl.pallas_call(
        body, name=name, grid=(SWA_KV, L // B),
        in_specs=[qrow] + keys + vals + [pl.BlockSpec((1, G * LANES), lambda hk, qb: (0, hk))],
        out_specs=[qrow, qrow], out_shape=[shape, shape],
        compiler_params=_params("parallel", "parallel"),
    )(qk, qk, qk, qk, proj, proj, proj, sink2)


def _swa_attn_bwd(qk, proj, do, lse2, delta, scale, name):
    L = qk.shape[0]
    G, B = SWA_G, BLOCK
    nqk = HEADS + SWA_KV
    c = scale * LOG2E

    def body(q_ref, km_ref, kp_ref, kc_ref, vm_ref, vp_ref, vc_ref, do_ref, lse_ref, dl_ref, dq_ref, dk_ref, dv_ref):
        qb = pl.program_id(1)

        @pl.when(qb == 0)
        def _():
            dk_ref[...] = jnp.zeros_like(dk_ref)
            dv_ref[...] = jnp.zeros_like(dv_ref)

        q, kc, vc, s, live = _swa_parts(qb, q_ref, km_ref, kp_ref, kc_ref, vm_ref, vp_ref, vc_ref, c)
        dob = jnp.concatenate([_bf(do_ref[:, i * LANES:(i + 1) * LANES]) for i in range(G)], axis=0)
        p = jnp.where(live, jnp.exp2(s - _stack_col(lse_ref)), 0.0)
        dp = lax.dot_general(dob, vc, NT, preferred_element_type=F32)
        dsb = (p * (dp - _stack_col(dl_ref))).astype(BF16)
        dq = lax.dot_general(dsb, kc, NN, preferred_element_type=F32) * scale
        for i in range(G):
            dq_ref[:, i * LANES:(i + 1) * LANES] = dq[i * B:(i + 1) * B]
        dkc = lax.dot_general(dsb, q, TN, preferred_element_type=F32) * scale
        dvc = lax.dot_general(p.astype(BF16), dob, TN, preferred_element_type=F32)
        starts = (0, pl.multiple_of(jnp.maximum(qb - 1, 0) * B, B), pl.multiple_of(qb * B, B))
        for n, st in enumerate(starts):
            dk_ref[pl.ds(st, B), :] += dkc[n * B:(n + 1) * B]
            dv_ref[pl.ds(st, B), :] += dvc[n * B:(n + 1) * B]

    qrow, keys, vals = _swa_specs(nqk)
    res = pl.BlockSpec((L, LANES), lambda hk, qb: (0, hk))
    return pl.pallas_call(
        body, name=name, grid=(SWA_KV, L // B),
        in_specs=[qrow] + keys + vals + [qrow, qrow, qrow],
        out_specs=[qrow, res, res],
        out_shape=[jax.ShapeDtypeStruct((L, HEADS * LANES), F32), jax.ShapeDtypeStruct((L, SWA_KV * LANES), F32),
                   jax.ShapeDtypeStruct((L, SWA_KV * LANES), F32)],
        compiler_params=_params("parallel", "arbitrary"),
    )(qk, qk, qk, qk, proj, proj, proj, do, lse2, delta)


def _delta(do, o, name):
    L, HW = do.shape
    tr = BLOCK

    def body(do_ref, o_ref, d_ref):
        for h in range(HW // LANES):
            cols = slice(h * LANES, (h + 1) * LANES)
            d = jnp.sum(do_ref[:, cols].astype(F32) * o_ref[:, cols], axis=1, keepdims=True)
            d_ref[:, cols] = jnp.broadcast_to(d, (tr, LANES))

    spec = pl.BlockSpec((tr, HW), lambda i: (i, 0))
    return pl.pallas_call(body, name=name, grid=(L // tr,), in_specs=[spec, spec], out_specs=spec,
                          out_shape=jax.ShapeDtypeStruct((L, HW), F32), compiler_params=_params("parallel"))(do, o)


def _sink_grad(lse2, delta, sink2, name):
    L, HW = lse2.shape
    tr = 128

    def body(lse_ref, dl_ref, s_ref, o_ref):
        @pl.when(pl.program_id(0) == 0)
        def _():
            o_ref[...] = jnp.zeros_like(o_ref)

        o_ref[...] -= jnp.sum(jnp.exp2(s_ref[...] - lse_ref[...]) * dl_ref[...], axis=0, keepdims=True)

    row = pl.BlockSpec((tr, HW), lambda i: (i, 0))
    vec = pl.BlockSpec((1, HW), lambda i: (0, 0))
    return pl.pallas_call(body, name=name, grid=(L // tr,), in_specs=[row, row, vec], out_specs=vec,
                          out_shape=jax.ShapeDtypeStruct((1, HW), F32), compiler_params=_params("arbitrary"))(
        lse2, delta, sink2)


def _tri(lower):
    r = lax.broadcasted_iota(jnp.int32, (BLOCK, BLOCK), 0)
    c = lax.broadcasted_iota(jnp.int32, (BLOCK, BLOCK), 1)
    return jnp.where((c <= r) if lower else (c >= r), 1.0, 0.0).astype(F32)


def _gate_cumsum(proj, fg_tile, b_pad, name):
    L = proj.shape[0]

    def body(fg_ref, b_ref, c_ref, carry):
        @pl.when(pl.program_id(0) == 0)
        def _():
            carry[...] = jnp.zeros_like(carry)

        x = fg_ref[...] + b_ref[...]
        lf = jnp.minimum(x, 0.0) - jnp.log(1.0 + jnp.exp(-jnp.abs(x)))
        c = jnp.dot(_tri(True), lf, precision=lax.Precision.HIGHEST, preferred_element_type=F32) + carry[...]
        c_ref[...] = c
        carry[...] = c[BLOCK - 1:BLOCK, :]

    return pl.pallas_call(
        body, name=name, grid=(L // BLOCK,),
        in_specs=[pl.BlockSpec((BLOCK, LANES), lambda i: (i, fg_tile)), pl.BlockSpec((1, LANES), lambda i: (0, 0))],
        out_specs=pl.BlockSpec((BLOCK, LANES), lambda i: (i, 0)),
        out_shape=jax.ShapeDtypeStruct((L, LANES), F32),
        scratch_shapes=[pltpu.VMEM((1, LANES), F32)],
        compiler_params=_params("arbitrary"),
    )(proj, b_pad)


def _gate_cumsum_bwd(dc, proj, fg_tile, b_pad, name):
    L = proj.shape[0]
    nb = L // BLOCK

    def body(dc_ref, fg_ref, b_ref, dfg_ref, db_ref, carry):
        @pl.when(pl.program_id(0) == 0)
        def _():
            carry[...] = jnp.zeros_like(carry)
            db_ref[...] = jnp.zeros_like(db_ref)

        dlf = jnp.dot(_tri(False), dc_ref[...], precision=lax.Precision.HIGHEST,
                      preferred_element_type=F32) + carry[...]
        carry[...] = dlf[0:1, :]
        x = fg_ref[...] + b_ref[...]
        lanes = lax.broadcasted_iota(jnp.int32, (BLOCK, LANES), 1)
        rows = (nb - 1 - pl.program_id(0)) * BLOCK + lax.broadcasted_iota(jnp.int32, (BLOCK, LANES), 0)
        dfg = jnp.where((lanes < HEADS) & (rows >= PAD), dlf / (1.0 + jnp.exp(x)), 0.0)
        dfg_ref[...] = jnp.concatenate([dfg, jnp.zeros_like(dfg)], axis=1)
        db_ref[...] += jnp.sum(dfg, axis=0, keepdims=True)

    dfg, db = pl.pallas_call(
        body, name=name, grid=(nb,),
        in_specs=[pl.BlockSpec((BLOCK, LANES), lambda i: (nb - 1 - i, 0)),
                  pl.BlockSpec((BLOCK, LANES), lambda i: (nb - 1 - i, fg_tile)),
                  pl.BlockSpec((1, LANES), lambda i: (0, 0))],
        out_specs=[pl.BlockSpec((BLOCK, 2 * LANES), lambda i: (nb - 1 - i, 0)),
                   pl.BlockSpec((1, LANES), lambda i: (0, 0))],
        out_shape=[jax.ShapeDtypeStruct((L, 2 * LANES), F32), jax.ShapeDtypeStruct((1, LANES), F32)],
        scratch_shapes=[pltpu.VMEM((1, LANES), F32)],
        compiler_params=_params("arbitrary"),
    )(dc, proj, b_pad)
    return dfg, db[0]


def _rope_tables(L, dim, theta, lane0):
    half = dim // 2
    pos = (jnp.arange(L) - PAD).astype(F32)
    inv = theta ** (-jnp.arange(0, dim, 2, dtype=F32) / dim)
    ang = pos[:, None] * inv[None, :]
    cos, sin = jnp.cos(ang), jnp.sin(ang)
    C = jnp.ones((L, LANES), F32).at[:, lane0:lane0 + half].set(cos).at[:, lane0 + half:lane0 + dim].set(cos)
    S1 = jnp.zeros((L, LANES), F32).at[:, lane0:lane0 + half].set(-sin)
    S2 = jnp.zeros((L, LANES), F32).at[:, lane0 + half:lane0 + dim].set(sin)
    return C, S1, S2


def _rot(x, C, S1, S2, R):
    return x * C + pltpu.roll(x, LANES - R, 1) * S1 + pltpu.roll(x, R, 1) * S2


def _rot_t(dy, C, S1, S2, R):
    return dy * C + pltpu.roll(dy * S1, R, 1) + pltpu.roll(dy * S2, LANES - R, 1)


def _rope(x, nt, tabs, R, name, transpose=False, shared=None, shared_tile=0, out_dtype=F32):
    L = x.shape[0]
    T = BLOCK
    fn = _rot_t if transpose else _rot

    def body(*refs):
        if shared is None:
            x_ref, c_ref, s1_ref, s2_ref, o_ref = refs
        else:
            x_ref, sh_ref, c_ref, s1_ref, s2_ref, o_ref = refs
            rs = fn(sh_ref[...], c_ref[...], s1_ref[...], s2_ref[...], R)
        for h in range(nt):
            cols = slice(h * LANES, (h + 1) * LANES)
            if shared is None:
                o_ref[:, cols] = fn(x_ref[:, cols], c_ref[...], s1_ref[...], s2_ref[...], R).astype(out_dtype)
            else:
                o_ref[:, cols] = (x_ref[:, cols] + rs).astype(out_dtype)

    wide = pl.BlockSpec((T, nt * LANES), lambda i: (i, 0))
    tab = pl.BlockSpec((T, LANES), lambda i: (i, 0))
    ins, specs = [x], [wide]
    if shared is not None:
        ins.append(shared)
        specs.append(pl.BlockSpec((T, LANES), lambda i: (i, shared_tile)))
    return pl.pallas_call(body, name=name, grid=(L // T,), in_specs=specs + [tab, tab, tab], out_specs=wide,
                          out_shape=jax.ShapeDtypeStruct((L, nt * LANES), out_dtype),
                          compiler_params=_params("parallel"))(*ins, *tabs)


def _rope_shared_bwd(dk, nt, tabs, R, name):
    L = dk.shape[0]
    tr = 128

    def body(dk_ref, c_ref, s1_ref, s2_ref, o_ref):
        acc = dk_ref[:, 0:LANES]
        for h in range(1, nt):
            acc = acc + dk_ref[:, h * LANES:(h + 1) * LANES]
        o_ref[...] = _rot_t(acc, c_ref[...], s1_ref[...], s2_ref[...], R)

    tab = pl.BlockSpec((tr, LANES), lambda i: (i, 0))
    return pl.pallas_call(body, name=name, grid=(L // tr,),
                          in_specs=[pl.BlockSpec((tr, nt * LANES), lambda i: (i, 0)), tab, tab, tab], out_specs=tab,
                          out_shape=jax.ShapeDtypeStruct((L, LANES), F32), compiler_params=_params("parallel"))(
        dk, *tabs)


def _rms_fwd(pa, gq, gkv, name):
    L = pa.shape[0]
    tr = 128
    Q, KV = MLA_Q_LORA, MLA_KV_LORA

    def body(pa_ref, gq_ref, gkv_ref, q_ref, kv_ref):
        for lo, n, g_ref, o_ref in ((0, Q, gq_ref, q_ref), (Q, KV, gkv_ref, kv_ref)):
            x = pa_ref[:, lo:lo + n]
            r = lax.rsqrt(jnp.mean(x * x, axis=1, keepdims=True) + RMS_EPS)
            o_ref[...] = (x * r * g_ref[...]).astype(BF16)

    return pl.pallas_call(
        body, name=name, grid=(L // tr,),
        in_specs=[pl.BlockSpec((tr, pa.shape[1]), lambda i: (i, 0)), pl.BlockSpec((1, Q), lambda i: (0, 0)),
                  pl.BlockSpec((1, KV), lambda i: (0, 0))],
        out_specs=[pl.BlockSpec((tr, Q), lambda i: (i, 0)), pl.BlockSpec((tr, KV), lambda i: (i, 0))],
        out_shape=[jax.ShapeDtypeStruct((L, Q), BF16), jax.ShapeDtypeStruct((L, KV), BF16)],
        compiler_params=_params("parallel"),
    )(pa, gq.reshape(1, Q), gkv.reshape(1, KV))


def _rms_bwd(pa, dq, dkv, dkr, gq, gkv, name):
    L, W = pa.shape
    tr = 128
    Q, KV = MLA_Q_LORA, MLA_KV_LORA

    def body(pa_ref, dq_ref, dkv_ref, dkr_ref, gq_ref, gkv_ref, dpa_ref, dgq_ref, dgkv_ref):
        @pl.when(pl.program_id(0) == 0)
        def _():
            dgq_ref[...] = jnp.zeros_like(dgq_ref)
            dgkv_ref[...] = jnp.zeros_like(dgkv_ref)

        for lo, n, g_ref, dy_ref, dg_ref in ((0, Q, gq_ref, dq_ref, dgq_ref), (Q, KV, gkv_ref, dkv_ref, dgkv_ref)):
            x = pa_ref[:, lo:lo + n]
            r = lax.rsqrt(jnp.mean(x * x, axis=1, keepdims=True) + RMS_EPS)
            xh = x * r
            dy = dy_ref[...]
            dxh = dy * g_ref[...]
            dpa_ref[:, lo:lo + n] = (r * (dxh - xh * jnp.mean(dxh * xh, axis=1, keepdims=True))).astype(BF16)
            dg_ref[...] += jnp.sum(dy * xh, axis=0, keepdims=True)
        dpa_ref[:, Q + KV:W] = dkr_ref[...].astype(BF16)

    vq = pl.BlockSpec((1, Q), lambda i: (0, 0))
    vkv = pl.BlockSpec((1, KV), lambda i: (0, 0))
    dpa, dgq, dgkv = pl.pallas_call(
        body, name=name, grid=(L // tr,),
        in_specs=[pl.BlockSpec((tr, W), lambda i: (i, 0)), pl.BlockSpec((tr, Q), lambda i: (i, 0)),
                  pl.BlockSpec((tr, KV), lambda i: (i, 0)), pl.BlockSpec((tr, LANES), lambda i: (i, 0)), vq, vkv],
        out_specs=[pl.BlockSpec((tr, W), lambda i: (i, 0)), vq, vkv],
        out_shape=[jax.ShapeDtypeStruct((L, W), BF16), jax.ShapeDtypeStruct((1, Q), F32),
                   jax.ShapeDtypeStruct((1, KV), F32)],
        compiler_params=_params("arbitrary"),
    )(pa, dq, dkv, dkr, gq.reshape(1, Q), gkv.reshape(1, KV))
    return dpa, dgq[0], dgkv[0]


def _loss_head(h, target, name):
    L = h.shape[0]
    tr = BLOCK
    inv = 1.0 / D_MODEL

    def body(h_ref, t_ref, loss_ref, dh_ref):
        i = pl.program_id(0)

        @pl.when(i == 0)
        def _():
            loss_ref[...] = jnp.zeros_like(loss_ref)
            dh_ref[...] = jnp.zeros_like(dh_ref)

        @pl.when(i > 0)
        def _():
            e = h_ref[...] - t_ref[...]
            dh_ref[...] = e * inv
            loss_ref[...] += jnp.sum((e * e).reshape(tr // 8, 8, D_MODEL), axis=0) * (0.5 * inv)

    row = pl.BlockSpec((tr, D_MODEL), lambda i: (i, 0))
    loss, dh = pl.pallas_call(
        body, name=name, grid=(L // tr,),
        in_specs=[row, pl.BlockSpec((tr, D_MODEL), lambda i: (jnp.maximum(i - 1, 0), 0))],
        out_specs=[pl.BlockSpec((8, D_MODEL), lambda i: (0, 0)), row],
        out_shape=[jax.ShapeDtypeStruct((8, D_MODEL), F32), jax.ShapeDtypeStruct((L, D_MODEL), F32)],
        compiler_params=_params("arbitrary"),
    )(h, target)
    return loss, dh


def _pad_heads_cols(w, nh, d, dp=LANES):
    K = w.shape[0]
    return jnp.pad(w.reshape(K, nh, d), ((0, 0), (0, 0), (0, dp - d))).reshape(K, nh * dp)


def _unpad_heads_cols(w, nh, d, dp=LANES):
    K = w.shape[0]
    return w.reshape(K, nh, dp)[:, :, :d].reshape(K, nh * d)


def _pad_heads_rows(w, nh, d):
    N = w.shape[1]
    return jnp.pad(w.reshape(nh, d, N), ((0, 0), (0, LANES - d), (0, 0))).reshape(nh * LANES, N)


def _unpad_heads_rows(w, nh, d):
    N = w.shape[1]
    return w.reshape(nh, LANES, N)[:, :d, :].reshape(nh * d, N)


def _fox_fwd(h, w_in, b_f, w_o, tag):
    L = h.shape[0]
    hd = HEADS * HEAD_DIM
    W = jnp.concatenate([_pad_heads_cols(w_in[:, i * hd:(i + 1) * hd], HEADS, HEAD_DIM) for i in range(3)]
                        + [jnp.pad(w_in[:, 3 * hd:], ((0, 0), (0, 2 * LANES - HEADS)))], axis=1)
    Wo = _pad_heads_rows(w_o, HEADS, HEAD_DIM)
    b_pad = jnp.pad(b_f, (0, LANES - HEADS)).reshape(1, LANES)
    proj = _mm(h, W, "nn", tag + "_proj")
    c = _gate_cumsum(proj, 3 * HEADS, b_pad, tag + "_cumsum")
    dead = (jnp.arange(L) < PAD)[:, None]
    cr2 = jnp.where(dead, -NEG, c[:, :HEADS] * LOG2E).T.reshape(HEADS, 1, L)
    cfg = _Dense(L, True)
    scale = HEAD_DIM ** -0.5
    o, lse2 = _flash_fwd(cfg, proj, 0, proj, HEADS, proj, 2 * HEADS, HEADS, DENSE_HS_FWD, scale, tag + "_attn", cr2=cr2)
    mix = _mm(o, Wo, "nn", tag + "_out")
    return mix, (h, W, Wo, b_pad, proj, cr2, o, lse2)


def _fox_bwd(dmix, res, tag):
    h, W, Wo, b_pad, proj, cr2, o, lse2 = res
    L = h.shape[0]
    cfg = _Dense(L, True)
    scale = HEAD_DIM ** -0.5
    dWo = _mm(o, dmix, "tn", tag + "_dwo")
    do = _mm(dmix, Wo, "nt", tag + "_do", out_dtype=BF16)
    qkv = (proj, 0, proj, HEADS, proj, 2 * HEADS)
    delta = _delta(do, o, tag + "_delta")
    dk, dv, dq = _flash_bwd(cfg, *qkv, do, lse2, delta, HEADS, DENSE_HS, scale, tag + "_bwd", cr2=cr2, lane_sums=True)
    dc = jnp.pad((dq[:, LANES - 1::LANES] - dk[:, LANES - 1::LANES]) * (1.0 / scale), ((0, 0), (0, LANES - HEADS)))
    dfg, db = _gate_cumsum_bwd(dc, proj, 3 * HEADS, b_pad, tag + "_cumsum_bwd")
    dproj = jnp.concatenate([dq.astype(BF16), dk.astype(BF16), dv.astype(BF16), dfg.astype(BF16)], axis=1)
    dW = _mm(h, dproj, "tn", tag + "_dw")
    dh = _mm(dproj, W, "nt", tag + "_dh")
    hp = HEADS * LANES
    dw_in = jnp.concatenate([_unpad_heads_cols(dW[:, i * hp:(i + 1) * hp], HEADS, HEAD_DIM) for i in range(3)]
                            + [dW[:, 3 * hp:3 * hp + HEADS]], axis=1)
    return dh, dict(w_in=dw_in, b_f=db[:HEADS], w_o=_unpad_heads_rows(dWo, HEADS, HEAD_DIM))


def _swa_fwd(h, w_in, sinks, w_o, tag):
    L = h.shape[0]
    qd, kd = HEADS * HEAD_DIM, SWA_KV * HEAD_DIM
    W = jnp.concatenate([_pad_heads_cols(w_in[:, :qd], HEADS, HEAD_DIM),
                         _pad_heads_cols(w_in[:, qd:qd + kd], SWA_KV, HEAD_DIM),
                         _pad_heads_cols(w_in[:, qd + kd:], SWA_KV, HEAD_DIM)], axis=1)
    Wo = _pad_heads_rows(w_o, HEADS, HEAD_DIM)
    sink2 = jnp.repeat(sinks * LOG2E, LANES).reshape(1, HEADS * LANES)
    tabs = _rope_tables(L, ROPE_DIM, ROPE_THETA, 0)
    proj = _mm(h, W, "nn", tag + "_proj")
    nqk = HEADS + SWA_KV
    qk = _rope(proj, nqk, tabs, ROPE_DIM // 2, tag + "_rope")
    scale = HEAD_DIM ** -0.5
    o, lse2 = _swa_attn_fwd(qk, proj, sink2, scale, tag + "_attn")
    mix = _mm(o, Wo, "nn", tag + "_out")
    return mix, (h, W, Wo, sink2, tabs, proj, qk, o, lse2)


def _swa_bwd(dmix, res, tag):
    h, W, Wo, sink2, tabs, proj, qk, o, lse2 = res
    L = h.shape[0]
    nqk = HEADS + SWA_KV
    scale = HEAD_DIM ** -0.5
    dWo = _mm(o, dmix, "tn", tag + "_dwo")
    do = _mm(dmix, Wo, "nt", tag + "_do", out_dtype=BF16)
    delta = _delta(do, o, tag + "_delta")
    dsink = _sink_grad(lse2, delta, sink2, tag + "_dsink")[0, ::LANES]
    dq, dk, dv = _swa_attn_bwd(qk, proj, do, lse2, delta, scale, tag + "_bwd")
    dqk = _rope(jnp.concatenate([dq, dk], axis=1), nqk, tabs, ROPE_DIM // 2, tag + "_rope_bwd", transpose=True,
                out_dtype=BF16)
    dproj = jnp.concatenate([dqk, dv.astype(BF16)], axis=1)
    dW = _mm(h, dproj, "tn", tag + "_dw")
    dh = _mm(dproj, W, "nt", tag + "_dh")
    hp = HEADS * LANES
    dw_in = jnp.concatenate([_unpad_heads_cols(dW[:, :hp], HEADS, HEAD_DIM),
                             _unpad_heads_cols(dW[:, hp:hp + SWA_KV * LANES], SWA_KV, HEAD_DIM),
                             _unpad_heads_cols(dW[:, hp + SWA_KV * LANES:], SWA_KV, HEAD_DIM)], axis=1)
    return dh, dict(w_in=dw_in, sinks=dsink, w_o=_unpad_heads_rows(dWo, HEADS, HEAD_DIM))


def _mla_fwd(h, w_a, g_q, g_kv, w_uq, w_ukv, w_o, tag):
    L = h.shape[0]
    Q, KV = MLA_Q_LORA, MLA_KV_LORA
    dqk = MLA_NOPE + MLA_ROPE
    kr_w = jnp.pad(w_a[:, Q + KV:], ((0, 0), (MLA_NOPE, LANES - dqk)))
    Wa = jnp.concatenate([w_a[:, :Q + KV], kr_w], axis=1)
    Wuq = _pad_heads_cols(w_uq, HEADS, dqk)
    ukv = w_ukv.reshape(KV, HEADS, MLA_NOPE + HEAD_DIM)
    Wukv = jnp.concatenate([_pad_heads_cols(ukv[:, :, :MLA_NOPE].reshape(KV, -1), HEADS, MLA_NOPE),
                            _pad_heads_cols(ukv[:, :, MLA_NOPE:].reshape(KV, -1), HEADS, HEAD_DIM)], axis=1)
    Wo = _pad_heads_rows(w_o, HEADS, HEAD_DIM)
    tabs = _rope_tables(L, MLA_ROPE, MLA_ROPE_THETA, MLA_NOPE)
    R = MLA_ROPE // 2
    pa = _mm(h, Wa, "nn", tag + "_proj")
    cqn, ckvn = _rms_fwd(pa, g_q, g_kv, tag + "_rms")
    q0 = _mm(cqn, Wuq, "nn", tag + "_uq")
    qr = _rope(q0, HEADS, tabs, R, tag + "_rope_q")
    kv0 = _mm(ckvn, Wukv, "nn", tag + "_ukv")
    kk = _rope(kv0, HEADS, tabs, R, tag + "_rope_k", shared=pa, shared_tile=(Q + KV) // LANES)
    cfg = _Dense(L, False)
    scale = dqk ** -0.5
    o, lse2 = _flash_fwd(cfg, qr, 0, kk, 0, kv0, HEADS, HEADS, DENSE_HS_FWD, scale, tag + "_attn")
    mix = _mm(o, Wo, "nn", tag + "_out")
    return mix, (h, Wa, Wuq, Wukv, Wo, g_q, g_kv, tabs, pa, cqn, ckvn, qr, kk, kv0, o, lse2)


def _mla_bwd(dmix, res, tag):
    h, Wa, Wuq, Wukv, Wo, g_q, g_kv, tabs, pa, cqn, ckvn, qr, kk, kv0, o, lse2 = res
    L = h.shape[0]
    Q, KV = MLA_Q_LORA, MLA_KV_LORA
    dqk = MLA_NOPE + MLA_ROPE
    R = MLA_ROPE // 2
    cfg = _Dense(L, False)
    scale = dqk ** -0.5
    dWo = _mm(o, dmix, "tn", tag + "_dwo")
    do = _mm(dmix, Wo, "nt", tag + "_do", out_dtype=BF16)
    delta = _delta(do, o, tag + "_delta")
    dk, dv, dqr = _flash_bwd(cfg, qr, 0, kk, 0, kv0, HEADS, do, lse2, delta, HEADS, DENSE_HS, scale, tag + "_bwd")
    dq0 = _rope(dqr, HEADS, tabs, R, tag + "_rope_q_bwd", transpose=True, out_dtype=BF16)
    dWuq = _mm(cqn, dq0, "tn", tag + "_dwuq")
    dcqn = _mm(dq0, Wuq, "nt", tag + "_dcq")
    dkv = jnp.concatenate([dk, dv], axis=1).astype(BF16)
    dWukv = _mm(ckvn, dkv, "tn", tag + "_dwukv")
    dckvn = _mm(dkv, Wukv, "nt", tag + "_dckv")
    dkr = _rope_shared_bwd(dk, HEADS, tabs, R, tag + "_rope_k_bwd")
    dpa, dgq, dgkv = _rms_bwd(pa, dcqn, dckvn, dkr, g_q, g_kv, tag + "_rms_bwd")
    dWa = _mm(h, dpa, "tn", tag + "_dw")
    dh = _mm(dpa, Wa, "nt", tag + "_dh")
    hp = HEADS * LANES
    dw_a = jnp.concatenate([dWa[:, :Q + KV], dWa[:, Q + KV + MLA_NOPE:Q + KV + dqk]], axis=1)
    dk_n = dWukv[:, :hp].reshape(KV, HEADS, LANES)[:, :, :MLA_NOPE]
    dv_n = dWukv[:, hp:].reshape(KV, HEADS, LANES)[:, :, :HEAD_DIM]
    dw_ukv = jnp.concatenate([dk_n, dv_n], axis=2).reshape(KV, HEADS * (MLA_NOPE + HEAD_DIM))
    return dh, dict(w_a=dw_a, g_q=dgq, g_kv=dgkv, w_uq=_unpad_heads_cols(dWuq, HEADS, dqk), w_ukv=dw_ukv,
                    w_o=_unpad_heads_rows(dWo, HEADS, HEAD_DIM))


MATMUL_WEIGHTS = ("fox_w_in", "fox_w_o", "swa_w_in", "swa_w_o", "mla_w_a", "mla_w_uq", "mla_w_ukv", "mla_w_o",
                  "ffn_w_in", "ffn_w_out")


def _local_step(x, target, w):
    w = {k: (_bf(v) if k in MATMUL_WEIGHTS else v) for k, v in w.items()}
    h = jnp.concatenate([jnp.zeros((PAD, D_MODEL), F32), w["meta_tokens"], x], axis=0)
    hb = h.astype(BF16)
    saved = []
    for i in range(DEPTH):
        kind, j = i % 3, i // 3
        tag = "l%d" % i
        if kind == 0:
            mix, mres = _fox_fwd(hb, w["fox_w_in"][j], w["fox_b_f"][j], w["fox_w_o"][j], tag + "_fox")
        elif kind == 1:
            mix, mres = _swa_fwd(hb, w["swa_w_in"][j], w["swa_sinks"][j], w["swa_w_o"][j], tag + "_swa")
        else:
            mix, mres = _mla_fwd(hb, w["mla_w_a"][j], w["mla_g_q"][j], w["mla_g_kv"][j], w["mla_w_uq"][j],
                                 w["mla_w_ukv"][j], w["mla_w_o"][j], tag + "_mla")
        h1, h1b, xh1, rs1 = _ln_fwd(h, mix, w["ln1_g"][i], w["ln1_b"][i], tag + "_ln1")
        u = _mm(h1b, w["ffn_w_in"][i], "nn", tag + "_ffn_in")
        a = _conv_glu_fwd(u, w["ffn_conv_w"][i], w["ffn_conv_b"][i], tag + "_conv")
        ffn = _mm(a, w["ffn_w_out"][i], "nn", tag + "_ffn_out")
        h2, h2b, xh2, rs2 = _ln_fwd(h1, ffn, w["ln2_g"][i], w["ln2_b"][i], tag + "_ln2")
        saved.append((mres, xh1, rs1, h1b, u, a, xh2, rs2))
        h, hb = h2, h2b
    loss, dh = _loss_head(h, target, "loss_head")

    g = {k: [None] * v.shape[0] for k, v in w.items() if k != "meta_tokens"}
    ga = None
    for i in reversed(range(DEPTH)):
        kind, j = i % 3, i // 3
        tag = "l%d" % i
        mres, xh1, rs1, h1b, u, a, xh2, rs2 = saved[i]
        dz2, dz2b, g["ln2_g"][i], g["ln2_b"][i] = _ln_bwd(ga, dh, xh2, rs2, w["ln2_g"][i], tag + "_ln2_bwd")
        g["ffn_w_out"][i] = _mm(a, dz2b, "tn", tag + "_dw_out")
        da = _mm(dz2b, w["ffn_w_out"][i], "nt", tag + "_da")
        du, g["ffn_conv_w"][i], g["ffn_conv_b"][i] = _conv_glu_bwd(da, u, w["ffn_conv_w"][i], w["ffn_conv_b"][i],
                                                                   tag + "_conv_bwd")
        g["ffn_w_in"][i] = _mm(h1b, du, "tn", tag + "_dw_in")
        dh1 = _mm(du, w["ffn_w_in"][i], "nt", tag + "_dh1")
        dz1, dz1b, g["ln1_g"][i], g["ln1_b"][i] = _ln_bwd(dz2, dh1, xh1, rs1, w["ln1_g"][i], tag + "_ln1_bwd")
        if kind == 0:
            dh, mg = _fox_bwd(dz1b, mres, tag + "_fox")
            pre = "fox_"
        elif kind == 1:
            dh, mg = _swa_bwd(dz1b, mres, tag + "_swa")
            pre = "swa_"
        else:
            dh, mg = _mla_bwd(dz1b, mres, tag + "_mla")
            pre = "mla_"
        for k, v in mg.items():
            g[pre + k][j] = v
        ga = dz1
    dh0 = _axpy(ga, dh, "dh0")
    grads = {k: jnp.stack(v) for k, v in g.items()}
    grads["meta_tokens"] = dh0[PAD:BLOCK]
    return loss, dh0, grads


SHARDED = (("meta_tokens", 1), ("fox_w_in", 2), ("fox_w_o", 1), ("swa_w_in", 2), ("swa_w_o", 1), ("mla_w_a", 1),
           ("mla_g_q", 1), ("mla_g_kv", 1), ("mla_w_uq", 2), ("mla_w_ukv", 2), ("mla_w_o", 1), ("ffn_w_in", 2),
           ("ffn_conv_w", 2), ("ffn_w_out", 1))
REPLICATED = ("ln1_g", "ln1_b", "ln2_g", "ln2_b", "fox_b_f", "swa_sinks", "ffn_conv_b")
WEIGHTS = ("meta_tokens", "ln1_g", "ln1_b", "ln2_g", "ln2_b", "fox_w_in", "fox_b_f", "fox_w_o", "swa_w_in",
           "swa_sinks", "swa_w_o", "mla_w_a", "mla_g_q", "mla_g_kv", "mla_w_uq", "mla_w_ukv", "mla_w_o", "ffn_w_in",
           "ffn_conv_w", "ffn_conv_b", "ffn_w_out")


def _rows(n):
    return -(-n // ROW)


def _pack(arrs, multiple):
    parts = []
    for a in arrs:
        n = math.prod(a.shape)
        parts.append(jnp.pad(a.reshape(-1), (0, _rows(n) * ROW - n)).reshape(-1, ROW))
    total = sum(p.shape[0] for p in parts)
    pad = -total % multiple
    if pad:
        parts.append(jnp.zeros((pad, ROW), parts[0].dtype))
    return jnp.concatenate(parts, axis=0)


def _unpack(flat, shapes):
    out, r = [], 0
    for s in shapes:
        n = math.prod(s)
        out.append(flat[r:r + _rows(n)].reshape(-1)[:n].reshape(s))
        r += _rows(n)
    return out


def _pack_bf16(w, names):
    return _pack([_bf(w[n]) if n in MATMUL_WEIGHTS else lax.bitcast_convert_type(w[n], BF16) for n in names], 2 * ROW)


HBM_SPEC = pl.BlockSpec(memory_space=pltpu.HBM)


def _place():
    x, y, c = lax.axis_index("x"), lax.axis_index("y"), lax.axis_index("c")
    chips = [(1 - x, y), (x, 1 - y), (1 - x, 1 - y)]
    return x, y, c, chips


def _gather_weights(shard):
    R = shard.shape[0]
    Rh = R // 2

    def body(s_ref, o_ref, send_sems, recv_sems):
        x, y, c, chips = _place()
        sib = (x, y, 1 - c)

        def half(k, hc):
            return o_ref.at[k, pl.ds(hc * Rh, Rh), :]

        def copy(j, src, dst, to):
            return pltpu.make_async_remote_copy(src_ref=src, dst_ref=dst, send_sem=send_sems.at[j],
                                                recv_sem=recv_sems.at[j], device_id=to, device_id_type=MESH)

        me = 2 * x + y
        first = [copy(j, s_ref.at[pl.ds(c * Rh, Rh), :], half(me, c), (tx, ty, c)) for j, (tx, ty) in enumerate(chips)]
        for cp in first:
            cp.start()
        passed = []
        for j, (tx, ty) in enumerate(chips):
            k = 2 * tx + ty
            copy(j, half(k, c), half(k, c), (tx, ty, c)).wait_recv()
            fw = copy(3 + j, half(k, c), half(k, c), sib)
            fw.start()
            passed.append(fw)
        for j, (tx, ty) in enumerate(chips):
            k = 2 * tx + ty
            copy(3 + j, half(k, 1 - c), half(k, 1 - c), sib).wait_recv()
        for cp in first + passed:
            cp.wait_send()

    return pl.pallas_call(
        body, name="gather_weights", out_shape=jax.ShapeDtypeStruct((N_CHIPS, R, ROW), shard.dtype),
        in_specs=[HBM_SPEC], out_specs=HBM_SPEC,
        scratch_shapes=[pltpu.SemaphoreType.DMA((6,)), pltpu.SemaphoreType.DMA((6,))],
    )(shard)


def _swap_halves(G):
    R = G.shape[1]
    Rh = R // 2

    def body(g_ref, a_ref, send_sem, recv_sem):
        x, y, c, _ = _place()
        cp = pltpu.make_async_remote_copy(src_ref=g_ref.at[:, pl.ds((1 - c) * Rh, Rh), :], dst_ref=a_ref,
                                          send_sem=send_sem, recv_sem=recv_sem, device_id=(x, y, 1 - c),
                                          device_id_type=MESH)
        cp.start()
        cp.wait()

    return pl.pallas_call(
        body, name="reduce_swap_halves", out_shape=jax.ShapeDtypeStruct((N_CHIPS, Rh, ROW), G.dtype),
        in_specs=[HBM_SPEC], out_specs=HBM_SPEC,
        scratch_shapes=[pltpu.SemaphoreType.DMA, pltpu.SemaphoreType.DMA],
    )(G)


def _exchange_chips(P):
    def body(p_ref, b_ref, send_sems, recv_sems):
        x, y, c, chips = _place()
        me = 2 * x + y

        def copy(j, src, dst, to):
            return pltpu.make_async_remote_copy(src_ref=src, dst_ref=dst, send_sem=send_sems.at[j],
                                                recv_sem=recv_sems.at[j], device_id=to, device_id_type=MESH)

        sends = [copy(j, p_ref.at[2 * tx + ty], b_ref.at[me], (tx, ty, c)) for j, (tx, ty) in enumerate(chips)]
        for cp in sends:
            cp.start()
        for j, (tx, ty) in enumerate(chips):
            k = 2 * tx + ty
            copy(j, p_ref.at[k], b_ref.at[k], (tx, ty, c)).wait_recv()
        for cp in sends:
            cp.wait_send()

    return pl.pallas_call(
        body, name="reduce_exchange_chips", out_shape=jax.ShapeDtypeStruct(P.shape, P.dtype),
        in_specs=[HBM_SPEC], out_specs=HBM_SPEC,
        scratch_shapes=[pltpu.SemaphoreType.DMA((3,)), pltpu.SemaphoreType.DMA((3,))],
    )(P)


def _swap_reduced(Fh):
    def body(f_ref, o_ref, send_sem, recv_sem):
        x, y, c, _ = _place()
        cp = pltpu.make_async_remote_copy(src_ref=f_ref, dst_ref=o_ref, send_sem=send_sem, recv_sem=recv_sem,
                                          device_id=(x, y, 1 - c), device_id_type=MESH)
        cp.start()
        cp.wait()

    return pl.pallas_call(
        body, name="reduce_swap_reduced", out_shape=jax.ShapeDtypeStruct(Fh.shape, Fh.dtype),
        in_specs=[HBM_SPEC], out_specs=HBM_SPEC,
        scratch_shapes=[pltpu.SemaphoreType.DMA, pltpu.SemaphoreType.DMA],
    )(Fh)


def _gather_small(v):
    m_per = v.shape[0]

    def body(x_ref, out_ref, send_sems, recv_sems, local_sem):
        x, y, c, chips = _place()
        me, sibling = (x, y, c), (x, y, 1 - c)

        def rows(px, py, pc):
            return out_ref.at[pl.ds((4 * px + 2 * py + pc) * m_per, m_per), :]

        def copy(k, block, to, src=None):
            return pltpu.make_async_remote_copy(src_ref=rows(*block) if src is None else src, dst_ref=rows(*block),
                                                send_sem=send_sems.at[k], recv_sem=recv_sems.at[k], device_id=to,
                                                device_id_type=MESH)

        mine = pltpu.make_async_copy(x_ref, rows(*me), local_sem)
        mine.start()
        first = [copy(0, me, sibling, src=x_ref)]
        first += [copy(1 + j, me, (*chip, c), src=x_ref) for j, chip in enumerate(chips)]
        for cp in first:
            cp.start()
        passed = [copy(4 + j, (*chip, c), sibling) for j, chip in enumerate(chips)]
        for j, chip in enumerate(chips):
            copy(1 + j, (*chip, c), me).wait_recv()
            passed[j].start()
        copy(0, sibling, me).wait_recv()
        for j, chip in enumerate(chips):
            copy(4 + j, (*chip, 1 - c), me).wait_recv()
        for cp in first + passed:
            cp.wait_send()
        mine.wait()

    return pl.pallas_call(
        body, name="gather_small", out_shape=jax.ShapeDtypeStruct((N_DEV * m_per, ROW), v.dtype),
        in_specs=[pl.BlockSpec(memory_space=pltpu.VMEM)], out_specs=pl.BlockSpec(memory_space=pltpu.VMEM),
        scratch_shapes=[pltpu.SemaphoreType.DMA((7,)), pltpu.SemaphoreType.DMA((7,)), pltpu.SemaphoreType.DMA],
    )(v)


def _sum_slots(a, n, name):
    M = a.shape[0] // n
    tr = _pick(M, (512, 256, 128, 64, 40, 8))
    nb = M // tr

    def body(*refs):
        acc = refs[0][...].astype(F32)
        for r in refs[1:-1]:
            acc = acc + r[...].astype(F32)
        refs[-1][...] = acc

    specs = [pl.BlockSpec((tr, ROW), functools.partial(lambda i, k: (k * nb + i, 0), k=k)) for k in range(n)]
    return pl.pallas_call(body, name=name, grid=(nb,), in_specs=specs,
                          out_specs=pl.BlockSpec((tr, ROW), lambda i: (i, 0)),
                          out_shape=jax.ShapeDtypeStruct((M, ROW), F32), compiler_params=_params("parallel"))(*([a] * n))


def _add(a, b, name, out_dtype):
    M = a.shape[0]
    tr = _pick(M, (512, 256, 128, 64, 40, 8))

    def body(a_ref, b_ref, o_ref):
        o_ref[...] = (a_ref[...] + b_ref[...]).astype(out_dtype)

    row = pl.BlockSpec((tr, ROW), lambda i: (i, 0))
    return pl.pallas_call(body, name=name, grid=(M // tr,), in_specs=[row, row], out_specs=row,
                          out_shape=jax.ShapeDtypeStruct((M, ROW), out_dtype), compiler_params=_params("parallel"))(a, b)


def _adamw(g, w, m, v, name):
    shp = w.shape
    N = shp[-1]
    M = math.prod(shp[:-1])
    g, w, m, v = (a.reshape(M, N) for a in (g, w, m, v))
    tr = _pick(M, tuple(t for t in (512, 256, 128, 64, 40, 32, 16, 8) if t * N <= 256 * 1024))
    c1 = 1.0 - ADAM_B1 ** ADAM_STEP
    c2 = 1.0 - ADAM_B2 ** ADAM_STEP

    def body(g_ref, w_ref, m_ref, v_ref, d_ref, nm_ref, nv_ref):
        gg = g_ref[...]
        nm = ADAM_B1 * m_ref[...] + (1.0 - ADAM_B1) * gg
        nv = ADAM_B2 * v_ref[...] + (1.0 - ADAM_B2) * (gg * gg)
        nm_ref[...] = nm
        nv_ref[...] = nv
        d_ref[...] = -ADAM_LR * ((nm / c1) / (jnp.sqrt(nv / c2) + ADAM_EPS) + ADAM_WD * w_ref[...])

    row = pl.BlockSpec((tr, N), lambda i: (i, 0))
    shape = jax.ShapeDtypeStruct((M, N), F32)
    outs = pl.pallas_call(body, name=name, grid=(M // tr,), in_specs=[row] * 4, out_specs=[row] * 3,
                          out_shape=[shape] * 3, compiler_params=_params("parallel"))(g, w, m, v)
    return [o.reshape(shp) for o in outs]


def kernel(x, meta_tokens, ln1_g, ln1_b, ln2_g, ln2_b, fox_w_in, fox_b_f, fox_w_o, swa_w_in, swa_sinks, swa_w_o, mla_w_a, mla_g_q, mla_g_kv, mla_w_uq, mla_w_ukv, mla_w_o, ffn_w_in, ffn_conv_w, ffn_conv_b, ffn_w_out, loss_target, m_meta_tokens, m_ln1_g, m_ln1_b, m_ln2_g, m_ln2_b, m_fox_w_in, m_fox_b_f, m_fox_w_o, m_swa_w_in, m_swa_sinks, m_swa_w_o, m_mla_w_a, m_mla_g_q, m_mla_g_kv, m_mla_w_uq, m_mla_w_ukv, m_mla_w_o, m_ffn_w_in, m_ffn_conv_w, m_ffn_conv_b, m_ffn_w_out, v_meta_tokens, v_ln1_g, v_ln1_b, v_ln2_g, v_ln2_b, v_fox_w_in, v_fox_b_f, v_fox_w_o, v_swa_w_in, v_swa_sinks, v_swa_w_o, v_mla_w_a, v_mla_g_q, v_mla_g_kv, v_mla_w_uq, v_mla_w_ukv, v_mla_w_o, v_ffn_w_in, v_ffn_conv_w, v_ffn_conv_b, v_ffn_w_out):
    given = dict(locals())
    w = {n: given[n] for n in WEIGHTS}
    m = {n: given["m_" + n] for n in WEIGHTS}
    v = {n: given["v_" + n] for n in WEIGHTS}
    sh_names = [n for n, _ in SHARDED]
    sh_shapes = [w[n].shape for n in sh_names]

    me = 2 * lax.axis_index("x") + lax.axis_index("y")
    c = lax.axis_index("c")
    packed = _pack_bf16(w, sh_names)
    gathered = lax.dynamic_update_index_in_dim(_gather_weights(packed), packed, me, 0)
    full = dict(w)
    r0 = 0
    for (n, ax), s in zip(SHARDED, sh_shapes):
        cnt = math.prod(s) * (1 if n in MATMUL_WEIGHTS else 2)
        t = gathered[:, r0:r0 + _rows(cnt)].reshape(N_CHIPS, -1)[:, :cnt]
        r0 += _rows(cnt)
        t = t.reshape((N_CHIPS,) + s) if n in MATMUL_WEIGHTS else \
            lax.bitcast_convert_type(t.reshape((N_CHIPS,) + s + (2,)), F32)
        t = jnp.moveaxis(t, 0, ax)
        full[n] = t.reshape(s[:ax] + (N_CHIPS * s[ax],) + s[ax + 1:])

    loss_part, dh0, grads = _local_step(x[0], loss_target[0], full)
    loss = lax.psum(jnp.sum(loss_part), ("x", "y", "c"))
    grad_x = dh0[BLOCK:][None]

    parts = []
    for (n, ax), s in zip(SHARDED, sh_shapes):
        g_n = grads[n].reshape(s[:ax] + (N_CHIPS, s[ax]) + s[ax + 1:])
        g_n = jnp.moveaxis(g_n, ax, 0).reshape(N_CHIPS, -1)
        parts.append(jnp.pad(g_n, ((0, 0), (0, _rows(g_n.shape[1]) * ROW - g_n.shape[1]))))
    width = sum(p.shape[1] for p in parts)
    parts.append(jnp.zeros((N_CHIPS, -width % (2 * ROW * ROW)), F32))
    G = jnp.concatenate(parts, axis=1).reshape(N_CHIPS, -1, ROW)
    Rh = G.shape[1] // 2
    mine = lax.dynamic_slice_in_dim(G, c * Rh, Rh, axis=1)
    P = _add(mine.reshape(N_CHIPS * Rh, ROW), _swap_halves(G).reshape(N_CHIPS * Rh, ROW), "reduce_pair_sum", BF16)
    P = P.reshape(N_CHIPS, Rh, ROW)
    B = lax.dynamic_update_index_in_dim(_exchange_chips(P), lax.dynamic_index_in_dim(P, me, 0, keepdims=False), me, 0)
    Fh = _sum_slots(B.reshape(N_CHIPS * Rh, ROW), N_CHIPS, "reduce_chip_sum")
    other = _swap_reduced(Fh)
    Fg = jnp.concatenate([jnp.where(c == 0, Fh, other), jnp.where(c == 0, other, Fh)], axis=0)
    out = {}
    for n, g_n in zip(sh_names, _unpack(Fg, sh_shapes)):
        out["grad", n] = g_n
        out["delta", n], out["new_m", n], out["new_v", n] = _adamw(g_n, w[n], m[n], v[n], "adamw_" + n)

    rp_shapes = [w[n].shape for n in REPLICATED]
    small = _gather_small(_pack([grads[n] for n in REPLICATED], 8))
    g_r = _sum_slots(small, N_DEV, "reduce_small_sum")
    d_r, m_r, v_r = _adamw(g_r, _pack([w[n] for n in REPLICATED], 8), _pack([m[n] for n in REPLICATED], 8),
                           _pack([v[n] for n in REPLICATED], 8), "adamw_replicated")

    for kind, fr in (("grad", g_r), ("delta", d_r), ("new_m", m_r), ("new_v", v_r)):
        for n, a in zip(REPLICATED, _unpack(fr, rp_shapes)):
            out[kind, n] = a
    return (loss, grad_x, *[out[k, n] for k in ("grad", "delta", "new_m", "new_v") for n in WEIGHTS])
```

```python
import functools
import math

import numpy as np
import jax
import jax.numpy as jnp
from jax import lax
from jax.experimental import pallas as pl
from jax.experimental.pallas import tpu as pltpu

F32 = jnp.float32
BF16 = jnp.bfloat16

D_MODEL = 1024
DEPTH = 4
BLOCK = 128
N_META = 16
PAD = BLOCK - N_META
NEG = -1e30
ALPHA = (2.0 * DEPTH) ** 0.25
LN_EPS = 1e-5
RMS_EPS = 1e-6
HEADS = 16
HEAD_DIM = 64
LANES = 128
SWA_KV = 2
SWA_G = HEADS // SWA_KV
WINDOW = 128
ROPE_THETA = 500000.0
ROPE_DIM = 16
MLA_Q_LORA = 384
MLA_KV_LORA = 256
MLA_NOPE = 64
MLA_ROPE = 32
MLA_ROPE_THETA = 10000.0
D_FF = 2816
ADAM_LR = 0.001
ADAM_B1 = 0.9
ADAM_B2 = 0.999
ADAM_EPS = 1e-08
ADAM_WD = 0.01
ADAM_STEP = 10
N_CHIPS = 4
N_DEV = 8
ROW = 1024
VMEM_LIMIT = 48 * 1024 * 1024
MESH = pl.DeviceIdType.MESH
LOG2E = 1.4426950408889634
DENSE_HS = 2
DENSE_HS_FWD = 4

NN = (((1,), (0,)), ((), ()))
NT = (((1,), (1,)), ((), ()))
TN = (((0,), (0,)), ((), ()))


def _pick(n, cands):
    for c in cands:
        if n % c == 0:
            return c
    return n


def _params(*sem):
    return pltpu.CompilerParams(dimension_semantics=sem, vmem_limit_bytes=VMEM_LIMIT)


def _bf(x):
    return x if x.dtype == BF16 else x.astype(BF16)


def _mm(a, b, mode, name, out_dtype=F32):
    if mode == "nn":
        (M, K), (_, N) = a.shape, b.shape
    elif mode == "nt":
        (M, K), (N, _) = a.shape, b.shape
    else:
        (K, M), (_, N) = a.shape, b.shape
    tm = _pick(M, (1664, 1408, 1024, 640, 512, 384, 256, 128))
    tn = _pick(N, (640, 512, 384, 1408, 256, 128))
    tk = K if (K <= 1024 and mode != "tn") else _pick(K, (640, 512, 384, 1408, 256, 128))
    nk = K // tk
    dn = {"nn": NN, "nt": NT, "tn": TN}[mode]

    def body(a_ref, b_ref, o_ref, *acc):
        part = lax.dot_general(_bf(a_ref[...]), _bf(b_ref[...]), dn, preferred_element_type=F32)
        if nk == 1:
            o_ref[...] = part.astype(out_dtype)
            return
        acc_ref, = acc
        k = pl.program_id(2)

        @pl.when(k == 0)
        def _():
            acc_ref[...] = part

        @pl.when(k > 0)
        def _():
            acc_ref[...] += part

        @pl.when(k == nk - 1)
        def _():
            o_ref[...] = acc_ref[...].astype(out_dtype)

    if mode == "tn":
        a_spec = pl.BlockSpec((tk, tm), lambda i, j, k: (k, i))
    else:
        a_spec = pl.BlockSpec((tm, tk), lambda i, j, k: (i, k))
    if mode == "nt":
        b_spec = pl.BlockSpec((tn, tk), lambda i, j, k: (j, k))
    else:
        b_spec = pl.BlockSpec((tk, tn), lambda i, j, k: (k, j))
    return pl.pallas_call(
        body, name=name, grid=(M // tm, N // tn, nk),
        in_specs=[a_spec, b_spec],
        out_specs=pl.BlockSpec((tm, tn), lambda i, j, k: (i, j)),
        out_shape=jax.ShapeDtypeStruct((M, N), out_dtype),
        scratch_shapes=[pltpu.VMEM((tm, tn), F32)] if nk > 1 else [],
        compiler_params=_params("parallel", "parallel", "arbitrary"),
    )(a, b)


def _ln_fwd(h, mix, g, b, name):
    L = h.shape[0]
    tr = 128

    def body(h_ref, m_ref, g_ref, b_ref, o_ref, ob_ref, xh_ref, rs_ref):
        z = ALPHA * h_ref[...] + m_ref[...]
        mu = jnp.mean(z, axis=1, keepdims=True)
        zc = z - mu
        var = jnp.mean(zc * zc, axis=1, keepdims=True)
        rstd = lax.rsqrt(var + LN_EPS)
        xh = zc * rstd
        xh_ref[...] = xh
        rs_ref[...] = rstd
        out = xh * g_ref[...] + b_ref[...]
        o_ref[...] = out
        ob_ref[...] = out.astype(BF16)

    row = pl.BlockSpec((tr, D_MODEL), lambda i: (i, 0))
    vec = pl.BlockSpec((1, D_MODEL), lambda i: (0, 0))
    return pl.pallas_call(
        body, name=name, grid=(L // tr,),
        in_specs=[row, row, vec, vec],
        out_specs=[row, row, row, pl.BlockSpec((tr, 1), lambda i: (i, 0))],
        out_shape=[jax.ShapeDtypeStruct((L, D_MODEL), F32), jax.ShapeDtypeStruct((L, D_MODEL), BF16),
                   jax.ShapeDtypeStruct((L, D_MODEL), F32), jax.ShapeDtypeStruct((L, 1), F32)],
        compiler_params=_params("parallel"),
    )(h, mix, g.reshape(1, D_MODEL), b.reshape(1, D_MODEL))


def _ln_bwd(ga, gb, xhat, rstd, g, name):
    L = xhat.shape[0]
    tr = 128
    two = ga is not None

    def body(*refs):
        if two:
            ga_ref, gb_ref, xh_ref, rs_ref, g_ref, dz_ref, dzb_ref, dg_ref, db_ref = refs
            dy = ALPHA * ga_ref[...] + gb_ref[...]
        else:
            gb_ref, xh_ref, rs_ref, g_ref, dz_ref, dzb_ref, dg_ref, db_ref = refs
            dy = gb_ref[...]
        xh = xh_ref[...]
        dxh = dy * g_ref[...]
        c1 = jnp.mean(dxh, axis=1, keepdims=True)
        c2 = jnp.mean(dxh * xh, axis=1, keepdims=True)
        dz = rs_ref[...] * (dxh - c1 - xh * c2)
        dz_ref[...] = dz
        dzb_ref[...] = dz.astype(BF16)

        @pl.when(pl.program_id(0) == 0)
        def _():
            dg_ref[...] = jnp.zeros_like(dg_ref)
            db_ref[...] = jnp.zeros_like(db_ref)

        dg_ref[...] += jnp.sum(dy * xh, axis=0, keepdims=True)
        db_ref[...] += jnp.sum(dy, axis=0, keepdims=True)

    row = pl.BlockSpec((tr, D_MODEL), lambda i: (i, 0))
    vec = pl.BlockSpec((1, D_MODEL), lambda i: (0, 0))
    ins = ([ga] if two else []) + [gb, xhat, rstd, g.reshape(1, D_MODEL)]
    specs = ([row] if two else []) + [row, row, pl.BlockSpec((tr, 1), lambda i: (i, 0)), vec]
    dz, dzb, dg, db = pl.pallas_call(
        body, name=name, grid=(L // tr,),
        in_specs=specs, out_specs=[row, row, vec, vec],
        out_shape=[jax.ShapeDtypeStruct((L, D_MODEL), F32), jax.ShapeDtypeStruct((L, D_MODEL), BF16),
                   jax.ShapeDtypeStruct((1, D_MODEL), F32), jax.ShapeDtypeStruct((1, D_MODEL), F32)],
        compiler_params=_params("arbitrary"),
    )(*ins)
    return dz, dzb, dg[0], db[0]


def _axpy(a, b, name):
    L, N = a.shape
    tr = 128

    def body(a_ref, b_ref, o_ref):
        o_ref[...] = ALPHA * a_ref[...] + b_ref[...]

    row = pl.BlockSpec((tr, N), lambda i: (i, 0))
    return pl.pallas_call(body, name=name, grid=(L // tr,), in_specs=[row, row], out_specs=row,
                          out_shape=jax.ShapeDtypeStruct((L, N), F32), compiler_params=_params("parallel"))(a, b)


def _shift_down(cur, prev8, n):
    out = pltpu.roll(cur, n, 0)
    rows = lax.broadcasted_iota(jnp.int32, (8, cur.shape[1]), 0)
    top = out[0:8]
    for r in range(n):
        top = jnp.where(rows == r, prev8[8 - n + r:8 - n + r + 1, :], top)
    return top if cur.shape[0] == 8 else jnp.concatenate([top, out[8:]], axis=0)


def _shift_up(cur, next8, n):
    tr = cur.shape[0]
    out = pltpu.roll(cur, tr - n, 0)
    rows = lax.broadcasted_iota(jnp.int32, (8, cur.shape[1]), 0)
    bottom = out[tr - 8:tr]
    for r in range(n):
        bottom = jnp.where(rows == 8 - n + r, next8[r:r + 1, :], bottom)
    return jnp.concatenate([out[:tr - 8], bottom], axis=0)


def _silu(x):
    return x / (1.0 + jnp.exp(-x))


def _conv(cur, prev8, cw_ref, cb_ref):
    d2, d1 = _shift_down(cur, prev8, 2), _shift_down(cur, prev8, 1)
    return cb_ref[...] + d2 * cw_ref[0:1, :] + d1 * cw_ref[1:2, :] + cur * cw_ref[2:3, :], d2, d1


def _valid_rows(i, tr, u_ref, up_ref):
    rows = i * tr + lax.broadcasted_iota(jnp.int32, u_ref.shape, 0)
    prow = i * tr - 8 + lax.broadcasted_iota(jnp.int32, up_ref.shape, 0)
    return jnp.where(rows >= PAD, u_ref[...], 0.0), jnp.where(prow >= PAD, up_ref[...], 0.0), rows


def _conv_glu_fwd(u, cw, cb, name):
    L, F2 = u.shape
    F = F2 // 2
    tr = 128

    def body(u_ref, up_ref, cw_ref, cb_ref, a_ref):
        cur, prev, _ = _valid_rows(pl.program_id(0), tr, u_ref, up_ref)
        y, _, _ = _conv(cur, prev, cw_ref, cb_ref)
        a_ref[...] = (_silu(y[:, :F]) * y[:, F:]).astype(BF16)

    return pl.pallas_call(
        body, name=name, grid=(L // tr,),
        in_specs=[pl.BlockSpec((tr, F2), lambda i: (i, 0)),
                  pl.BlockSpec((8, F2), lambda i: (jnp.maximum(i * (tr // 8) - 1, 0), 0)),
                  pl.BlockSpec((3, F2), lambda i: (0, 0)),
                  pl.BlockSpec((1, F2), lambda i: (0, 0))],
        out_specs=pl.BlockSpec((tr, F), lambda i: (i, 0)),
        out_shape=jax.ShapeDtypeStruct((L, F), BF16),
        compiler_params=_params("parallel"),
    )(u, u, cw, cb.reshape(1, F2))


def _conv_glu_bwd(da, u, cw, cb, name):
    L, F2 = u.shape
    F = F2 // 2
    tr = 128
    nb = L // tr

    def dy_of(yv, dav):
        g, val = yv[:, :F], yv[:, F:]
        sg = 1.0 / (1.0 + jnp.exp(-g))
        dg = dav * val * (sg * (1.0 + g * (1.0 - sg)))
        dv = dav * (g * sg)
        return jnp.concatenate([dg, dv], axis=1)

    def body(da_ref, dan_ref, u_ref, up_ref, un_ref, cw_ref, cb_ref, du_ref, dcw_ref, dcb_ref):
        i = pl.program_id(0)
        cur, prev, rows = _valid_rows(i, tr, u_ref, up_ref)
        y, d2, d1 = _conv(cur, prev, cw_ref, cb_ref)
        dy = dy_of(y, da_ref[...])
        yn, _, _ = _conv(un_ref[...], cur[tr - 8:tr, :], cw_ref, cb_ref)
        dyn = jnp.where(i < nb - 1, dy_of(yn, dan_ref[...]), 0.0)
        du = dy * cw_ref[2:3, :] + _shift_up(dy, dyn, 1) * cw_ref[1:2, :] + _shift_up(dy, dyn, 2) * cw_ref[0:1, :]
        du_ref[...] = jnp.where(rows >= PAD, du, 0.0).astype(BF16)

        @pl.when(i == 0)
        def _():
            dcw_ref[...] = jnp.zeros_like(dcw_ref)
            dcb_ref[...] = jnp.zeros_like(dcb_ref)

        dcw_ref[0:1, :] += jnp.sum(dy * d2, axis=0, keepdims=True)
        dcw_ref[1:2, :] += jnp.sum(dy * d1, axis=0, keepdims=True)
        dcw_ref[2:3, :] += jnp.sum(dy * cur, axis=0, keepdims=True)
        dcb_ref[...] += jnp.sum(dy, axis=0, keepdims=True)

    nxt = lambda i: (jnp.minimum((i + 1) * (tr // 8), L // 8 - 1), 0)
    prv = lambda i: (jnp.maximum(i * (tr // 8) - 1, 0), 0)
    du, dcw, dcb = pl.pallas_call(
        body, name=name, grid=(nb,),
        in_specs=[pl.BlockSpec((tr, F), lambda i: (i, 0)), pl.BlockSpec((8, F), nxt),
                  pl.BlockSpec((tr, F2), lambda i: (i, 0)), pl.BlockSpec((8, F2), prv), pl.BlockSpec((8, F2), nxt),
                  pl.BlockSpec((3, F2), lambda i: (0, 0)), pl.BlockSpec((1, F2), lambda i: (0, 0))],
        out_specs=[pl.BlockSpec((tr, F2), lambda i: (i, 0)), pl.BlockSpec((3, F2), lambda i: (0, 0)),
                   pl.BlockSpec((1, F2), lambda i: (0, 0))],
        out_shape=[jax.ShapeDtypeStruct((L, F2), BF16), jax.ShapeDtypeStruct((3, F2), F32),
                   jax.ShapeDtypeStruct((1, F2), F32)],
        compiler_params=_params("arbitrary"),
    )(da, da, u, u, u, cw, cb.reshape(1, F2))
    return du, dcw, dcb[0]


def _dense_mask(qpos, kpos):
    return (kpos <= qpos) & (kpos >= PAD)


def _tables(pairs):
    qt, kt, ft = [], [], []
    for grp in pairs:
        for n, (qb, kb, msk) in enumerate(grp):
            qt.append(qb)
            kt.append(kb)
            ft.append((1 if n == 0 else 0) | (2 if n == len(grp) - 1 else 0) | (4 if msk else 0))
    return tuple(jnp.asarray(np.asarray(t, np.int32)) for t in (qt, kt, ft))


class _Dense:
    mask = staticmethod(_dense_mask)

    def __init__(self, L, pad_in_cr2):
        self.T = T = 640 if L % 640 == 0 else 128
        nb = L // T
        m = lambda qb, kb: kb == qb or (kb == 0 and not pad_in_cr2) or (qb * T < PAD)
        self.q_major = _tables([[(qb, kb, m(qb, kb)) for kb in range(qb + 1)] for qb in range(nb)])
        self.k_major = _tables([[(qb, kb, m(qb, kb)) for qb in range(kb, nb)] for kb in range(nb)])


CHUNK = 32


def _chunk_scores(raw, r, c, cr, masked, mask, T, qb, kb):
    s = raw * c
    if cr is not None:
        s = s - cr
    live = None
    if masked:
        qpos = qb * T + r * CHUNK + lax.broadcasted_iota(jnp.int32, (CHUNK, T), 0)
        kpos = kb * T + lax.broadcasted_iota(jnp.int32, (CHUNK, T), 1)
        live = mask(qpos, kpos) & (qpos >= PAD)
        s = jnp.where(live, s, NEG)
    return s, live


def _prob(s, lse, live):
    p = jnp.exp2(s - lse)
    return p if live is None else jnp.where(live, p, 0.0)


def _both(flag, fn):
    pl.when(flag != 0)(lambda: fn(True))
    pl.when(flag == 0)(lambda: fn(False))


def _flash_fwd(cfg, qa, q_off, ka, k_off, va, v_off, H, hs, scale, name, cr2=None):
    L = qa.shape[0]
    T = cfg.T
    qt, kt, ft = cfg.q_major
    npairs = qt.shape[0]
    c = scale * LOG2E
    decay = cr2 is not None
    W = hs * LANES

    def body(qt_ref, kt_ref, ft_ref, *refs):
        it = iter(refs)
        q_ref, k_ref, v_ref = next(it), next(it), next(it)
        cr_ref = next(it) if decay else None
        o_ref, lse_ref, m_sc, l_sc, acc_sc = next(it), next(it), next(it), next(it), next(it)
        n = pl.program_id(1)
        qb, kb, f = qt_ref[n], kt_ref[n], ft_ref[n]

        @pl.when((f & 1) != 0)
        def _():
            m_sc[...] = jnp.full(m_sc.shape, NEG, F32)
            l_sc[...] = jnp.zeros_like(l_sc)
            acc_sc[...] = jnp.zeros_like(acc_sc)

        def step(masked):
            for i in range(hs):
                cols = slice(i * LANES, (i + 1) * LANES)
                s = lax.dot_general(_bf(q_ref[:, cols]), _bf(k_ref[:, cols]), NT, preferred_element_type=F32) * c
                if decay:
                    s = s - cr_ref[i]
                if masked:
                    qpos = qb * T + lax.broadcasted_iota(jnp.int32, (T, T), 0)
                    kpos = kb * T + lax.broadcasted_iota(jnp.int32, (T, T), 1)
                    s = jnp.where(cfg.mask(qpos, kpos), s, NEG)
                m_prev = m_sc[i]
                m_new = jnp.maximum(m_prev, jnp.max(s, axis=1, keepdims=True))
                alpha = jnp.exp2(m_prev - m_new)
                p = jnp.exp2(s - m_new)
                l_sc[i] = alpha * l_sc[i] + jnp.sum(p, axis=1, keepdims=True)
                acc_sc[i] = alpha * acc_sc[i] + lax.dot_general(p.astype(BF16), _bf(v_ref[:, cols]), NN,
                                                                preferred_element_type=F32)
                m_sc[i] = m_new

        _both(f & 4, step)

        @pl.when((f & 2) != 0)
        def _():
            for i in range(hs):
                cols = slice(i * LANES, (i + 1) * LANES)
                l = l_sc[i]
                o_ref[:, cols] = acc_sc[i] / l
                lse_ref[:, cols] = jnp.broadcast_to(m_sc[i] + jnp.log(l) * LOG2E, (T, LANES))

    qrow = lambda off: pl.BlockSpec((T, W), lambda h, n, qt, kt, ft: (qt[n], off // hs + h))
    krow = lambda off: pl.BlockSpec((T, W), lambda h, n, qt, kt, ft: (kt[n], off // hs + h))
    ins, specs = [qa, ka, va], [qrow(q_off), krow(k_off), krow(v_off)]
    if decay:
        ins.append(cr2)
        specs.append(pl.BlockSpec((hs, 1, T), lambda h, n, qt, kt, ft: (h, 0, kt[n])))
    full = jax.ShapeDtypeStruct((L, H * LANES), F32)
    return pl.pallas_call(
        body, name=name, out_shape=[full, full],
        grid_spec=pltpu.PrefetchScalarGridSpec(
            num_scalar_prefetch=3, grid=(H // hs, npairs), in_specs=specs, out_specs=[qrow(0), qrow(0)],
            scratch_shapes=[pltpu.VMEM((hs, T, 1), F32), pltpu.VMEM((hs, T, 1), F32), pltpu.VMEM((hs, T, LANES), F32)]),
        compiler_params=_params("parallel", "arbitrary"),
    )(qt, kt, ft, *ins)


def _flash_bwd(cfg, qa, q_off, ka, k_off, va, v_off, do, lse2, delta, H, hs, scale, name, cr2=None, lane_sums=False):
    L = qa.shape[0]
    T = cfg.T
    qt, kt, ft = cfg.k_major
    npairs = qt.shape[0]
    c = scale * LOG2E
    decay = cr2 is not None
    W = hs * LANES

    def body(qt_ref, kt_ref, ft_ref, *refs):
        it = iter(refs)
        q_ref, k_ref, v_ref = next(it), next(it), next(it)
        cr_ref = next(it) if decay else None
        do_ref, lse_ref, dl_ref, dk_ref, dv_ref, dq_ref, dk_sc, dv_sc, s_sc, dp_sc, p_sc, ds_sc = (
            next(it) for _ in range(12))
        n = pl.program_id(1)
        qb, kb, f = qt_ref[n], kt_ref[n], ft_ref[n]

        @pl.when(n == 0)
        def _():
            dq_ref[...] = jnp.zeros_like(dq_ref)

        @pl.when((f & 1) != 0)
        def _():
            dk_sc[...] = jnp.zeros_like(dk_sc)
            dv_sc[...] = jnp.zeros_like(dv_sc)

        def step(masked):
            last = lax.broadcasted_iota(jnp.int32, (T, LANES), 1) == LANES - 1
            rows = pl.ds(pl.multiple_of(qb * T, T), T)
            for i in range(hs):
                cols = slice(i * LANES, (i + 1) * LANES)
                q, k, v, dob = _bf(q_ref[:, cols]), _bf(k_ref[:, cols]), _bf(v_ref[:, cols]), _bf(do_ref[:, cols])
                k1 = jnp.where(last, 1.0, k_ref[:, cols]).astype(BF16) if lane_sums else k
                q1 = jnp.where(last, 1.0, q_ref[:, cols]).astype(BF16) if lane_sums else q
                cr = cr_ref[i] if decay else None
                s_sc[i] = lax.dot_general(q, k, NT, preferred_element_type=F32)
                dp_sc[i] = lax.dot_general(dob, v, NT, preferred_element_type=F32)

                def chunk(r, carry, i=i, cr=cr):
                    rs = pl.ds(r * CHUNK, CHUNK)
                    s, live = _chunk_scores(s_sc[i, rs, :], r, c, cr, masked, cfg.mask, T, qb, kb)
                    p = _prob(s, lse_ref[rs, i * LANES:i * LANES + 1], live)
                    p_sc[i, rs, :] = p.astype(BF16)
                    ds_sc[i, rs, :] = (p * (dp_sc[i, rs, :] - dl_ref[rs, i * LANES:i * LANES + 1])).astype(BF16)
                    return carry

                for r in range(T // CHUNK):
                    chunk(r, 0)
                dv_sc[i] += lax.dot_general(p_sc[i], dob, TN, preferred_element_type=F32)
                dk_sc[i] += lax.dot_general(ds_sc[i], q1, TN, preferred_element_type=F32)
                dq_ref[rows, cols] += lax.dot_general(ds_sc[i], k1, NN, preferred_element_type=F32)

        _both(f & 4, step)

        @pl.when((f & 2) != 0)
        def _():
            for i in range(hs):
                cols = slice(i * LANES, (i + 1) * LANES)
                dk_ref[:, cols] = dk_sc[i] * scale
                dv_ref[:, cols] = dv_sc[i]

        @pl.when(n == npairs - 1)
        def _():
            dq_ref[...] = dq_ref[...] * scale

    qrow = lambda off: pl.BlockSpec((T, W), lambda h, n, qt, kt, ft: (qt[n], off // hs + h))
    krow = lambda off: pl.BlockSpec((T, W), lambda h, n, qt, kt, ft: (kt[n], off // hs + h))
    ins, specs = [qa, ka, va], [qrow(q_off), krow(k_off), krow(v_off)]
    if decay:
        ins.append(cr2)
        specs.append(pl.BlockSpec((hs, 1, T), lambda h, n, qt, kt, ft: (h, 0, kt[n])))
    ins += [do, lse2, delta]
    specs += [qrow(0), qrow(0), qrow(0)]
    full = jax.ShapeDtypeStruct((L, H * LANES), F32)
    return pl.pallas_call(
        body, name=name, out_shape=[full, full, full],
        grid_spec=pltpu.PrefetchScalarGridSpec(
            num_scalar_prefetch=3, grid=(H // hs, npairs), in_specs=specs,
            out_specs=[krow(0), krow(0), pl.BlockSpec((L, W), lambda h, n, qt, kt, ft: (0, h))],
            scratch_shapes=[pltpu.VMEM((hs, T, LANES), F32), pltpu.VMEM((hs, T, LANES), F32),
                            pltpu.VMEM((hs, T, T), F32), pltpu.VMEM((hs, T, T), F32),
                            pltpu.VMEM((hs, T, T), BF16), pltpu.VMEM((hs, T, T), BF16)]),
        compiler_params=_params("parallel", "arbitrary"),
    )(qt, kt, ft, *ins)


def _swa_parts(qb, q_ref, km_ref, kp_ref, kc_ref, vm_ref, vp_ref, vc_ref, c):
    G, B = SWA_G, BLOCK
    q = jnp.concatenate([_bf(q_ref[:, i * LANES:(i + 1) * LANES]) for i in range(G)], axis=0)
    kc = jnp.concatenate([_bf(km_ref[...]), _bf(kp_ref[...]), _bf(kc_ref[...])], axis=0)
    vc = jnp.concatenate([_bf(vm_ref[...]), _bf(vp_ref[...]), _bf(vc_ref[...])], axis=0)
    s = lax.dot_general(q, kc, NT, preferred_element_type=F32) * c
    row = lax.broadcasted_iota(jnp.int32, (G * B, 3 * B), 0)
    col = lax.broadcasted_iota(jnp.int32, (G * B, 3 * B), 1)
    qpos = qb * B + (row & (B - 1))
    kpos = jnp.where(col < B, col, jnp.where(col < 2 * B, (qb - 1) * B + col - B, qb * B + col - 2 * B))
    d = qpos - kpos
    live = ((col < B) & (kpos >= PAD) & (kpos <= qpos)) | ((col >= B) & (kpos >= B) & (d >= 0) & (d < WINDOW))
    return q, kc, vc, jnp.where(live, s, NEG), live


def _stack_col(ref):
    return jnp.concatenate([ref[:, i * LANES:i * LANES + 1] for i in range(SWA_G)], axis=0)


def _swa_specs(nqk):
    G, B = SWA_G, BLOCK
    qrow = pl.BlockSpec((B, G * LANES), lambda hk, qb: (qb, hk))
    kv = lambda off, blk: pl.BlockSpec((B, LANES), lambda hk, qb: (blk(qb), off + hk))
    zero, prev, cur = (lambda qb: 0), (lambda qb: jnp.maximum(qb - 1, 0)), (lambda qb: qb)
    keys = [kv(HEADS, zero), kv(HEADS, prev), kv(HEADS, cur)]
    vals = [kv(nqk, zero), kv(nqk, prev), kv(nqk, cur)]
    return qrow, keys, vals


def _swa_attn_fwd(qk, proj, sink2, scale, name):
    L = qk.shape[0]
    G, B = SWA_G, BLOCK
    nqk = HEADS + SWA_KV
    c = scale * LOG2E

    def body(q_ref, km_ref, kp_ref, kc_ref, vm_ref, vp_ref, vc_ref, sink_ref, o_ref, lse_ref):
        qb = pl.program_id(1)
        q, kc, vc, s, live = _swa_parts(qb, q_ref, km_ref, kp_ref, kc_ref, vm_ref, vp_ref, vc_ref, c)
        sink = jnp.concatenate([jnp.broadcast_to(sink_ref[:, i * LANES:i * LANES + 1], (B, 1)) for i in range(G)], axis=0)
        m = jnp.maximum(jnp.max(s, axis=1, keepdims=True), sink)
        p = jnp.exp2(s - m)
        l = jnp.sum(p, axis=1, keepdims=True) + jnp.exp2(sink - m)
        o = lax.dot_general(p.astype(BF16), vc, NN, preferred_element_type=F32) / l
        lse = m + jnp.log(l) * LOG2E
        for i in range(G):
            o_ref[:, i * LANES:(i + 1) * LANES] = o[i * B:(i + 1) * B]
            lse_ref[:, i * LANES:(i + 1) * LANES] = jnp.broadcast_to(lse[i * B:(i + 1) * B], (B, LANES))

    qrow, keys, vals = _swa_specs(nqk)
    shape = jax.ShapeDtypeStruct((L, HEADS * LANES), F32)
    return pl.pallas_call(
        body, name=name, grid=(SWA_KV, L // B),
        in_specs=[qrow] + keys + vals + [pl.BlockSpec((1, G * LANES), lambda hk, qb: (0, hk))],
        out_specs=[qrow, qrow], out_shape=[shape, shape],
        compiler_params=_params("parallel", "parallel"),
    )(qk, qk, qk, qk, proj, proj, proj, sink2)


def _swa_attn_bwd(qk, proj, do, lse2, delta, scale, name):
    L = qk.shape[0]
    G, B = SWA_G, BLOCK
    nqk = HEADS + SWA_KV
    c = scale * LOG2E

    def body(q_ref, km_ref, kp_ref, kc_ref, vm_ref, vp_ref, vc_ref, do_ref, lse_ref, dl_ref, dq_ref, dk_ref, dv_ref):
        qb = pl.program_id(1)

        @pl.when(qb == 0)
        def _():
            dk_ref[...] = jnp.zeros_like(dk_ref)
            dv_ref[...] = jnp.zeros_like(dv_ref)

        q, kc, vc, s, live = _swa_parts(qb, q_ref, km_ref, kp_ref, kc_ref, vm_ref, vp_ref, vc_ref, c)
        dob = jnp.concatenate([_bf(do_ref[:, i * LANES:(i + 1) * LANES]) for i in range(G)], axis=0)
        p = jnp.where(live, jnp.exp2(s - _stack_col(lse_ref)), 0.0)
        dp = lax.dot_general(dob, vc, NT, preferred_element_type=F32)
        dsb = (p * (dp - _stack_col(dl_ref))).astype(BF16)
        dq = lax.dot_general(dsb, kc, NN, preferred_element_type=F32) * scale
        for i in range(G):
            dq_ref[:, i * LANES:(i + 1) * LANES] = dq[i * B:(i + 1) * B]
        dkc = lax.dot_general(dsb, q, TN, preferred_element_type=F32) * scale
        dvc = lax.dot_general(p.astype(BF16), dob, TN, preferred_element_type=F32)
        starts = (0, pl.multiple_of(jnp.maximum(qb - 1, 0) * B, B), pl.multiple_of(qb * B, B))
        for n, st in enumerate(starts):
            dk_ref[pl.ds(st, B), :] += dkc[n * B:(n + 1) * B]
            dv_ref[pl.ds(st, B), :] += dvc[n * B:(n + 1) * B]

    qrow, keys, vals = _swa_specs(nqk)
    res = pl.BlockSpec((L, LANES), lambda hk, qb: (0, hk))
    return pl.pallas_call(
        body, name=name, grid=(SWA_KV, L // B),
        in_specs=[qrow] + keys + vals + [qrow, qrow, qrow],
        out_specs=[qrow, res, res],
        out_shape=[jax.ShapeDtypeStruct((L, HEADS * LANES), F32), jax.ShapeDtypeStruct((L, SWA_KV * LANES), F32),
                   jax.ShapeDtypeStruct((L, SWA_KV * LANES), F32)],
        compiler_params=_params("parallel", "arbitrary"),
    )(qk, qk, qk, qk, proj, proj, proj, do, lse2, delta)


def _delta(do, o, name):
    L, HW = do.shape
    tr = BLOCK

    def body(do_ref, o_ref, d_ref):
        for h in range(HW // LANES):
            cols = slice(h * LANES, (h + 1) * LANES)
            d = jnp.sum(do_ref[:, cols].astype(F32) * o_ref[:, cols], axis=1, keepdims=True)
            d_ref[:, cols] = jnp.broadcast_to(d, (tr, LANES))

    spec = pl.BlockSpec((tr, HW), lambda i: (i, 0))
    return pl.pallas_call(body, name=name, grid=(L // tr,), in_specs=[spec, spec], out_specs=spec,
                          out_shape=jax.ShapeDtypeStruct((L, HW), F32), compiler_params=_params("parallel"))(do, o)


def _sink_grad(lse2, delta, sink2, name):
    L, HW = lse2.shape
    tr = 128

    def body(lse_ref, dl_ref, s_ref, o_ref):
        @pl.when(pl.program_id(0) == 0)
        def _():
            o_ref[...] = jnp.zeros_like(o_ref)

        o_ref[...] -= jnp.sum(jnp.exp2(s_ref[...] - lse_ref[...]) * dl_ref[...], axis=0, keepdims=True)

    row = pl.BlockSpec((tr, HW), lambda i: (i, 0))
    vec = pl.BlockSpec((1, HW), lambda i: (0, 0))
    return pl.pallas_call(body, name=name, grid=(L // tr,), in_specs=[row, row, vec], out_specs=vec,
                          out_shape=jax.ShapeDtypeStruct((1, HW), F32), compiler_params=_params("arbitrary"))(
        lse2, delta, sink2)


def _tri(lower):
    r = lax.broadcasted_iota(jnp.int32, (BLOCK, BLOCK), 0)
    c = lax.broadcasted_iota(jnp.int32, (BLOCK, BLOCK), 1)
    return jnp.where((c <= r) if lower else (c >= r), 1.0, 0.0).astype(F32)


def _gate_cumsum(proj, fg_tile, b_pad, name):
    L = proj.shape[0]

    def body(fg_ref, b_ref, c_ref, carry):
        @pl.when(pl.program_id(0) == 0)
        def _():
            carry[...] = jnp.zeros_like(carry)

        x = fg_ref[...] + b_ref[...]
        lf = jnp.minimum(x, 0.0) - jnp.log(1.0 + jnp.exp(-jnp.abs(x)))
        c = jnp.dot(_tri(True), lf, precision=lax.Precision.HIGHEST, preferred_element_type=F32) + carry[...]
        c_ref[...] = c
        carry[...] = c[BLOCK - 1:BLOCK, :]

    return pl.pallas_call(
        body, name=name, grid=(L // BLOCK,),
        in_specs=[pl.BlockSpec((BLOCK, LANES), lambda i: (i, fg_tile)), pl.BlockSpec((1, LANES), lambda i: (0, 0))],
        out_specs=pl.BlockSpec((BLOCK, LANES), lambda i: (i, 0)),
        out_shape=jax.ShapeDtypeStruct((L, LANES), F32),
        scratch_shapes=[pltpu.VMEM((1, LANES), F32)],
        compiler_params=_params("arbitrary"),
    )(proj, b_pad)


def _gate_cumsum_bwd(dc, proj, fg_tile, b_pad, name):
    L = proj.shape[0]
    nb = L // BLOCK

    def body(dc_ref, fg_ref, b_ref, dfg_ref, db_ref, carry):
        @pl.when(pl.program_id(0) == 0)
        def _():
            carry[...] = jnp.zeros_like(carry)
            db_ref[...] = jnp.zeros_like(db_ref)

        dlf = jnp.dot(_tri(False), dc_ref[...], precision=lax.Precision.HIGHEST,
                      preferred_element_type=F32) + carry[...]
        carry[...] = dlf[0:1, :]
        x = fg_ref[...] + b_ref[...]
        lanes = lax.broadcasted_iota(jnp.int32, (BLOCK, LANES), 1)
        rows = (nb - 1 - pl.program_id(0)) * BLOCK + lax.broadcasted_iota(jnp.int32, (BLOCK, LANES), 0)
        dfg = jnp.where((lanes < HEADS) & (rows >= PAD), dlf / (1.0 + jnp.exp(x)), 0.0)
        dfg_ref[...] = jnp.concatenate([dfg, jnp.zeros_like(dfg)], axis=1)
        db_ref[...] += jnp.sum(dfg, axis=0, keepdims=True)

    dfg, db = pl.pallas_call(
        body, name=name, grid=(nb,),
        in_specs=[pl.BlockSpec((BLOCK, LANES), lambda i: (nb - 1 - i, 0)),
                  pl.BlockSpec((BLOCK, LANES), lambda i: (nb - 1 - i, fg_tile)),
                  pl.BlockSpec((1, LANES), lambda i: (0, 0))],
        out_specs=[pl.BlockSpec((BLOCK, 2 * LANES), lambda i: (nb - 1 - i, 0)),
                   pl.BlockSpec((1, LANES), lambda i: (0, 0))],
        out_shape=[jax.ShapeDtypeStruct((L, 2 * LANES), F32), jax.ShapeDtypeStruct((1, LANES), F32)],
        scratch_shapes=[pltpu.VMEM((1, LANES), F32)],
        compiler_params=_params("arbitrary"),
    )(dc, proj, b_pad)
    return dfg, db[0]


def _rope_tables(L, dim, theta, lane0):
    half = dim // 2
    pos = (jnp.arange(L) - PAD).astype(F32)
    inv = theta ** (-jnp.arange(0, dim, 2, dtype=F32) / dim)
    ang = pos[:, None] * inv[None, :]
    cos, sin = jnp.cos(ang), jnp.sin(ang)
    C = jnp.ones((L, LANES), F32).at[:, lane0:lane0 + half].set(cos).at[:, lane0 + half:lane0 + dim].set(cos)
    S1 = jnp.zeros((L, LANES), F32).at[:, lane0:lane0 + half].set(-sin)
    S2 = jnp.zeros((L, LANES), F32).at[:, lane0 + half:lane0 + dim].set(sin)
    return C, S1, S2


def _rot(x, C, S1, S2, R):
    return x * C + pltpu.roll(x, LANES - R, 1) * S1 + pltpu.roll(x, R, 1) * S2


def _rot_t(dy, C, S1, S2, R):
    return dy * C + pltpu.roll(dy * S1, R, 1) + pltpu.roll(dy * S2, LANES - R, 1)


def _rope(x, nt, tabs, R, name, transpose=False, shared=None, shared_tile=0, out_dtype=F32):
    L = x.shape[0]
    T = BLOCK
    fn = _rot_t if transpose else _rot

    def body(*refs):
        if shared is None:
            x_ref, c_ref, s1_ref, s2_ref, o_ref = refs
        else:
            x_ref, sh_ref, c_ref, s1_ref, s2_ref, o_ref = refs
            rs = fn(sh_ref[...], c_ref[...], s1_ref[...], s2_ref[...], R)
        for h in range(nt):
            cols = slice(h * LANES, (h + 1) * LANES)
            if shared is None:
                o_ref[:, cols] = fn(x_ref[:, cols], c_ref[...], s1_ref[...], s2_ref[...], R).astype(out_dtype)
            else:
                o_ref[:, cols] = (x_ref[:, cols] + rs).astype(out_dtype)

    wide = pl.BlockSpec((T, nt * LANES), lambda i: (i, 0))
    tab = pl.BlockSpec((T, LANES), lambda i: (i, 0))
    ins, specs = [x], [wide]
    if shared is not None:
        ins.append(shared)
        specs.append(pl.BlockSpec((T, LANES), lambda i: (i, shared_tile)))
    return pl.pallas_call(body, name=name, grid=(L // T,), in_specs=specs + [tab, tab, tab], out_specs=wide,
                          out_shape=jax.ShapeDtypeStruct((L, nt * LANES), out_dtype),
                          compiler_params=_params("parallel"))(*ins, *tabs)


def _rope_shared_bwd(dk, nt, tabs, R, name):
    L = dk.shape[0]
    tr = 128

    def body(dk_ref, c_ref, s1_ref, s2_ref, o_ref):
        acc = dk_ref[:, 0:LANES]
        for h in range(1, nt):
            acc = acc + dk_ref[:, h * LANES:(h + 1) * LANES]
        o_ref[...] = _rot_t(acc, c_ref[...], s1_ref[...], s2_ref[...], R)

    tab = pl.BlockSpec((tr, LANES), lambda i: (i, 0))
    return pl.pallas_call(body, name=name, grid=(L // tr,),
                          in_specs=[pl.BlockSpec((tr, nt * LANES), lambda i: (i, 0)), tab, tab, tab], out_specs=tab,
                          out_shape=jax.ShapeDtypeStruct((L, LANES), F32), compiler_params=_params("parallel"))(
        dk, *tabs)


def _rms_fwd(pa, gq, gkv, name):
    L = pa.shape[0]
    tr = 128
    Q, KV = MLA_Q_LORA, MLA_KV_LORA

    def body(pa_ref, gq_ref, gkv_ref, q_ref, kv_ref):
        for lo, n, g_ref, o_ref in ((0, Q, gq_ref, q_ref), (Q, KV, gkv_ref, kv_ref)):
            x = pa_ref[:, lo:lo + n]
            r = lax.rsqrt(jnp.mean(x * x, axis=1, keepdims=True) + RMS_EPS)
            o_ref[...] = (x * r * g_ref[...]).astype(BF16)

    return pl.pallas_call(
        body, name=name, grid=(L // tr,),
        in_specs=[pl.BlockSpec((tr, pa.shape[1]), lambda i: (i, 0)), pl.BlockSpec((1, Q), lambda i: (0, 0)),
                  pl.BlockSpec((1, KV), lambda i: (0, 0))],
        out_specs=[pl.BlockSpec((tr, Q), lambda i: (i, 0)), pl.BlockSpec((tr, KV), lambda i: (i, 0))],
        out_shape=[jax.ShapeDtypeStruct((L, Q), BF16), jax.ShapeDtypeStruct((L, KV), BF16)],
        compiler_params=_params("parallel"),
    )(pa, gq.reshape(1, Q), gkv.reshape(1, KV))


def _rms_bwd(pa, dq, dkv, dkr, gq, gkv, name):
    L, W = pa.shape
    tr = 128
    Q, KV = MLA_Q_LORA, MLA_KV_LORA

    def body(pa_ref, dq_ref, dkv_ref, dkr_ref, gq_ref, gkv_ref, dpa_ref, dgq_ref, dgkv_ref):
        @pl.when(pl.program_id(0) == 0)
        def _():
            dgq_ref[...] = jnp.zeros_like(dgq_ref)
            dgkv_ref[...] = jnp.zeros_like(dgkv_ref)

        for lo, n, g_ref, dy_ref, dg_ref in ((0, Q, gq_ref, dq_ref, dgq_ref), (Q, KV, gkv_ref, dkv_ref, dgkv_ref)):
            x = pa_ref[:, lo:lo + n]
            r = lax.rsqrt(jnp.mean(x * x, axis=1, keepdims=True) + RMS_EPS)
            xh = x * r
            dy = dy_ref[...]
            dxh = dy * g_ref[...]
            dpa_ref[:, lo:lo + n] = (r * (dxh - xh * jnp.mean(dxh * xh, axis=1, keepdims=True))).astype(BF16)
            dg_ref[...] += jnp.sum(dy * xh, axis=0, keepdims=True)
        dpa_ref[:, Q + KV:W] = dkr_ref[...].astype(BF16)

    vq = pl.BlockSpec((1, Q), lambda i: (0, 0))
    vkv = pl.BlockSpec((1, KV), lambda i: (0, 0))
    dpa, dgq, dgkv = pl.pallas_call(
        body, name=name, grid=(L // tr,),
        in_specs=[pl.BlockSpec((tr, W), lambda i: (i, 0)), pl.BlockSpec((tr, Q), lambda i: (i, 0)),
                  pl.BlockSpec((tr, KV), lambda i: (i, 0)), pl.BlockSpec((tr, LANES), lambda i: (i, 0)), vq, vkv],
        out_specs=[pl.BlockSpec((tr, W), lambda i: (i, 0)), vq, vkv],
        out_shape=[jax.ShapeDtypeStruct((L, W), BF16), jax.ShapeDtypeStruct((1, Q), F32),
                   jax.ShapeDtypeStruct((1, KV), F32)],
        compiler_params=_params("arbitrary"),
    )(pa, dq, dkv, dkr, gq.reshape(1, Q), gkv.reshape(1, KV))
    return dpa, dgq[0], dgkv[0]


def _loss_head(h, target, name):
    L = h.shape[0]
    tr = BLOCK
    inv = 1.0 / D_MODEL

    def body(h_ref, t_ref, loss_ref, dh_ref):
        i = pl.program_id(0)

        @pl.when(i == 0)
        def _():
            loss_ref[...] = jnp.zeros_like(loss_ref)
            dh_ref[...] = jnp.zeros_like(dh_ref)

        @pl.when(i > 0)
        def _():
            e = h_ref[...] - t_ref[...]
            dh_ref[...] = e * inv
            loss_ref[...] += jnp.sum((e * e).reshape(tr // 8, 8, D_MODEL), axis=0) * (0.5 * inv)

    row = pl.BlockSpec((tr, D_MODEL), lambda i: (i, 0))
    loss, dh = pl.pallas_call(
        body, name=name, grid=(L // tr,),
        in_specs=[row, pl.BlockSpec((tr, D_MODEL), lambda i: (jnp.maximum(i - 1, 0), 0))],
        out_specs=[pl.BlockSpec((8, D_MODEL), lambda i: (0, 0)), row],
        out_shape=[jax.ShapeDtypeStruct((8, D_MODEL), F32), jax.ShapeDtypeStruct((L, D_MODEL), F32)],
        compiler_params=_params("arbitrary"),
    )(h, target)
    return loss, dh


def _pad_heads_cols(w, nh, d, dp=LANES):
    K = w.shape[0]
    return jnp.pad(w.reshape(K, nh, d), ((0, 0), (0, 0), (0, dp - d))).reshape(K, nh * dp)


def _unpad_heads_cols(w, nh, d, dp=LANES):
    K = w.shape[0]
    return w.reshape(K, nh, dp)[:, :, :d].reshape(K, nh * d)


def _pad_heads_rows(w, nh, d):
    N = w.shape[1]
    return jnp.pad(w.reshape(nh, d, N), ((0, 0), (0, LANES - d), (0, 0))).reshape(nh * LANES, N)


def _unpad_heads_rows(w, nh, d):
    N = w.shape[1]
    return w.reshape(nh, LANES, N)[:, :d, :].reshape(nh * d, N)


def _fox_fwd(h, w_in, b_f, w_o, tag):
    L = h.shape[0]
    hd = HEADS * HEAD_DIM
    W = jnp.concatenate([_pad_heads_cols(w_in[:, i * hd:(i + 1) * hd], HEADS, HEAD_DIM) for i in range(3)]
                        + [jnp.pad(w_in[:, 3 * hd:], ((0, 0), (0, 2 * LANES - HEADS)))], axis=1)
    Wo = _pad_heads_rows(w_o, HEADS, HEAD_DIM)
    b_pad = jnp.pad(b_f, (0, LANES - HEADS)).reshape(1, LANES)
    proj = _mm(h, W, "nn", tag + "_proj")
    c = _gate_cumsum(proj, 3 * HEADS, b_pad, tag + "_cumsum")
    dead = (jnp.arange(L) < PAD)[:, None]
    cr2 = jnp.where(dead, -NEG, c[:, :HEADS] * LOG2E).T.reshape(HEADS, 1, L)
    cfg = _Dense(L, True)
    scale = HEAD_DIM ** -0.5
    o, lse2 = _flash_fwd(cfg, proj, 0, proj, HEADS, proj, 2 * HEADS, HEADS, DENSE_HS_FWD, scale, tag + "_attn", cr2=cr2)
    mix = _mm(o, Wo, "nn", tag + "_out")
    return mix, (h, W, Wo, b_pad, proj, cr2, o, lse2)


def _fox_bwd(dmix, res, tag):
    h, W, Wo, b_pad, proj, cr2, o, lse2 = res
    L = h.shape[0]
    cfg = _Dense(L, True)
    scale = HEAD_DIM ** -0.5
    dWo = _mm(o, dmix, "tn", tag + "_dwo")
    do = _mm(dmix, Wo, "nt", tag + "_do", out_dtype=BF16)
    qkv = (proj, 0, proj, HEADS, proj, 2 * HEADS)
    delta = _delta(do, o, tag + "_delta")
    dk, dv, dq = _flash_bwd(cfg, *qkv, do, lse2, delta, HEADS, DENSE_HS, scale, tag + "_bwd", cr2=cr2, lane_sums=True)
    dc = jnp.pad((dq[:, LANES - 1::LANES] - dk[:, LANES - 1::LANES]) * (1.0 / scale), ((0, 0), (0, LANES - HEADS)))
    dfg, db = _gate_cumsum_bwd(dc, proj, 3 * HEADS, b_pad, tag + "_cumsum_bwd")
    dproj = jnp.concatenate([dq.astype(BF16), dk.astype(BF16), dv.astype(BF16), dfg.astype(BF16)], axis=1)
    dW = _mm(h, dproj, "tn", tag + "_dw")
    dh = _mm(dproj, W, "nt", tag + "_dh")
    hp = HEADS * LANES
    dw_in = jnp.concatenate([_unpad_heads_cols(dW[:, i * hp:(i + 1) * hp], HEADS, HEAD_DIM) for i in range(3)]
                            + [dW[:, 3 * hp:3 * hp + HEADS]], axis=1)
    return dh, dict(w_in=dw_in, b_f=db[:HEADS], w_o=_unpad_heads_rows(dWo, HEADS, HEAD_DIM))


def _swa_fwd(h, w_in, sinks, w_o, tag):
    L = h.shape[0]
    qd, kd = HEADS * HEAD_DIM, SWA_KV * HEAD_DIM
    W = jnp.concatenate([_pad_heads_cols(w_in[:, :qd], HEADS, HEAD_DIM),
                         _pad_heads_cols(w_in[:, qd:qd + kd], SWA_KV, HEAD_DIM),
                         _pad_heads_cols(w_in[:, qd + kd:], SWA_KV, HEAD_DIM)], axis=1)
    Wo = _pad_heads_rows(w_o, HEADS, HEAD_DIM)
    sink2 = jnp.repeat(sinks * LOG2E, LANES).reshape(1, HEADS * LANES)
    tabs = _rope_tables(L, ROPE_DIM, ROPE_THETA, 0)
    proj = _mm(h, W, "nn", tag + "_proj")
    nqk = HEADS + SWA_KV
    qk = _rope(proj, nqk, tabs, ROPE_DIM // 2, tag + "_rope", out_dtype=BF16)
    scale = HEAD_DIM ** -0.5
    o, lse2 = _swa_attn_fwd(qk, proj, sink2, scale, tag + "_attn")
    mix = _mm(o, Wo, "nn", tag + "_out")
    return mix, (h, W, Wo, sink2, tabs, proj, qk, o, lse2)


def _swa_bwd(dmix, res, tag):
    h, W, Wo, sink2, tabs, proj, qk, o, lse2 = res
    L = h.shape[0]
    nqk = HEADS + SWA_KV
    scale = HEAD_DIM ** -0.5
    dWo = _mm(o, dmix, "tn", tag + "_dwo")
    do = _mm(dmix, Wo, "nt", tag + "_do", out_dtype=BF16)
    delta = _delta(do, o, tag + "_delta")
    dsink = _sink_grad(lse2, delta, sink2, tag + "_dsink")[0, ::LANES]
    dq, dk, dv = _swa_attn_bwd(qk, proj, do, lse2, delta, scale, tag + "_bwd")
    dqk = _rope(jnp.concatenate([dq, dk], axis=1), nqk, tabs, ROPE_DIM // 2, tag + "_rope_bwd", transpose=True,
                out_dtype=BF16)
    dproj = jnp.concatenate([dqk, dv.astype(BF16)], axis=1)
    dW = _mm(h, dproj, "tn", tag + "_dw")
    dh = _mm(dproj, W, "nt", tag + "_dh")
    hp = HEADS * LANES
    dw_in = jnp.concatenate([_unpad_heads_cols(dW[:, :hp], HEADS, HEAD_DIM),
                             _unpad_heads_cols(dW[:, hp:hp + SWA_KV * LANES], SWA_KV, HEAD_DIM),
                             _unpad_heads_cols(dW[:, hp + SWA_KV * LANES:], SWA_KV, HEAD_DIM)], axis=1)
    return dh, dict(w_in=dw_in, sinks=dsink, w_o=_unpad_heads_rows(dWo, HEADS, HEAD_DIM))


def _mla_fwd(h, w_a, g_q, g_kv, w_uq, w_ukv, w_o, tag):
    L = h.shape[0]
    Q, KV = MLA_Q_LORA, MLA_KV_LORA
    dqk = MLA_NOPE + MLA_ROPE
    kr_w = jnp.pad(w_a[:, Q + KV:], ((0, 0), (MLA_NOPE, LANES - dqk)))
    Wa = jnp.concatenate([w_a[:, :Q + KV], kr_w], axis=1)
    Wuq = _pad_heads_cols(w_uq, HEADS, dqk)
    ukv = w_ukv.reshape(KV, HEADS, MLA_NOPE + HEAD_DIM)
    Wukv = jnp.concatenate([_pad_heads_cols(ukv[:, :, :MLA_NOPE].reshape(KV, -1), HEADS, MLA_NOPE),
                            _pad_heads_cols(ukv[:, :, MLA_NOPE:].reshape(KV, -1), HEADS, HEAD_DIM)], axis=1)
    Wo = _pad_heads_rows(w_o, HEADS, HEAD_DIM)
    tabs = _rope_tables(L, MLA_ROPE, MLA_ROPE_THETA, MLA_NOPE)
    R = MLA_ROPE // 2
    pa = _mm(h, Wa, "nn", tag + "_proj")
    cqn, ckvn = _rms_fwd(pa, g_q, g_kv, tag + "_rms")
    q0 = _mm(cqn, Wuq, "nn", tag + "_uq")
    qr = _rope(q0, HEADS, tabs, R, tag + "_rope_q", out_dtype=BF16)
    kv0 = _mm(ckvn, Wukv, "nn", tag + "_ukv")
    kk = _rope(kv0, HEADS, tabs, R, tag + "_rope_k", shared=pa, shared_tile=(Q + KV) // LANES, out_dtype=BF16)
    cfg = _Dense(L, False)
    scale = dqk ** -0.5
    o, lse2 = _flash_fwd(cfg, qr, 0, kk, 0, kv0, HEADS, HEADS, DENSE_HS_FWD, scale, tag + "_attn")
    mix = _mm(o, Wo, "nn", tag + "_out")
    return mix, (h, Wa, Wuq, Wukv, Wo, g_q, g_kv, tabs, pa, cqn, ckvn, qr, kk, kv0, o, lse2)


def _mla_bwd(dmix, res, tag):
    h, Wa, Wuq, Wukv, Wo, g_q, g_kv, tabs, pa, cqn, ckvn, qr, kk, kv0, o, lse2 = res
    L = h.shape[0]
    Q, KV = MLA_Q_LORA, MLA_KV_LORA
    dqk = MLA_NOPE + MLA_ROPE
    R = MLA_ROPE // 2
    cfg = _Dense(L, False)
    scale = dqk ** -0.5
    dWo = _mm(o, dmix, "tn", tag + "_dwo")
    do = _mm(dmix, Wo, "nt", tag + "_do", out_dtype=BF16)
    delta = _delta(do, o, tag + "_delta")
    dk, dv, dqr = _flash_bwd(cfg, qr, 0, kk, 0, kv0, HEADS, do, lse2, delta, HEADS, DENSE_HS, scale, tag + "_bwd")
    dq0 = _rope(dqr, HEADS, tabs, R, tag + "_rope_q_bwd", transpose=True, out_dtype=BF16)
    dWuq = _mm(cqn, dq0, "tn", tag + "_dwuq")
    dcqn = _mm(dq0, Wuq, "nt", tag + "_dcq")
    dkv = jnp.concatenate([dk, dv], axis=1).astype(BF16)
    dWukv = _mm(ckvn, dkv, "tn", tag + "_dwukv")
    dckvn = _mm(dkv, Wukv, "nt", tag + "_dckv")
    dkr = _rope_shared_bwd(dk, HEADS, tabs, R, tag + "_rope_k_bwd")
    dpa, dgq, dgkv = _rms_bwd(pa, dcqn, dckvn, dkr, g_q, g_kv, tag + "_rms_bwd")
    dWa = _mm(h, dpa, "tn", tag + "_dw")
    dh = _mm(dpa, Wa, "nt", tag + "_dh")
    hp = HEADS * LANES
    dw_a = jnp.concatenate([dWa[:, :Q + KV], dWa[:, Q + KV + MLA_NOPE:Q + KV + dqk]], axis=1)
    dk_n = dWukv[:, :hp].reshape(KV, HEADS, LANES)[:, :, :MLA_NOPE]
    dv_n = dWukv[:, hp:].reshape(KV, HEADS, LANES)[:, :, :HEAD_DIM]
    dw_ukv = jnp.concatenate([dk_n, dv_n], axis=2).reshape(KV, HEADS * (MLA_NOPE + HEAD_DIM))
    return dh, dict(w_a=dw_a, g_q=dgq, g_kv=dgkv, w_uq=_unpad_heads_cols(dWuq, HEADS, dqk), w_ukv=dw_ukv,
                    w_o=_unpad_heads_rows(dWo, HEADS, HEAD_DIM))


MATMUL_WEIGHTS = ("fox_w_in", "fox_w_o", "swa_w_in", "swa_w_o", "mla_w_a", "mla_w_uq", "mla_w_ukv", "mla_w_o",
                  "ffn_w_in", "ffn_w_out")


def _local_step(x, target, w):
    w = {k: (_bf(v) if k in MATMUL_WEIGHTS else v) for k, v in w.items()}
    h = jnp.concatenate([jnp.zeros((PAD, D_MODEL), F32), w["meta_tokens"], x], axis=0)
    hb = h.astype(BF16)
    saved = []
    for i in range(DEPTH):
        kind, j = i % 3, i // 3
        tag = "l%d" % i
        if kind == 0:
            mix, mres = _fox_fwd(hb, w["fox_w_in"][j], w["fox_b_f"][j], w["fox_w_o"][j], tag + "_fox")
        elif kind == 1:
            mix, mres = _swa_fwd(hb, w["swa_w_in"][j], w["swa_sinks"][j], w["swa_w_o"][j], tag + "_swa")
        else:
            mix, mres = _mla_fwd(hb, w["mla_w_a"][j], w["mla_g_q"][j], w["mla_g_kv"][j], w["mla_w_uq"][j],
                                 w["mla_w_ukv"][j], w["mla_w_o"][j], tag + "_mla")
        h1, h1b, xh1, rs1 = _ln_fwd(h, mix, w["ln1_g"][i], w["ln1_b"][i], tag + "_ln1")
        u = _mm(h1b, w["ffn_w_in"][i], "nn", tag + "_ffn_in")
        a = _conv_glu_fwd(u, w["ffn_conv_w"][i], w["ffn_conv_b"][i], tag + "_conv")
        ffn = _mm(a, w["ffn_w_out"][i], "nn", tag + "_ffn_out")
        h2, h2b, xh2, rs2 = _ln_fwd(h1, ffn, w["ln2_g"][i], w["ln2_b"][i], tag + "_ln2")
        saved.append((mres, xh1, rs1, h1b, u, a, xh2, rs2))
        h, hb = h2, h2b
    loss, dh = _loss_head(h, target, "loss_head")

    g = {k: [None] * v.shape[0] for k, v in w.items() if k != "meta_tokens"}
    ga = None
    for i in reversed(range(DEPTH)):
        kind, j = i % 3, i // 3
        tag = "l%d" % i
        mres, xh1, rs1, h1b, u, a, xh2, rs2 = saved[i]
        dz2, dz2b, g["ln2_g"][i], g["ln2_b"][i] = _ln_bwd(ga, dh, xh2, rs2, w["ln2_g"][i], tag + "_ln2_bwd")
        g["ffn_w_out"][i] = _mm(a, dz2b, "tn", tag + "_dw_out")
        da = _mm(dz2b, w["ffn_w_out"][i], "nt", tag + "_da")
        du, g["ffn_conv_w"][i], g["ffn_conv_b"][i] = _conv_glu_bwd(da, u, w["ffn_conv_w"][i], w["ffn_conv_b"][i],
                                                                   tag + "_conv_bwd")
        g["ffn_w_in"][i] = _mm(h1b, du, "tn", tag + "_dw_in")
        dh1 = _mm(du, w["ffn_w_in"][i], "nt", tag + "_dh1")
        dz1, dz1b, g["ln1_g"][i], g["ln1_b"][i] = _ln_bwd(dz2, dh1, xh1, rs1, w["ln1_g"][i], tag + "_ln1_bwd")
        if kind == 0:
            dh, mg = _fox_bwd(dz1b, mres, tag + "_fox")
            pre = "fox_"
        elif kind == 1:
            dh, mg = _swa_bwd(dz1b, mres, tag + "_swa")
            pre = "swa_"
        else:
            dh, mg = _mla_bwd(dz1b, mres, tag + "_mla")
            pre = "mla_"
        for k, v in mg.items():
            g[pre + k][j] = v
        ga = dz1
    dh0 = _axpy(ga, dh, "dh0")
    grads = {k: jnp.stack(v) for k, v in g.items()}
    grads["meta_tokens"] = dh0[PAD:BLOCK]
    return loss, dh0, grads


SHARDED = (("meta_tokens", 1), ("fox_w_in", 2), ("fox_w_o", 1), ("swa_w_in", 2), ("swa_w_o", 1), ("mla_w_a", 1),
           ("mla_g_q", 1), ("mla_g_kv", 1), ("mla_w_uq", 2), ("mla_w_ukv", 2), ("mla_w_o", 1), ("ffn_w_in", 2),
           ("ffn_conv_w", 2), ("ffn_w_out", 1))
REPLICATED = ("ln1_g", "ln1_b", "ln2_g", "ln2_b", "fox_b_f", "swa_sinks", "ffn_conv_b")
WEIGHTS = ("meta_tokens", "ln1_g", "ln1_b", "ln2_g", "ln2_b", "fox_w_in", "fox_b_f", "fox_w_o", "swa_w_in",
           "swa_sinks", "swa_w_o", "mla_w_a", "mla_g_q", "mla_g_kv", "mla_w_uq", "mla_w_ukv", "mla_w_o", "ffn_w_in",
           "ffn_conv_w", "ffn_conv_b", "ffn_w_out")


def _rows(n):
    return -(-n // ROW)


def _pack(arrs, multiple):
    parts = []
    for a in arrs:
        n = math.prod(a.shape)
        parts.append(jnp.pad(a.reshape(-1), (0, _rows(n) * ROW - n)).reshape(-1, ROW))
    total = sum(p.shape[0] for p in parts)
    pad = -total % multiple
    if pad:
        parts.append(jnp.zeros((pad, ROW), parts[0].dtype))
    return jnp.concatenate(parts, axis=0)


def _unpack(flat, shapes):
    out, r = [], 0
    for s in shapes:
        n = math.prod(s)
        out.append(flat[r:r + _rows(n)].reshape(-1)[:n].reshape(s))
        r += _rows(n)
    return out


def _pack_bf16(w, names):
    return _pack([_bf(w[n]) if n in MATMUL_WEIGHTS else lax.bitcast_convert_type(w[n], BF16) for n in names], 2 * ROW)


HBM_SPEC = pl.BlockSpec(memory_space=pltpu.HBM)


def _place():
    x, y, c = lax.axis_index("x"), lax.axis_index("y"), lax.axis_index("c")
    chips = [(1 - x, y), (x, 1 - y), (1 - x, 1 - y)]
    return x, y, c, chips


def _gather_weights(shard):
    R = shard.shape[0]
    Rh = R // 2

    def body(s_ref, o_ref, send_sems, recv_sems):
        x, y, c, chips = _place()
        sib = (x, y, 1 - c)

        def half(k, hc):
            return o_ref.at[k, pl.ds(hc * Rh, Rh), :]

        def copy(j, src, dst, to):
            return pltpu.make_async_remote_copy(src_ref=src, dst_ref=dst, send_sem=send_sems.at[j],
                                                recv_sem=recv_sems.at[j], device_id=to, device_id_type=MESH)

        me = 2 * x + y
        first = [copy(j, s_ref.at[pl.ds(c * Rh, Rh), :], half(me, c), (tx, ty, c)) for j, (tx, ty) in enumerate(chips)]
        for cp in first:
            cp.start()
        passed = []
        for j, (tx, ty) in enumerate(chips):
            k = 2 * tx + ty
            copy(j, half(k, c), half(k, c), (tx, ty, c)).wait_recv()
            fw = copy(3 + j, half(k, c), half(k, c), sib)
            fw.start()
            passed.append(fw)
        for j, (tx, ty) in enumerate(chips):
            k = 2 * tx + ty
            copy(3 + j, half(k, 1 - c), half(k, 1 - c), sib).wait_recv()
        for cp in first + passed:
            cp.wait_send()

    return pl.pallas_call(
        body, name="gather_weights", out_shape=jax.ShapeDtypeStruct((N_CHIPS, R, ROW), shard.dtype),
        in_specs=[HBM_SPEC], out_specs=HBM_SPEC,
        scratch_shapes=[pltpu.SemaphoreType.DMA((6,)), pltpu.SemaphoreType.DMA((6,))],
    )(shard)


def _swap_halves(G):
    R = G.shape[1]
    Rh = R // 2

    def body(g_ref, a_ref, send_sem, recv_sem):
        x, y, c, _ = _place()
        cp = pltpu.make_async_remote_copy(src_ref=g_ref.at[:, pl.ds((1 - c) * Rh, Rh), :], dst_ref=a_ref,
                                          send_sem=send_sem, recv_sem=recv_sem, device_id=(x, y, 1 - c),
                                          device_id_type=MESH)
        cp.start()
        cp.wait()

    return pl.pallas_call(
        body, name="reduce_swap_halves", out_shape=jax.ShapeDtypeStruct((N_CHIPS, Rh, ROW), G.dtype),
        in_specs=[HBM_SPEC], out_specs=HBM_SPEC,
        scratch_shapes=[pltpu.SemaphoreType.DMA, pltpu.SemaphoreType.DMA],
    )(G)


def _exchange_chips(P):
    def body(p_ref, b_ref, send_sems, recv_sems):
        x, y, c, chips = _place()
        me = 2 * x + y

        def copy(j, src, dst, to):
            return pltpu.make_async_remote_copy(src_ref=src, dst_ref=dst, send_sem=send_sems.at[j],
                                                recv_sem=recv_sems.at[j], device_id=to, device_id_type=MESH)

        sends = [copy(j, p_ref.at[2 * tx + ty], b_ref.at[me], (tx, ty, c)) for j, (tx, ty) in enumerate(chips)]
        for cp in sends:
            cp.start()
        for j, (tx, ty) in enumerate(chips):
            k = 2 * tx + ty
            copy(j, p_ref.at[k], b_ref.at[k], (tx, ty, c)).wait_recv()
        for cp in sends:
            cp.wait_send()

    return pl.pallas_call(
        body, name="reduce_exchange_chips", out_shape=jax.ShapeDtypeStruct(P.shape, P.dtype),
        in_specs=[HBM_SPEC], out_specs=HBM_SPEC,
        scratch_shapes=[pltpu.SemaphoreType.DMA((3,)), pltpu.SemaphoreType.DMA((3,))],
    )(P)


def _swap_reduced(Fh):
    def body(f_ref, o_ref, send_sem, recv_sem):
        x, y, c, _ = _place()
        cp = pltpu.make_async_remote_copy(src_ref=f_ref, dst_ref=o_ref, send_sem=send_sem, recv_sem=recv_sem,
                                          device_id=(x, y, 1 - c), device_id_type=MESH)
        cp.start()
        cp.wait()

    return pl.pallas_call(
        body, name="reduce_swap_reduced", out_shape=jax.ShapeDtypeStruct(Fh.shape, Fh.dtype),
        in_specs=[HBM_SPEC], out_specs=HBM_SPEC,
        scratch_shapes=[pltpu.SemaphoreType.DMA, pltpu.SemaphoreType.DMA],
    )(Fh)


def _gather_small(v):
    m_per = v.shape[0]

    def body(x_ref, out_ref, send_sems, recv_sems, local_sem):
        x, y, c, chips = _place()
        me, sibling = (x, y, c), (x, y, 1 - c)

        def rows(px, py, pc):
            return out_ref.at[pl.ds((4 * px + 2 * py + pc) * m_per, m_per), :]

        def copy(k, block, to, src=None):
            return pltpu.make_async_remote_copy(src_ref=rows(*block) if src is None else src, dst_ref=rows(*block),
                                                send_sem=send_sems.at[k], recv_sem=recv_sems.at[k], device_id=to,
                                                device_id_type=MESH)

        mine = pltpu.make_async_copy(x_ref, rows(*me), local_sem)
        mine.start()
        first = [copy(0, me, sibling, src=x_ref)]
        first += [copy(1 + j, me, (*chip, c), src=x_ref) for j, chip in enumerate(chips)]
        for cp in first:
            cp.start()
        passed = [copy(4 + j, (*chip, c), sibling) for j, chip in enumerate(chips)]
        for j, chip in enumerate(chips):
            copy(1 + j, (*chip, c), me).wait_recv()
            passed[j].start()
        copy(0, sibling, me).wait_recv()
        for j, chip in enumerate(chips):
            copy(4 + j, (*chip, 1 - c), me).wait_recv()
        for cp in first + passed:
            cp.wait_send()
        mine.wait()

    return pl.pallas_call(
        body, name="gather_small", out_shape=jax.ShapeDtypeStruct((N_DEV * m_per, ROW), v.dtype),
        in_specs=[pl.BlockSpec(memory_space=pltpu.VMEM)], out_specs=pl.BlockSpec(memory_space=pltpu.VMEM),
        scratch_shapes=[pltpu.SemaphoreType.DMA((7,)), pltpu.SemaphoreType.DMA((7,)), pltpu.SemaphoreType.DMA],
    )(v)


def _sum_slots(a, n, name):
    M = a.shape[0] // n
    tr = _pick(M, (512, 256, 128, 64, 40, 8))
    nb = M // tr

    def body(*refs):
        acc = refs[0][...].astype(F32)
        for r in refs[1:-1]:
            acc = acc + r[...].astype(F32)
        refs[-1][...] = acc

    specs = [pl.BlockSpec((tr, ROW), functools.partial(lambda i, k: (k * nb + i, 0), k=k)) for k in range(n)]
    return pl.pallas_call(body, name=name, grid=(nb,), in_specs=specs,
                          out_specs=pl.BlockSpec((tr, ROW), lambda i: (i, 0)),
                          out_shape=jax.ShapeDtypeStruct((M, ROW), F32), compiler_params=_params("parallel"))(*([a] * n))


def _add(a, b, name, out_dtype):
    M = a.shape[0]
    tr = _pick(M, (512, 256, 128, 64, 40, 8))

    def body(a_ref, b_ref, o_ref):
        o_ref[...] = (a_ref[...] + b_ref[...]).astype(out_dtype)

    row = pl.BlockSpec((tr, ROW), lambda i: (i, 0))
    return pl.pallas_call(body, name=name, grid=(M // tr,), in_specs=[row, row], out_specs=row,
                          out_shape=jax.ShapeDtypeStruct((M, ROW), out_dtype), compiler_params=_params("parallel"))(a, b)


def _adamw(g, w, m, v, name):
    shp = w.shape
    N = shp[-1]
    M = math.prod(shp[:-1])
    g, w, m, v = (a.reshape(M, N) for a in (g, w, m, v))
    tr = _pick(M, tuple(t for t in (512, 256, 128, 64, 40, 32, 16, 8) if t * N <= 256 * 1024))
    c1 = 1.0 - ADAM_B1 ** ADAM_STEP
    c2 = 1.0 - ADAM_B2 ** ADAM_STEP

    def body(g_ref, w_ref, m_ref, v_ref, d_ref, nm_ref, nv_ref):
        gg = g_ref[...]
        nm = ADAM_B1 * m_ref[...] + (1.0 - ADAM_B1) * gg
        nv = ADAM_B2 * v_ref[...] + (1.0 - ADAM_B2) * (gg * gg)
        nm_ref[...] = nm
        nv_ref[...] = nv
        d_ref[...] = -ADAM_LR * ((nm / c1) / (jnp.sqrt(nv / c2) + ADAM_EPS) + ADAM_WD * w_ref[...])

    row = pl.BlockSpec((tr, N), lambda i: (i, 0))
    shape = jax.ShapeDtypeStruct((M, N), F32)
    outs = pl.pallas_call(body, name=name, grid=(M // tr,), in_specs=[row] * 4, out_specs=[row] * 3,
                          out_shape=[shape] * 3, compiler_params=_params("parallel"))(g, w, m, v)
    return [o.reshape(shp) for o in outs]


def kernel(x, meta_tokens, ln1_g, ln1_b, ln2_g, ln2_b, fox_w_in, fox_b_f, fox_w_o, swa_w_in, swa_sinks, swa_w_o, mla_w_a, mla_g_q, mla_g_kv, mla_w_uq, mla_w_ukv, mla_w_o, ffn_w_in, ffn_conv_w, ffn_conv_b, ffn_w_out, loss_target, m_meta_tokens, m_ln1_g, m_ln1_b, m_ln2_g, m_ln2_b, m_fox_w_in, m_fox_b_f, m_fox_w_o, m_swa_w_in, m_swa_sinks, m_swa_w_o, m_mla_w_a, m_mla_g_q, m_mla_g_kv, m_mla_w_uq, m_mla_w_ukv, m_mla_w_o, m_ffn_w_in, m_ffn_conv_w, m_ffn_conv_b, m_ffn_w_out, v_meta_tokens, v_ln1_g, v_ln1_b, v_ln2_g, v_ln2_b, v_fox_w_in, v_fox_b_f, v_fox_w_o, v_swa_w_in, v_swa_sinks, v_swa_w_o, v_mla_w_a, v_mla_g_q, v_mla_g_kv, v_mla_w_uq, v_mla_w_ukv, v_mla_w_o, v_ffn_w_in, v_ffn_conv_w, v_ffn_conv_b, v_ffn_w_out):
    given = dict(locals())
    w = {n: given[n] for n in WEIGHTS}
    m = {n: given["m_" + n] for n in WEIGHTS}
    v = {n: given["v_" + n] for n in WEIGHTS}
    sh_names = [n for n, _ in SHARDED]
    sh_shapes = [w[n].shape for n in sh_names]

    me = 2 * lax.axis_index("x") + lax.axis_index("y")
    c = lax.axis_index("c")
    packed = _pack_bf16(w, sh_names)
    gathered = lax.dynamic_update_index_in_dim(_gather_weights(packed), packed, me, 0)
    full = dict(w)
    r0 = 0
    for (n, ax), s in zip(SHARDED, sh_shapes):
        cnt = math.prod(s) * (1 if n in MATMUL_WEIGHTS else 2)
        t = gathered[:, r0:r0 + _rows(cnt)].reshape(N_CHIPS, -1)[:, :cnt]
        r0 += _rows(cnt)
        t = t.reshape((N_CHIPS,) + s) if n in MATMUL_WEIGHTS else \
            lax.bitcast_convert_type(t.reshape((N_CHIPS,) + s + (2,)), F32)
        t = jnp.moveaxis(t, 0, ax)
        full[n] = t.reshape(s[:ax] + (N_CHIPS * s[ax],) + s[ax + 1:])

    loss_part, dh0, grads = _local_step(x[0], loss_target[0], full)
    loss = lax.psum(jnp.sum(loss_part), ("x", "y", "c"))
    grad_x = dh0[BLOCK:][None]

    parts = []
    for (n, ax), s in zip(SHARDED, sh_shapes):
        g_n = grads[n].reshape(s[:ax] + (N_CHIPS, s[ax]) + s[ax + 1:])
        g_n = jnp.moveaxis(g_n, ax, 0).reshape(N_CHIPS, -1)
        parts.append(jnp.pad(g_n, ((0, 0), (0, _rows(g_n.shape[1]) * ROW - g_n.shape[1]))))
    width = sum(p.shape[1] for p in parts)
    parts.append(jnp.zeros((N_CHIPS, -width % (2 * ROW * ROW)), F32))
    G = jnp.concatenate(parts, axis=1).reshape(N_CHIPS, -1, ROW)
    Rh = G.shape[1] // 2
    mine = lax.dynamic_slice_in_dim(G, c * Rh, Rh, axis=1)
    P = _add(mine.reshape(N_CHIPS * Rh, ROW), _swap_halves(G).reshape(N_CHIPS * Rh, ROW), "reduce_pair_sum", BF16)
    P = P.reshape(N_CHIPS, Rh, ROW)
    B = lax.dynamic_update_index_in_dim(_exchange_chips(P), lax.dynamic_index_in_dim(P, me, 0, keepdims=False), me, 0)
    Fh = _sum_slots(B.reshape(N_CHIPS * Rh, ROW), N_CHIPS, "reduce_chip_sum")
    other = _swap_reduced(Fh)
    Fg = jnp.concatenate([jnp.where(c == 0, Fh, other), jnp.where(c == 0, other, Fh)], axis=0)
    out = {}
    for n, g_n in zip(sh_names, _unpack(Fg, sh_shapes)):
        out["grad", n] = g_n
        out["delta", n], out["new_m", n], out["new_v", n] = _adamw(g_n, w[n], m[n], v[n], "adamw_" + n)

    rp_shapes = [w[n].shape for n in REPLICATED]
    small = _gather_small(_pack([grads[n] for n in REPLICATED], 8))
    g_r = _sum_slots(small, N_DEV, "reduce_small_sum")
    d_r, m_r, v_r = _adamw(g_r, _pack([w[n] for n in REPLICATED], 8), _pack([m[n] for n in REPLICATED], 8),
                           _pack([v[n] for n in REPLICATED], 8), "adamw_replicated")

    for kind, fr in (("grad", g_r), ("delta", d_r), ("new_m", m_r), ("new_v", v_r)):
        for n, a in zip(REPLICATED, _unpack(fr, rp_shapes)):
            out[kind, n] = a
    return (loss, grad_x, *[out[k, n] for k in ("grad", "delta", "new_m", "new_v") for n in WEIGHTS])
```
